```python
import math
import jax, jax.numpy as jnp
from jax import lax
import numpy as np

D_MODEL = 1024
BATCH = 8
SEQ = 2048
DEPTH = 1

CHUNK = 64
D_MIX = D_MODEL

HG_HEADS = 4
HG_DK = 128
HG_DV = 128
HG_WIDTH = HG_HEADS * HG_DV

AT_HEADS = 8
AT_DH = 64
AT_WIDTH = AT_HEADS * AT_DH
Q_RANK = 256
KV_RANK = 128
IDX_HEADS = 4
IDX_DH = 64
TOPK_MAX = 256

IN_SIZES = (
    HG_HEADS * HG_DK,
    HG_HEADS * HG_DK,
    HG_HEADS * HG_DV,
    HG_HEADS * HG_DV,
    Q_RANK,
    KV_RANK,
    IDX_DH,
    IDX_HEADS,
)
D_IN = sum(IN_SIZES)
IN_SPLITS = tuple(int(v) for v in np.cumsum(IN_SIZES)[:-1])

N_EXPERTS = 64
TOP_K = 8
N_GROUPS = 8
TOPK_GROUPS = 4
D_EXPERT = 256
D_SHARED = 256
ROUTED_SCALE = 2.5

ALPHA = (2.0 * DEPTH) ** 0.25
BETA = (8.0 * DEPTH) ** -0.25
LN_EPS = 1e-5
RMS_EPS = 1e-6

kernel_name = "hybrid_hgrn2_dsa_moe_deepnorm"


def layer_norm(x, g, b):
    xf = x.astype(jnp.float32)
    mu = jnp.mean(xf, axis=-1, keepdims=True)
    var = jnp.mean(jnp.square(xf - mu), axis=-1, keepdims=True)
    return ((xf - mu) * lax.rsqrt(var + LN_EPS) * g + b).astype(x.dtype)


def rms_norm(x, g):
    xf = x.astype(jnp.float32)
    return xf * lax.rsqrt(jnp.mean(jnp.square(xf), axis=-1, keepdims=True) + RMS_EPS) * g


def alibi_slopes(n_heads):
    return jnp.exp2(-8.0 * (jnp.arange(n_heads, dtype=jnp.float32) + 1.0) / n_heads)


def hgrn2_mixer(q, f_logit, inp, gate, lb, norm_g):
    B, S, _ = q.shape
    n_c = S // CHUNK

    def heads(t, d):
        return t.reshape(B, n_c, CHUNK, HG_HEADS, d).transpose(0, 3, 1, 2, 4)

    qf = jax.nn.silu(q.astype(jnp.float32))
    f = lb + (1.0 - lb) * jax.nn.sigmoid(f_logit.astype(jnp.float32))
    kf = 1.0 - f
    qh, kh, lfh = heads(qf, HG_DK), heads(kf, HG_DK), heads(jnp.log(f), HG_DK)
    vh = heads(inp.astype(jnp.float32), HG_DV)

    b = jnp.cumsum(lfh, axis=3)
    b_last = b[:, :, :, -1:, :]
    q_dec = qh * jnp.exp(b)
    k_inv = kh * jnp.exp(-b)

    causal = jnp.tril(jnp.ones((CHUNK, CHUNK), dtype=bool))
    scores = jnp.where(causal, jnp.einsum('bhncd,bhnsd->bhncs', q_dec, k_inv), 0.0)
    o_intra = jnp.einsum('bhncs,bhnsv->bhncv', scores, vh)

    d_state = jnp.einsum('bhncd,bhncv->bhndv', kh * jnp.exp(b_last - b), vh)
    chunk_decay = jnp.exp(b_last[:, :, :, 0, :])

    def step(state, xs):
        ds, dec = xs
        return dec[..., None] * state + ds, state

    s0 = jnp.zeros((B, HG_HEADS, HG_DK, HG_DV), jnp.float32)
    _, s_prev = lax.scan(step, s0, (jnp.moveaxis(d_state, 2, 0), jnp.moveaxis(chunk_decay, 2, 0)))
    s_prev = jnp.moveaxis(s_prev, 0, 2)
    o_inter = jnp.einsum('bhncd,bhndv->bhncv', q_dec, s_prev)

    o = (o_intra + o_inter).transpose(0, 2, 3, 1, 4).reshape(B, S, HG_HEADS, HG_DV)
    o = rms_norm(o, norm_g) * jax.nn.silu(gate.astype(jnp.float32).reshape(B, S, HG_HEADS, HG_DV))
    return o.reshape(B, S, HG_WIDTH)


def dsa_mixer(c_q, c_kv, k_idx_raw, w_idx, q_norm_g, w_q_up, w_qidx_up, kv_norm_g, w_kv_up,
              idx_ln_g, idx_ln_b):
    B, S, _ = c_q.shape
    dt = c_q.dtype
    cq = rms_norm(c_q, q_norm_g).astype(dt)
    q = (cq @ w_q_up).reshape(B, S, AT_HEADS, AT_DH)
    q_idx = (cq @ w_qidx_up).reshape(B, S, IDX_HEADS, IDX_DH)
    ckv = rms_norm(c_kv, kv_norm_g).astype(dt)
    kv = ckv @ w_kv_up
    k = kv[..., :AT_WIDTH].reshape(B, S, AT_HEADS, AT_DH)
    v = kv[..., AT_WIDTH:].reshape(B, S, AT_HEADS, AT_DH)
    k_idx = layer_norm(k_idx_raw, idx_ln_g, idx_ln_b)
    w_heads = w_idx.astype(jnp.float32) * (IDX_HEADS ** -0.5) * (IDX_DH ** -0.5)

    k_top = min(TOPK_MAX, S // 4)
    slopes = alibi_slopes(AT_HEADS)
    scale = AT_DH ** -0.5
    key_pos = jnp.arange(S)

    def block(c):
        start = c * CHUNK
        end = start + CHUNK
        qb = lax.dynamic_slice_in_dim(q, start, CHUNK, axis=1)
        qib = lax.dynamic_slice_in_dim(q_idx, start, CHUNK, axis=1)
        wb = lax.dynamic_slice_in_dim(w_heads, start, CHUNK, axis=1)
        logits = jnp.einsum('bcid,bsd->bcis', qib, k_idx, preferred_element_type=jnp.float32)
        idx_score = jnp.einsum('bci,bcis->bcs', wb, jax.nn.relu(logits))
        idx_score = jnp.where(key_pos < end, idx_score, -jnp.inf)
        _, sel = lax.top_k(idx_score, k_top)
        k_sel = jax.vmap(lambda kk, ii: kk[ii])(k, sel)
        v_sel = jax.vmap(lambda vv, ii: vv[ii])(v, sel)
        s = jnp.einsum('bchd,bckhd->bchk', qb, k_sel, preferred_element_type=jnp.float32) * scale
        q_pos = start + jnp.arange(CHUNK)
        dist = jnp.abs(q_pos[None, :, None] - sel).astype(jnp.float32)
        s = s - slopes[None, None, :, None] * dist[:, :, None, :]
        s = jnp.where((sel < end)[:, :, None, :], s, -jnp.inf)
        p = jax.nn.softmax(s, axis=-1)
        return jnp.einsum('bchk,bckhd->bchd', p, v_sel.astype(jnp.float32))

    out = lax.map(block, jnp.arange(S // CHUNK))
    return out.transpose(1, 0, 2, 3, 4).reshape(B, S, AT_WIDTH)


def moe_ffn(x, w_router, router_bias, w_e_gate, w_e_up, w_e_down, w_s_gate, w_s_up, w_s_down):
    B, S, D = x.shape
    t = x.reshape(-1, D)
    T = t.shape[0]
    scores = jax.nn.sigmoid((t @ w_router).astype(jnp.float32))
    sel_scores = scores + router_bias.astype(jnp.float32)
    per_group = N_EXPERTS // N_GROUPS
    grp_score = lax.top_k(sel_scores.reshape(T, N_GROUPS, per_group), 2)[0].sum(-1)
    _, g_idx = lax.top_k(grp_score, TOPK_GROUPS)
    g_mask = jnp.any(g_idx[..., None] == jnp.arange(N_GROUPS), axis=1)
    e_mask = jnp.repeat(g_mask, per_group, axis=1)
    _, e_idx = lax.top_k(jnp.where(e_mask, sel_scores, -jnp.inf), TOP_K)
    w = jnp.take_along_axis(scores, e_idx, axis=1)
    w = w / jnp.sum(w, axis=-1, keepdims=True) * ROUTED_SCALE
    gates = jnp.zeros((T, N_EXPERTS), jnp.float32).at[jnp.arange(T)[:, None], e_idx].set(w)
    gates = gates.astype(t.dtype)

    def expert(acc, xs):
        wg, wu, wd, g = xs
        h = jax.nn.silu(t @ wg) * (t @ wu)
        return acc + g[:, None] * (h @ wd), None

    routed, _ = lax.scan(expert, jnp.zeros_like(t), (w_e_gate, w_e_up, w_e_down, gates.T))
    shared = (jax.nn.silu(t @ w_s_gate) * (t @ w_s_up)) @ w_s_down
    return (routed + shared).reshape(B, S, D)


def setup_inputs(seed: int = 0) -> dict:
    key = jax.random.key(seed)
    ks = jax.random.split(key, 32)
    f32 = jnp.float32
    L = DEPTH

    def nrm(k, shape, s):
        return jax.random.normal(k, shape, f32) * s

    return {
        "x": jax.random.normal(ks[0], (BATCH, SEQ, D_MODEL), f32),
        "w_in": nrm(ks[1], (L, D_MODEL, D_IN), D_MODEL ** -0.5),
        "hg_lb_logits": nrm(ks[2], (DEPTH + 1, HG_HEADS * HG_DK), 0.1),
        "hg_norm_g": 1.0 + nrm(ks[3], (L, HG_DV), 0.02),
        "q_norm_g": 1.0 + nrm(ks[4], (L, Q_RANK), 0.02),
        "w_q_up": nrm(ks[5], (L, Q_RANK, AT_WIDTH), Q_RANK ** -0.5),
        "w_qidx_up": nrm(ks[6], (L, Q_RANK, IDX_HEADS * IDX_DH), Q_RANK ** -0.5),
        "kv_norm_g": 1.0 + nrm(ks[7], (L, KV_RANK), 0.02),
        "w_kv_up": nrm(ks[8], (L, KV_RANK, 2 * AT_WIDTH), KV_RANK ** -0.5),
        "idx_ln_g": 1.0 + nrm(ks[9], (L, IDX_DH), 0.02),
        "idx_ln_b": nrm(ks[10], (L, IDX_DH), 0.02),
        "w_out": nrm(ks[11], (L, D_MIX, D_MODEL), BETA * D_MIX ** -0.5),
        "ln1_g": 1.0 + nrm(ks[12], (L, D_MODEL), 0.02),
        "ln1_b": nrm(ks[13], (L, D_MODEL), 0.02),
        "w_router": nrm(ks[14], (L, D_MODEL, N_EXPERTS), D_MODEL ** -0.5),
        "router_bias": nrm(ks[15], (L, N_EXPERTS), 0.01),
        "w_e_gate": nrm(ks[16], (L, N_EXPERTS, D_MODEL, D_EXPERT), D_MODEL ** -0.5),
        "w_e_up": nrm(ks[17], (L, N_EXPERTS, D_MODEL, D_EXPERT), D_MODEL ** -0.5),
        "w_e_down": nrm(ks[18], (L, N_EXPERTS, D_EXPERT, D_MODEL), BETA * D_EXPERT ** -0.5),
        "w_s_gate": nrm(ks[19], (L, D_MODEL, D_SHARED), D_MODEL ** -0.5),
        "w_s_up": nrm(ks[20], (L, D_MODEL, D_SHARED), D_MODEL ** -0.5),
        "w_s_down": nrm(ks[21], (L, D_SHARED, D_MODEL), BETA * D_SHARED ** -0.5),
        "ln2_g": 1.0 + nrm(ks[22], (L, D_MODEL), 0.02),
        "ln2_b": nrm(ks[23], (L, D_MODEL), 0.02),
    }


def reference(x, w_in, hg_lb_logits, hg_norm_g, q_norm_g, w_q_up, w_qidx_up, kv_norm_g, w_kv_up,
              idx_ln_g, idx_ln_b, w_out, ln1_g, ln1_b, w_router, router_bias, w_e_gate, w_e_up,
              w_e_down, w_s_gate, w_s_up, w_s_down, ln2_g, ln2_b):
    h = x
    lb_all = jnp.cumsum(jax.nn.softmax(hg_lb_logits.astype(jnp.float32), axis=0), axis=0)
    for l in range(DEPTH):
        proj = h @ w_in[l]
        hq, hf, hi, hg, cq, ckv, kidx, widx = jnp.split(proj, IN_SPLITS, axis=-1)
        o_a = hgrn2_mixer(hq, hf, hi, hg, lb_all[l], hg_norm_g[l])
        o_b = dsa_mixer(cq, ckv, kidx, widx, q_norm_g[l], w_q_up[l], w_qidx_up[l], kv_norm_g[l],
                        w_kv_up[l], idx_ln_g[l], idx_ln_b[l])
        mix = jnp.concatenate([o_a.astype(h.dtype), o_b.astype(h.dtype)], axis=-1) @ w_out[l]
        h = layer_norm(ALPHA * h + mix, ln1_g[l], ln1_b[l])
        ffn = moe_ffn(h, w_router[l], router_bias[l], w_e_gate[l], w_e_up[l], w_e_down[l],
                      w_s_gate[l], w_s_up[l], w_s_down[l])
        h = layer_norm(ALPHA * h + ffn, ln2_g[l], ln2_b[l])
    return h
```

```python
import functools

import jax
import jax.numpy as jnp
from jax import lax
from jax.experimental import pallas as pl
from jax.experimental.pallas import tpu as pltpu

F32 = jnp.float32
BF16 = jnp.bfloat16
I32 = jnp.int32
HIGHEST = lax.Precision.HIGHEST

CHUNK = 64
HG_HEADS = 4
HG_DK = 128
HG_DV = 128
HG_WIDTH = HG_HEADS * HG_DV
AT_HEADS = 8
AT_DH = 64
AT_WIDTH = AT_HEADS * AT_DH
Q_RANK = 256
KV_RANK = 128
IDX_HEADS = 4
IDX_DH = 64
TOPK_MAX = 256
N_EXPERTS = 64
TOP_K = 8
N_GROUPS = 8
TOPK_GROUPS = 4
D_EXPERT = 256
ROUTED_SCALE = 2.5
DEPTH = 1
ALPHA = (2.0 * DEPTH) ** 0.25
LN_EPS = 1e-5
RMS_EPS = 1e-6

LANES = 128
SUBLANES = 8
QBLK = 2 * CHUNK
KBLK = 256
VMEM_LIMIT = 56 * 1024 * 1024
INT_MIN = -2 ** 31


def _params(sem, vmem=VMEM_LIMIT):
    return pltpu.CompilerParams(dimension_semantics=sem, vmem_limit_bytes=vmem)


def _dot(a, b, precision=None):
    return jnp.dot(a, b, preferred_element_type=F32, precision=precision)


def _dot_nt(a, b, precision=None):
    return lax.dot_general(a, b, (((1,), (1,)), ((), ())), preferred_element_type=F32,
                           precision=precision)


def _dot_tn(a, b, precision=None):
    return lax.dot_general(a, b, (((0,), (0,)), ((), ())), preferred_element_type=F32,
                           precision=precision)


def _sigmoid(x):
    return 1.0 / (1.0 + jnp.exp(-x))


def _silu(x):
    return x * _sigmoid(x)


def _proj_main_kernel(x_ref, w_ref, o_ref):
    o_ref[...] = _dot(x_ref[...].astype(BF16), w_ref[...])


def _proj_main(x2, w_bf, tm=256):
    t, d = x2.shape
    n = w_bf.shape[1]
    return pl.pallas_call(
        _proj_main_kernel,
        grid=(t // tm,),
        in_specs=[pl.BlockSpec((tm, d), lambda i: (i, 0)),
                  pl.BlockSpec((d, n), lambda i: (0, 0))],
        out_specs=pl.BlockSpec((tm, n), lambda i: (i, 0)),
        out_shape=jax.ShapeDtypeStruct((t, n), F32),
        compiler_params=_params(("parallel",)),
        name="proj_main",
    )(x2, w_bf)


def _hgrn_kernel(p_ref, lbl_ref, ng_ref, o_ref, st_ref, *, n_chunks, layer):
    @pl.when(pl.program_id(1) == 0)
    def _():
        st_ref[...] = jnp.zeros_like(st_ref)

    lg = lbl_ref[...]
    ex = jnp.exp(lg - jnp.max(lg, axis=0, keepdims=True))
    lb_all = jnp.sum(ex[: layer + 1], axis=0, keepdims=True) / jnp.sum(ex, axis=0, keepdims=True)
    ng = ng_ref[...]

    r = lax.broadcasted_iota(I32, (CHUNK, CHUNK), 0)
    c = lax.broadcasted_iota(I32, (CHUNK, CHUNK), 1)
    causal = c <= r
    tri = jnp.where(causal, 1.0, 0.0).astype(F32)

    def chunk_body(j, carry):
        rows = pl.ds(pl.multiple_of(j * CHUNK, CHUNK), CHUNK)
        for h in range(HG_HEADS):
            lo = h * HG_DK
            q = _silu(p_ref[0, rows, lo:lo + HG_DK])
            fl = p_ref[0, rows, HG_WIDTH + lo:HG_WIDTH + lo + HG_DK]
            v = p_ref[0, rows, 2 * HG_WIDTH + lo:2 * HG_WIDTH + lo + HG_DV]
            gate = p_ref[0, rows, 3 * HG_WIDTH + lo:3 * HG_WIDTH + lo + HG_DV]
            lb = lb_all[:, lo:lo + HG_DK]
            f = lb + (1.0 - lb) * _sigmoid(fl)
            k = 1.0 - f
            b = _dot(tri, jnp.log(f), precision=HIGHEST)
            b_last = b[CHUNK - 1:CHUNK, :]
            q_dec = (q * jnp.exp(b)).astype(BF16)
            k_inv = (k * jnp.exp(-b)).astype(BF16)
            k_dec = (k * jnp.exp(b_last - b)).astype(BF16)
            vb = v.astype(BF16)
            scores = jnp.where(causal, _dot_nt(q_dec, k_inv), 0.0)
            o = _dot(scores.astype(BF16), vb)
            st = st_ref[h]
            o = o + _dot_nt(q_dec, st.astype(BF16))
            st_ref[h] = st * jnp.exp(b_last) + _dot_tn(vb, k_dec)
            o = o * lax.rsqrt(jnp.mean(o * o, axis=-1, keepdims=True) + RMS_EPS) * ng
            o_ref[0, rows, lo:lo + HG_DV] = (o * _silu(gate)).astype(o_ref.dtype)
        return carry

    lax.fori_loop(0, n_chunks, chunk_body, 0)


def _hgrn(proj, lb_logits, norm_g, layer, ct=512):
    b, s, w = proj.shape
    n_chunks = ct // CHUNK
    return pl.pallas_call(
        functools.partial(_hgrn_kernel, n_chunks=n_chunks, layer=layer),
        grid=(b, s // ct),
        in_specs=[pl.BlockSpec((1, ct, w), lambda i, j: (i, j, 0)),
                  pl.BlockSpec(lb_logits.shape, lambda i, j: (0, 0)),
                  pl.BlockSpec((1, HG_DV), lambda i, j: (0, 0))],
        out_specs=pl.BlockSpec((1, ct, HG_WIDTH), lambda i, j: (i, j, 0)),
        out_shape=jax.ShapeDtypeStruct((b, s, HG_WIDTH), BF16),
        scratch_shapes=[pltpu.VMEM((HG_HEADS, HG_DV, HG_DK), F32)],
        compiler_params=_params(("parallel", "arbitrary")),
        name="hgrn",
    )(proj, lb_logits, norm_g.reshape(1, HG_DV))


TAIL_W = 512
_CKV0 = Q_RANK
_KIDX0 = Q_RANK + KV_RANK


def _dsa_prep_kernel(x_ref, wt_ref, qg_ref, kvg_ref, lng_ref, lnb_ref, wqT_ref, wqiT_ref,
                     wk_ref, wvT_ref, qT_ref, qiT_ref, k_ref, vT_ref, kidx_ref, wT_ref):
    tail = _dot(x_ref[...], wt_ref[...], precision=HIGHEST)
    cq = tail[:, :Q_RANK]
    cqn = cq * lax.rsqrt(jnp.mean(cq * cq, axis=-1, keepdims=True) + RMS_EPS) * qg_ref[...]
    qT_ref[0] = (_dot_nt(wqT_ref[...], cqn.astype(BF16)) * (AT_DH ** -0.5)).astype(qT_ref.dtype)
    qiT_ref[0] = _dot_nt(wqiT_ref[...], cqn, precision=HIGHEST)
    ckv = tail[:, _CKV0:_CKV0 + KV_RANK]
    ckvn = (ckv * lax.rsqrt(jnp.mean(ckv * ckv, axis=-1, keepdims=True) + RMS_EPS)
            * kvg_ref[...]).astype(BF16)
    k_ref[0] = _dot(ckvn, wk_ref[...]).astype(k_ref.dtype)
    vT_ref[0, 0] = _dot_nt(wvT_ref[...], ckvn).astype(vT_ref.dtype)
    kraw = tail[:, _KIDX0:_KIDX0 + IDX_DH]
    mu = jnp.mean(kraw, axis=-1, keepdims=True)
    var = jnp.mean(jnp.square(kraw - mu), axis=-1, keepdims=True)
    kidx_ref[0] = (kraw - mu) * lax.rsqrt(var + LN_EPS) * lng_ref[...] + lnb_ref[...]
    tT = tail[:, _KIDX0:_KIDX0 + LANES].T
    wT_ref[0] = tT[IDX_DH:IDX_DH + SUBLANES, :] * (IDX_HEADS ** -0.5 * IDX_DH ** -0.5)


def _dsa_prep(x, w_tail, q_norm_g, kv_norm_g, idx_ln_g, idx_ln_b, wqT, wqiT, wk, wvT):
    b, s, d = x.shape
    tm = KBLK
    nb = s // tm
    full = lambda a: pl.BlockSpec(a.shape, lambda i, j: (0,) * a.ndim)
    args = (w_tail, q_norm_g.reshape(1, -1), kv_norm_g.reshape(1, -1), idx_ln_g.reshape(1, -1),
            idx_ln_b.reshape(1, -1), wqT, wqiT, wk, wvT)
    return pl.pallas_call(
        _dsa_prep_kernel,
        grid=(b, nb),
        in_specs=[pl.BlockSpec((None, tm, d), lambda i, j: (i, j, 0))] + [full(a) for a in args],
        out_specs=[pl.BlockSpec((1, AT_WIDTH, tm), lambda i, j: (i, 0, j)),
                   pl.BlockSpec((1, IDX_HEADS * IDX_DH, tm), lambda i, j: (i, 0, j)),
                   pl.BlockSpec((1, tm, AT_WIDTH), lambda i, j: (i, j, 0)),
                   pl.BlockSpec((1, 1, AT_WIDTH, tm), lambda i, j: (i, j, 0, 0)),
                   pl.BlockSpec((1, tm, IDX_DH), lambda i, j: (i, j, 0)),
                   pl.BlockSpec((1, SUBLANES, tm), lambda i, j: (i, 0, j))],
        out_shape=[jax.ShapeDtypeStruct((b, AT_WIDTH, s), BF16),
                   jax.ShapeDtypeStruct((b, IDX_HEADS * IDX_DH, s), F32),
                   jax.ShapeDtypeStruct((b, s, AT_WIDTH), BF16),
                   jax.ShapeDtypeStruct((b, nb, AT_WIDTH, tm), BF16),
                   jax.ShapeDtypeStruct((b, s, IDX_DH), F32),
                   jax.ShapeDtypeStruct((b, SUBLANES, s), F32)],
        compiler_params=_params(("parallel", "parallel")),
        name="dsa_prep",
    )(x, *args)


def _sortable(x):
    bits = lax.bitcast_convert_type(x, I32)
    return jnp.where(bits < 0, bits ^ jnp.int32(0x7FFFFFFF), bits)


def _dsa_attn_kernel(qT_ref, qiT_ref, wT_ref, kidx_ref, k_ref, vT_ref, o_ref,
                     keys_ref, q2_ref, qbd_ref, acc_ref, m_ref, l_ref, *, k_top):
    qb = pl.program_id(1)
    t0 = qb * QBLK
    e_max = t0 + QBLK
    nkb = (e_max + KBLK - 1) >> 8

    lane = lax.broadcasted_iota(I32, (1, QBLK), 1)
    t_idx = t0 + lane
    end_t = ((t_idx >> 6) + 1) * CHUNK
    srow = lax.broadcasted_iota(I32, (KBLK, QBLK), 0)

    for h in range(IDX_HEADS):
        q2_ref[:, h * QBLK:(h + 1) * QBLK] = qiT_ref[0, h * IDX_DH:(h + 1) * IDX_DH, :]
    wrow = jnp.concatenate([wT_ref[0, h:h + 1, :] for h in range(IDX_HEADS)], axis=1)

    def score_body(kb, carry):
        rows = pl.ds(pl.multiple_of(kb * KBLK, KBLK), KBLK)
        logits = _dot(kidx_ref[0, rows, :], q2_ref[...], precision=HIGHEST)
        r = jnp.maximum(logits, 0.0) * wrow
        sc = r[:, :QBLK]
        for h in range(1, IDX_HEADS):
            sc = sc + r[:, h * QBLK:(h + 1) * QBLK]
        valid = (srow + kb * KBLK) < end_t
        keys_ref[rows, :] = _sortable(jnp.where(valid, sc, -jnp.inf))
        return carry

    lax.fori_loop(0, nkb, score_body, 0)

    def count_ge(cand):
        def body(kb, acc):
            rows = pl.ds(pl.multiple_of(kb * KBLK, KBLK), KBLK)
            hit = jnp.where(keys_ref[rows, :] >= cand, 1, 0).astype(I32)
            return acc + jnp.sum(hit.reshape(KBLK // SUBLANES, SUBLANES, QBLK), axis=0)
        acc = lax.fori_loop(0, nkb, body, jnp.zeros((SUBLANES, QBLK), I32))
        return jnp.sum(acc, axis=0, keepdims=True)

    zero = jnp.zeros((1, QBLK), I32)
    thr = jnp.where(count_ge(zero) >= k_top, zero, jnp.full((1, QBLK), INT_MIN, I32))

    def bit_body(i, thr):
        cand = thr | (jnp.int32(1) << (30 - i))
        return jnp.where(count_ge(cand) >= k_top, cand, thr)

    thr = lax.fori_loop(0, 31, bit_body, thr)
    need = (k_top - (count_ge(thr + 1))).astype(F32)

    for j in range(AT_HEADS // 2):
        pair = qT_ref[0, j * LANES:(j + 1) * LANES, :]
        rr = lax.broadcasted_iota(I32, pair.shape, 0)
        zeros = jnp.zeros_like(pair)
        qbd_ref[j, :, :QBLK] = jnp.where(rr < AT_DH, pair, zeros)
        qbd_ref[j, :, QBLK:] = jnp.where(rr >= AT_DH, pair, zeros)
    acc_ref[...] = jnp.zeros_like(acc_ref)
    m_ref[...] = jnp.full_like(m_ref, -jnp.inf)
    l_ref[...] = jnp.zeros_like(l_ref)

    ri = lax.broadcasted_iota(I32, (KBLK, KBLK), 0)
    ci = lax.broadcasted_iota(I32, (KBLK, KBLK), 1)
    lstrict = jnp.where(ci < ri, 1.0, 0.0).astype(BF16)

    def attn_body(kb, carry):
        rows = pl.ds(pl.multiple_of(kb * KBLK, KBLK), KBLK)
        kblk = keys_ref[rows, :]
        s_idx = srow + kb * KBLK
        eq = kblk == thr
        eqf = jnp.where(eq, 1.0, 0.0)
        before = _dot(lstrict, eqf.astype(BF16)) + carry
        sel = ((kblk > thr) | (eq & (before < need))) & (s_idx < end_t)
        bias = jnp.where(sel, 0.0, -jnp.inf)
        dist = jnp.abs(t_idx - s_idx).astype(F32)
        for j in range(AT_HEADS // 2):
            s2 = _dot(k_ref[0, rows, j * LANES:(j + 1) * LANES], qbd_ref[j])
            for half in range(2):
                h = 2 * j + half
                slope = 2.0 ** (-8.0 * (h + 1) / AT_HEADS)
                st = s2[:, half * QBLK:(half + 1) * QBLK] - slope * dist + bias
                m_old = m_ref[h:h + 1, :]
                m_new = jnp.maximum(m_old, jnp.max(st, axis=0, keepdims=True))
                m_safe = jnp.where(m_new == -jnp.inf, 0.0, m_new)
                alpha = jnp.exp(m_old - m_safe)
                p = jnp.exp(st - m_safe)
                l_ref[h:h + 1, :] = alpha * l_ref[h:h + 1, :] + jnp.sum(p, axis=0, keepdims=True)
                m_ref[h:h + 1, :] = m_new
                hs = slice(h * AT_DH, (h + 1) * AT_DH)
                acc_ref[hs, :] = alpha * acc_ref[hs, :] + _dot(vT_ref[0, kb, hs, :], p.astype(BF16))
        return carry + jnp.sum(eqf, axis=0, keepdims=True)

    lax.fori_loop(0, nkb, attn_body, jnp.zeros((1, QBLK), F32))

    for h in range(AT_HEADS):
        hs = slice(h * AT_DH, (h + 1) * AT_DH)
        acc_ref[hs, :] = acc_ref[hs, :] / l_ref[h:h + 1, :]
    o_ref[0] = acc_ref[...].T.astype(o_ref.dtype)


def _dsa_attn(qT, qiT, wT, kidx, k, vT):
    b, _, s = qT.shape
    nb = s // KBLK
    k_top = min(TOPK_MAX, s // 4)
    return pl.pallas_call(
        functools.partial(_dsa_attn_kernel, k_top=k_top),
        grid=(b, s // QBLK),
        in_specs=[pl.BlockSpec((1, AT_WIDTH, QBLK), lambda i, j: (i, 0, j)),
                  pl.BlockSpec((1, IDX_HEADS * IDX_DH, QBLK), lambda i, j: (i, 0, j)),
                  pl.BlockSpec((1, SUBLANES, QBLK), lambda i, j: (i, 0, j)),
                  pl.BlockSpec((1, s, IDX_DH), lambda i, j: (i, 0, 0)),
                  pl.BlockSpec((1, s, AT_WIDTH), lambda i, j: (i, 0, 0)),
                  pl.BlockSpec((1, nb, AT_WIDTH, KBLK), lambda i, j: (i, 0, 0, 0))],
        out_specs=pl.BlockSpec((1, QBLK, AT_WIDTH), lambda i, j: (i, j, 0)),
        out_shape=jax.ShapeDtypeStruct((b, s, AT_WIDTH), BF16),
        scratch_shapes=[pltpu.VMEM((s, QBLK), I32),
                        pltpu.VMEM((IDX_DH, IDX_HEADS * QBLK), F32),
                        pltpu.VMEM((AT_HEADS // 2, LANES, 2 * QBLK), BF16),
                        pltpu.VMEM((AT_WIDTH, QBLK), F32),
                        pltpu.VMEM((AT_HEADS, QBLK), F32),
                        pltpu.VMEM((AT_HEADS, QBLK), F32)],
        compiler_params=_params(("parallel", "arbitrary")),
        name="dsa_attn",
    )(qT, qiT, wT, kidx, k, vT)


def _layer_norm(v, g, b):
    mu = jnp.mean(v, axis=-1, keepdims=True)
    var = jnp.mean(jnp.square(v - mu), axis=-1, keepdims=True)
    return (v - mu) * lax.rsqrt(var + LN_EPS) * g + b


def _out_ln_kernel(oa_ref, ob_ref, x_ref, wa_ref, wb_ref, g_ref, b_ref, h_ref):
    mix = _dot(oa_ref[...], wa_ref[...]) + _dot(ob_ref[...], wb_ref[...])
    h_ref[...] = _layer_norm(ALPHA * x_ref[...] + mix, g_ref[...], b_ref[...])


def _out_ln(oa, ob, x2, wa, wb, g, b, tm=512):
    t, d = x2.shape
    row = lambda w: pl.BlockSpec((tm, w), lambda i: (i, 0))
    full = lambda a: pl.BlockSpec(a.shape, lambda i: (0, 0))
    return pl.pallas_call(
        _out_ln_kernel,
        grid=(t // tm,),
        in_specs=[row(oa.shape[1]), row(ob.shape[1]), row(d), full(wa), full(wb), full(g), full(b)],
        out_specs=row(d),
        out_shape=jax.ShapeDtypeStruct((t, d), F32),
        compiler_params=_params(("parallel",)),
        name="out_ln",
    )(oa, ob, x2, wa, wb, g, b)


def _first_max(v, idx, big):
    m = jnp.max(v, axis=0, keepdims=True)
    first = jnp.min(jnp.where(v == m, idx, big), axis=0, keepdims=True)
    return m, first


def _router_kernel(h_ref, wrT_ref, bias_ref, g_ref):
    tm = h_ref.shape[0]
    per_group = N_EXPERTS // N_GROUPS
    logits = _dot_nt(wrT_ref[...], h_ref[...], precision=HIGHEST)
    scores = _sigmoid(logits)
    sel = scores + bias_ref[...]
    iota_g = lax.broadcasted_iota(I32, (per_group, tm), 0)
    neg = -jnp.inf

    grp = jnp.zeros((N_GROUPS, tm), F32)
    iota_grp = lax.broadcasted_iota(I32, (N_GROUPS, tm), 0)
    for g in range(N_GROUPS):
        blk = sel[g * per_group:(g + 1) * per_group, :]
        m1, i1 = _first_max(blk, iota_g, per_group)
        m2 = jnp.max(jnp.where(iota_g == i1, neg, blk), axis=0, keepdims=True)
        grp = jnp.where(iota_grp == g, m1 + m2, grp)

    gmask = jnp.zeros((N_GROUPS, tm), F32)
    work = grp
    for _ in range(TOPK_GROUPS):
        _, gi = _first_max(work, iota_grp, N_GROUPS)
        hit = iota_grp == gi
        gmask = jnp.where(hit, 1.0, gmask)
        work = jnp.where(hit, neg, work)

    e_mask = jnp.concatenate(
        [jnp.broadcast_to(gmask[g:g + 1, :], (per_group, tm)) for g in range(N_GROUPS)], axis=0)
    iota_e = lax.broadcasted_iota(I32, (N_EXPERTS, tm), 0)
    work = jnp.where(e_mask > 0.5, sel, neg)
    w = jnp.zeros((N_EXPERTS, tm), F32)
    for _ in range(TOP_K):
        _, ei = _first_max(work, iota_e, N_EXPERTS)
        hit = iota_e == ei
        w = jnp.where(hit, scores, w)
        work = jnp.where(hit, neg, work)

    gates = w / jnp.sum(w, axis=0, keepdims=True) * ROUTED_SCALE
    gates = jnp.concatenate([gates, jnp.zeros((LANES - N_EXPERTS, tm), F32)], axis=0)
    g_ref[...] = gates.T


def _router(h1, wrT, bias_col, tm=256):
    t, d = h1.shape
    return pl.pallas_call(
        _router_kernel,
        grid=(t // tm,),
        in_specs=[pl.BlockSpec((tm, d), lambda i: (i, 0)),
                  pl.BlockSpec(wrT.shape, lambda i: (0, 0)),
                  pl.BlockSpec(bias_col.shape, lambda i: (0, 0))],
        out_specs=pl.BlockSpec((tm, LANES), lambda i: (i, 0)),
        out_shape=jax.ShapeDtypeStruct((t, LANES), F32),
        compiler_params=_params(("parallel",)),
        name="router",
    )(h1, wrT, bias_col)


def _moe_kernel(h_ref, g_ref, wg_ref, wu_ref, wd_ref, sg_ref, su_ref, sd_ref, lg_ref, lb_ref,
                o_ref, xb_ref, acc_ref):
    e = pl.program_id(1)

    @pl.when(e == 0)
    def _():
        xb = h_ref[...].astype(BF16)
        xb_ref[...] = xb
        hs = _silu(_dot(xb, sg_ref[...])) * _dot(xb, su_ref[...])
        acc_ref[...] = _dot(hs.astype(BF16), sd_ref[...])

    xb = xb_ref[...]
    a = _dot(xb, wg_ref[0].astype(BF16))
    u = _dot(xb, wu_ref[0].astype(BF16))
    lane = lax.broadcasted_iota(I32, g_ref.shape, 1)
    gcol = jnp.sum(jnp.where(lane == e, g_ref[...], 0.0), axis=1, keepdims=True)
    hmid = (_silu(a) * u * gcol).astype(BF16)
    acc_ref[...] += _dot(hmid, wd_ref[0].astype(BF16))

    @pl.when(e == pl.num_programs(1) - 1)
    def _():
        o_ref[...] = _layer_norm(ALPHA * h_ref[...] + acc_ref[...], lg_ref[...], lb_ref[...])


def _moe(h1, gates, wg, wu, wd, sg, su, sd, lg, lb, tm=1024):
    t, d = h1.shape
    ne = wg.shape[0]
    full = lambda a: pl.BlockSpec(a.shape, lambda i, e: (0, 0))
    return pl.pallas_call(
        _moe_kernel,
        grid=(t // tm, ne),
        in_specs=[pl.BlockSpec((tm, d), lambda i, e: (i, 0)),
                  pl.BlockSpec((tm, LANES), lambda i, e: (i, 0)),
                  pl.BlockSpec((1, d, D_EXPERT), lambda i, e: (e, 0, 0)),
                  pl.BlockSpec((1, d, D_EXPERT), lambda i, e: (e, 0, 0)),
                  pl.BlockSpec((1, D_EXPERT, d), lambda i, e: (e, 0, 0)),
                  full(sg), full(su), full(sd), full(lg), full(lb)],
        out_specs=pl.BlockSpec((tm, d), lambda i, e: (i, 0)),
        out_shape=jax.ShapeDtypeStruct((t, d), F32),
        scratch_shapes=[pltpu.VMEM((tm, d), BF16), pltpu.VMEM((tm, d), F32)],
        compiler_params=_params(("parallel", "arbitrary")),
        name="moe",
    )(h1, gates, wg, wu, wd, sg, su, sd, lg, lb)


def kernel(x, w_in, hg_lb_logits, hg_norm_g, q_norm_g, w_q_up, w_qidx_up, kv_norm_g, w_kv_up,
           idx_ln_g, idx_ln_b, w_out, ln1_g, ln1_b, w_router, router_bias, w_e_gate, w_e_up,
           w_e_down, w_s_gate, w_s_up, w_s_down, ln2_g, ln2_b):
    b, s, d = x.shape
    h = x
    for l in range(DEPTH):
        h2 = h.reshape(b * s, d)
        n_main = 4 * HG_WIDTH
        w_main = w_in[l, :, :n_main].astype(BF16)
        w_tail = jnp.pad(w_in[l, :, n_main:], ((0, 0), (0, TAIL_W - (w_in.shape[2] - n_main))))

        proj = _proj_main(h2, w_main).reshape(b, s, n_main)
        o_a = _hgrn(proj, hg_lb_logits, hg_norm_g[l], l)

        wqT = w_q_up[l].T.astype(BF16)
        wqiT = w_qidx_up[l].T
        wk = w_kv_up[l][:, :AT_WIDTH].astype(BF16)
        wvT = w_kv_up[l][:, AT_WIDTH:].T.astype(BF16)
        qT, qiT, k, vT, kidx, wT = _dsa_prep(h, w_tail, q_norm_g[l], kv_norm_g[l], idx_ln_g[l],
                                             idx_ln_b[l], wqT, wqiT, wk, wvT)
        o_b = _dsa_attn(qT, qiT, wT, kidx, k, vT)

        h1 = _out_ln(o_a.reshape(b * s, HG_WIDTH), o_b.reshape(b * s, AT_WIDTH), h2,
                     w_out[l, :HG_WIDTH].astype(BF16), w_out[l, HG_WIDTH:].astype(BF16),
                     ln1_g[l].reshape(1, d), ln1_b[l].reshape(1, d))
        gates = _router(h1, w_router[l].T, router_bias[l].reshape(N_EXPERTS, 1))
        out = _moe(h1, gates, w_e_gate[l], w_e_up[l], w_e_down[l], w_s_gate[l].astype(BF16),
                   w_s_up[l].astype(BF16), w_s_down[l].astype(BF16),
                   ln2_g[l].reshape(1, d), ln2_b[l].reshape(1, d))
        h = out.reshape(b, s, d)
    return h
```

```python
import functools

import jax
import jax.numpy as jnp
from jax import lax
from jax.experimental import pallas as pl
from jax.experimental.pallas import tpu as pltpu

F32 = jnp.float32
BF16 = jnp.bfloat16
I32 = jnp.int32
HIGHEST = lax.Precision.HIGHEST

CHUNK = 64
HG_HEADS = 4
HG_DK = 128
HG_DV = 128
HG_WIDTH = HG_HEADS * HG_DV
AT_HEADS = 8
AT_DH = 64
AT_WIDTH = AT_HEADS * AT_DH
Q_RANK = 256
KV_RANK = 128
IDX_HEADS = 4
IDX_DH = 64
TOPK_MAX = 256
N_EXPERTS = 64
TOP_K = 8
N_GROUPS = 8
TOPK_GROUPS = 4
D_EXPERT = 256
ROUTED_SCALE = 2.5
DEPTH = 1
ALPHA = (2.0 * DEPTH) ** 0.25
LN_EPS = 1e-5
RMS_EPS = 1e-6

LANES = 128
SUBLANES = 8
QBLK = 2 * CHUNK
KBLK = 256
VMEM_LIMIT = 56 * 1024 * 1024
INT_MIN = -2 ** 31


def _params(sem, vmem=VMEM_LIMIT):
    return pltpu.CompilerParams(dimension_semantics=sem, vmem_limit_bytes=vmem)


def _dot(a, b, precision=None):
    return jnp.dot(a, b, preferred_element_type=F32, precision=precision)


def _dot_nt(a, b, precision=None):
    return lax.dot_general(a, b, (((1,), (1,)), ((), ())), preferred_element_type=F32,
                           precision=precision)


def _dot_tn(a, b, precision=None):
    return lax.dot_general(a, b, (((0,), (0,)), ((), ())), preferred_element_type=F32,
                           precision=precision)


def _split(x):
    hi = x.astype(BF16)
    return hi, (x - hi.astype(F32)).astype(BF16)


def _cat_hhl(x, axis):
    hi, lo = _split(x)
    return jnp.concatenate([hi, hi, lo], axis=axis)


def _cat_hlh(x, axis):
    hi = lax.reduce_precision(x, exponent_bits=8, mantissa_bits=7)
    lo = x - hi
    return jnp.concatenate([hi.astype(BF16), lo.astype(BF16), hi.astype(BF16)], axis=axis)


def _sigmoid(x):
    return 1.0 / (1.0 + jnp.exp(-x))


def _silu(x):
    return x * _sigmoid(x)


def _proj_main_kernel(x_ref, w_ref, o_ref):
    o_ref[...] = _dot(x_ref[...].astype(BF16), w_ref[...])


def _proj_main(x2, w_bf, tm=256):
    t, d = x2.shape
    n = w_bf.shape[1]
    return pl.pallas_call(
        _proj_main_kernel,
        grid=(t // tm,),
        in_specs=[pl.BlockSpec((tm, d), lambda i: (i, 0)),
                  pl.BlockSpec((d, n), lambda i: (0, 0))],
        out_specs=pl.BlockSpec((tm, n), lambda i: (i, 0)),
        out_shape=jax.ShapeDtypeStruct((t, n), F32),
        compiler_params=_params(("parallel",)),
        name="proj_main",
    )(x2, w_bf)


def _hgrn_kernel(p_ref, lbl_ref, ng_ref, o_ref, st_ref, *, n_chunks, layer):
    @pl.when(pl.program_id(1) == 0)
    def _():
        st_ref[...] = jnp.zeros_like(st_ref)

    lg = lbl_ref[...]
    ex = jnp.exp(lg - jnp.max(lg, axis=0, keepdims=True))
    lb_all = jnp.sum(ex[: layer + 1], axis=0, keepdims=True) / jnp.sum(ex, axis=0, keepdims=True)
    ng = ng_ref[...]

    r = lax.broadcasted_iota(I32, (CHUNK, CHUNK), 0)
    c = lax.broadcasted_iota(I32, (CHUNK, CHUNK), 1)
    causal = c <= r
    tri = jnp.where(causal, 1.0, 0.0).astype(BF16)
    tri2 = jnp.concatenate([tri, tri], axis=1)

    def chunk_body(j, carry):
        rows = pl.ds(pl.multiple_of(j * CHUNK, CHUNK), CHUNK)
        for h in range(HG_HEADS):
            lo = h * HG_DK
            q = _silu(p_ref[0, rows, lo:lo + HG_DK])
            fl = p_ref[0, rows, HG_WIDTH + lo:HG_WIDTH + lo + HG_DK]
            v = p_ref[0, rows, 2 * HG_WIDTH + lo:2 * HG_WIDTH + lo + HG_DV]
            gate = p_ref[0, rows, 3 * HG_WIDTH + lo:3 * HG_WIDTH + lo + HG_DV]
            lb = lb_all[:, lo:lo + HG_DK]
            f = lb + (1.0 - lb) * _sigmoid(fl)
            k = 1.0 - f
            lf_hi, lf_lo = _split(jnp.log(f))
            b = _dot(tri2, jnp.concatenate([lf_hi, lf_lo], axis=0))
            b_last = b[CHUNK - 1:CHUNK, :]
            q_dec = (q * jnp.exp(b)).astype(BF16)
            k_inv = (k * jnp.exp(-b)).astype(BF16)
            k_dec = (k * jnp.exp(b_last - b)).astype(BF16)
            vb = v.astype(BF16)
            scores = jnp.where(causal, _dot_nt(q_dec, k_inv), 0.0)
            o = _dot(scores.astype(BF16), vb)
            st = st_ref[h]
            o = o + _dot_nt(q_dec, st.astype(BF16))
            st_ref[h] = st * jnp.exp(b_last) + _dot_tn(vb, k_dec)
            o = o * lax.rsqrt(jnp.mean(o * o, axis=-1, keepdims=True) + RMS_EPS) * ng
            o_ref[0, rows, lo:lo + HG_DV] = (o * _silu(gate)).astype(o_ref.dtype)
        return carry

    lax.fori_loop(0, n_chunks, chunk_body, 0)


def _hgrn(proj, lb_logits, norm_g, layer, ct=512):
    b, s, w = proj.shape
    n_chunks = ct // CHUNK
    return pl.pallas_call(
        functools.partial(_hgrn_kernel, n_chunks=n_chunks, layer=layer),
        grid=(b, s // ct),
        in_specs=[pl.BlockSpec((1, ct, w), lambda i, j: (i, j, 0)),
                  pl.BlockSpec(lb_logits.shape, lambda i, j: (0, 0)),
                  pl.BlockSpec((1, HG_DV), lambda i, j: (0, 0))],
        out_specs=pl.BlockSpec((1, ct, HG_WIDTH), lambda i, j: (i, j, 0)),
        out_shape=jax.ShapeDtypeStruct((b, s, HG_WIDTH), BF16),
        scratch_shapes=[pltpu.VMEM((HG_HEADS, HG_DV, HG_DK), F32)],
        compiler_params=_params(("parallel", "arbitrary")),
        name="hgrn",
    )(proj, lb_logits, norm_g.reshape(1, HG_DV))


TAIL_W = 512
_CKV0 = Q_RANK
_KIDX0 = Q_RANK + KV_RANK
KIDX_W = 4 * IDX_DH


def _dsa_prep_kernel(x_ref, wt_ref, qg_ref, kvg_ref, lng_ref, lnb_ref, wqT_ref, wqiT_ref,
                     wk_ref, wvT_ref, qT_ref, qiT_ref, k_ref, vT_ref, kidx_ref, wT_ref):
    tail = _dot(_cat_hhl(x_ref[...], 1), wt_ref[...])
    cq = tail[:, :Q_RANK]
    cqn = cq * lax.rsqrt(jnp.mean(cq * cq, axis=-1, keepdims=True) + RMS_EPS) * qg_ref[...]
    qT_ref[0] = (_dot_nt(wqT_ref[...], cqn.astype(BF16)) * (AT_DH ** -0.5)).astype(qT_ref.dtype)
    qiT_ref[0] = _dot_nt(wqiT_ref[...], _cat_hhl(cqn, 1))
    ckv = tail[:, _CKV0:_CKV0 + KV_RANK]
    ckvn = (ckv * lax.rsqrt(jnp.mean(ckv * ckv, axis=-1, keepdims=True) + RMS_EPS)
            * kvg_ref[...]).astype(BF16)
    k_ref[0] = _dot(ckvn, wk_ref[...]).astype(k_ref.dtype)
    vT_ref[0, 0] = _dot_nt(wvT_ref[...], ckvn).astype(vT_ref.dtype)
    k128 = tail[:, _KIDX0:_KIDX0 + LANES]
    in_k = lax.broadcasted_iota(I32, (1, LANES), 1) < IDX_DH
    mu = jnp.sum(jnp.where(in_k, k128, 0.0), axis=-1, keepdims=True) * (1.0 / IDX_DH)
    dev = jnp.where(in_k, k128 - mu, 0.0)
    var = jnp.sum(dev * dev, axis=-1, keepdims=True) * (1.0 / IDX_DH)
    y = dev * lax.rsqrt(var + LN_EPS) * lng_ref[...] + lnb_ref[...]
    hi, lo = _split(y)
    hif = hi.astype(F32)
    hi2 = (hif + pltpu.roll(hif, IDX_DH, 1)).astype(BF16)
    kidx_ref[0] = jnp.concatenate([hi2, lo], axis=1)
    tT = k128.T
    wT_ref[0] = tT[IDX_DH:IDX_DH + SUBLANES, :] * (IDX_HEADS ** -0.5 * IDX_DH ** -0.5)


def _dsa_prep(x, w_tail, q_norm_g, kv_norm_g, idx_ln_g, idx_ln_b, wqT, wqiT, wk, wvT):
    b, s, d = x.shape
    tm = KBLK
    nb = s // tm
    full = lambda a: pl.BlockSpec(a.shape, lambda i, j: (0,) * a.ndim)
    pad_lanes = lambda v: jnp.pad(v.reshape(1, -1), ((0, 0), (0, LANES - v.shape[0])))
    args = (_cat_hlh(w_tail, 0), q_norm_g.reshape(1, -1), kv_norm_g.reshape(1, -1),
            pad_lanes(idx_ln_g), pad_lanes(idx_ln_b), wqT, _cat_hlh(wqiT, 1), wk, wvT)
    return pl.pallas_call(
        _dsa_prep_kernel,
        grid=(b, nb),
        in_specs=[pl.BlockSpec((None, tm, d), lambda i, j: (i, j, 0))] + [full(a) for a in args],
        out_specs=[pl.BlockSpec((1, AT_WIDTH, tm), lambda i, j: (i, 0, j)),
                   pl.BlockSpec((1, IDX_HEADS * IDX_DH, tm), lambda i, j: (i, 0, j)),
                   pl.BlockSpec((1, tm, AT_WIDTH), lambda i, j: (i, j, 0)),
                   pl.BlockSpec((1, 1, AT_WIDTH, tm), lambda i, j: (i, j, 0, 0)),
                   pl.BlockSpec((1, tm, KIDX_W), lambda i, j: (i, j, 0)),
                   pl.BlockSpec((1, SUBLANES, tm), lambda i, j: (i, 0, j))],
        out_shape=[jax.ShapeDtypeStruct((b, AT_WIDTH, s), BF16),
                   jax.ShapeDtypeStruct((b, IDX_HEADS * IDX_DH, s), F32),
                   jax.ShapeDtypeStruct((b, s, AT_WIDTH), BF16),
                   jax.ShapeDtypeStruct((b, nb, AT_WIDTH, tm), BF16),
                   jax.ShapeDtypeStruct((b, s, KIDX_W), BF16),
                   jax.ShapeDtypeStruct((b, SUBLANES, s), F32)],
        compiler_params=_params(("parallel", "parallel")),
        name="dsa_prep",
    )(x, *args)


def _sortable(x):
    bits = lax.bitcast_convert_type(x, I32)
    return jnp.where(bits < 0, bits ^ jnp.int32(0x7FFFFFFF), bits)


def _dsa_attn_kernel(qT_ref, qiT_ref, wT_ref, kidx_ref, k_ref, vT_ref, o_ref,
                     keys_ref, q2_ref, qbd_ref, acc_ref, m_ref, l_ref, *, k_top):
    qb = pl.program_id(1)
    t0 = qb * QBLK
    e_max = t0 + QBLK
    nkb = (e_max + KBLK - 1) >> 8

    lane = lax.broadcasted_iota(I32, (1, QBLK), 1)
    t_idx = t0 + lane
    end_t = ((t_idx >> 6) + 1) * CHUNK
    srow = lax.broadcasted_iota(I32, (KBLK, QBLK), 0)

    for h in range(IDX_HEADS):
        hi, lo = _split(qiT_ref[0, h * IDX_DH:(h + 1) * IDX_DH, :])
        cols = slice(h * QBLK, (h + 1) * QBLK)
        q2_ref[0 * IDX_DH:1 * IDX_DH, cols] = hi
        q2_ref[1 * IDX_DH:2 * IDX_DH, cols] = lo
        q2_ref[2 * IDX_DH:3 * IDX_DH, cols] = hi
        q2_ref[3 * IDX_DH:4 * IDX_DH, cols] = jnp.zeros_like(hi)
    wrow = jnp.concatenate([wT_ref[0, h:h + 1, :] for h in range(IDX_HEADS)], axis=1)

    def score_body(kb, carry):
        rows = pl.ds(pl.multiple_of(kb * KBLK, KBLK), KBLK)
        logits = _dot(kidx_ref[0, rows, :], q2_ref[...])
        r = jnp.maximum(logits, 0.0) * wrow
        sc = r[:, :QBLK]
        for h in range(1, IDX_HEADS):
            sc = sc + r[:, h * QBLK:(h + 1) * QBLK]
        valid = (srow + kb * KBLK) < end_t
        keys_ref[rows, :] = _sortable(jnp.where(valid, sc, -jnp.inf))
        return carry

    lax.fori_loop(0, nkb, score_body, 0)

    def count_ge(cand):
        def body(kb, acc):
            rows = pl.ds(pl.multiple_of(kb * KBLK, KBLK), KBLK)
            hit = jnp.where(keys_ref[rows, :] >= cand, 1, 0).astype(I32)
            return acc + jnp.sum(hit.reshape(KBLK // SUBLANES, SUBLANES, QBLK), axis=0)
        acc = lax.fori_loop(0, nkb, body, jnp.zeros((SUBLANES, QBLK), I32))
        return jnp.sum(acc, axis=0, keepdims=True)

    zero = jnp.zeros((1, QBLK), I32)
    thr = jnp.where(count_ge(zero) >= k_top, zero, jnp.full((1, QBLK), INT_MIN, I32))

    def bit_body(i, thr):
        cand = thr | (jnp.int32(1) << (30 - i))
        return jnp.where(count_ge(cand) >= k_top, cand, thr)

    thr = lax.fori_loop(0, 31, bit_body, thr)
    need = (k_top - (count_ge(thr + 1))).astype(F32)

    for j in range(AT_HEADS // 2):
        pair = qT_ref[0, j * LANES:(j + 1) * LANES, :]
        rr = lax.broadcasted_iota(I32, pair.shape, 0)
        zeros = jnp.zeros_like(pair)
        qbd_ref[j, :, :QBLK] = jnp.where(rr < AT_DH, pair, zeros)
        qbd_ref[j, :, QBLK:] = jnp.where(rr >= AT_DH, pair, zeros)
    acc_ref[...] = jnp.zeros_like(acc_ref)
    m_ref[...] = jnp.full_like(m_ref, -jnp.inf)
    l_ref[...] = jnp.zeros_like(l_ref)

    ri = lax.broadcasted_iota(I32, (KBLK, KBLK), 0)
    ci = lax.broadcasted_iota(I32, (KBLK, KBLK), 1)
    lstrict = jnp.where(ci < ri, 1.0, 0.0).astype(BF16)

    def attn_body(kb, carry):
        rows = pl.ds(pl.multiple_of(kb * KBLK, KBLK), KBLK)
        kblk = keys_ref[rows, :]
        s_idx = srow + kb * KBLK
        eq = kblk == thr
        eqf = jnp.where(eq, 1.0, 0.0)
        before = _dot(lstrict, eqf.astype(BF16)) + carry
        sel = ((kblk > thr) | (eq & (before < need))) & (s_idx < end_t)
        bias = jnp.where(sel, 0.0, -jnp.inf)
        dist = jnp.abs(t_idx - s_idx).astype(F32)
        s2s = [_dot(k_ref[0, rows, j * LANES:(j + 1) * LANES], qbd_ref[j])
               for j in range(AT_HEADS // 2)]
        for j in range(AT_HEADS // 2):
            ps, alphas = [], []
            for half in range(2):
                h = 2 * j + half
                slope = 2.0 ** (-8.0 * (h + 1) / AT_HEADS)
                st = s2s[j][:, half * QBLK:(half + 1) * QBLK] - slope * dist + bias
                m_old = m_ref[h:h + 1, :]
                m_new = jnp.maximum(m_old, jnp.max(st, axis=0, keepdims=True))
                m_safe = jnp.where(m_new == -jnp.inf, 0.0, m_new)
                alpha = jnp.exp(m_old - m_safe)
                p = jnp.exp(st - m_safe)
                l_ref[h:h + 1, :] = alpha * l_ref[h:h + 1, :] + jnp.sum(p, axis=0, keepdims=True)
                m_ref[h:h + 1, :] = m_new
                ps.append(p.astype(BF16))
                alphas.append(alpha)
            o2 = _dot(vT_ref[0, kb, j * LANES:(j + 1) * LANES, :], jnp.concatenate(ps, axis=1))
            for half in range(2):
                hs = slice((2 * j + half) * AT_DH, (2 * j + half + 1) * AT_DH)
                acc_ref[hs, :] = (alphas[half] * acc_ref[hs, :]
                                  + o2[half * AT_DH:(half + 1) * AT_DH, half * QBLK:(half + 1) * QBLK])
        return carry + jnp.sum(eqf, axis=0, keepdims=True)

    lax.fori_loop(0, nkb, attn_body, jnp.zeros((1, QBLK), F32))

    for h in range(AT_HEADS):
        hs = slice(h * AT_DH, (h + 1) * AT_DH)
        acc_ref[hs, :] = acc_ref[hs, :] / l_ref[h:h + 1, :]
    o_ref[0] = acc_ref[...].T.astype(o_ref.dtype)


def _dsa_attn(qT, qiT, wT, kidx, k, vT):
    b, _, s = qT.shape
    nb = s // KBLK
    k_top = min(TOPK_MAX, s // 4)
    return pl.pallas_call(
        functools.partial(_dsa_attn_kernel, k_top=k_top),
        grid=(b, s // QBLK),
        in_specs=[pl.BlockSpec((1, AT_WIDTH, QBLK), lambda i, j: (i, 0, j)),
                  pl.BlockSpec((1, IDX_HEADS * IDX_DH, QBLK), lambda i, j: (i, 0, j)),
                  pl.BlockSpec((1, SUBLANES, QBLK), lambda i, j: (i, 0, j)),
                  pl.BlockSpec((1, s, KIDX_W), lambda i, j: (i, 0, 0)),
                  pl.BlockSpec((1, s, AT_WIDTH), lambda i, j: (i, 0, 0)),
                  pl.BlockSpec((1, nb, AT_WIDTH, KBLK), lambda i, j: (i, 0, 0, 0))],
        out_specs=pl.BlockSpec((1, QBLK, AT_WIDTH), lambda i, j: (i, j, 0)),
        out_shape=jax.ShapeDtypeStruct((b, s, AT_WIDTH), BF16),
        scratch_shapes=[pltpu.VMEM((s, QBLK), I32),
                        pltpu.VMEM((KIDX_W, IDX_HEADS * QBLK), BF16),
                        pltpu.VMEM((AT_HEADS // 2, LANES, 2 * QBLK), BF16),
                        pltpu.VMEM((AT_WIDTH, QBLK), F32),
                        pltpu.VMEM((AT_HEADS, QBLK), F32),
                        pltpu.VMEM((AT_HEADS, QBLK), F32)],
        compiler_params=_params(("parallel", "arbitrary")),
        name="dsa_attn",
    )(qT, qiT, wT, kidx, k, vT)


def _layer_norm(v, g, b):
    mu = jnp.mean(v, axis=-1, keepdims=True)
    var = jnp.mean(jnp.square(v - mu), axis=-1, keepdims=True)
    return (v - mu) * lax.rsqrt(var + LN_EPS) * g + b


def _out_ln_kernel(oa_ref, ob_ref, x_ref, wa_ref, wb_ref, g_ref, b_ref, h_ref):
    mix = _dot(oa_ref[...], wa_ref[...]) + _dot(ob_ref[...], wb_ref[...])
    h_ref[...] = _layer_norm(ALPHA * x_ref[...] + mix, g_ref[...], b_ref[...])


def _out_ln(oa, ob, x2, wa, wb, g, b, tm=512):
    t, d = x2.shape
    row = lambda w: pl.BlockSpec((tm, w), lambda i: (i, 0))
    full = lambda a: pl.BlockSpec(a.shape, lambda i: (0, 0))
    return pl.pallas_call(
        _out_ln_kernel,
        grid=(t // tm,),
        in_specs=[row(oa.shape[1]), row(ob.shape[1]), row(d), full(wa), full(wb), full(g), full(b)],
        out_specs=row(d),
        out_shape=jax.ShapeDtypeStruct((t, d), F32),
        compiler_params=_params(("parallel",)),
        name="out_ln",
    )(oa, ob, x2, wa, wb, g, b)


def _first_max(v, idx, big):
    m = jnp.max(v, axis=0, keepdims=True)
    first = jnp.min(jnp.where(v == m, idx, big), axis=0, keepdims=True)
    return m, first


def _router_kernel(h_ref, wrT_ref, bias_ref, g_ref):
    tm = h_ref.shape[0]
    per_group = N_EXPERTS // N_GROUPS
    logits = _dot_nt(wrT_ref[...], _cat_hhl(h_ref[...], 1))
    scores = _sigmoid(logits)
    sel = scores + bias_ref[...]
    iota_g = lax.broadcasted_iota(I32, (per_group, tm), 0)
    neg = -jnp.inf

    grp = jnp.zeros((N_GROUPS, tm), F32)
    iota_grp = lax.broadcasted_iota(I32, (N_GROUPS, tm), 0)
    for g in range(N_GROUPS):
        blk = sel[g * per_group:(g + 1) * per_group, :]
        m1, i1 = _first_max(blk, iota_g, per_group)
        m2 = jnp.max(jnp.where(iota_g == i1, neg, blk), axis=0, keepdims=True)
        grp = jnp.where(iota_grp == g, m1 + m2, grp)

    gmask = jnp.zeros((N_GROUPS, tm), F32)
    work = grp
    for _ in range(TOPK_GROUPS):
        _, gi = _first_max(work, iota_grp, N_GROUPS)
        hit = iota_grp == gi
        gmask = jnp.where(hit, 1.0, gmask)
        work = jnp.where(hit, neg, work)

    e_mask = jnp.concatenate(
        [jnp.broadcast_to(gmask[g:g + 1, :], (per_group, tm)) for g in range(N_GROUPS)], axis=0)
    iota_e = lax.broadcasted_iota(I32, (N_EXPERTS, tm), 0)
    work = jnp.where(e_mask > 0.5, sel, neg)
    w = jnp.zeros((N_EXPERTS, tm), F32)
    for _ in range(TOP_K):
        _, ei = _first_max(work, iota_e, N_EXPERTS)
        hit = iota_e == ei
        w = jnp.where(hit, scores, w)
        work = jnp.where(hit, neg, work)

    gates = w / jnp.sum(w, axis=0, keepdims=True) * ROUTED_SCALE
    gates = jnp.concatenate([gates, jnp.zeros((LANES - N_EXPERTS, tm), F32)], axis=0)
    g_ref[...] = gates.T


def _router(h1, wrT, bias_col, tm=256):
    t, d = h1.shape
    return pl.pallas_call(
        _router_kernel,
        grid=(t // tm,),
        in_specs=[pl.BlockSpec((tm, d), lambda i: (i, 0)),
                  pl.BlockSpec(wrT.shape, lambda i: (0, 0)),
                  pl.BlockSpec(bias_col.shape, lambda i: (0, 0))],
        out_specs=pl.BlockSpec((tm, LANES), lambda i: (i, 0)),
        out_shape=jax.ShapeDtypeStruct((t, LANES), F32),
        compiler_params=_params(("parallel",)),
        name="router",
    )(h1, wrT, bias_col)


def _moe_kernel(h_ref, g_ref, wg_ref, wu_ref, wd_ref, sg_ref, su_ref, sd_ref, lg_ref, lb_ref,
                o_ref, xb_ref, acc_ref):
    e = pl.program_id(1)

    @pl.when(e == 0)
    def _():
        xb = h_ref[...].astype(BF16)
        xb_ref[...] = xb
        hs = _silu(_dot(xb, sg_ref[...])) * _dot(xb, su_ref[...])
        acc_ref[...] = _dot(hs.astype(BF16), sd_ref[...])

    xb = xb_ref[...]
    a = _dot(xb, wg_ref[0].astype(BF16))
    u = _dot(xb, wu_ref[0].astype(BF16))
    lane = lax.broadcasted_iota(I32, g_ref.shape, 1)
    gcol = jnp.sum(jnp.where(lane == e, g_ref[...], 0.0), axis=1, keepdims=True)
    hmid = (_silu(a) * u * gcol).astype(BF16)
    acc_ref[...] += _dot(hmid, wd_ref[0].astype(BF16))

    @pl.when(e == pl.num_programs(1) - 1)
    def _():
        o_ref[...] = _layer_norm(ALPHA * h_ref[...] + acc_ref[...], lg_ref[...], lb_ref[...])


def _moe(h1, gates, wg, wu, wd, sg, su, sd, lg, lb, tm=1024):
    t, d = h1.shape
    ne = wg.shape[0]
    full = lambda a: pl.BlockSpec(a.shape, lambda i, e: (0, 0))
    return pl.pallas_call(
        _moe_kernel,
        grid=(t // tm, ne),
        in_specs=[pl.BlockSpec((tm, d), lambda i, e: (i, 0)),
                  pl.BlockSpec((tm, LANES), lambda i, e: (i, 0)),
                  pl.BlockSpec((1, d, D_EXPERT), lambda i, e: (e, 0, 0)),
                  pl.BlockSpec((1, d, D_EXPERT), lambda i, e: (e, 0, 0)),
                  pl.BlockSpec((1, D_EXPERT, d), lambda i, e: (e, 0, 0)),
                  full(sg), full(su), full(sd), full(lg), full(lb)],
        out_specs=pl.BlockSpec((tm, d), lambda i, e: (i, 0)),
        out_shape=jax.ShapeDtypeStruct((t, d), F32),
        scratch_shapes=[pltpu.VMEM((tm, d), BF16), pltpu.VMEM((tm, d), F32)],
        compiler_params=_params(("parallel", "arbitrary")),
        name="moe",
    )(h1, gates, wg, wu, wd, sg, su, sd, lg, lb)


def kernel(x, w_in, hg_lb_logits, hg_norm_g, q_norm_g, w_q_up, w_qidx_up, kv_norm_g, w_kv_up,
           idx_ln_g, idx_ln_b, w_out, ln1_g, ln1_b, w_router, router_bias, w_e_gate, w_e_up,
           w_e_down, w_s_gate, w_s_up, w_s_down, ln2_g, ln2_b):
    b, s, d = x.shape
    h = x
    for l in range(DEPTH):
        h2 = h.reshape(b * s, d)
        n_main = 4 * HG_WIDTH
        w_main = w_in[l, :, :n_main].astype(BF16)
        w_tail = jnp.pad(w_in[l, :, n_main:], ((0, 0), (0, TAIL_W - (w_in.shape[2] - n_main))))

        proj = _proj_main(h2, w_main).reshape(b, s, n_main)
        o_a = _hgrn(proj, hg_lb_logits, hg_norm_g[l], l)

        wqT = w_q_up[l].T.astype(BF16)
        wqiT = w_qidx_up[l].T
        wk = w_kv_up[l][:, :AT_WIDTH].astype(BF16)
        wvT = w_kv_up[l][:, AT_WIDTH:].T.astype(BF16)
        qT, qiT, k, vT, kidx, wT = _dsa_prep(h, w_tail, q_norm_g[l], kv_norm_g[l], idx_ln_g[l],
                                             idx_ln_b[l], wqT, wqiT, wk, wvT)
        o_b = _dsa_attn(qT, qiT, wT, kidx, k, vT)

        h1 = _out_ln(o_a.reshape(b * s, HG_WIDTH), o_b.reshape(b * s, AT_WIDTH), h2,
                     w_out[l, :HG_WIDTH].astype(BF16), w_out[l, HG_WIDTH:].astype(BF16),
                     ln1_g[l].reshape(1, d), ln1_b[l].reshape(1, d))
        gates = _router(h1, _cat_hlh(w_router[l].T, 1), router_bias[l].reshape(N_EXPERTS, 1))
        out = _moe(h1, gates, w_e_gate[l], w_e_up[l], w_e_down[l], w_s_gate[l].astype(BF16),
                   w_s_up[l].astype(BF16), w_s_down[l].astype(BF16),
                   ln2_g[l].reshape(1, d), ln2_b[l].reshape(1, d))
        h = out.reshape(b, s, d)
    return h
```

```python
import functools

import jax
import jax.numpy as jnp
from jax import lax
from jax.experimental import pallas as pl
from jax.experimental.pallas import tpu as pltpu

F32 = jnp.float32
BF16 = jnp.bfloat16
I32 = jnp.int32
HIGHEST = lax.Precision.HIGHEST

CHUNK = 64
HG_HEADS = 4
HG_DK = 128
HG_DV = 128
HG_WIDTH = HG_HEADS * HG_DV
AT_HEADS = 8
AT_DH = 64
AT_WIDTH = AT_HEADS * AT_DH
Q_RANK = 256
KV_RANK = 128
IDX_HEADS = 4
IDX_DH = 64
TOPK_MAX = 256
N_EXPERTS = 64
TOP_K = 8
N_GROUPS = 8
TOPK_GROUPS = 4
D_EXPERT = 256
ROUTED_SCALE = 2.5
DEPTH = 1
ALPHA = (2.0 * DEPTH) ** 0.25
LN_EPS = 1e-5
RMS_EPS = 1e-6

LANES = 128
SUBLANES = 8
QBLK = 2 * CHUNK
KBLK = 256
VMEM_LIMIT = 56 * 1024 * 1024
INT_MIN = -2 ** 31


def _params(sem, vmem=VMEM_LIMIT):
    return pltpu.CompilerParams(dimension_semantics=sem, vmem_limit_bytes=vmem)


def _dot(a, b, precision=None):
    return jnp.dot(a, b, preferred_element_type=F32, precision=precision)


def _dot_nt(a, b, precision=None):
    return lax.dot_general(a, b, (((1,), (1,)), ((), ())), preferred_element_type=F32,
                           precision=precision)


def _dot_tn(a, b, precision=None):
    return lax.dot_general(a, b, (((0,), (0,)), ((), ())), preferred_element_type=F32,
                           precision=precision)


def _split(x):
    hi = x.astype(BF16)
    return hi, (x - hi.astype(F32)).astype(BF16)


def _cat_hhl(x, axis):
    hi, lo = _split(x)
    return jnp.concatenate([hi, hi, lo], axis=axis)


def _cat_hlh(x, axis):
    hi, lo = _split(x)
    return jnp.concatenate([hi, lo, hi], axis=axis)


def _sigmoid(x):
    return 1.0 / (1.0 + jnp.exp(-x))


def _silu(x):
    return x * _sigmoid(x)


def _proj_main_kernel(x_ref, w_ref, o_ref):
    o_ref[...] = _dot(x_ref[...].astype(BF16), w_ref[...])


def _proj_main(x2, w_bf, tm=256):
    t, d = x2.shape
    n = w_bf.shape[1]
    return pl.pallas_call(
        _proj_main_kernel,
        grid=(t // tm,),
        in_specs=[pl.BlockSpec((tm, d), lambda i: (i, 0)),
                  pl.BlockSpec((d, n), lambda i: (0, 0))],
        out_specs=pl.BlockSpec((tm, n), lambda i: (i, 0)),
        out_shape=jax.ShapeDtypeStruct((t, n), F32),
        compiler_params=_params(("parallel",)),
        name="proj_main",
    )(x2, w_bf)


def _hgrn_kernel(p_ref, lbl_ref, ng_ref, o_ref, st_ref, *, n_chunks, layer):
    @pl.when(pl.program_id(1) == 0)
    def _():
        st_ref[...] = jnp.zeros_like(st_ref)

    lg = lbl_ref[...]
    ex = jnp.exp(lg - jnp.max(lg, axis=0, keepdims=True))
    lb_all = jnp.sum(ex[: layer + 1], axis=0, keepdims=True) / jnp.sum(ex, axis=0, keepdims=True)
    ng = ng_ref[...]

    r = lax.broadcasted_iota(I32, (CHUNK, CHUNK), 0)
    c = lax.broadcasted_iota(I32, (CHUNK, CHUNK), 1)
    causal = c <= r
    tri = jnp.where(causal, 1.0, 0.0).astype(BF16)
    tri2 = jnp.concatenate([tri, tri], axis=1)

    def chunk_body(j, carry):
        rows = pl.ds(pl.multiple_of(j * CHUNK, CHUNK), CHUNK)
        for h in range(HG_HEADS):
            lo = h * HG_DK
            q = _silu(p_ref[0, rows, lo:lo + HG_DK])
            fl = p_ref[0, rows, HG_WIDTH + lo:HG_WIDTH + lo + HG_DK]
            v = p_ref[0, rows, 2 * HG_WIDTH + lo:2 * HG_WIDTH + lo + HG_DV]
            gate = p_ref[0, rows, 3 * HG_WIDTH + lo:3 * HG_WIDTH + lo + HG_DV]
            lb = lb_all[:, lo:lo + HG_DK]
            f = lb + (1.0 - lb) * _sigmoid(fl)
            k = 1.0 - f
            lf_hi, lf_lo = _split(jnp.log(f))
            b = _dot(tri2, jnp.concatenate([lf_hi, lf_lo], axis=0))
            b_last = b[CHUNK - 1:CHUNK, :]
            q_dec = (q * jnp.exp(b)).astype(BF16)
            k_inv = (k * jnp.exp(-b)).astype(BF16)
            k_dec = (k * jnp.exp(b_last - b)).astype(BF16)
            vb = v.astype(BF16)
            scores = jnp.where(causal, _dot_nt(q_dec, k_inv), 0.0)
            o = _dot(scores.astype(BF16), vb)
            st = st_ref[h]
            o = o + _dot_nt(q_dec, st.astype(BF16))
            st_ref[h] = st * jnp.exp(b_last) + _dot_tn(vb, k_dec)
            o = o * lax.rsqrt(jnp.mean(o * o, axis=-1, keepdims=True) + RMS_EPS) * ng
            o_ref[0, rows, lo:lo + HG_DV] = (o * _silu(gate)).astype(o_ref.dtype)
        return carry

    lax.fori_loop(0, n_chunks, chunk_body, 0)


def _hgrn(proj, lb_logits, norm_g, layer, ct=512):
    b, s, w = proj.shape
    n_chunks = ct // CHUNK
    return pl.pallas_call(
        functools.partial(_hgrn_kernel, n_chunks=n_chunks, layer=layer),
        grid=(b, s // ct),
        in_specs=[pl.BlockSpec((1, ct, w), lambda i, j: (i, j, 0)),
                  pl.BlockSpec(lb_logits.shape, lambda i, j: (0, 0)),
                  pl.BlockSpec((1, HG_DV), lambda i, j: (0, 0))],
        out_specs=pl.BlockSpec((1, ct, HG_WIDTH), lambda i, j: (i, j, 0)),
        out_shape=jax.ShapeDtypeStruct((b, s, HG_WIDTH), BF16),
        scratch_shapes=[pltpu.VMEM((HG_HEADS, HG_DV, HG_DK), F32)],
        compiler_params=_params(("parallel", "arbitrary")),
        name="hgrn",
    )(proj, lb_logits, norm_g.reshape(1, HG_DV))


TAIL_W = 512
_CKV0 = Q_RANK
_KIDX0 = Q_RANK + KV_RANK
KIDX_W = 4 * IDX_DH


def _dsa_prep_kernel(x_ref, wt_ref, qg_ref, kvg_ref, lng_ref, lnb_ref, wqT_ref, wqiT_ref,
                     wk_ref, wvT_ref, qT_ref, qiT_ref, k_ref, vT_ref, kidx_ref, wT_ref, w3_ref):
    @pl.when((pl.program_id(0) == 0) & (pl.program_id(1) == 0))
    def _():
        w3_ref[...] = _cat_hlh(wt_ref[...], 0)

    tail = _dot(_cat_hhl(x_ref[...], 1), w3_ref[...])
    cq = tail[:, :Q_RANK]
    cqn = cq * lax.rsqrt(jnp.mean(cq * cq, axis=-1, keepdims=True) + RMS_EPS) * qg_ref[...]
    qT_ref[0] = (_dot_nt(wqT_ref[...], cqn.astype(BF16)) * (AT_DH ** -0.5)).astype(qT_ref.dtype)
    qiT_ref[0] = _dot_nt(_cat_hlh(wqiT_ref[...], 1), _cat_hhl(cqn, 1))
    ckv = tail[:, _CKV0:_CKV0 + KV_RANK]
    ckvn = (ckv * lax.rsqrt(jnp.mean(ckv * ckv, axis=-1, keepdims=True) + RMS_EPS)
            * kvg_ref[...]).astype(BF16)
    k_ref[0] = _dot(ckvn, wk_ref[...]).astype(k_ref.dtype)
    vT_ref[0, 0] = _dot_nt(wvT_ref[...], ckvn).astype(vT_ref.dtype)
    k128 = tail[:, _KIDX0:_KIDX0 + LANES]
    in_k = lax.broadcasted_iota(I32, (1, LANES), 1) < IDX_DH
    mu = jnp.sum(jnp.where(in_k, k128, 0.0), axis=-1, keepdims=True) * (1.0 / IDX_DH)
    dev = jnp.where(in_k, k128 - mu, 0.0)
    var = jnp.sum(dev * dev, axis=-1, keepdims=True) * (1.0 / IDX_DH)
    y = dev * lax.rsqrt(var + LN_EPS) * lng_ref[...] + lnb_ref[...]
    hi, lo = _split(y)
    hif = hi.astype(F32)
    hi2 = (hif + pltpu.roll(hif, IDX_DH, 1)).astype(BF16)
    kidx_ref[0] = jnp.concatenate([hi2, lo], axis=1)
    tT = k128.T
    wT_ref[0] = tT[IDX_DH:IDX_DH + SUBLANES, :] * (IDX_HEADS ** -0.5 * IDX_DH ** -0.5)


def _dsa_prep(x, w_tail, q_norm_g, kv_norm_g, idx_ln_g, idx_ln_b, wqT, wqiT, wk, wvT):
    b, s, d = x.shape
    tm = KBLK
    nb = s // tm
    full = lambda a: pl.BlockSpec(a.shape, lambda i, j: (0,) * a.ndim)
    pad_lanes = lambda v: jnp.pad(v.reshape(1, -1), ((0, 0), (0, LANES - v.shape[0])))
    args = (w_tail, q_norm_g.reshape(1, -1), kv_norm_g.reshape(1, -1),
            pad_lanes(idx_ln_g), pad_lanes(idx_ln_b), wqT, wqiT, wk, wvT)
    return pl.pallas_call(
        _dsa_prep_kernel,
        grid=(b, nb),
        in_specs=[pl.BlockSpec((None, tm, d), lambda i, j: (i, j, 0))] + [full(a) for a in args],
        out_specs=[pl.BlockSpec((1, AT_WIDTH, tm), lambda i, j: (i, 0, j)),
                   pl.BlockSpec((1, IDX_HEADS * IDX_DH, tm), lambda i, j: (i, 0, j)),
                   pl.BlockSpec((1, tm, AT_WIDTH), lambda i, j: (i, j, 0)),
                   pl.BlockSpec((1, 1, AT_WIDTH, tm), lambda i, j: (i, j, 0, 0)),
                   pl.BlockSpec((1, tm, KIDX_W), lambda i, j: (i, j, 0)),
                   pl.BlockSpec((1, SUBLANES, tm), lambda i, j: (i, 0, j))],
        out_shape=[jax.ShapeDtypeStruct((b, AT_WIDTH, s), BF16),
                   jax.ShapeDtypeStruct((b, IDX_HEADS * IDX_DH, s), F32),
                   jax.ShapeDtypeStruct((b, s, AT_WIDTH), BF16),
                   jax.ShapeDtypeStruct((b, nb, AT_WIDTH, tm), BF16),
                   jax.ShapeDtypeStruct((b, s, KIDX_W), BF16),
                   jax.ShapeDtypeStruct((b, SUBLANES, s), F32)],
        scratch_shapes=[pltpu.VMEM((3 * d, TAIL_W), BF16)],
        compiler_params=_params(("arbitrary", "arbitrary")),
        name="dsa_prep",
    )(x, *args)


def _sortable(x):
    bits = lax.bitcast_convert_type(x, I32)
    return jnp.where(bits < 0, bits ^ jnp.int32(0x7FFFFFFF), bits)


def _dsa_attn_kernel(qT_ref, qiT_ref, wT_ref, kidx_ref, k_ref, vT_ref, o_ref,
                     keys_ref, q2_ref, qbd_ref, acc_ref, m_ref, l_ref, *, k_top):
    qb = pl.program_id(1)
    t0 = qb * QBLK
    e_max = t0 + QBLK
    nkb = (e_max + KBLK - 1) >> 8

    lane = lax.broadcasted_iota(I32, (1, QBLK), 1)
    t_idx = t0 + lane
    end_t = ((t_idx >> 6) + 1) * CHUNK
    srow = lax.broadcasted_iota(I32, (KBLK, QBLK), 0)

    for h in range(IDX_HEADS):
        hi, lo = _split(qiT_ref[0, h * IDX_DH:(h + 1) * IDX_DH, :])
        cols = slice(h * QBLK, (h + 1) * QBLK)
        q2_ref[0 * IDX_DH:1 * IDX_DH, cols] = hi
        q2_ref[1 * IDX_DH:2 * IDX_DH, cols] = lo
        q2_ref[2 * IDX_DH:3 * IDX_DH, cols] = hi
        q2_ref[3 * IDX_DH:4 * IDX_DH, cols] = jnp.zeros_like(hi)
    wrow = jnp.concatenate([wT_ref[0, h:h + 1, :] for h in range(IDX_HEADS)], axis=1)

    def score_body(kb, carry):
        rows = pl.ds(pl.multiple_of(kb * KBLK, KBLK), KBLK)
        logits = _dot(kidx_ref[0, rows, :], q2_ref[...])
        r = jnp.maximum(logits, 0.0) * wrow
        sc = r[:, :QBLK]
        for h in range(1, IDX_HEADS):
            sc = sc + r[:, h * QBLK:(h + 1) * QBLK]
        valid = (srow + kb * KBLK) < end_t
        keys_ref[rows, :] = _sortable(jnp.where(valid, sc, -jnp.inf))
        return carry

    lax.fori_loop(0, nkb, score_body, 0)

    def count_ge(cand):
        def body(kb, acc):
            rows = pl.ds(pl.multiple_of(kb * KBLK, KBLK), KBLK)
            hit = jnp.where(keys_ref[rows, :] >= cand, 1, 0).astype(I32)
            return acc + jnp.sum(hit.reshape(KBLK // SUBLANES, SUBLANES, QBLK), axis=0)
        acc = lax.fori_loop(0, nkb, body, jnp.zeros((SUBLANES, QBLK), I32))
        return jnp.sum(acc, axis=0, keepdims=True)

    zero = jnp.zeros((1, QBLK), I32)
    thr = jnp.where(count_ge(zero) >= k_top, zero, jnp.full((1, QBLK), INT_MIN, I32))

    def bit_body(i, thr):
        cand = thr | (jnp.int32(1) << (30 - i))
        return jnp.where(count_ge(cand) >= k_top, cand, thr)

    thr = lax.fori_loop(0, 31, bit_body, thr)
    need = (k_top - (count_ge(thr + 1))).astype(F32)

    for j in range(AT_HEADS // 2):
        pair = qT_ref[0, j * LANES:(j + 1) * LANES, :]
        rr = lax.broadcasted_iota(I32, pair.shape, 0)
        zeros = jnp.zeros_like(pair)
        qbd_ref[j, :, :QBLK] = jnp.where(rr < AT_DH, pair, zeros)
        qbd_ref[j, :, QBLK:] = jnp.where(rr >= AT_DH, pair, zeros)
    acc_ref[...] = jnp.zeros_like(acc_ref)
    m_ref[...] = jnp.full_like(m_ref, -jnp.inf)
    l_ref[...] = jnp.zeros_like(l_ref)

    ri = lax.broadcasted_iota(I32, (KBLK, KBLK), 0)
    ci = lax.broadcasted_iota(I32, (KBLK, KBLK), 1)
    lstrict = jnp.where(ci < ri, 1.0, 0.0).astype(BF16)

    def attn_body(kb, carry):
        rows = pl.ds(pl.multiple_of(kb * KBLK, KBLK), KBLK)
        kblk = keys_ref[rows, :]
        s_idx = srow + kb * KBLK
        eq = kblk == thr
        eqf = jnp.where(eq, 1.0, 0.0)
        before = _dot(lstrict, eqf.astype(BF16)) + carry
        sel = ((kblk > thr) | (eq & (before < need))) & (s_idx < end_t)
        bias = jnp.where(sel, 0.0, -jnp.inf)
        dist = jnp.abs(t_idx - s_idx).astype(F32)
        s2s = [_dot(k_ref[0, rows, j * LANES:(j + 1) * LANES], qbd_ref[j])
               for j in range(AT_HEADS // 2)]
        for j in range(AT_HEADS // 2):
            ps, alphas = [], []
            for half in range(2):
                h = 2 * j + half
                slope = 2.0 ** (-8.0 * (h + 1) / AT_HEADS)
                st = s2s[j][:, half * QBLK:(half + 1) * QBLK] - slope * dist + bias
                m_old = m_ref[h:h + 1, :]
                m_new = jnp.maximum(m_old, jnp.max(st, axis=0, keepdims=True))
                m_safe = jnp.where(m_new == -jnp.inf, 0.0, m_new)
                alpha = jnp.exp(m_old - m_safe)
                p = jnp.exp(st - m_safe)
                l_ref[h:h + 1, :] = alpha * l_ref[h:h + 1, :] + jnp.sum(p, axis=0, keepdims=True)
                m_ref[h:h + 1, :] = m_new
                ps.append(p.astype(BF16))
                alphas.append(alpha)
            o2 = _dot(vT_ref[0, kb, j * LANES:(j + 1) * LANES, :], jnp.concatenate(ps, axis=1))
            for half in range(2):
                hs = slice((2 * j + half) * AT_DH, (2 * j + half + 1) * AT_DH)
                acc_ref[hs, :] = (alphas[half] * acc_ref[hs, :]
                                  + o2[half * AT_DH:(half + 1) * AT_DH, half * QBLK:(half + 1) * QBLK])
        return carry + jnp.sum(eqf, axis=0, keepdims=True)

    lax.fori_loop(0, nkb, attn_body, jnp.zeros((1, QBLK), F32))

    for h in range(AT_HEADS):
        hs = slice(h * AT_DH, (h + 1) * AT_DH)
        acc_ref[hs, :] = acc_ref[hs, :] / l_ref[h:h + 1, :]
    o_ref[0] = acc_ref[...].T.astype(o_ref.dtype)


def _dsa_attn(qT, qiT, wT, kidx, k, vT):
    b, _, s = qT.shape
    nb = s // KBLK
    k_top = min(TOPK_MAX, s // 4)
    return pl.pallas_call(
        functools.partial(_dsa_attn_kernel, k_top=k_top),
        grid=(b, s // QBLK),
        in_specs=[pl.BlockSpec((1, AT_WIDTH, QBLK), lambda i, j: (i, 0, j)),
                  pl.BlockSpec((1, IDX_HEADS * IDX_DH, QBLK), lambda i, j: (i, 0, j)),
                  pl.BlockSpec((1, SUBLANES, QBLK), lambda i, j: (i, 0, j)),
                  pl.BlockSpec((1, s, KIDX_W), lambda i, j: (i, 0, 0)),
                  pl.BlockSpec((1, s, AT_WIDTH), lambda i, j: (i, 0, 0)),
                  pl.BlockSpec((1, nb, AT_WIDTH, KBLK), lambda i, j: (i, 0, 0, 0))],
        out_specs=pl.BlockSpec((1, QBLK, AT_WIDTH), lambda i, j: (i, j, 0)),
        out_shape=jax.ShapeDtypeStruct((b, s, AT_WIDTH), BF16),
        scratch_shapes=[pltpu.VMEM((s, QBLK), I32),
                        pltpu.VMEM((KIDX_W, IDX_HEADS * QBLK), BF16),
                        pltpu.VMEM((AT_HEADS // 2, LANES, 2 * QBLK), BF16),
                        pltpu.VMEM((AT_WIDTH, QBLK), F32),
                        pltpu.VMEM((AT_HEADS, QBLK), F32),
                        pltpu.VMEM((AT_HEADS, QBLK), F32)],
        compiler_params=_params(("parallel", "arbitrary")),
        name="dsa_attn",
    )(qT, qiT, wT, kidx, k, vT)


def _layer_norm(v, g, b):
    mu = jnp.mean(v, axis=-1, keepdims=True)
    var = jnp.mean(jnp.square(v - mu), axis=-1, keepdims=True)
    return (v - mu) * lax.rsqrt(var + LN_EPS) * g + b


def _out_ln_kernel(oa_ref, ob_ref, x_ref, wa_ref, wb_ref, g_ref, b_ref, h_ref, hb_ref):
    mix = _dot(oa_ref[...], wa_ref[...]) + _dot(ob_ref[...], wb_ref[...])
    h = _layer_norm(ALPHA * x_ref[...] + mix, g_ref[...], b_ref[...])
    h_ref[...] = h
    hb_ref[...] = h.astype(BF16)


def _out_ln(oa, ob, x2, wa, wb, g, b, tm=512):
    t, d = x2.shape
    row = lambda w: pl.BlockSpec((tm, w), lambda i: (i, 0))
    full = lambda a: pl.BlockSpec(a.shape, lambda i: (0, 0))
    return pl.pallas_call(
        _out_ln_kernel,
        grid=(t // tm,),
        in_specs=[row(oa.shape[1]), row(ob.shape[1]), row(d), full(wa), full(wb), full(g), full(b)],
        out_specs=[row(d), row(d)],
        out_shape=[jax.ShapeDtypeStruct((t, d), F32), jax.ShapeDtypeStruct((t, d), BF16)],
        compiler_params=_params(("parallel",)),
        name="out_ln",
    )(oa, ob, x2, wa, wb, g, b)


def _first_max(v, idx, big):
    m = jnp.max(v, axis=0, keepdims=True)
    first = jnp.min(jnp.where(v == m, idx, big), axis=0, keepdims=True)
    return m, first


def _router_kernel(h_ref, wrT_ref, bias_ref, eidx_ref, gwT_ref, cnt_ref):
    tm = h_ref.shape[0]
    per_group = N_EXPERTS // N_GROUPS
    logits = _dot_nt(_cat_hlh(wrT_ref[...], 1), _cat_hhl(h_ref[...], 1))
    scores = _sigmoid(logits)
    sel = scores + bias_ref[...]
    iota_g = lax.broadcasted_iota(I32, (per_group, tm), 0)
    neg = -jnp.inf

    grp = jnp.zeros((N_GROUPS, tm), F32)
    iota_grp = lax.broadcasted_iota(I32, (N_GROUPS, tm), 0)
    for g in range(N_GROUPS):
        blk = sel[g * per_group:(g + 1) * per_group, :]
        m1, i1 = _first_max(blk, iota_g, per_group)
        m2 = jnp.max(jnp.where(iota_g == i1, neg, blk), axis=0, keepdims=True)
        grp = jnp.where(iota_grp == g, m1 + m2, grp)

    gmask = jnp.zeros((N_GROUPS, tm), F32)
    work = grp
    for _ in range(TOPK_GROUPS):
        _, gi = _first_max(work, iota_grp, N_GROUPS)
        hit = iota_grp == gi
        gmask = jnp.where(hit, 1.0, gmask)
        work = jnp.where(hit, neg, work)

    e_mask = jnp.concatenate(
        [jnp.broadcast_to(gmask[g:g + 1, :], (per_group, tm)) for g in range(N_GROUPS)], axis=0)
    iota_e = lax.broadcasted_iota(I32, (N_EXPERTS, tm), 0)
    work = jnp.where(e_mask > 0.5, sel, neg)
    iota_k = lax.broadcasted_iota(I32, (TOP_K, tm), 0)
    eidx = jnp.zeros((TOP_K, tm), I32)
    w = jnp.zeros((TOP_K, tm), F32)
    chosen = jnp.zeros((N_EXPERTS, tm), F32)
    for k in range(TOP_K):
        _, ei = _first_max(work, iota_e, N_EXPERTS)
        hit = iota_e == ei
        wk = jnp.sum(jnp.where(hit, scores, 0.0), axis=0, keepdims=True)
        eidx = jnp.where(iota_k == k, ei, eidx)
        w = jnp.where(iota_k == k, wk, w)
        chosen = jnp.where(hit, 1.0, chosen)
        work = jnp.where(hit, neg, work)

    gates = w / jnp.sum(w, axis=0, keepdims=True) * ROUTED_SCALE
    eidx_ref[...] = eidx
    gates = jnp.concatenate([gates, jnp.zeros((LANES - TOP_K, tm), F32)], axis=0)
    gwT_ref[...] = gates.T
    cnt_ref[0] = jnp.broadcast_to(jnp.sum(chosen, axis=1, keepdims=True), (N_EXPERTS, LANES))


def _router(h1, wrT, bias_col, tm):
    t, d = h1.shape
    nt = t // tm
    return pl.pallas_call(
        _router_kernel,
        grid=(nt,),
        in_specs=[pl.BlockSpec((tm, d), lambda i: (i, 0)),
                  pl.BlockSpec(wrT.shape, lambda i: (0, 0)),
                  pl.BlockSpec(bias_col.shape, lambda i: (0, 0))],
        out_specs=[pl.BlockSpec((TOP_K, tm), lambda i: (0, i)),
                   pl.BlockSpec((tm, LANES), lambda i: (i, 0)),
                   pl.BlockSpec((1, N_EXPERTS, LANES), lambda i: (i, 0, 0))],
        out_shape=[jax.ShapeDtypeStruct((TOP_K, t), I32),
                   jax.ShapeDtypeStruct((t, LANES), F32),
                   jax.ShapeDtypeStruct((nt, N_EXPERTS, LANES), F32)],
        compiler_params=_params(("parallel",)),
        name="router",
    )(h1, wrT, bias_col)


MOE_TM = 512
SEG = 16
EXP_TM = 512
ROW_BUF = TOP_K * MOE_TM + N_EXPERTS * SEG
P_BLK = 256
G_BLK = 512


def _moe_plan(cnt, n_exp_tiles):
    nt = cnt.shape[0]
    n16 = (cnt + (SEG - 1)) // SEG
    so16 = jnp.cumsum(n16, axis=1) - n16
    r16 = jnp.sum(n16, axis=0)
    per = EXP_TM // SEG
    rp16 = (r16 + (per - 1)) // per * per
    ends = jnp.cumsum(rp16)
    off16 = ends - rp16
    go16 = off16[None, :] + jnp.cumsum(n16, axis=0) - n16
    n_et = (ends[-1] // per).astype(I32).reshape(1)
    tile_expert = jnp.searchsorted(ends // per, jnp.arange(n_exp_tiles, dtype=I32), side="right")
    tile_expert = jnp.minimum(tile_expert, N_EXPERTS - 1).astype(I32)
    flat = lambda a: a.reshape(nt * N_EXPERTS).astype(I32)
    return dict(n16=flat(n16), so16=flat(so16), go16=flat(go16),
                rtot16=jnp.sum(n16, axis=1).astype(I32), r16=r16.astype(I32), rp16=rp16.astype(I32),
                off16=off16.astype(I32), n_et=n_et, tile_expert=tile_expert,
                so_rows=(so16 * SEG).astype(F32))


def _chunk(ref, c):
    return ref.at[pl.ds(pl.multiple_of(c * SEG, SEG), SEG)]


def _segment_copies(i, n16_ref, so16_ref, go16_ref, make_copy):
    def expert_body(e, total):
        n = n16_ref[i * N_EXPERTS + e]
        so = so16_ref[i * N_EXPERTS + e]
        go = go16_ref[i * N_EXPERTS + e]

        def chunk_body(c, carry):
            make_copy(so + c, go + c).start()
            return carry

        lax.fori_loop(0, n, chunk_body, 0)
        return total + n

    return lax.fori_loop(0, N_EXPERTS, expert_body, 0)


def _wait_copies(count, make_copy):
    def body(c, carry):
        make_copy(0, 0).wait()
        return carry

    lax.fori_loop(0, count, body, 0)


def _row_positions(eidx, so_col):
    tm = eidx.shape[1]
    iota_e = lax.broadcasted_iota(I32, (N_EXPERTS, tm), 0)
    onehot = jnp.zeros((N_EXPERTS, tm), F32)
    for k in range(TOP_K):
        onehot = jnp.where(iota_e == eidx[k:k + 1, :], 1.0, onehot)
    earlier = lax.broadcasted_iota(I32, (tm, tm), 0) < lax.broadcasted_iota(I32, (tm, tm), 1)
    rank = _dot(onehot.astype(BF16), jnp.where(earlier, 1.0, 0.0).astype(BF16))
    lpos = so_col + rank
    return [jnp.sum(jnp.where(iota_e == eidx[k:k + 1, :], lpos, 0.0), axis=0, keepdims=True)
            for k in range(TOP_K)]


def _dispatch_kernel(n16_ref, so16_ref, go16_ref, rtot_ref, r16_ref, rp16_ref, off16_ref,
                     x_ref, eidx_ref, so_ref, xs_hbm, lpT_ref, buf_ref, zero_ref, sem):
    i = pl.program_id(0)
    tm = x_ref.shape[0]
    lps = _row_positions(eidx_ref[...], so_ref[0][:, 0:1])
    iota_k = lax.broadcasted_iota(I32, (LANES, tm), 0)
    lp_all = jnp.zeros((LANES, tm), F32)
    for k in range(TOP_K):
        lp_all = jnp.where(iota_k == k, lps[k], lp_all)
    lpT_ref[...] = lp_all.T

    x = x_ref[...]
    n_blk = (rtot_ref[i] * SEG + (P_BLK - 1)) // P_BLK

    def blk_body(rb, carry):
        r0 = pl.multiple_of(rb * P_BLK, P_BLK)
        rows = (lax.broadcasted_iota(I32, (P_BLK, tm), 0) + r0).astype(F32)
        p = jnp.zeros((P_BLK, tm), F32)
        for k in range(TOP_K):
            p = jnp.where(rows == lps[k], 1.0, p)
        buf_ref[pl.ds(r0, P_BLK), :] = _dot(p.astype(BF16), x).astype(BF16)
        return carry

    lax.fori_loop(0, n_blk, blk_body, 0)

    out_copy = lambda src, dst: pltpu.make_async_copy(_chunk(buf_ref, src), _chunk(xs_hbm, dst), sem)
    _wait_copies(_segment_copies(i, n16_ref, so16_ref, go16_ref, out_copy), out_copy)

    @pl.when(i == pl.num_programs(0) - 1)
    def _():
        zero_ref[...] = jnp.zeros_like(zero_ref)
        zero_copy = lambda src, dst: pltpu.make_async_copy(zero_ref, _chunk(xs_hbm, dst), sem)

        def zero_chunk(c, carry):
            zero_copy(0, c).start()
            return carry

        def tail_body(e, total):
            lax.fori_loop(off16_ref[e] + r16_ref[e], off16_ref[e] + rp16_ref[e], zero_chunk, 0)
            return total + rp16_ref[e] - r16_ref[e]

        n_tail = lax.fori_loop(0, N_EXPERTS, tail_body, 0)
        used = off16_ref[N_EXPERTS - 1] + rp16_ref[N_EXPERTS - 1]
        n_all = xs_hbm.shape[0] // SEG
        lax.fori_loop(used, n_all, zero_chunk, 0)
        _wait_copies(n_tail + n_all - used, zero_copy)


def _dispatch(hb, eidx, plan, n_rows):
    t, d = hb.shape
    nt = t // MOE_TM
    so_rows = jnp.broadcast_to(plan["so_rows"][:, :, None], (nt, N_EXPERTS, LANES))
    grid_spec = pltpu.PrefetchScalarGridSpec(
        num_scalar_prefetch=7,
        grid=(nt,),
        in_specs=[pl.BlockSpec((MOE_TM, d), lambda i, *_: (i, 0)),
                  pl.BlockSpec((TOP_K, MOE_TM), lambda i, *_: (0, i)),
                  pl.BlockSpec((1, N_EXPERTS, LANES), lambda i, *_: (i, 0, 0))],
        out_specs=[pl.BlockSpec(memory_space=pl.ANY),
                   pl.BlockSpec((MOE_TM, LANES), lambda i, *_: (i, 0))],
        scratch_shapes=[pltpu.VMEM((ROW_BUF, d), BF16), pltpu.VMEM((SEG, d), BF16),
                        pltpu.SemaphoreType.DMA(())])
    return pl.pallas_call(
        _dispatch_kernel,
        grid_spec=grid_spec,
        out_shape=[jax.ShapeDtypeStruct((n_rows, d), BF16),
                   jax.ShapeDtypeStruct((t, LANES), F32)],
        compiler_params=_params(("arbitrary",)),
        name="moe_dispatch",
    )(plan["n16"], plan["so16"], plan["go16"], plan["rtot16"], plan["r16"], plan["rp16"],
      plan["off16"], hb, eidx, so_rows)


def _experts_kernel(te_ref, net_ref, x_ref, wg_ref, wu_ref, wd_ref, y_ref):
    used = pl.program_id(0) < net_ref[0]

    @pl.when(used)
    def _():
        x = x_ref[...]
        hmid = _silu(_dot(x, wg_ref[0].astype(BF16))) * _dot(x, wu_ref[0].astype(BF16))
        y_ref[...] = _dot(hmid.astype(BF16), wd_ref[0].astype(BF16)).astype(y_ref.dtype)

    @pl.when(jnp.logical_not(used))
    def _():
        y_ref[...] = jnp.zeros_like(y_ref)


def _experts(xs, wg, wu, wd, plan):
    n_rows, d = xs.shape
    x_map = lambda j, te, net: (jnp.minimum(j, net[0] - 1), 0)
    w_map = lambda j, te, net: (te[jnp.minimum(j, net[0] - 1)], 0, 0)
    grid_spec = pltpu.PrefetchScalarGridSpec(
        num_scalar_prefetch=2,
        grid=(n_rows // EXP_TM,),
        in_specs=[pl.BlockSpec((EXP_TM, d), x_map),
                  pl.BlockSpec((1, d, D_EXPERT), w_map),
                  pl.BlockSpec((1, d, D_EXPERT), w_map),
                  pl.BlockSpec((1, D_EXPERT, d), w_map)],
        out_specs=pl.BlockSpec((EXP_TM, d), lambda j, te, net: (j, 0)))
    return pl.pallas_call(
        _experts_kernel,
        grid_spec=grid_spec,
        out_shape=jax.ShapeDtypeStruct((n_rows, d), BF16),
        compiler_params=_params(("arbitrary",)),
        name="moe_experts",
    )(plan["tile_expert"], plan["n_et"], xs, wg, wu, wd)


def _combine_kernel(n16_ref, so16_ref, go16_ref, rtot_ref,
                    ys_hbm, lpT_ref, gwT_ref, h_ref, sg_ref, su_ref, sd_ref, lg_ref, lb_ref,
                    o_ref, ybuf_ref, acc_ref, sem):
    i = pl.program_id(0)
    tm = h_ref.shape[0]

    @pl.when(i == 0)
    def _():
        ybuf_ref[...] = jnp.zeros_like(ybuf_ref)

    in_copy = lambda dst, src: pltpu.make_async_copy(_chunk(ys_hbm, src), _chunk(ybuf_ref, dst), sem)
    n_copies = _segment_copies(i, n16_ref, so16_ref, go16_ref, in_copy)

    h = h_ref[...]
    xb = h.astype(BF16)
    hs = _silu(_dot(xb, sg_ref[...])) * _dot(xb, su_ref[...])
    acc_ref[...] = ALPHA * h + _dot(hs.astype(BF16), sd_ref[...])
    _wait_copies(n_copies, in_copy)

    lp = lpT_ref[...]
    gw = gwT_ref[...]
    lps = [lp[:, k:k + 1] for k in range(TOP_K)]
    gws = [gw[:, k:k + 1] for k in range(TOP_K)]
    n_blk = (rtot_ref[i] * SEG + (G_BLK - 1)) // G_BLK

    def blk_body(cb, carry):
        c0 = pl.multiple_of(cb * G_BLK, G_BLK)
        cols = (lax.broadcasted_iota(I32, (tm, G_BLK), 1) + c0).astype(F32)
        g = jnp.zeros((tm, G_BLK), F32)
        for k in range(TOP_K):
            g = jnp.where(cols == lps[k], gws[k], g)
        acc_ref[...] += _dot(g.astype(BF16), ybuf_ref[pl.ds(c0, G_BLK), :])
        return carry

    lax.fori_loop(0, n_blk, blk_body, 0)
    o_ref[...] = _layer_norm(acc_ref[...], lg_ref[...], lb_ref[...])


def _combine(ys, lpT, gwT, h1, sg, su, sd, lg, lb, plan):
    t, d = h1.shape
    nt = t // MOE_TM
    row = lambda w: pl.BlockSpec((MOE_TM, w), lambda i, *_: (i, 0))
    full = lambda a: pl.BlockSpec(a.shape, lambda i, *_: (0, 0))
    grid_spec = pltpu.PrefetchScalarGridSpec(
        num_scalar_prefetch=4,
        grid=(nt,),
        in_specs=[pl.BlockSpec(memory_space=pl.ANY), row(LANES), row(LANES), row(d),
                  full(sg), full(su), full(sd), full(lg), full(lb)],
        out_specs=row(d),
        scratch_shapes=[pltpu.VMEM((ROW_BUF, d), BF16), pltpu.VMEM((MOE_TM, d), F32),
                        pltpu.SemaphoreType.DMA(())])
    return pl.pallas_call(
        _combine_kernel,
        grid_spec=grid_spec,
        out_shape=jax.ShapeDtypeStruct((t, d), F32),
        compiler_params=_params(("arbitrary",)),
        name="moe_combine",
    )(plan["n16"], plan["so16"], plan["go16"], plan["rtot16"], ys, lpT, gwT, h1, sg, su, sd, lg, lb)


def _moe(h1, hb, eidx, gwT, cnt, wg, wu, wd, sg, su, sd, lg, lb):
    t = h1.shape[0]
    nt = t // MOE_TM
    max_rows = nt * ROW_BUF + N_EXPERTS * EXP_TM
    n_exp_tiles = pl.cdiv(max_rows, EXP_TM)
    plan = _moe_plan(cnt[:, :, 0].astype(I32), n_exp_tiles)
    xs, lpT = _dispatch(hb, eidx, plan, n_exp_tiles * EXP_TM)
    ys = _experts(xs, wg, wu, wd, plan)
    return _combine(ys, lpT, gwT, h1, sg, su, sd, lg, lb, plan)


def kernel(x, w_in, hg_lb_logits, hg_norm_g, q_norm_g, w_q_up, w_qidx_up, kv_norm_g, w_kv_up,
           idx_ln_g, idx_ln_b, w_out, ln1_g, ln1_b, w_router, router_bias, w_e_gate, w_e_up,
           w_e_down, w_s_gate, w_s_up, w_s_down, ln2_g, ln2_b):
    b, s, d = x.shape
    h = x
    for l in range(DEPTH):
        h2 = h.reshape(b * s, d)
        n_main = 4 * HG_WIDTH
        w_main = w_in[l, :, :n_main].astype(BF16)
        w_tail = jnp.pad(w_in[l, :, n_main:], ((0, 0), (0, TAIL_W - (w_in.shape[2] - n_main))))

        proj = _proj_main(h2, w_main).reshape(b, s, n_main)
        o_a = _hgrn(proj, hg_lb_logits, hg_norm_g[l], l)

        wqT = w_q_up[l].T.astype(BF16)
        wqiT = w_qidx_up[l].T
        wk = w_kv_up[l][:, :AT_WIDTH].astype(BF16)
        wvT = w_kv_up[l][:, AT_WIDTH:].T.astype(BF16)
        qT, qiT, k, vT, kidx, wT = _dsa_prep(h, w_tail, q_norm_g[l], kv_norm_g[l], idx_ln_g[l],
                                             idx_ln_b[l], wqT, wqiT, wk, wvT)
        o_b = _dsa_attn(qT, qiT, wT, kidx, k, vT)

        h1, hb = _out_ln(o_a.reshape(b * s, HG_WIDTH), o_b.reshape(b * s, AT_WIDTH), h2,
                         w_out[l, :HG_WIDTH].astype(BF16), w_out[l, HG_WIDTH:].astype(BF16),
                         ln1_g[l].reshape(1, d), ln1_b[l].reshape(1, d))
        eidx, gwT, cnt = _router(h1, w_router[l].T, router_bias[l].reshape(N_EXPERTS, 1), MOE_TM)
        out = _moe(h1, hb, eidx, gwT, cnt, w_e_gate[l], w_e_up[l], w_e_down[l],
                   w_s_gate[l].astype(BF16), w_s_up[l].astype(BF16), w_s_down[l].astype(BF16),
                   ln2_g[l].reshape(1, d), ln2_b[l].reshape(1, d))
        h = out.reshape(b, s, d)
    return h
```

```python
import functools

import jax
import jax.numpy as jnp
from jax import lax
from jax.experimental import pallas as pl
from jax.experimental.pallas import tpu as pltpu

F32 = jnp.float32
BF16 = jnp.bfloat16
I32 = jnp.int32
HIGHEST = lax.Precision.HIGHEST

CHUNK = 64
HG_HEADS = 4
HG_DK = 128
HG_DV = 128
HG_WIDTH = HG_HEADS * HG_DV
AT_HEADS = 8
AT_DH = 64
AT_WIDTH = AT_HEADS * AT_DH
Q_RANK = 256
KV_RANK = 128
IDX_HEADS = 4
IDX_DH = 64
TOPK_MAX = 256
N_EXPERTS = 64
TOP_K = 8
N_GROUPS = 8
TOPK_GROUPS = 4
D_EXPERT = 256
ROUTED_SCALE = 2.5
DEPTH = 1
ALPHA = (2.0 * DEPTH) ** 0.25
LN_EPS = 1e-5
RMS_EPS = 1e-6

LANES = 128
SUBLANES = 8
QBLK = 2 * CHUNK
KBLK = 256
VMEM_LIMIT = 56 * 1024 * 1024
INT_MIN = -2 ** 31


def _params(sem, vmem=VMEM_LIMIT):
    return pltpu.CompilerParams(dimension_semantics=sem, vmem_limit_bytes=vmem)


def _dot(a, b, precision=None):
    return jnp.dot(a, b, preferred_element_type=F32, precision=precision)


def _dot_nt(a, b, precision=None):
    return lax.dot_general(a, b, (((1,), (1,)), ((), ())), preferred_element_type=F32,
                           precision=precision)


def _dot_tn(a, b, precision=None):
    return lax.dot_general(a, b, (((0,), (0,)), ((), ())), preferred_element_type=F32,
                           precision=precision)


def _split(x):
    hi = x.astype(BF16)
    return hi, (x - hi.astype(F32)).astype(BF16)


def _cat_hhl(x, axis):
    hi, lo = _split(x)
    return jnp.concatenate([hi, hi, lo], axis=axis)


def _cat_hlh(x, axis):
    hi, lo = _split(x)
    return jnp.concatenate([hi, lo, hi], axis=axis)


def _split3(x):
    hi = x.astype(BF16)
    r = x - hi.astype(F32)
    mid = r.astype(BF16)
    return hi, mid, (r - mid.astype(F32)).astype(BF16)


def _cat_act6(x, axis):
    h, m, l = _split3(x)
    return jnp.concatenate([h, h, m, h, l, m], axis=axis)


def _cat_wgt6(x, axis):
    h, m, l = _split3(x)
    return jnp.concatenate([h, m, h, l, h, m], axis=axis)


def _sigmoid(x):
    return 1.0 / (1.0 + jnp.exp(-x))


def _silu(x):
    return x * _sigmoid(x)


def _proj_main_kernel(x_ref, w_ref, o_ref):
    o_ref[...] = _dot(x_ref[...].astype(BF16), w_ref[...])


def _proj_main(x2, w_bf, tm=256):
    t, d = x2.shape
    n = w_bf.shape[1]
    return pl.pallas_call(
        _proj_main_kernel,
        grid=(t // tm,),
        in_specs=[pl.BlockSpec((tm, d), lambda i: (i, 0)),
                  pl.BlockSpec((d, n), lambda i: (0, 0))],
        out_specs=pl.BlockSpec((tm, n), lambda i: (i, 0)),
        out_shape=jax.ShapeDtypeStruct((t, n), F32),
        compiler_params=_params(("parallel",)),
        name="proj_main",
    )(x2, w_bf)


def _hgrn_kernel(p_ref, lbl_ref, ng_ref, o_ref, st_ref, *, n_chunks, layer):
    @pl.when(pl.program_id(1) == 0)
    def _():
        st_ref[...] = jnp.zeros_like(st_ref)

    lg = lbl_ref[...]
    ex = jnp.exp(lg - jnp.max(lg, axis=0, keepdims=True))
    lb_all = jnp.sum(ex[: layer + 1], axis=0, keepdims=True) / jnp.sum(ex, axis=0, keepdims=True)
    ng = ng_ref[...]

    r = lax.broadcasted_iota(I32, (CHUNK, CHUNK), 0)
    c = lax.broadcasted_iota(I32, (CHUNK, CHUNK), 1)
    causal = c <= r
    tri = jnp.where(causal, 1.0, 0.0).astype(BF16)
    tri2 = jnp.concatenate([tri, tri], axis=1)

    def chunk_body(j, carry):
        rows = pl.ds(pl.multiple_of(j * CHUNK, CHUNK), CHUNK)
        for h in range(HG_HEADS):
            lo = h * HG_DK
            q = _silu(p_ref[0, rows, lo:lo + HG_DK])
            fl = p_ref[0, rows, HG_WIDTH + lo:HG_WIDTH + lo + HG_DK]
            v = p_ref[0, rows, 2 * HG_WIDTH + lo:2 * HG_WIDTH + lo + HG_DV]
            gate = p_ref[0, rows, 3 * HG_WIDTH + lo:3 * HG_WIDTH + lo + HG_DV]
            lb = lb_all[:, lo:lo + HG_DK]
            f = lb + (1.0 - lb) * _sigmoid(fl)
            k = 1.0 - f
            lf_hi, lf_lo = _split(jnp.log(f))
            b = _dot(tri2, jnp.concatenate([lf_hi, lf_lo], axis=0))
            b_last = b[CHUNK - 1:CHUNK, :]
            q_dec = (q * jnp.exp(b)).astype(BF16)
            k_inv = (k * jnp.exp(-b)).astype(BF16)
            k_dec = (k * jnp.exp(b_last - b)).astype(BF16)
            vb = v.astype(BF16)
            scores = jnp.where(causal, _dot_nt(q_dec, k_inv), 0.0)
            o = _dot(scores.astype(BF16), vb)
            st = st_ref[h]
            o = o + _dot_nt(q_dec, st.astype(BF16))
            st_ref[h] = st * jnp.exp(b_last) + _dot_tn(vb, k_dec)
            o = o * lax.rsqrt(jnp.mean(o * o, axis=-1, keepdims=True) + RMS_EPS) * ng
            o_ref[0, rows, lo:lo + HG_DV] = (o * _silu(gate)).astype(o_ref.dtype)
        return carry

    lax.fori_loop(0, n_chunks, chunk_body, 0)


def _hgrn(proj, lb_logits, norm_g, layer, ct=512):
    b, s, w = proj.shape
    n_chunks = ct // CHUNK
    return pl.pallas_call(
        functools.partial(_hgrn_kernel, n_chunks=n_chunks, layer=layer),
        grid=(b, s // ct),
        in_specs=[pl.BlockSpec((1, ct, w), lambda i, j: (i, j, 0)),
                  pl.BlockSpec(lb_logits.shape, lambda i, j: (0, 0)),
                  pl.BlockSpec((1, HG_DV), lambda i, j: (0, 0))],
        out_specs=pl.BlockSpec((1, ct, HG_WIDTH), lambda i, j: (i, j, 0)),
        out_shape=jax.ShapeDtypeStruct((b, s, HG_WIDTH), BF16),
        scratch_shapes=[pltpu.VMEM((HG_HEADS, HG_DV, HG_DK), F32)],
        compiler_params=_params(("parallel", "arbitrary")),
        name="hgrn",
    )(proj, lb_logits, norm_g.reshape(1, HG_DV))


SEL_W = Q_RANK + LANES
KIDX_W = 6 * IDX_DH


def _dsa_prep_kernel(x_ref, wsel_ref, wckv_ref, qg_ref, kvg_ref, lng_ref, lnb_ref, wqT_ref, wqiT_ref,
                     wk_ref, wvT_ref, qT_ref, qiT_ref, k_ref, vT_ref, kidx_ref, wT_ref, w6_ref):
    @pl.when((pl.program_id(0) == 0) & (pl.program_id(1) == 0))
    def _():
        w6_ref[...] = _cat_wgt6(wsel_ref[...], 0)

    x = x_ref[...]
    sel = _dot(_cat_act6(x, 1), w6_ref[...])
    cq = sel[:, :Q_RANK]
    cqn = cq * lax.rsqrt(jnp.mean(cq * cq, axis=-1, keepdims=True) + RMS_EPS) * qg_ref[...]
    qT_ref[0] = (_dot_nt(wqT_ref[...], cqn.astype(BF16)) * (AT_DH ** -0.5)).astype(qT_ref.dtype)
    qiT_ref[0] = _dot_nt(_cat_wgt6(wqiT_ref[...], 1), _cat_act6(cqn, 1))
    ckv = _dot(x.astype(BF16), wckv_ref[...])
    ckvn = (ckv * lax.rsqrt(jnp.mean(ckv * ckv, axis=-1, keepdims=True) + RMS_EPS)
            * kvg_ref[...]).astype(BF16)
    k_ref[0] = _dot(ckvn, wk_ref[...]).astype(k_ref.dtype)
    vT_ref[0, 0] = _dot_nt(wvT_ref[...], ckvn).astype(vT_ref.dtype)
    k128 = sel[:, Q_RANK:]
    in_k = lax.broadcasted_iota(I32, (1, LANES), 1) < IDX_DH
    mu = jnp.sum(jnp.where(in_k, k128, 0.0), axis=-1, keepdims=True) * (1.0 / IDX_DH)
    dev = jnp.where(in_k, k128 - mu, 0.0)
    var = jnp.sum(dev * dev, axis=-1, keepdims=True) * (1.0 / IDX_DH)
    y = dev * lax.rsqrt(var + LN_EPS) * lng_ref[...] + lnb_ref[...]
    h, m, l = (p.astype(F32) for p in _split3(y))
    up = lambda p: pltpu.roll(p, IDX_DH, 1)
    kidx_ref[0] = jnp.concatenate([h + up(h), m + up(h), l + up(m)], axis=1).astype(BF16)
    tT = k128.T
    wT_ref[0] = tT[IDX_DH:IDX_DH + SUBLANES, :] * (IDX_HEADS ** -0.5 * IDX_DH ** -0.5)


def _dsa_prep(x, w_sel, w_ckv, q_norm_g, kv_norm_g, idx_ln_g, idx_ln_b, wqT, wqiT, wk, wvT):
    b, s, d = x.shape
    tm = KBLK
    nb = s // tm
    full = lambda a: pl.BlockSpec(a.shape, lambda i, j: (0,) * a.ndim)
    pad_lanes = lambda v: jnp.pad(v.reshape(1, -1), ((0, 0), (0, LANES - v.shape[0])))
    args = (w_sel, w_ckv, q_norm_g.reshape(1, -1), kv_norm_g.reshape(1, -1),
            pad_lanes(idx_ln_g), pad_lanes(idx_ln_b), wqT, wqiT, wk, wvT)
    return pl.pallas_call(
        _dsa_prep_kernel,
        grid=(b, nb),
        in_specs=[pl.BlockSpec((None, tm, d), lambda i, j: (i, j, 0))] + [full(a) for a in args],
        out_specs=[pl.BlockSpec((1, AT_WIDTH, tm), lambda i, j: (i, 0, j)),
                   pl.BlockSpec((1, IDX_HEADS * IDX_DH, tm), lambda i, j: (i, 0, j)),
                   pl.BlockSpec((1, tm, AT_WIDTH), lambda i, j: (i, j, 0)),
                   pl.BlockSpec((1, 1, AT_WIDTH, tm), lambda i, j: (i, j, 0, 0)),
                   pl.BlockSpec((1, tm, KIDX_W), lambda i, j: (i, j, 0)),
                   pl.BlockSpec((1, SUBLANES, tm), lambda i, j: (i, 0, j))],
        out_shape=[jax.ShapeDtypeStruct((b, AT_WIDTH, s), BF16),
                   jax.ShapeDtypeStruct((b, IDX_HEADS * IDX_DH, s), F32),
                   jax.ShapeDtypeStruct((b, s, AT_WIDTH), BF16),
                   jax.ShapeDtypeStruct((b, nb, AT_WIDTH, tm), BF16),
                   jax.ShapeDtypeStruct((b, s, KIDX_W), BF16),
                   jax.ShapeDtypeStruct((b, SUBLANES, s), F32)],
        scratch_shapes=[pltpu.VMEM((6 * d, SEL_W), BF16)],
        compiler_params=_params(("arbitrary", "arbitrary")),
        name="dsa_prep",
    )(x, *args)


def _sortable(x):
    bits = lax.bitcast_convert_type(x, I32)
    return jnp.where(bits < 0, bits ^ jnp.int32(0x7FFFFFFF), bits)


def _dsa_attn_kernel(qT_ref, qiT_ref, wT_ref, kidx_ref, k_ref, vT_ref, o_ref,
                     keys_ref, q2_ref, qbd_ref, acc_ref, m_ref, l_ref, *, k_top):
    qb = pl.program_id(1)
    t0 = qb * QBLK
    e_max = t0 + QBLK
    nkb = (e_max + KBLK - 1) >> 8

    lane = lax.broadcasted_iota(I32, (1, QBLK), 1)
    t_idx = t0 + lane
    end_t = ((t_idx >> 6) + 1) * CHUNK
    srow = lax.broadcasted_iota(I32, (KBLK, QBLK), 0)

    for h in range(IDX_HEADS):
        q2_ref[:, h * QBLK:(h + 1) * QBLK] = _cat_wgt6(qiT_ref[0, h * IDX_DH:(h + 1) * IDX_DH, :], 0)
    wrow = jnp.concatenate([wT_ref[0, h:h + 1, :] for h in range(IDX_HEADS)], axis=1)

    def score_body(kb, carry):
        rows = pl.ds(pl.multiple_of(kb * KBLK, KBLK), KBLK)
        logits = _dot(kidx_ref[0, rows, :], q2_ref[...])
        r = jnp.maximum(logits, 0.0) * wrow
        sc = r[:, :QBLK]
        for h in range(1, IDX_HEADS):
            sc = sc + r[:, h * QBLK:(h + 1) * QBLK]
        valid = (srow + kb * KBLK) < end_t
        keys_ref[rows, :] = _sortable(jnp.where(valid, sc, -jnp.inf))
        return carry

    lax.fori_loop(0, nkb, score_body, 0)

    def count_ge(cand):
        def body(kb, acc):
            rows = pl.ds(pl.multiple_of(kb * KBLK, KBLK), KBLK)
            hit = jnp.where(keys_ref[rows, :] >= cand, 1, 0).astype(I32)
            return acc + jnp.sum(hit.reshape(KBLK // SUBLANES, SUBLANES, QBLK), axis=0)
        acc = lax.fori_loop(0, nkb, body, jnp.zeros((SUBLANES, QBLK), I32))
        return jnp.sum(acc, axis=0, keepdims=True)

    zero = jnp.zeros((1, QBLK), I32)
    thr = jnp.where(count_ge(zero) >= k_top, zero, jnp.full((1, QBLK), INT_MIN, I32))

    def bit_body(i, thr):
        cand = thr | (jnp.int32(1) << (30 - i))
        return jnp.where(count_ge(cand) >= k_top, cand, thr)

    thr = lax.fori_loop(0, 31, bit_body, thr)
    need = (k_top - (count_ge(thr + 1))).astype(F32)

    for j in range(AT_HEADS // 2):
        pair = qT_ref[0, j * LANES:(j + 1) * LANES, :]
        rr = lax.broadcasted_iota(I32, pair.shape, 0)
        zeros = jnp.zeros_like(pair)
        qbd_ref[j, :, :QBLK] = jnp.where(rr < AT_DH, pair, zeros)
        qbd_ref[j, :, QBLK:] = jnp.where(rr >= AT_DH, pair, zeros)
    acc_ref[...] = jnp.zeros_like(acc_ref)
    m_ref[...] = jnp.full_like(m_ref, -jnp.inf)
    l_ref[...] = jnp.zeros_like(l_ref)

    ri = lax.broadcasted_iota(I32, (KBLK, KBLK), 0)
    ci = lax.broadcasted_iota(I32, (KBLK, KBLK), 1)
    lstrict = jnp.where(ci < ri, 1.0, 0.0).astype(BF16)

    def attn_body(kb, carry):
        rows = pl.ds(pl.multiple_of(kb * KBLK, KBLK), KBLK)
        kblk = keys_ref[rows, :]
        s_idx = srow + kb * KBLK
        eq = kblk == thr
        eqf = jnp.where(eq, 1.0, 0.0)
        before = _dot(lstrict, eqf.astype(BF16)) + carry
        sel = ((kblk > thr) | (eq & (before < need))) & (s_idx < end_t)
        bias = jnp.where(sel, 0.0, -jnp.inf)
        dist = jnp.abs(t_idx - s_idx).astype(F32)
        s2s = [_dot(k_ref[0, rows, j * LANES:(j + 1) * LANES], qbd_ref[j])
               for j in range(AT_HEADS // 2)]
        for j in range(AT_HEADS // 2):
            ps, alphas = [], []
            for half in range(2):
                h = 2 * j + half
                slope = 2.0 ** (-8.0 * (h + 1) / AT_HEADS)
                st = s2s[j][:, half * QBLK:(half + 1) * QBLK] - slope * dist + bias
                m_old = m_ref[h:h + 1, :]
                m_new = jnp.maximum(m_old, jnp.max(st, axis=0, keepdims=True))
                m_safe = jnp.where(m_new == -jnp.inf, 0.0, m_new)
                alpha = jnp.exp(m_old - m_safe)
                p = jnp.exp(st - m_safe)
                l_ref[h:h + 1, :] = alpha * l_ref[h:h + 1, :] + jnp.sum(p, axis=0, keepdims=True)
                m_ref[h:h + 1, :] = m_new
                ps.append(p.astype(BF16))
                alphas.append(alpha)
            o2 = _dot(vT_ref[0, kb, j * LANES:(j + 1) * LANES, :], jnp.concatenate(ps, axis=1))
            for half in range(2):
                hs = slice((2 * j + half) * AT_DH, (2 * j + half + 1) * AT_DH)
                acc_ref[hs, :] = (alphas[half] * acc_ref[hs, :]
                                  + o2[half * AT_DH:(half + 1) * AT_DH, half * QBLK:(half + 1) * QBLK])
        return carry + jnp.sum(eqf, axis=0, keepdims=True)

    lax.fori_loop(0, nkb, attn_body, jnp.zeros((1, QBLK), F32))

    for h in range(AT_HEADS):
        hs = slice(h * AT_DH, (h + 1) * AT_DH)
        acc_ref[hs, :] = acc_ref[hs, :] / l_ref[h:h + 1, :]
    o_ref[0] = acc_ref[...].T.astype(o_ref.dtype)


def _dsa_attn(qT, qiT, wT, kidx, k, vT):
    b, _, s = qT.shape
    nb = s // KBLK
    k_top = min(TOPK_MAX, s // 4)
    return pl.pallas_call(
        functools.partial(_dsa_attn_kernel, k_top=k_top),
        grid=(b, s // QBLK),
        in_specs=[pl.BlockSpec((1, AT_WIDTH, QBLK), lambda i, j: (i, 0, j)),
                  pl.BlockSpec((1, IDX_HEADS * IDX_DH, QBLK), lambda i, j: (i, 0, j)),
                  pl.BlockSpec((1, SUBLANES, QBLK), lambda i, j: (i, 0, j)),
                  pl.BlockSpec((1, s, KIDX_W), lambda i, j: (i, 0, 0)),
                  pl.BlockSpec((1, s, AT_WIDTH), lambda i, j: (i, 0, 0)),
                  pl.BlockSpec((1, nb, AT_WIDTH, KBLK), lambda i, j: (i, 0, 0, 0))],
        out_specs=pl.BlockSpec((1, QBLK, AT_WIDTH), lambda i, j: (i, j, 0)),
        out_shape=jax.ShapeDtypeStruct((b, s, AT_WIDTH), BF16),
        scratch_shapes=[pltpu.VMEM((s, QBLK), I32),
                        pltpu.VMEM((KIDX_W, IDX_HEADS * QBLK), BF16),
                        pltpu.VMEM((AT_HEADS // 2, LANES, 2 * QBLK), BF16),
                        pltpu.VMEM((AT_WIDTH, QBLK), F32),
                        pltpu.VMEM((AT_HEADS, QBLK), F32),
                        pltpu.VMEM((AT_HEADS, QBLK), F32)],
        compiler_params=_params(("parallel", "arbitrary")),
        name="dsa_attn",
    )(qT, qiT, wT, kidx, k, vT)


def _layer_norm(v, g, b):
    mu = jnp.mean(v, axis=-1, keepdims=True)
    var = jnp.mean(jnp.square(v - mu), axis=-1, keepdims=True)
    return (v - mu) * lax.rsqrt(var + LN_EPS) * g + b


def _out_ln_kernel(oa_ref, ob_ref, x_ref, wa_ref, wb_ref, g_ref, b_ref, h_ref, hb_ref):
    mix = _dot(oa_ref[...], wa_ref[...]) + _dot(ob_ref[...], wb_ref[...])
    h = _layer_norm(ALPHA * x_ref[...] + mix, g_ref[...], b_ref[...])
    h_ref[...] = h
    hb_ref[...] = h.astype(BF16)


def _out_ln(oa, ob, x2, wa, wb, g, b, tm=512):
    t, d = x2.shape
    row = lambda w: pl.BlockSpec((tm, w), lambda i: (i, 0))
    full = lambda a: pl.BlockSpec(a.shape, lambda i: (0, 0))
    return pl.pallas_call(
        _out_ln_kernel,
        grid=(t // tm,),
        in_specs=[row(oa.shape[1]), row(ob.shape[1]), row(d), full(wa), full(wb), full(g), full(b)],
        out_specs=[row(d), row(d)],
        out_shape=[jax.ShapeDtypeStruct((t, d), F32), jax.ShapeDtypeStruct((t, d), BF16)],
        compiler_params=_params(("parallel",)),
        name="out_ln",
    )(oa, ob, x2, wa, wb, g, b)


def _first_max(v, idx, big):
    m = jnp.max(v, axis=0, keepdims=True)
    first = jnp.min(jnp.where(v == m, idx, big), axis=0, keepdims=True)
    return m, first


def _router_kernel(h_ref, wrT_ref, bias_ref, eidx_ref, gT_ref, cnt_ref):
    tm = h_ref.shape[0]
    per_group = N_EXPERTS // N_GROUPS
    logits = _dot_nt(_cat_hlh(wrT_ref[...], 1), _cat_hhl(h_ref[...], 1))
    scores = _sigmoid(logits)
    sel = scores + bias_ref[...]
    iota_g = lax.broadcasted_iota(I32, (per_group, tm), 0)
    neg = -jnp.inf

    grp = jnp.zeros((N_GROUPS, tm), F32)
    iota_grp = lax.broadcasted_iota(I32, (N_GROUPS, tm), 0)
    for g in range(N_GROUPS):
        blk = sel[g * per_group:(g + 1) * per_group, :]
        m1, i1 = _first_max(blk, iota_g, per_group)
        m2 = jnp.max(jnp.where(iota_g == i1, neg, blk), axis=0, keepdims=True)
        grp = jnp.where(iota_grp == g, m1 + m2, grp)

    gmask = jnp.zeros((N_GROUPS, tm), F32)
    work = grp
    for _ in range(TOPK_GROUPS):
        _, gi = _first_max(work, iota_grp, N_GROUPS)
        hit = iota_grp == gi
        gmask = jnp.where(hit, 1.0, gmask)
        work = jnp.where(hit, neg, work)

    e_mask = jnp.concatenate(
        [jnp.broadcast_to(gmask[g:g + 1, :], (per_group, tm)) for g in range(N_GROUPS)], axis=0)
    iota_e = lax.broadcasted_iota(I32, (N_EXPERTS, tm), 0)
    work = jnp.where(e_mask > 0.5, sel, neg)
    iota_k = lax.broadcasted_iota(I32, (TOP_K, tm), 0)
    eidx = jnp.zeros((TOP_K, tm), I32)
    w = jnp.zeros((N_EXPERTS, tm), F32)
    chosen = jnp.zeros((N_EXPERTS, tm), F32)
    for k in range(TOP_K):
        _, ei = _first_max(work, iota_e, N_EXPERTS)
        hit = iota_e == ei
        eidx = jnp.where(iota_k == k, ei, eidx)
        w = jnp.where(hit, scores, w)
        chosen = jnp.where(hit, 1.0, chosen)
        work = jnp.where(hit, neg, work)

    gates = w / jnp.sum(w, axis=0, keepdims=True) * ROUTED_SCALE
    eidx_ref[...] = eidx
    gates = jnp.concatenate([gates, jnp.zeros((LANES - N_EXPERTS, tm), F32)], axis=0)
    gT_ref[...] = gates.T
    cnt_ref[0] = jnp.broadcast_to(jnp.sum(chosen, axis=1, keepdims=True), (N_EXPERTS, LANES))


def _router(h1, wrT, bias_col, tm):
    t, d = h1.shape
    nt = t // tm
    return pl.pallas_call(
        _router_kernel,
        grid=(nt,),
        in_specs=[pl.BlockSpec((tm, d), lambda i: (i, 0)),
                  pl.BlockSpec(wrT.shape, lambda i: (0, 0)),
                  pl.BlockSpec(bias_col.shape, lambda i: (0, 0))],
        out_specs=[pl.BlockSpec((TOP_K, tm), lambda i: (0, i)),
                   pl.BlockSpec((tm, LANES), lambda i: (i, 0)),
                   pl.BlockSpec((1, N_EXPERTS, LANES), lambda i: (i, 0, 0))],
        out_shape=[jax.ShapeDtypeStruct((TOP_K, t), I32),
                   jax.ShapeDtypeStruct((t, LANES), F32),
                   jax.ShapeDtypeStruct((nt, N_EXPERTS, LANES), F32)],
        compiler_params=_params(("parallel",)),
        name="router",
    )(h1, wrT, bias_col)


MOE_TM = 512
SEG = 16
EXP_TM = 512
ROW_BUF = TOP_K * MOE_TM + N_EXPERTS * SEG
P_BLK = 256
G_BLK = 512


def _moe_plan(cnt, n_exp_tiles):
    nt = cnt.shape[0]
    n16 = (cnt + (SEG - 1)) // SEG
    so16 = jnp.cumsum(n16, axis=1) - n16
    r16 = jnp.sum(n16, axis=0)
    per = EXP_TM // SEG
    rp16 = (r16 + (per - 1)) // per * per
    ends = jnp.cumsum(rp16)
    off16 = ends - rp16
    go16 = off16[None, :] + jnp.cumsum(n16, axis=0) - n16
    n_et = (ends[-1] // per).astype(I32).reshape(1)
    tiles = jnp.arange(n_exp_tiles, dtype=I32)
    tile_expert = jnp.sum((ends // per)[None, :] <= tiles[:, None], axis=1)
    tile_expert = jnp.minimum(tile_expert, N_EXPERTS - 1).astype(I32)
    flat = lambda a: a.reshape(nt * N_EXPERTS).astype(I32)
    twice = lambda a: jnp.concatenate([a, a], axis=1).astype(F32)
    seg_lo, seg_hi = twice(so16 * SEG), twice((so16 + n16) * SEG)
    as_rows = lambda a: jnp.broadcast_to(a[:, None, :], (nt, SUBLANES, 2 * N_EXPERTS))
    as_cols = lambda a: jnp.broadcast_to(a[:, :, None], (nt, 2 * N_EXPERTS, LANES))
    return dict(n16=flat(n16), so16=flat(so16), go16=flat(go16),
                rtot16=jnp.sum(n16, axis=1).astype(I32), r16=r16.astype(I32), rp16=rp16.astype(I32),
                off16=off16.astype(I32), n_et=n_et, tile_expert=tile_expert,
                lo_rows=as_rows(seg_lo), hi_rows=as_rows(seg_hi),
                lo_cols=as_cols(seg_lo), hi_cols=as_cols(seg_hi))


def _chunk(ref, c):
    return ref.at[pl.ds(pl.multiple_of(c * SEG, SEG), SEG)]


def _segment_copies(i, n16_ref, so16_ref, go16_ref, make_copy):
    def expert_body(e, total):
        n = n16_ref[i * N_EXPERTS + e]
        so = so16_ref[i * N_EXPERTS + e]
        go = go16_ref[i * N_EXPERTS + e]

        def chunk_body(c, carry):
            make_copy(so + c, go + c).start()
            return carry

        lax.fori_loop(0, n, chunk_body, 0)
        return total + n

    return lax.fori_loop(0, N_EXPERTS, expert_body, 0)


def _wait_copies(count, make_copy):
    def body(c, carry):
        make_copy(0, 0).wait()
        return carry

    lax.fori_loop(0, count, body, 0)


POS_SPLIT = 64


def _dispatch_kernel(n16_ref, so16_ref, go16_ref, rtot_ref, r16_ref, rp16_ref, off16_ref,
                     x_ref, eidx_ref, locol_ref, lorow_ref, hirow_ref,
                     xs_hbm, posT_ref, buf_ref, zero_ref, sem):
    i = pl.program_id(0)
    tm = x_ref.shape[0]
    eidx = eidx_ref[...]
    iota_e = lax.broadcasted_iota(I32, (N_EXPERTS, tm), 0)
    onehot = jnp.zeros((N_EXPERTS, tm), F32)
    for k in range(TOP_K):
        onehot = jnp.where(iota_e == eidx[k:k + 1, :], 1.0, onehot)
    earlier = lax.broadcasted_iota(I32, (tm, tm), 0) < lax.broadcasted_iota(I32, (tm, tm), 1)
    rank = _dot(onehot.astype(BF16), jnp.where(earlier, 1.0, 0.0).astype(BF16))
    pos1 = jnp.where(onehot > 0.5, locol_ref[0][:N_EXPERTS, 0:1] + rank + 1.0, 0.0)
    pos_hi = jnp.floor(pos1 * (1.0 / POS_SPLIT)) * POS_SPLIT
    pos2 = jnp.concatenate([pos_hi, pos1 - pos_hi], axis=0)
    posT_ref[...] = pos2.T
    pos2b = pos2.astype(BF16)

    x = x_ref[...]
    seg_lo = lorow_ref[0][0:1, :]
    seg_hi = hirow_ref[0][0:1, :]
    n_blk = (rtot_ref[i] * SEG + (P_BLK - 1)) // P_BLK

    def blk_body(rb, carry):
        r0 = pl.multiple_of(rb * P_BLK, P_BLK)
        r_e = (lax.broadcasted_iota(I32, (P_BLK, 2 * N_EXPERTS), 0) + r0).astype(F32)
        owner = jnp.where((r_e >= seg_lo) & (r_e < seg_hi), 1.0, 0.0).astype(BF16)
        want = _dot(owner, pos2b)
        r_t = (lax.broadcasted_iota(I32, (P_BLK, tm), 0) + (r0 + 1)).astype(F32)
        p = jnp.where(want == r_t, 1.0, 0.0).astype(BF16)
        buf_ref[pl.ds(r0, P_BLK), :] = _dot(p, x).astype(BF16)
        return carry

    lax.fori_loop(0, n_blk, blk_body, 0)

    out_copy = lambda src, dst: pltpu.make_async_copy(_chunk(buf_ref, src), _chunk(xs_hbm, dst), sem)
    _wait_copies(_segment_copies(i, n16_ref, so16_ref, go16_ref, out_copy), out_copy)

    @pl.when(i == pl.num_programs(0) - 1)
    def _():
        zero_ref[...] = jnp.zeros_like(zero_ref)
        zero_copy = lambda src, dst: pltpu.make_async_copy(zero_ref, _chunk(xs_hbm, dst), sem)

        def zero_chunk(c, carry):
            zero_copy(0, c).start()
            return carry

        def tail_body(e, total):
            lax.fori_loop(off16_ref[e] + r16_ref[e], off16_ref[e] + rp16_ref[e], zero_chunk, 0)
            return total + rp16_ref[e] - r16_ref[e]

        n_tail = lax.fori_loop(0, N_EXPERTS, tail_body, 0)
        used = off16_ref[N_EXPERTS - 1] + rp16_ref[N_EXPERTS - 1]
        n_all = xs_hbm.shape[0] // SEG
        lax.fori_loop(used, n_all, zero_chunk, 0)
        _wait_copies(n_tail + n_all - used, zero_copy)


def _dispatch(hb, eidx, plan, n_rows):
    t, d = hb.shape
    nt = t // MOE_TM
    grid_spec = pltpu.PrefetchScalarGridSpec(
        num_scalar_prefetch=7,
        grid=(nt,),
        in_specs=[pl.BlockSpec((MOE_TM, d), lambda i, *_: (i, 0)),
                  pl.BlockSpec((TOP_K, MOE_TM), lambda i, *_: (0, i)),
                  pl.BlockSpec((1, 2 * N_EXPERTS, LANES), lambda i, *_: (i, 0, 0)),
                  pl.BlockSpec((1, SUBLANES, 2 * N_EXPERTS), lambda i, *_: (i, 0, 0)),
                  pl.BlockSpec((1, SUBLANES, 2 * N_EXPERTS), lambda i, *_: (i, 0, 0))],
        out_specs=[pl.BlockSpec(memory_space=pl.ANY),
                   pl.BlockSpec((MOE_TM, LANES), lambda i, *_: (i, 0))],
        scratch_shapes=[pltpu.VMEM((ROW_BUF, d), BF16), pltpu.VMEM((SEG, d), BF16),
                        pltpu.SemaphoreType.DMA(())])
    return pl.pallas_call(
        _dispatch_kernel,
        grid_spec=grid_spec,
        out_shape=[jax.ShapeDtypeStruct((n_rows, d), BF16),
                   jax.ShapeDtypeStruct((t, LANES), F32)],
        compiler_params=_params(("arbitrary",)),
        name="moe_dispatch",
    )(plan["n16"], plan["so16"], plan["go16"], plan["rtot16"], plan["r16"], plan["rp16"],
      plan["off16"], hb, eidx, plan["lo_cols"], plan["lo_rows"], plan["hi_rows"])


def _experts_kernel(te_ref, net_ref, x_ref, wg_ref, wu_ref, wd_ref, y_ref, wgb_ref, wub_ref, wdb_ref):
    j = pl.program_id(0)
    used = j < net_ref[0]

    @pl.when(used & ((j == 0) | (te_ref[j] != te_ref[jnp.maximum(j - 1, 0)])))
    def _():
        wgb_ref[...] = wg_ref[0].astype(BF16)
        wub_ref[...] = wu_ref[0].astype(BF16)
        wdb_ref[...] = wd_ref[0].astype(BF16)

    @pl.when(used)
    def _():
        x = x_ref[...]
        hmid = _silu(_dot(x, wgb_ref[...])) * _dot(x, wub_ref[...])
        y_ref[...] = _dot(hmid.astype(BF16), wdb_ref[...]).astype(y_ref.dtype)

    @pl.when(jnp.logical_not(used))
    def _():
        y_ref[...] = jnp.zeros_like(y_ref)


def _experts(xs, wg, wu, wd, plan):
    n_rows, d = xs.shape
    x_map = lambda j, te, net: (jnp.minimum(j, net[0] - 1), 0)
    w_map = lambda j, te, net: (te[jnp.minimum(j, net[0] - 1)], 0, 0)
    grid_spec = pltpu.PrefetchScalarGridSpec(
        num_scalar_prefetch=2,
        grid=(n_rows // EXP_TM,),
        in_specs=[pl.BlockSpec((EXP_TM, d), x_map),
                  pl.BlockSpec((1, d, D_EXPERT), w_map),
                  pl.BlockSpec((1, d, D_EXPERT), w_map),
                  pl.BlockSpec((1, D_EXPERT, d), w_map)],
        out_specs=pl.BlockSpec((EXP_TM, d), lambda j, te, net: (j, 0)),
        scratch_shapes=[pltpu.VMEM((d, D_EXPERT), BF16), pltpu.VMEM((d, D_EXPERT), BF16),
                        pltpu.VMEM((D_EXPERT, d), BF16)])
    return pl.pallas_call(
        _experts_kernel,
        grid_spec=grid_spec,
        out_shape=jax.ShapeDtypeStruct((n_rows, d), BF16),
        compiler_params=_params(("arbitrary",)),
        name="moe_experts",
    )(plan["tile_expert"], plan["n_et"], xs, wg, wu, wd)


def _combine_kernel(n16_ref, so16_ref, go16_ref, rtot_ref,
                    ys_hbm, posT_ref, gT_ref, locol_ref, hicol_ref, h_ref,
                    sg_ref, su_ref, sd_ref, lg_ref, lb_ref, o_ref, ybuf_ref, acc_ref, sem):
    i = pl.program_id(0)
    tm = h_ref.shape[0]

    @pl.when(i == 0)
    def _():
        ybuf_ref[...] = jnp.zeros_like(ybuf_ref)

    in_copy = lambda dst, src: pltpu.make_async_copy(_chunk(ys_hbm, src), _chunk(ybuf_ref, dst), sem)
    n_copies = _segment_copies(i, n16_ref, so16_ref, go16_ref, in_copy)

    h = h_ref[...]
    xb = h.astype(BF16)
    hs = _silu(_dot(xb, sg_ref[...])) * _dot(xb, su_ref[...])
    acc_ref[...] = ALPHA * h + _dot(hs.astype(BF16), sd_ref[...])
    _wait_copies(n_copies, in_copy)

    pos2b = posT_ref[...].astype(BF16)
    g_hi, g_lo = _split(gT_ref[...])
    gate2b = (g_hi.astype(F32) + pltpu.roll(g_lo.astype(F32), N_EXPERTS, 1)).astype(BF16)
    seg_lo = jnp.broadcast_to(locol_ref[0][:, 0:1], (2 * N_EXPERTS, G_BLK))
    seg_hi = jnp.broadcast_to(hicol_ref[0][:, 0:1], (2 * N_EXPERTS, G_BLK))
    n_blk = (rtot_ref[i] * SEG + (G_BLK - 1)) // G_BLK

    def blk_body(cb, carry):
        c0 = pl.multiple_of(cb * G_BLK, G_BLK)
        c_e = (lax.broadcasted_iota(I32, (2 * N_EXPERTS, G_BLK), 1) + c0).astype(F32)
        owner = jnp.where((c_e >= seg_lo) & (c_e < seg_hi), 1.0, 0.0).astype(BF16)
        want = _dot(pos2b, owner)
        gate = _dot(gate2b, owner)
        c_t = (lax.broadcasted_iota(I32, (tm, G_BLK), 1) + (c0 + 1)).astype(F32)
        g = jnp.where(want == c_t, gate, 0.0).astype(BF16)
        acc_ref[...] += _dot(g, ybuf_ref[pl.ds(c0, G_BLK), :])
        return carry

    lax.fori_loop(0, n_blk, blk_body, 0)
    o_ref[...] = _layer_norm(acc_ref[...], lg_ref[...], lb_ref[...])


def _combine(ys, posT, gT, h1, sg, su, sd, lg, lb, plan):
    t, d = h1.shape
    nt = t // MOE_TM
    row = lambda w: pl.BlockSpec((MOE_TM, w), lambda i, *_: (i, 0))
    full = lambda a: pl.BlockSpec(a.shape, lambda i, *_: (0, 0))
    seg = pl.BlockSpec((1, 2 * N_EXPERTS, LANES), lambda i, *_: (i, 0, 0))
    grid_spec = pltpu.PrefetchScalarGridSpec(
        num_scalar_prefetch=4,
        grid=(nt,),
        in_specs=[pl.BlockSpec(memory_space=pl.ANY), row(LANES), row(LANES), seg, seg, row(d),
                  full(sg), full(su), full(sd), full(lg), full(lb)],
        out_specs=row(d),
        scratch_shapes=[pltpu.VMEM((ROW_BUF, d), BF16), pltpu.VMEM((MOE_TM, d), F32),
                        pltpu.SemaphoreType.DMA(())])
    return pl.pallas_call(
        _combine_kernel,
        grid_spec=grid_spec,
        out_shape=jax.ShapeDtypeStruct((t, d), F32),
        compiler_params=_params(("arbitrary",)),
        name="moe_combine",
    )(plan["n16"], plan["so16"], plan["go16"], plan["rtot16"], ys, posT, gT,
      plan["lo_cols"], plan["hi_cols"], h1, sg, su, sd, lg, lb)


def _moe(h1, hb, eidx, gT, cnt, wg, wu, wd, sg, su, sd, lg, lb):
    t = h1.shape[0]
    nt = t // MOE_TM
    max_rows = nt * ROW_BUF + N_EXPERTS * EXP_TM
    n_exp_tiles = pl.cdiv(max_rows, EXP_TM)
    plan = _moe_plan(cnt[:, :, 0].astype(I32), n_exp_tiles)
    xs, posT = _dispatch(hb, eidx, plan, n_exp_tiles * EXP_TM)
    ys = _experts(xs, wg, wu, wd, plan)
    return _combine(ys, posT, gT, h1, sg, su, sd, lg, lb, plan)


def kernel(x, w_in, hg_lb_logits, hg_norm_g, q_norm_g, w_q_up, w_qidx_up, kv_norm_g, w_kv_up,
           idx_ln_g, idx_ln_b, w_out, ln1_g, ln1_b, w_router, router_bias, w_e_gate, w_e_up,
           w_e_down, w_s_gate, w_s_up, w_s_down, ln2_g, ln2_b):
    b, s, d = x.shape
    h = x
    for l in range(DEPTH):
        h2 = h.reshape(b * s, d)
        n_main = 4 * HG_WIDTH
        w_main = w_in[l, :, :n_main].astype(BF16)
        c_kv, c_idx = n_main + Q_RANK, n_main + Q_RANK + KV_RANK
        w_sel = jnp.concatenate([w_in[l, :, n_main:c_kv], w_in[l, :, c_idx:]], axis=1)
        w_sel = jnp.pad(w_sel, ((0, 0), (0, SEL_W - w_sel.shape[1])))
        w_ckv = w_in[l, :, c_kv:c_idx].astype(BF16)

        proj = _proj_main(h2, w_main).reshape(b, s, n_main)
        o_a = _hgrn(proj, hg_lb_logits, hg_norm_g[l], l)

        wqT = w_q_up[l].T.astype(BF16)
        wqiT = w_qidx_up[l].T
        wk = w_kv_up[l][:, :AT_WIDTH].astype(BF16)
        wvT = w_kv_up[l][:, AT_WIDTH:].T.astype(BF16)
        qT, qiT, k, vT, kidx, wT = _dsa_prep(h, w_sel, w_ckv, q_norm_g[l], kv_norm_g[l],
                                             idx_ln_g[l], idx_ln_b[l], wqT, wqiT, wk, wvT)
        o_b = _dsa_attn(qT, qiT, wT, kidx, k, vT)

        h1, hb = _out_ln(o_a.reshape(b * s, HG_WIDTH), o_b.reshape(b * s, AT_WIDTH), h2,
                         w_out[l, :HG_WIDTH].astype(BF16), w_out[l, HG_WIDTH:].astype(BF16),
                         ln1_g[l].reshape(1, d), ln1_b[l].reshape(1, d))
        eidx, gwT, cnt = _router(h1, w_router[l].T, router_bias[l].reshape(N_EXPERTS, 1), MOE_TM)
        out = _moe(h1, hb, eidx, gwT, cnt, w_e_gate[l], w_e_up[l], w_e_down[l],
                   w_s_gate[l].astype(BF16), w_s_up[l].astype(BF16), w_s_down[l].astype(BF16),
                   ln2_g[l].reshape(1, d), ln2_b[l].reshape(1, d))
        h = out.reshape(b, s, d)
    return h
```

```python
import functools

import jax
import jax.numpy as jnp
from jax import lax
from jax.experimental import pallas as pl
from jax.experimental.pallas import tpu as pltpu

F32 = jnp.float32
BF16 = jnp.bfloat16
I32 = jnp.int32
HIGHEST = lax.Precision.HIGHEST

CHUNK = 64
HG_HEADS = 4
HG_DK = 128
HG_DV = 128
HG_WIDTH = HG_HEADS * HG_DV
AT_HEADS = 8
AT_DH = 64
AT_WIDTH = AT_HEADS * AT_DH
Q_RANK = 256
KV_RANK = 128
IDX_HEADS = 4
IDX_DH = 64
TOPK_MAX = 256
N_EXPERTS = 64
TOP_K = 8
N_GROUPS = 8
TOPK_GROUPS = 4
D_EXPERT = 256
ROUTED_SCALE = 2.5
DEPTH = 1
ALPHA = (2.0 * DEPTH) ** 0.25
LN_EPS = 1e-5
RMS_EPS = 1e-6

LANES = 128
SUBLANES = 8
QBLK = 2 * CHUNK
KBLK = 256
VMEM_LIMIT = 56 * 1024 * 1024
INT_MIN = -2 ** 31


def _params(sem, vmem=VMEM_LIMIT):
    return pltpu.CompilerParams(dimension_semantics=sem, vmem_limit_bytes=vmem)


def _dot(a, b, precision=None):
    return jnp.dot(a, b, preferred_element_type=F32, precision=precision)


def _dot_nt(a, b, precision=None):
    return lax.dot_general(a, b, (((1,), (1,)), ((), ())), preferred_element_type=F32,
                           precision=precision)


def _dot_tn(a, b, precision=None):
    return lax.dot_general(a, b, (((0,), (0,)), ((), ())), preferred_element_type=F32,
                           precision=precision)


def _split(x):
    hi = x.astype(BF16)
    return hi, (x - hi.astype(F32)).astype(BF16)


def _cat_hhl(x, axis):
    hi, lo = _split(x)
    return jnp.concatenate([hi, hi, lo], axis=axis)


def _cat_hlh(x, axis):
    hi, lo = _split(x)
    return jnp.concatenate([hi, lo, hi], axis=axis)


def _split3(x):
    hi = x.astype(BF16)
    r = x - hi.astype(F32)
    mid = r.astype(BF16)
    return hi, mid, (r - mid.astype(F32)).astype(BF16)


def _cat_act6(x, axis):
    h, m, l = _split3(x)
    return jnp.concatenate([h, h, m, h, l, m], axis=axis)


def _cat_wgt6(x, axis):
    h, m, l = _split3(x)
    return jnp.concatenate([h, m, h, l, h, m], axis=axis)


def _sigmoid(x):
    return 1.0 / (1.0 + jnp.exp(-x))


def _silu(x):
    return x * _sigmoid(x)


def _proj_main_kernel(x_ref, w_ref, o_ref):
    o_ref[...] = _dot(x_ref[...].astype(BF16), w_ref[...])


def _proj_main(x2, w_bf, tm=256):
    t, d = x2.shape
    n = w_bf.shape[1]
    return pl.pallas_call(
        _proj_main_kernel,
        grid=(t // tm,),
        in_specs=[pl.BlockSpec((tm, d), lambda i: (i, 0)),
                  pl.BlockSpec((d, n), lambda i: (0, 0))],
        out_specs=pl.BlockSpec((tm, n), lambda i: (i, 0)),
        out_shape=jax.ShapeDtypeStruct((t, n), F32),
        compiler_params=_params(("parallel",)),
        name="proj_main",
    )(x2, w_bf)


def _hgrn_kernel(p_ref, lbl_ref, ng_ref, o_ref, st_ref, *, n_chunks, layer):
    @pl.when(pl.program_id(1) == 0)
    def _():
        st_ref[...] = jnp.zeros_like(st_ref)

    lg = lbl_ref[...]
    ex = jnp.exp(lg - jnp.max(lg, axis=0, keepdims=True))
    lb_all = jnp.sum(ex[: layer + 1], axis=0, keepdims=True) / jnp.sum(ex, axis=0, keepdims=True)
    ng = ng_ref[...]

    r = lax.broadcasted_iota(I32, (CHUNK, CHUNK), 0)
    c = lax.broadcasted_iota(I32, (CHUNK, CHUNK), 1)
    causal = c <= r
    tri = jnp.where(causal, 1.0, 0.0).astype(BF16)
    tri2 = jnp.concatenate([tri, tri], axis=1)

    units = [(sq, h) for sq in range(p_ref.shape[0]) for h in range(HG_HEADS)]

    def chunk_body(j, carry):
        rows = pl.ds(pl.multiple_of(j * CHUNK, CHUNK), CHUNK)
        ks, bs = [], []
        for sq, h in units:
            lo = h * HG_DK
            lb = lb_all[:, lo:lo + HG_DK]
            f = lb + (1.0 - lb) * _sigmoid(p_ref[sq, rows, HG_WIDTH + lo:HG_WIDTH + lo + HG_DK])
            lf_hi, lf_lo = _split(jnp.log(f))
            ks.append(1.0 - f)
            bs.append(_dot(tri2, jnp.concatenate([lf_hi, lf_lo], axis=0)))
        scs, ois, vbs = [], [], []
        for (sq, h), k, b in zip(units, ks, bs):
            lo = h * HG_DK
            b_last = b[CHUNK - 1:CHUNK, :]
            q_dec = (_silu(p_ref[sq, rows, lo:lo + HG_DK]) * jnp.exp(b)).astype(BF16)
            k_inv = (k * jnp.exp(-b)).astype(BF16)
            k_dec = (k * jnp.exp(b_last - b)).astype(BF16)
            vb = p_ref[sq, rows, 2 * HG_WIDTH + lo:2 * HG_WIDTH + lo + HG_DV].astype(BF16)
            st = st_ref[sq, h]
            scs.append(jnp.where(causal, _dot_nt(q_dec, k_inv), 0.0).astype(BF16))
            ois.append(_dot_nt(q_dec, st.astype(BF16)))
            st_ref[sq, h] = st * jnp.exp(b_last) + _dot_tn(vb, k_dec)
            vbs.append(vb)
        for (sq, h), sc, oi, vb in zip(units, scs, ois, vbs):
            lo = h * HG_DK
            o = _dot(sc, vb) + oi
            o = o * lax.rsqrt(jnp.mean(o * o, axis=-1, keepdims=True) + RMS_EPS) * ng
            gate = p_ref[sq, rows, 3 * HG_WIDTH + lo:3 * HG_WIDTH + lo + HG_DV]
            o_ref[sq, rows, lo:lo + HG_DV] = (o * _silu(gate)).astype(o_ref.dtype)
        return carry

    lax.fori_loop(0, n_chunks, chunk_body, 0)


def _hgrn(proj, lb_logits, norm_g, layer, ct=512, n_seq=2):
    b, s, w = proj.shape
    n_seq = min(n_seq, b)
    n_chunks = ct // CHUNK
    return pl.pallas_call(
        functools.partial(_hgrn_kernel, n_chunks=n_chunks, layer=layer),
        grid=(b // n_seq, s // ct),
        in_specs=[pl.BlockSpec((n_seq, ct, w), lambda i, j: (i, j, 0)),
                  pl.BlockSpec(lb_logits.shape, lambda i, j: (0, 0)),
                  pl.BlockSpec((1, HG_DV), lambda i, j: (0, 0))],
        out_specs=pl.BlockSpec((n_seq, ct, HG_WIDTH), lambda i, j: (i, j, 0)),
        out_shape=jax.ShapeDtypeStruct((b, s, HG_WIDTH), BF16),
        scratch_shapes=[pltpu.VMEM((n_seq, HG_HEADS, HG_DV, HG_DK), F32)],
        compiler_params=_params(("parallel", "arbitrary")),
        name="hgrn",
    )(proj, lb_logits, norm_g.reshape(1, HG_DV))


SEL_W = Q_RANK + LANES
KIDX_W = 6 * IDX_DH


def _dsa_prep_kernel(x_ref, wsel_ref, wckv_ref, qg_ref, kvg_ref, lng_ref, lnb_ref, wqT_ref, wqiT_ref,
                     wk_ref, wvT_ref, qT_ref, qiT_ref, k_ref, vT_ref, kidx_ref, wT_ref, w6_ref):
    @pl.when((pl.program_id(0) == 0) & (pl.program_id(1) == 0))
    def _():
        w6_ref[...] = _cat_wgt6(wsel_ref[...], 0)

    x = x_ref[...]
    sel = _dot(_cat_act6(x, 1), w6_ref[...])
    cq = sel[:, :Q_RANK]
    cqn = cq * lax.rsqrt(jnp.mean(cq * cq, axis=-1, keepdims=True) + RMS_EPS) * qg_ref[...]
    qT_ref[0] = (_dot_nt(wqT_ref[...], cqn.astype(BF16)) * (AT_DH ** -0.5)).astype(qT_ref.dtype)
    qiT_ref[0] = _dot_nt(_cat_wgt6(wqiT_ref[...], 1), _cat_act6(cqn, 1))
    ckv = _dot(x.astype(BF16), wckv_ref[...])
    ckvn = (ckv * lax.rsqrt(jnp.mean(ckv * ckv, axis=-1, keepdims=True) + RMS_EPS)
            * kvg_ref[...]).astype(BF16)
    k_ref[0] = _dot(ckvn, wk_ref[...]).astype(k_ref.dtype)
    vT_ref[0, 0] = _dot_nt(wvT_ref[...], ckvn).astype(vT_ref.dtype)
    k128 = sel[:, Q_RANK:]
    in_k = lax.broadcasted_iota(I32, (1, LANES), 1) < IDX_DH
    mu = jnp.sum(jnp.where(in_k, k128, 0.0), axis=-1, keepdims=True) * (1.0 / IDX_DH)
    dev = jnp.where(in_k, k128 - mu, 0.0)
    var = jnp.sum(dev * dev, axis=-1, keepdims=True) * (1.0 / IDX_DH)
    y = dev * lax.rsqrt(var + LN_EPS) * lng_ref[...] + lnb_ref[...]
    h, m, l = (p.astype(F32) for p in _split3(y))
    up = lambda p: pltpu.roll(p, IDX_DH, 1)
    kidx_ref[0] = jnp.concatenate([h + up(h), m + up(h), l + up(m)], axis=1).astype(BF16)
    tT = k128.T
    wT_ref[0] = tT[IDX_DH:IDX_DH + SUBLANES, :] * (IDX_HEADS ** -0.5 * IDX_DH ** -0.5)


def _dsa_prep(x, w_sel, w_ckv, q_norm_g, kv_norm_g, idx_ln_g, idx_ln_b, wqT, wqiT, wk, wvT):
    b, s, d = x.shape
    tm = KBLK
    nb = s // tm
    full = lambda a: pl.BlockSpec(a.shape, lambda i, j: (0,) * a.ndim)
    pad_lanes = lambda v: jnp.pad(v.reshape(1, -1), ((0, 0), (0, LANES - v.shape[0])))
    args = (w_sel, w_ckv, q_norm_g.reshape(1, -1), kv_norm_g.reshape(1, -1),
            pad_lanes(idx_ln_g), pad_lanes(idx_ln_b), wqT, wqiT, wk, wvT)
    return pl.pallas_call(
        _dsa_prep_kernel,
        grid=(b, nb),
        in_specs=[pl.BlockSpec((None, tm, d), lambda i, j: (i, j, 0))] + [full(a) for a in args],
        out_specs=[pl.BlockSpec((1, AT_WIDTH, tm), lambda i, j: (i, 0, j)),
                   pl.BlockSpec((1, IDX_HEADS * IDX_DH, tm), lambda i, j: (i, 0, j)),
                   pl.BlockSpec((1, tm, AT_WIDTH), lambda i, j: (i, j, 0)),
                   pl.BlockSpec((1, 1, AT_WIDTH, tm), lambda i, j: (i, j, 0, 0)),
                   pl.BlockSpec((1, tm, KIDX_W), lambda i, j: (i, j, 0)),
                   pl.BlockSpec((1, SUBLANES, tm), lambda i, j: (i, 0, j))],
        out_shape=[jax.ShapeDtypeStruct((b, AT_WIDTH, s), BF16),
                   jax.ShapeDtypeStruct((b, IDX_HEADS * IDX_DH, s), F32),
                   jax.ShapeDtypeStruct((b, s, AT_WIDTH), BF16),
                   jax.ShapeDtypeStruct((b, nb, AT_WIDTH, tm), BF16),
                   jax.ShapeDtypeStruct((b, s, KIDX_W), BF16),
                   jax.ShapeDtypeStruct((b, SUBLANES, s), F32)],
        scratch_shapes=[pltpu.VMEM((6 * d, SEL_W), BF16)],
        compiler_params=_params(("arbitrary", "arbitrary")),
        name="dsa_prep",
    )(x, *args)


def _sortable(x):
    bits = lax.bitcast_convert_type(x, I32)
    return jnp.where(bits < 0, bits ^ jnp.int32(0x7FFFFFFF), bits)


def _dsa_attn_kernel(qT_ref, qiT_ref, wT_ref, kidx_ref, k_ref, vT_ref, o_ref,
                     keys_ref, q2_ref, qbd_ref, acc_ref, m_ref, l_ref, *, k_top):
    qb = pl.program_id(1)
    t0 = qb * QBLK
    e_max = t0 + QBLK
    nkb = (e_max + KBLK - 1) >> 8

    lane = lax.broadcasted_iota(I32, (1, QBLK), 1)
    t_idx = t0 + lane
    end_t = ((t_idx >> 6) + 1) * CHUNK
    srow = lax.broadcasted_iota(I32, (KBLK, QBLK), 0)

    for h in range(IDX_HEADS):
        q2_ref[:, h * QBLK:(h + 1) * QBLK] = _cat_wgt6(qiT_ref[0, h * IDX_DH:(h + 1) * IDX_DH, :], 0)
    wrow = jnp.concatenate([wT_ref[0, h:h + 1, :] for h in range(IDX_HEADS)], axis=1)

    def score_body(kb, carry):
        rows = pl.ds(pl.multiple_of(kb * KBLK, KBLK), KBLK)
        logits = _dot(kidx_ref[0, rows, :], q2_ref[...])
        r = jnp.maximum(logits, 0.0) * wrow
        sc = r[:, :QBLK]
        for h in range(1, IDX_HEADS):
            sc = sc + r[:, h * QBLK:(h + 1) * QBLK]
        valid = (srow + kb * KBLK) < end_t
        keys_ref[rows, :] = _sortable(jnp.where(valid, sc, -jnp.inf))
        return carry

    lax.fori_loop(0, nkb, score_body, 0)

    def count_ge(cand):
        def body(kb, acc):
            rows = pl.ds(pl.multiple_of(kb * KBLK, KBLK), KBLK)
            hit = jnp.where(keys_ref[rows, :] >= cand, 1, 0).astype(I32)
            return acc + jnp.sum(hit.reshape(KBLK // SUBLANES, SUBLANES, QBLK), axis=0)
        acc = lax.fori_loop(0, nkb, body, jnp.zeros((SUBLANES, QBLK), I32))
        return jnp.sum(acc, axis=0, keepdims=True)

    zero = jnp.zeros((1, QBLK), I32)
    thr = jnp.where(count_ge(zero) >= k_top, zero, jnp.full((1, QBLK), INT_MIN, I32))

    def bit_body(i, thr):
        cand = thr | (jnp.int32(1) << (30 - i))
        return jnp.where(count_ge(cand) >= k_top, cand, thr)

    thr = lax.fori_loop(0, 31, bit_body, thr)
    need = (k_top - (count_ge(thr + 1))).astype(F32)

    for j in range(AT_HEADS // 2):
        pair = qT_ref[0, j * LANES:(j + 1) * LANES, :]
        rr = lax.broadcasted_iota(I32, pair.shape, 0)
        zeros = jnp.zeros_like(pair)
        qbd_ref[j, :, :QBLK] = jnp.where(rr < AT_DH, pair, zeros)
        qbd_ref[j, :, QBLK:] = jnp.where(rr >= AT_DH, pair, zeros)
    acc_ref[...] = jnp.zeros_like(acc_ref)
    m_ref[...] = jnp.full_like(m_ref, -jnp.inf)
    l_ref[...] = jnp.zeros_like(l_ref)

    ri = lax.broadcasted_iota(I32, (KBLK, KBLK), 0)
    ci = lax.broadcasted_iota(I32, (KBLK, KBLK), 1)
    lstrict = jnp.where(ci < ri, 1.0, 0.0).astype(BF16)

    def attn_body(kb, carry):
        rows = pl.ds(pl.multiple_of(kb * KBLK, KBLK), KBLK)
        kblk = keys_ref[rows, :]
        s_idx = srow + kb * KBLK
        eq = kblk == thr
        eqf = jnp.where(eq, 1.0, 0.0)
        before = _dot(lstrict, eqf.astype(BF16)) + carry
        sel = ((kblk > thr) | (eq & (before < need))) & (s_idx < end_t)
        bias = jnp.where(sel, 0.0, -jnp.inf)
        dist = jnp.abs(t_idx - s_idx).astype(F32)
        s2s = [_dot(k_ref[0, rows, j * LANES:(j + 1) * LANES], qbd_ref[j])
               for j in range(AT_HEADS // 2)]
        for j in range(AT_HEADS // 2):
            ps, alphas = [], []
            for half in range(2):
                h = 2 * j + half
                slope = 2.0 ** (-8.0 * (h + 1) / AT_HEADS)
                st = s2s[j][:, half * QBLK:(half + 1) * QBLK] - slope * dist + bias
                m_old = m_ref[h:h + 1, :]
                m_new = jnp.maximum(m_old, jnp.max(st, axis=0, keepdims=True))
                m_safe = jnp.where(m_new == -jnp.inf, 0.0, m_new)
                alpha = jnp.exp(m_old - m_safe)
                p = jnp.exp(st - m_safe)
                l_ref[h:h + 1, :] = alpha * l_ref[h:h + 1, :] + jnp.sum(p, axis=0, keepdims=True)
                m_ref[h:h + 1, :] = m_new
                ps.append(p.astype(BF16))
                alphas.append(alpha)
            o2 = _dot(vT_ref[0, kb, j * LANES:(j + 1) * LANES, :], jnp.concatenate(ps, axis=1))
            for half in range(2):
                hs = slice((2 * j + half) * AT_DH, (2 * j + half + 1) * AT_DH)
                acc_ref[hs, :] = (alphas[half] * acc_ref[hs, :]
                                  + o2[half * AT_DH:(half + 1) * AT_DH, half * QBLK:(half + 1) * QBLK])
        return carry + jnp.sum(eqf, axis=0, keepdims=True)

    lax.fori_loop(0, nkb, attn_body, jnp.zeros((1, QBLK), F32))

    for h in range(AT_HEADS):
        hs = slice(h * AT_DH, (h + 1) * AT_DH)
        acc_ref[hs, :] = acc_ref[hs, :] / l_ref[h:h + 1, :]
    o_ref[0] = acc_ref[...].T.astype(o_ref.dtype)


def _dsa_attn(qT, qiT, wT, kidx, k, vT):
    b, _, s = qT.shape
    nb = s // KBLK
    k_top = min(TOPK_MAX, s // 4)
    return pl.pallas_call(
        functools.partial(_dsa_attn_kernel, k_top=k_top),
        grid=(b, s // QBLK),
        in_specs=[pl.BlockSpec((1, AT_WIDTH, QBLK), lambda i, j: (i, 0, j)),
                  pl.BlockSpec((1, IDX_HEADS * IDX_DH, QBLK), lambda i, j: (i, 0, j)),
                  pl.BlockSpec((1, SUBLANES, QBLK), lambda i, j: (i, 0, j)),
                  pl.BlockSpec((1, s, KIDX_W), lambda i, j: (i, 0, 0)),
                  pl.BlockSpec((1, s, AT_WIDTH), lambda i, j: (i, 0, 0)),
                  pl.BlockSpec((1, nb, AT_WIDTH, KBLK), lambda i, j: (i, 0, 0, 0))],
        out_specs=pl.BlockSpec((1, QBLK, AT_WIDTH), lambda i, j: (i, j, 0)),
        out_shape=jax.ShapeDtypeStruct((b, s, AT_WIDTH), BF16),
        scratch_shapes=[pltpu.VMEM((s, QBLK), I32),
                        pltpu.VMEM((KIDX_W, IDX_HEADS * QBLK), BF16),
                        pltpu.VMEM((AT_HEADS // 2, LANES, 2 * QBLK), BF16),
                        pltpu.VMEM((AT_WIDTH, QBLK), F32),
                        pltpu.VMEM((AT_HEADS, QBLK), F32),
                        pltpu.VMEM((AT_HEADS, QBLK), F32)],
        compiler_params=_params(("parallel", "arbitrary")),
        name="dsa_attn",
    )(qT, qiT, wT, kidx, k, vT)


def _layer_norm(v, g, b):
    mu = jnp.mean(v, axis=-1, keepdims=True)
    var = jnp.mean(jnp.square(v - mu), axis=-1, keepdims=True)
    return (v - mu) * lax.rsqrt(var + LN_EPS) * g + b


def _out_ln_kernel(oa_ref, ob_ref, x_ref, wa_ref, wb_ref, g_ref, b_ref, h_ref, hb_ref):
    mix = _dot(oa_ref[...], wa_ref[...]) + _dot(ob_ref[...], wb_ref[...])
    h = _layer_norm(ALPHA * x_ref[...] + mix, g_ref[...], b_ref[...])
    h_ref[...] = h
    hb_ref[...] = h.astype(BF16)


def _out_ln(oa, ob, x2, wa, wb, g, b, tm=512):
    t, d = x2.shape
    row = lambda w: pl.BlockSpec((tm, w), lambda i: (i, 0))
    full = lambda a: pl.BlockSpec(a.shape, lambda i: (0, 0))
    return pl.pallas_call(
        _out_ln_kernel,
        grid=(t // tm,),
        in_specs=[row(oa.shape[1]), row(ob.shape[1]), row(d), full(wa), full(wb), full(g), full(b)],
        out_specs=[row(d), row(d)],
        out_shape=[jax.ShapeDtypeStruct((t, d), F32), jax.ShapeDtypeStruct((t, d), BF16)],
        compiler_params=_params(("parallel",)),
        name="out_ln",
    )(oa, ob, x2, wa, wb, g, b)


def _first_max(v, idx, big):
    m = jnp.max(v, axis=0, keepdims=True)
    first = jnp.min(jnp.where(v == m, idx, big), axis=0, keepdims=True)
    return m, first


def _router_kernel(h_ref, wrT_ref, bias_ref, eidx_ref, gT_ref, cnt_ref):
    tm = h_ref.shape[0]
    per_group = N_EXPERTS // N_GROUPS
    logits = _dot_nt(_cat_hlh(wrT_ref[...], 1), _cat_hhl(h_ref[...], 1))
    scores = _sigmoid(logits)
    sel = scores + bias_ref[...]
    iota_g = lax.broadcasted_iota(I32, (per_group, tm), 0)
    neg = -jnp.inf

    grp = jnp.zeros((N_GROUPS, tm), F32)
    iota_grp = lax.broadcasted_iota(I32, (N_GROUPS, tm), 0)
    for g in range(N_GROUPS):
        blk = sel[g * per_group:(g + 1) * per_group, :]
        m1, i1 = _first_max(blk, iota_g, per_group)
        m2 = jnp.max(jnp.where(iota_g == i1, neg, blk), axis=0, keepdims=True)
        grp = jnp.where(iota_grp == g, m1 + m2, grp)

    gmask = jnp.zeros((N_GROUPS, tm), F32)
    work = grp
    for _ in range(TOPK_GROUPS):
        _, gi = _first_max(work, iota_grp, N_GROUPS)
        hit = iota_grp == gi
        gmask = jnp.where(hit, 1.0, gmask)
        work = jnp.where(hit, neg, work)

    e_mask = jnp.concatenate(
        [jnp.broadcast_to(gmask[g:g + 1, :], (per_group, tm)) for g in range(N_GROUPS)], axis=0)
    iota_e = lax.broadcasted_iota(I32, (N_EXPERTS, tm), 0)
    work = jnp.where(e_mask > 0.5, sel, neg)
    iota_k = lax.broadcasted_iota(I32, (TOP_K, tm), 0)
    eidx = jnp.zeros((TOP_K, tm), I32)
    w = jnp.zeros((N_EXPERTS, tm), F32)
    chosen = jnp.zeros((N_EXPERTS, tm), F32)
    for k in range(TOP_K):
        _, ei = _first_max(work, iota_e, N_EXPERTS)
        hit = iota_e == ei
        eidx = jnp.where(iota_k == k, ei, eidx)
        w = jnp.where(hit, scores, w)
        chosen = jnp.where(hit, 1.0, chosen)
        work = jnp.where(hit, neg, work)

    gates = w / jnp.sum(w, axis=0, keepdims=True) * ROUTED_SCALE
    eidx_ref[...] = eidx
    gates = jnp.concatenate([gates, jnp.zeros((LANES - N_EXPERTS, tm), F32)], axis=0)
    gT_ref[...] = gates.T
    cnt_ref[0] = jnp.broadcast_to(jnp.sum(chosen, axis=1, keepdims=True), (N_EXPERTS, LANES))


def _router(h1, wrT, bias_col, tm):
    t, d = h1.shape
    nt = t // tm
    return pl.pallas_call(
        _router_kernel,
        grid=(nt,),
        in_specs=[pl.BlockSpec((tm, d), lambda i: (i, 0)),
                  pl.BlockSpec(wrT.shape, lambda i: (0, 0)),
                  pl.BlockSpec(bias_col.shape, lambda i: (0, 0))],
        out_specs=[pl.BlockSpec((TOP_K, tm), lambda i: (0, i)),
                   pl.BlockSpec((tm, LANES), lambda i: (i, 0)),
                   pl.BlockSpec((1, N_EXPERTS, LANES), lambda i: (i, 0, 0))],
        out_shape=[jax.ShapeDtypeStruct((TOP_K, t), I32),
                   jax.ShapeDtypeStruct((t, LANES), F32),
                   jax.ShapeDtypeStruct((nt, N_EXPERTS, LANES), F32)],
        compiler_params=_params(("parallel",)),
        name="router",
    )(h1, wrT, bias_col)


MOE_TM = 512
SEG = 16
EXP_TM = 512
ROW_BUF = TOP_K * MOE_TM + N_EXPERTS * SEG
P_BLK = 256
G_BLK = 512


def _moe_plan(cnt, n_exp_tiles):
    nt = cnt.shape[0]
    n16 = (cnt + (SEG - 1)) // SEG
    so16 = jnp.cumsum(n16, axis=1) - n16
    r16 = jnp.sum(n16, axis=0)
    per = EXP_TM // SEG
    rp16 = (r16 + (per - 1)) // per * per
    ends = jnp.cumsum(rp16)
    off16 = ends - rp16
    go16 = off16[None, :] + jnp.cumsum(n16, axis=0) - n16
    n_et = (ends[-1] // per).astype(I32).reshape(1)
    tiles = jnp.arange(n_exp_tiles, dtype=I32)
    tile_expert = jnp.sum((ends // per)[None, :] <= tiles[:, None], axis=1)
    tile_expert = jnp.minimum(tile_expert, N_EXPERTS - 1).astype(I32)
    flat = lambda a: a.reshape(nt * N_EXPERTS).astype(I32)
    twice = lambda a: jnp.concatenate([a, a], axis=1).astype(F32)
    seg_lo, seg_hi = twice(so16 * SEG), twice((so16 + n16) * SEG)
    as_rows = lambda a: jnp.broadcast_to(a[:, None, :], (nt, SUBLANES, 2 * N_EXPERTS))
    as_cols = lambda a: jnp.broadcast_to(a[:, :, None], (nt, 2 * N_EXPERTS, LANES))
    return dict(n16=flat(n16), so16=flat(so16), go16=flat(go16),
                rtot16=jnp.sum(n16, axis=1).astype(I32), r16=r16.astype(I32), rp16=rp16.astype(I32),
                off16=off16.astype(I32), n_et=n_et, tile_expert=tile_expert,
                lo_rows=as_rows(seg_lo), hi_rows=as_rows(seg_hi),
                lo_cols=as_cols(seg_lo), hi_cols=as_cols(seg_hi))


def _chunk(ref, c):
    return ref.at[pl.ds(pl.multiple_of(c * SEG, SEG), SEG)]


def _segment_copies(i, n16_ref, so16_ref, go16_ref, make_copy):
    def expert_body(e, total):
        n = n16_ref[i * N_EXPERTS + e]
        so = so16_ref[i * N_EXPERTS + e]
        go = go16_ref[i * N_EXPERTS + e]

        def chunk_body(c, carry):
            make_copy(so + c, go + c).start()
            return carry

        lax.fori_loop(0, n, chunk_body, 0)
        return total + n

    return lax.fori_loop(0, N_EXPERTS, expert_body, 0)


def _wait_copies(count, make_copy):
    def body(c, carry):
        make_copy(0, 0).wait()
        return carry

    lax.fori_loop(0, count, body, 0)


POS_SPLIT = 64


def _dispatch_kernel(n16_ref, so16_ref, go16_ref, rtot_ref, r16_ref, rp16_ref, off16_ref,
                     x_ref, eidx_ref, locol_ref, lorow_ref, hirow_ref,
                     xs_hbm, posT_ref, buf_ref, zero_ref, pend_ref, sem):
    i = pl.program_id(0)
    slot = lax.rem(i, 2)
    buf = buf_ref.at[slot]
    tm = x_ref.shape[0]
    eidx = eidx_ref[...]
    iota_e = lax.broadcasted_iota(I32, (N_EXPERTS, tm), 0)
    onehot = jnp.zeros((N_EXPERTS, tm), F32)
    for k in range(TOP_K):
        onehot = jnp.where(iota_e == eidx[k:k + 1, :], 1.0, onehot)
    earlier = lax.broadcasted_iota(I32, (tm, tm), 0) < lax.broadcasted_iota(I32, (tm, tm), 1)
    rank = _dot(onehot.astype(BF16), jnp.where(earlier, 1.0, 0.0).astype(BF16))
    pos1 = jnp.where(onehot > 0.5, locol_ref[0][:N_EXPERTS, 0:1] + rank + 1.0, 0.0)
    pos_hi = jnp.floor(pos1 * (1.0 / POS_SPLIT)) * POS_SPLIT
    pos2 = jnp.concatenate([pos_hi, pos1 - pos_hi], axis=0)
    posT_ref[...] = pos2.T
    pos2b = pos2.astype(BF16)

    x = x_ref[...]
    seg_lo = lorow_ref[0][0:1, :]
    seg_hi = hirow_ref[0][0:1, :]
    n_blk = (rtot_ref[i] * SEG + (P_BLK - 1)) // P_BLK

    def blk_body(rb, carry):
        r0 = pl.multiple_of(rb * P_BLK, P_BLK)
        r_e = (lax.broadcasted_iota(I32, (P_BLK, 2 * N_EXPERTS), 0) + r0).astype(F32)
        owner = jnp.where((r_e >= seg_lo) & (r_e < seg_hi), 1.0, 0.0).astype(BF16)
        want = _dot(owner, pos2b)
        r_t = (lax.broadcasted_iota(I32, (P_BLK, tm), 0) + (r0 + 1)).astype(F32)
        p = jnp.where(want == r_t, 1.0, 0.0).astype(BF16)
        buf[pl.ds(r0, P_BLK), :] = _dot(p, x).astype(BF16)
        return carry

    lax.fori_loop(0, n_blk, blk_body, 0)

    @pl.when(i > 0)
    def _():
        prev_copy = lambda src, dst: pltpu.make_async_copy(
            _chunk(buf_ref.at[1 - slot], src), _chunk(xs_hbm, dst), sem.at[1 - slot])
        _wait_copies(pend_ref[0], prev_copy)

    out_copy = lambda src, dst: pltpu.make_async_copy(_chunk(buf, src), _chunk(xs_hbm, dst),
                                                      sem.at[slot])
    pend_ref[0] = _segment_copies(i, n16_ref, so16_ref, go16_ref, out_copy)

    @pl.when(i == pl.num_programs(0) - 1)
    def _():
        _wait_copies(pend_ref[0], out_copy)
        zero_ref[...] = jnp.zeros_like(zero_ref)
        zero_copy = lambda src, dst: pltpu.make_async_copy(zero_ref, _chunk(xs_hbm, dst),
                                                           sem.at[slot])

        def zero_chunk(c, carry):
            zero_copy(0, c).start()
            return carry

        def tail_body(e, total):
            lax.fori_loop(off16_ref[e] + r16_ref[e], off16_ref[e] + rp16_ref[e], zero_chunk, 0)
            return total + rp16_ref[e] - r16_ref[e]

        n_tail = lax.fori_loop(0, N_EXPERTS, tail_body, 0)
        used = off16_ref[N_EXPERTS - 1] + rp16_ref[N_EXPERTS - 1]
        n_all = xs_hbm.shape[0] // SEG
        lax.fori_loop(used, n_all, zero_chunk, 0)
        _wait_copies(n_tail + n_all - used, zero_copy)


def _dispatch(hb, eidx, plan, n_rows):
    t, d = hb.shape
    nt = t // MOE_TM
    grid_spec = pltpu.PrefetchScalarGridSpec(
        num_scalar_prefetch=7,
        grid=(nt,),
        in_specs=[pl.BlockSpec((MOE_TM, d), lambda i, *_: (i, 0)),
                  pl.BlockSpec((TOP_K, MOE_TM), lambda i, *_: (0, i)),
                  pl.BlockSpec((1, 2 * N_EXPERTS, LANES), lambda i, *_: (i, 0, 0)),
                  pl.BlockSpec((1, SUBLANES, 2 * N_EXPERTS), lambda i, *_: (i, 0, 0)),
                  pl.BlockSpec((1, SUBLANES, 2 * N_EXPERTS), lambda i, *_: (i, 0, 0))],
        out_specs=[pl.BlockSpec(memory_space=pl.ANY),
                   pl.BlockSpec((MOE_TM, LANES), lambda i, *_: (i, 0))],
        scratch_shapes=[pltpu.VMEM((2, ROW_BUF, d), BF16), pltpu.VMEM((SEG, d), BF16),
                        pltpu.SMEM((1,), I32), pltpu.SemaphoreType.DMA((2,))])
    return pl.pallas_call(
        _dispatch_kernel,
        grid_spec=grid_spec,
        out_shape=[jax.ShapeDtypeStruct((n_rows, d), BF16),
                   jax.ShapeDtypeStruct((t, LANES), F32)],
        compiler_params=_params(("arbitrary",)),
        name="moe_dispatch",
    )(plan["n16"], plan["so16"], plan["go16"], plan["rtot16"], plan["r16"], plan["rp16"],
      plan["off16"], hb, eidx, plan["lo_cols"], plan["lo_rows"], plan["hi_rows"])


def _experts_kernel(te_ref, net_ref, x_ref, wg_ref, wu_ref, wd_ref, y_ref, wgb_ref, wub_ref, wdb_ref):
    j = pl.program_id(0)
    used = j < net_ref[0]

    @pl.when(used & ((j == 0) | (te_ref[j] != te_ref[jnp.maximum(j - 1, 0)])))
    def _():
        wgb_ref[...] = wg_ref[0].astype(BF16)
        wub_ref[...] = wu_ref[0].astype(BF16)
        wdb_ref[...] = wd_ref[0].astype(BF16)

    @pl.when(used)
    def _():
        x = x_ref[...]
        hmid = _silu(_dot(x, wgb_ref[...])) * _dot(x, wub_ref[...])
        y_ref[...] = _dot(hmid.astype(BF16), wdb_ref[...]).astype(y_ref.dtype)

    @pl.when(jnp.logical_not(used))
    def _():
        y_ref[...] = jnp.zeros_like(y_ref)


def _experts(xs, wg, wu, wd, plan):
    n_rows, d = xs.shape
    x_map = lambda j, te, net: (jnp.minimum(j, net[0] - 1), 0)
    w_map = lambda j, te, net: (te[jnp.minimum(j, net[0] - 1)], 0, 0)
    grid_spec = pltpu.PrefetchScalarGridSpec(
        num_scalar_prefetch=2,
        grid=(n_rows // EXP_TM,),
        in_specs=[pl.BlockSpec((EXP_TM, d), x_map),
                  pl.BlockSpec((1, d, D_EXPERT), w_map),
                  pl.BlockSpec((1, d, D_EXPERT), w_map),
                  pl.BlockSpec((1, D_EXPERT, d), w_map)],
        out_specs=pl.BlockSpec((EXP_TM, d), lambda j, te, net: (j, 0)),
        scratch_shapes=[pltpu.VMEM((d, D_EXPERT), BF16), pltpu.VMEM((d, D_EXPERT), BF16),
                        pltpu.VMEM((D_EXPERT, d), BF16)])
    return pl.pallas_call(
        _experts_kernel,
        grid_spec=grid_spec,
        out_shape=jax.ShapeDtypeStruct((n_rows, d), BF16),
        compiler_params=_params(("arbitrary",)),
        name="moe_experts",
    )(plan["tile_expert"], plan["n_et"], xs, wg, wu, wd)


def _combine_kernel(n16_ref, so16_ref, go16_ref, rtot_ref,
                    ys_hbm, posT_ref, gT_ref, locol_ref, hicol_ref, h_ref,
                    sg_ref, su_ref, sd_ref, lg_ref, lb_ref, o_ref, ybuf_ref, acc_ref, pend_ref, sem):
    i = pl.program_id(0)
    slot = lax.rem(i, 2)
    ybuf = ybuf_ref.at[slot]
    tm = h_ref.shape[0]

    def fetch(tile, to_slot):
        copy = lambda dst, src: pltpu.make_async_copy(
            _chunk(ys_hbm, src), _chunk(ybuf_ref.at[to_slot], dst), sem.at[to_slot])
        pend_ref[to_slot] = _segment_copies(tile, n16_ref, so16_ref, go16_ref, copy)

    @pl.when(i == 0)
    def _():
        ybuf_ref[...] = jnp.zeros_like(ybuf_ref)
        fetch(0, 0)

    @pl.when(i + 1 < pl.num_programs(0))
    def _():
        fetch(i + 1, 1 - slot)

    h = h_ref[...]
    xb = h.astype(BF16)
    hs = _silu(_dot(xb, sg_ref[...])) * _dot(xb, su_ref[...])
    acc_ref[...] = ALPHA * h + _dot(hs.astype(BF16), sd_ref[...])
    _wait_copies(pend_ref[slot], lambda dst, src: pltpu.make_async_copy(
        _chunk(ys_hbm, src), _chunk(ybuf, dst), sem.at[slot]))

    pos2b = posT_ref[...].astype(BF16)
    g_hi, g_lo = _split(gT_ref[...])
    gate2b = (g_hi.astype(F32) + pltpu.roll(g_lo.astype(F32), N_EXPERTS, 1)).astype(BF16)
    seg_lo = jnp.broadcast_to(locol_ref[0][:, 0:1], (2 * N_EXPERTS, G_BLK))
    seg_hi = jnp.broadcast_to(hicol_ref[0][:, 0:1], (2 * N_EXPERTS, G_BLK))
    n_blk = (rtot_ref[i] * SEG + (G_BLK - 1)) // G_BLK

    def blk_body(cb, carry):
        c0 = pl.multiple_of(cb * G_BLK, G_BLK)
        c_e = (lax.broadcasted_iota(I32, (2 * N_EXPERTS, G_BLK), 1) + c0).astype(F32)
        owner = jnp.where((c_e >= seg_lo) & (c_e < seg_hi), 1.0, 0.0).astype(BF16)
        want = _dot(pos2b, owner)
        gate = _dot(gate2b, owner)
        c_t = (lax.broadcasted_iota(I32, (tm, G_BLK), 1) + (c0 + 1)).astype(F32)
        g = jnp.where(want == c_t, gate, 0.0).astype(BF16)
        acc_ref[...] += _dot(g, ybuf[pl.ds(c0, G_BLK), :])
        return carry

    lax.fori_loop(0, n_blk, blk_body, 0)
    o_ref[...] = _layer_norm(acc_ref[...], lg_ref[...], lb_ref[...])


def _combine(ys, posT, gT, h1, sg, su, sd, lg, lb, plan):
    t, d = h1.shape
    nt = t // MOE_TM
    row = lambda w: pl.BlockSpec((MOE_TM, w), lambda i, *_: (i, 0))
    full = lambda a: pl.BlockSpec(a.shape, lambda i, *_: (0, 0))
    seg = pl.BlockSpec((1, 2 * N_EXPERTS, LANES), lambda i, *_: (i, 0, 0))
    grid_spec = pltpu.PrefetchScalarGridSpec(
        num_scalar_prefetch=4,
        grid=(nt,),
        in_specs=[pl.BlockSpec(memory_space=pl.ANY), row(LANES), row(LANES), seg, seg, row(d),
                  full(sg), full(su), full(sd), full(lg), full(lb)],
        out_specs=row(d),
        scratch_shapes=[pltpu.VMEM((2, ROW_BUF, d), BF16), pltpu.VMEM((MOE_TM, d), F32),
                        pltpu.SMEM((2,), I32), pltpu.SemaphoreType.DMA((2,))])
    return pl.pallas_call(
        _combine_kernel,
        grid_spec=grid_spec,
        out_shape=jax.ShapeDtypeStruct((t, d), F32),
        compiler_params=_params(("arbitrary",)),
        name="moe_combine",
    )(plan["n16"], plan["so16"], plan["go16"], plan["rtot16"], ys, posT, gT,
      plan["lo_cols"], plan["hi_cols"], h1, sg, su, sd, lg, lb)


def _moe(h1, hb, eidx, gT, cnt, wg, wu, wd, sg, su, sd, lg, lb):
    t = h1.shape[0]
    nt = t // MOE_TM
    max_rows = nt * ROW_BUF + N_EXPERTS * EXP_TM
    n_exp_tiles = pl.cdiv(max_rows, EXP_TM)
    plan = _moe_plan(cnt[:, :, 0].astype(I32), n_exp_tiles)
    xs, posT = _dispatch(hb, eidx, plan, n_exp_tiles * EXP_TM)
    ys = _experts(xs, wg, wu, wd, plan)
    return _combine(ys, posT, gT, h1, sg, su, sd, lg, lb, plan)


def kernel(x, w_in, hg_lb_logits, hg_norm_g, q_norm_g, w_q_up, w_qidx_up, kv_norm_g, w_kv_up,
           idx_ln_g, idx_ln_b, w_out, ln1_g, ln1_b, w_router, router_bias, w_e_gate, w_e_up,
           w_e_down, w_s_gate, w_s_up, w_s_down, ln2_g, ln2_b):
    b, s, d = x.shape
    h = x
    for l in range(DEPTH):
        h2 = h.reshape(b * s, d)
        n_main = 4 * HG_WIDTH
        w_main = w_in[l, :, :n_main].astype(BF16)
        c_kv, c_idx = n_main + Q_RANK, n_main + Q_RANK + KV_RANK
        w_sel = jnp.concatenate([w_in[l, :, n_main:c_kv], w_in[l, :, c_idx:]], axis=1)
        w_sel = jnp.pad(w_sel, ((0, 0), (0, SEL_W - w_sel.shape[1])))
        w_ckv = w_in[l, :, c_kv:c_idx].astype(BF16)

        proj = _proj_main(h2, w_main).reshape(b, s, n_main)
        o_a = _hgrn(proj, hg_lb_logits, hg_norm_g[l], l)

        wqT = w_q_up[l].T.astype(BF16)
        wqiT = w_qidx_up[l].T
        wk = w_kv_up[l][:, :AT_WIDTH].astype(BF16)
        wvT = w_kv_up[l][:, AT_WIDTH:].T.astype(BF16)
        qT, qiT, k, vT, kidx, wT = _dsa_prep(h, w_sel, w_ckv, q_norm_g[l], kv_norm_g[l],
                                             idx_ln_g[l], idx_ln_b[l], wqT, wqiT, wk, wvT)
        o_b = _dsa_attn(qT, qiT, wT, kidx, k, vT)

        h1, hb = _out_ln(o_a.reshape(b * s, HG_WIDTH), o_b.reshape(b * s, AT_WIDTH), h2,
                         w_out[l, :HG_WIDTH].astype(BF16), w_out[l, HG_WIDTH:].astype(BF16),
                         ln1_g[l].reshape(1, d), ln1_b[l].reshape(1, d))
        eidx, gwT, cnt = _router(h1, w_router[l].T, router_bias[l].reshape(N_EXPERTS, 1), MOE_TM)
        out = _moe(h1, hb, eidx, gwT, cnt, w_e_gate[l], w_e_up[l], w_e_down[l],
                   w_s_gate[l].astype(BF16), w_s_up[l].astype(BF16), w_s_down[l].astype(BF16),
                   ln2_g[l].reshape(1, d), ln2_b[l].reshape(1, d))
        h = out.reshape(b, s, d)
    return h
```

```python
import functools

import jax
import jax.numpy as jnp
from jax import lax
from jax.experimental import pallas as pl
from jax.experimental.pallas import tpu as pltpu

F32 = jnp.float32
BF16 = jnp.bfloat16
I32 = jnp.int32
HIGHEST = lax.Precision.HIGHEST

CHUNK = 64
HG_HEADS = 4
HG_DK = 128
HG_DV = 128
HG_WIDTH = HG_HEADS * HG_DV
AT_HEADS = 8
AT_DH = 64
AT_WIDTH = AT_HEADS * AT_DH
Q_RANK = 256
KV_RANK = 128
IDX_HEADS = 4
IDX_DH = 64
TOPK_MAX = 256
N_EXPERTS = 64
TOP_K = 8
N_GROUPS = 8
TOPK_GROUPS = 4
D_EXPERT = 256
ROUTED_SCALE = 2.5
DEPTH = 1
ALPHA = (2.0 * DEPTH) ** 0.25
LN_EPS = 1e-5
RMS_EPS = 1e-6

LANES = 128
SUBLANES = 8
QBLK = 4 * CHUNK
KBLK = 256
VMEM_LIMIT = 56 * 1024 * 1024
INT_MIN = -2 ** 31


def _params(sem, vmem=VMEM_LIMIT):
    return pltpu.CompilerParams(dimension_semantics=sem, vmem_limit_bytes=vmem)


def _dot(a, b, precision=None):
    return jnp.dot(a, b, preferred_element_type=F32, precision=precision)


def _dot_nt(a, b, precision=None):
    return lax.dot_general(a, b, (((1,), (1,)), ((), ())), preferred_element_type=F32,
                           precision=precision)


def _dot_tn(a, b, precision=None):
    return lax.dot_general(a, b, (((0,), (0,)), ((), ())), preferred_element_type=F32,
                           precision=precision)


def _split(x):
    hi = x.astype(BF16)
    return hi, (x - hi.astype(F32)).astype(BF16)


def _cat_hhl(x, axis):
    hi, lo = _split(x)
    return jnp.concatenate([hi, hi, lo], axis=axis)


def _cat_hlh(x, axis):
    hi, lo = _split(x)
    return jnp.concatenate([hi, lo, hi], axis=axis)


def _split3(x):
    hi = x.astype(BF16)
    r = x - hi.astype(F32)
    mid = r.astype(BF16)
    return hi, mid, (r - mid.astype(F32)).astype(BF16)


def _cat_act6(x, axis):
    h, m, l = _split3(x)
    return jnp.concatenate([h, h, m, h, l, m], axis=axis)


def _cat_wgt6(x, axis):
    h, m, l = _split3(x)
    return jnp.concatenate([h, m, h, l, h, m], axis=axis)


def _sigmoid(x):
    return 1.0 / (1.0 + jnp.exp(-x))


def _silu(x):
    return x * _sigmoid(x)


def _proj_main_kernel(x_ref, w_ref, o_ref):
    o_ref[...] = _dot(x_ref[...].astype(BF16), w_ref[...])


def _proj_main(x2, w_bf, tm=256):
    t, d = x2.shape
    n = w_bf.shape[1]
    return pl.pallas_call(
        _proj_main_kernel,
        grid=(t // tm,),
        in_specs=[pl.BlockSpec((tm, d), lambda i: (i, 0)),
                  pl.BlockSpec((d, n), lambda i: (0, 0))],
        out_specs=pl.BlockSpec((tm, n), lambda i: (i, 0)),
        out_shape=jax.ShapeDtypeStruct((t, n), F32),
        compiler_params=_params(("parallel",)),
        name="proj_main",
    )(x2, w_bf)


def _hgrn_kernel(p_ref, lbl_ref, ng_ref, o_ref, st_ref, *, n_chunks, layer):
    @pl.when(pl.program_id(1) == 0)
    def _():
        st_ref[...] = jnp.zeros_like(st_ref)

    lg = lbl_ref[...]
    ex = jnp.exp(lg - jnp.max(lg, axis=0, keepdims=True))
    lb_all = jnp.sum(ex[: layer + 1], axis=0, keepdims=True) / jnp.sum(ex, axis=0, keepdims=True)
    ng = ng_ref[...]

    r = lax.broadcasted_iota(I32, (CHUNK, CHUNK), 0)
    c = lax.broadcasted_iota(I32, (CHUNK, CHUNK), 1)
    causal = c <= r
    tri = jnp.where(causal, 1.0, 0.0).astype(BF16)
    tri2 = jnp.concatenate([tri, tri], axis=1)

    units = [(sq, h) for sq in range(p_ref.shape[0]) for h in range(HG_HEADS)]

    def chunk_body(j, carry):
        rows = pl.ds(pl.multiple_of(j * CHUNK, CHUNK), CHUNK)
        ks, bs = [], []
        for sq, h in units:
            lo = h * HG_DK
            lb = lb_all[:, lo:lo + HG_DK]
            f = lb + (1.0 - lb) * _sigmoid(p_ref[sq, rows, HG_WIDTH + lo:HG_WIDTH + lo + HG_DK])
            lf_hi, lf_lo = _split(jnp.log(f))
            ks.append(1.0 - f)
            bs.append(_dot(tri2, jnp.concatenate([lf_hi, lf_lo], axis=0)))
        scs, ois, vbs = [], [], []
        for (sq, h), k, b in zip(units, ks, bs):
            lo = h * HG_DK
            b_last = b[CHUNK - 1:CHUNK, :]
            q_dec = (_silu(p_ref[sq, rows, lo:lo + HG_DK]) * jnp.exp(b)).astype(BF16)
            k_inv = (k * jnp.exp(-b)).astype(BF16)
            k_dec = (k * jnp.exp(b_last - b)).astype(BF16)
            vb = p_ref[sq, rows, 2 * HG_WIDTH + lo:2 * HG_WIDTH + lo + HG_DV].astype(BF16)
            st = st_ref[sq, h]
            scs.append(jnp.where(causal, _dot_nt(q_dec, k_inv), 0.0).astype(BF16))
            ois.append(_dot_nt(q_dec, st.astype(BF16)))
            st_ref[sq, h] = st * jnp.exp(b_last) + _dot_tn(vb, k_dec)
            vbs.append(vb)
        for (sq, h), sc, oi, vb in zip(units, scs, ois, vbs):
            lo = h * HG_DK
            o = _dot(sc, vb) + oi
            o = o * lax.rsqrt(jnp.mean(o * o, axis=-1, keepdims=True) + RMS_EPS) * ng
            gate = p_ref[sq, rows, 3 * HG_WIDTH + lo:3 * HG_WIDTH + lo + HG_DV]
            o_ref[sq, rows, lo:lo + HG_DV] = (o * _silu(gate)).astype(o_ref.dtype)
        return carry

    lax.fori_loop(0, n_chunks, chunk_body, 0)


def _hgrn(proj, lb_logits, norm_g, layer, ct=512, n_seq=2):
    b, s, w = proj.shape
    n_seq = min(n_seq, b)
    n_chunks = ct // CHUNK
    return pl.pallas_call(
        functools.partial(_hgrn_kernel, n_chunks=n_chunks, layer=layer),
        grid=(b // n_seq, s // ct),
        in_specs=[pl.BlockSpec((n_seq, ct, w), lambda i, j: (i, j, 0)),
                  pl.BlockSpec(lb_logits.shape, lambda i, j: (0, 0)),
                  pl.BlockSpec((1, HG_DV), lambda i, j: (0, 0))],
        out_specs=pl.BlockSpec((n_seq, ct, HG_WIDTH), lambda i, j: (i, j, 0)),
        out_shape=jax.ShapeDtypeStruct((b, s, HG_WIDTH), BF16),
        scratch_shapes=[pltpu.VMEM((n_seq, HG_HEADS, HG_DV, HG_DK), F32)],
        compiler_params=_params(("parallel", "arbitrary")),
        name="hgrn",
    )(proj, lb_logits, norm_g.reshape(1, HG_DV))


SEL_W = Q_RANK + LANES
KIDX_W = 6 * IDX_DH


def _dsa_prep_kernel(x_ref, wsel_ref, wckv_ref, qg_ref, kvg_ref, lng_ref, lnb_ref, wqT_ref, wqiT_ref,
                     wk_ref, wvT_ref, qT_ref, qiT_ref, k_ref, vT_ref, kidx_ref, wT_ref, w6_ref):
    @pl.when((pl.program_id(0) == 0) & (pl.program_id(1) == 0))
    def _():
        w6_ref[...] = _cat_wgt6(wsel_ref[...], 0)

    x = x_ref[...]
    sel = _dot(_cat_act6(x, 1), w6_ref[...])
    cq = sel[:, :Q_RANK]
    cqn = cq * lax.rsqrt(jnp.mean(cq * cq, axis=-1, keepdims=True) + RMS_EPS) * qg_ref[...]
    qT_ref[0] = (_dot_nt(wqT_ref[...], cqn.astype(BF16)) * (AT_DH ** -0.5)).astype(qT_ref.dtype)
    qiT_ref[0] = _dot_nt(_cat_wgt6(wqiT_ref[...], 1), _cat_act6(cqn, 1))
    ckv = _dot(x.astype(BF16), wckv_ref[...])
    ckvn = (ckv * lax.rsqrt(jnp.mean(ckv * ckv, axis=-1, keepdims=True) + RMS_EPS)
            * kvg_ref[...]).astype(BF16)
    k_ref[0] = _dot(ckvn, wk_ref[...]).astype(k_ref.dtype)
    vT_ref[0, 0] = _dot_nt(wvT_ref[...], ckvn).astype(vT_ref.dtype)
    k128 = sel[:, Q_RANK:]
    in_k = lax.broadcasted_iota(I32, (1, LANES), 1) < IDX_DH
    mu = jnp.sum(jnp.where(in_k, k128, 0.0), axis=-1, keepdims=True) * (1.0 / IDX_DH)
    dev = jnp.where(in_k, k128 - mu, 0.0)
    var = jnp.sum(dev * dev, axis=-1, keepdims=True) * (1.0 / IDX_DH)
    y = dev * lax.rsqrt(var + LN_EPS) * lng_ref[...] + lnb_ref[...]
    h, m, l = (p.astype(F32) for p in _split3(y))
    up = lambda p: pltpu.roll(p, IDX_DH, 1)
    kidx_ref[0] = jnp.concatenate([h + up(h), m + up(h), l + up(m)], axis=1).astype(BF16)
    tT = k128.T
    wT_ref[0] = tT[IDX_DH:IDX_DH + SUBLANES, :] * (IDX_HEADS ** -0.5 * IDX_DH ** -0.5)


def _dsa_prep(x, w_sel, w_ckv, q_norm_g, kv_norm_g, idx_ln_g, idx_ln_b, wqT, wqiT, wk, wvT):
    b, s, d = x.shape
    tm = KBLK
    nb = s // tm
    full = lambda a: pl.BlockSpec(a.shape, lambda i, j: (0,) * a.ndim)
    pad_lanes = lambda v: jnp.pad(v.reshape(1, -1), ((0, 0), (0, LANES - v.shape[0])))
    args = (w_sel, w_ckv, q_norm_g.reshape(1, -1), kv_norm_g.reshape(1, -1),
            pad_lanes(idx_ln_g), pad_lanes(idx_ln_b), wqT, wqiT, wk, wvT)
    return pl.pallas_call(
        _dsa_prep_kernel,
        grid=(b, nb),
        in_specs=[pl.BlockSpec((None, tm, d), lambda i, j: (i, j, 0))] + [full(a) for a in args],
        out_specs=[pl.BlockSpec((1, AT_WIDTH, tm), lambda i, j: (i, 0, j)),
                   pl.BlockSpec((1, IDX_HEADS * IDX_DH, tm), lambda i, j: (i, 0, j)),
                   pl.BlockSpec((1, tm, AT_WIDTH), lambda i, j: (i, j, 0)),
                   pl.BlockSpec((1, 1, AT_WIDTH, tm), lambda i, j: (i, j, 0, 0)),
                   pl.BlockSpec((1, tm, KIDX_W), lambda i, j: (i, j, 0)),
                   pl.BlockSpec((1, SUBLANES, tm), lambda i, j: (i, 0, j))],
        out_shape=[jax.ShapeDtypeStruct((b, AT_WIDTH, s), BF16),
                   jax.ShapeDtypeStruct((b, IDX_HEADS * IDX_DH, s), F32),
                   jax.ShapeDtypeStruct((b, s, AT_WIDTH), BF16),
                   jax.ShapeDtypeStruct((b, nb, AT_WIDTH, tm), BF16),
                   jax.ShapeDtypeStruct((b, s, KIDX_W), BF16),
                   jax.ShapeDtypeStruct((b, SUBLANES, s), F32)],
        scratch_shapes=[pltpu.VMEM((6 * d, SEL_W), BF16)],
        compiler_params=_params(("arbitrary", "arbitrary")),
        name="dsa_prep",
    )(x, *args)


def _sortable(x):
    bits = lax.bitcast_convert_type(x, I32)
    return jnp.where(bits < 0, bits ^ jnp.int32(0x7FFFFFFF), bits)


def _dsa_attn_kernel(qT_ref, qiT_ref, wT_ref, kidx_ref, k_ref, vT_ref, o_ref,
                     keys_ref, q2_ref, qbd_ref, acc_ref, m_ref, l_ref, *, k_top):
    qb = pl.program_id(1)
    t0 = qb * QBLK
    e_max = t0 + QBLK
    nkb = (e_max + KBLK - 1) >> 8

    lane = lax.broadcasted_iota(I32, (1, QBLK), 1)
    t_idx = t0 + lane
    end_t = ((t_idx >> 6) + 1) * CHUNK
    srow = lax.broadcasted_iota(I32, (KBLK, QBLK), 0)

    for h in range(IDX_HEADS):
        q2_ref[:, h * QBLK:(h + 1) * QBLK] = _cat_wgt6(qiT_ref[0, h * IDX_DH:(h + 1) * IDX_DH, :], 0)
    wrow = jnp.concatenate([wT_ref[0, h:h + 1, :] for h in range(IDX_HEADS)], axis=1)

    def score_body(kb, carry):
        rows = pl.ds(pl.multiple_of(kb * KBLK, KBLK), KBLK)
        logits = _dot(kidx_ref[0, rows, :], q2_ref[...])
        r = jnp.maximum(logits, 0.0) * wrow
        sc = r[:, :QBLK]
        for h in range(1, IDX_HEADS):
            sc = sc + r[:, h * QBLK:(h + 1) * QBLK]
        valid = (srow + kb * KBLK) < end_t
        keys_ref[rows, :] = _sortable(jnp.where(valid, sc, -jnp.inf))
        return carry

    lax.fori_loop(0, nkb, score_body, 0)

    def count_ge(cand):
        def body(kb, acc):
            rows = pl.ds(pl.multiple_of(kb * KBLK, KBLK), KBLK)
            hit = jnp.where(keys_ref[rows, :] >= cand, 1, 0).astype(I32)
            return acc + jnp.sum(hit.reshape(KBLK // SUBLANES, SUBLANES, QBLK), axis=0)
        acc = lax.fori_loop(0, nkb, body, jnp.zeros((SUBLANES, QBLK), I32))
        return jnp.sum(acc, axis=0, keepdims=True)

    zero = jnp.zeros((1, QBLK), I32)
    thr = jnp.where(count_ge(zero) >= k_top, zero, jnp.full((1, QBLK), INT_MIN, I32))

    def bit_body(i, thr):
        cand = thr | (jnp.int32(1) << (30 - i))
        return jnp.where(count_ge(cand) >= k_top, cand, thr)

    thr = lax.fori_loop(0, 31, bit_body, thr)
    need = (k_top - (count_ge(thr + 1))).astype(F32)

    for j in range(AT_HEADS // 2):
        pair = qT_ref[0, j * LANES:(j + 1) * LANES, :]
        rr = lax.broadcasted_iota(I32, pair.shape, 0)
        zeros = jnp.zeros_like(pair)
        qbd_ref[j, :, :QBLK] = jnp.where(rr < AT_DH, pair, zeros)
        qbd_ref[j, :, QBLK:] = jnp.where(rr >= AT_DH, pair, zeros)
    acc_ref[...] = jnp.zeros_like(acc_ref)
    m_ref[...] = jnp.full_like(m_ref, -jnp.inf)
    l_ref[...] = jnp.zeros_like(l_ref)

    ri = lax.broadcasted_iota(I32, (KBLK, KBLK), 0)
    ci = lax.broadcasted_iota(I32, (KBLK, KBLK), 1)
    lstrict = jnp.where(ci < ri, 1.0, 0.0).astype(BF16)

    def attn_body(kb, carry):
        rows = pl.ds(pl.multiple_of(kb * KBLK, KBLK), KBLK)
        kblk = keys_ref[rows, :]
        s_idx = srow + kb * KBLK
        eq = kblk == thr
        eqf = jnp.where(eq, 1.0, 0.0)
        before = _dot(lstrict, eqf.astype(BF16)) + carry
        sel = ((kblk > thr) | (eq & (before < need))) & (s_idx < end_t)
        bias = jnp.where(sel, 0.0, -jnp.inf)
        dist = jnp.abs(t_idx - s_idx).astype(F32)
        s2s = [_dot(k_ref[0, rows, j * LANES:(j + 1) * LANES], qbd_ref[j])
               for j in range(AT_HEADS // 2)]
        for j in range(AT_HEADS // 2):
            ps, alphas = [], []
            for half in range(2):
                h = 2 * j + half
                slope = 2.0 ** (-8.0 * (h + 1) / AT_HEADS)
                st = s2s[j][:, half * QBLK:(half + 1) * QBLK] - slope * dist + bias
                m_old = m_ref[h:h + 1, :]
                m_new = jnp.maximum(m_old, jnp.max(st, axis=0, keepdims=True))
                m_safe = jnp.where(m_new == -jnp.inf, 0.0, m_new)
                alpha = jnp.exp(m_old - m_safe)
                p = jnp.exp(st - m_safe)
                l_ref[h:h + 1, :] = alpha * l_ref[h:h + 1, :] + jnp.sum(p, axis=0, keepdims=True)
                m_ref[h:h + 1, :] = m_new
                ps.append(p.astype(BF16))
                alphas.append(alpha)
            o2 = _dot(vT_ref[0, kb, j * LANES:(j + 1) * LANES, :], jnp.concatenate(ps, axis=1))
            for half in range(2):
                hs = slice((2 * j + half) * AT_DH, (2 * j + half + 1) * AT_DH)
                acc_ref[hs, :] = (alphas[half] * acc_ref[hs, :]
                                  + o2[half * AT_DH:(half + 1) * AT_DH, half * QBLK:(half + 1) * QBLK])
        return carry + jnp.sum(eqf, axis=0, keepdims=True)

    lax.fori_loop(0, nkb, attn_body, jnp.zeros((1, QBLK), F32))

    for h in range(AT_HEADS):
        hs = slice(h * AT_DH, (h + 1) * AT_DH)
        acc_ref[hs, :] = acc_ref[hs, :] / l_ref[h:h + 1, :]
    o_ref[0] = acc_ref[...].T.astype(o_ref.dtype)


def _dsa_attn(qT, qiT, wT, kidx, k, vT):
    b, _, s = qT.shape
    nb = s // KBLK
    k_top = min(TOPK_MAX, s // 4)
    return pl.pallas_call(
        functools.partial(_dsa_attn_kernel, k_top=k_top),
        grid=(b, s // QBLK),
        in_specs=[pl.BlockSpec((1, AT_WIDTH, QBLK), lambda i, j: (i, 0, j)),
                  pl.BlockSpec((1, IDX_HEADS * IDX_DH, QBLK), lambda i, j: (i, 0, j)),
                  pl.BlockSpec((1, SUBLANES, QBLK), lambda i, j: (i, 0, j)),
                  pl.BlockSpec((1, s, KIDX_W), lambda i, j: (i, 0, 0)),
                  pl.BlockSpec((1, s, AT_WIDTH), lambda i, j: (i, 0, 0)),
                  pl.BlockSpec((1, nb, AT_WIDTH, KBLK), lambda i, j: (i, 0, 0, 0))],
        out_specs=pl.BlockSpec((1, QBLK, AT_WIDTH), lambda i, j: (i, j, 0)),
        out_shape=jax.ShapeDtypeStruct((b, s, AT_WIDTH), BF16),
        scratch_shapes=[pltpu.VMEM((s, QBLK), I32),
                        pltpu.VMEM((KIDX_W, IDX_HEADS * QBLK), BF16),
                        pltpu.VMEM((AT_HEADS // 2, LANES, 2 * QBLK), BF16),
                        pltpu.VMEM((AT_WIDTH, QBLK), F32),
                        pltpu.VMEM((AT_HEADS, QBLK), F32),
                        pltpu.VMEM((AT_HEADS, QBLK), F32)],
        compiler_params=_params(("parallel", "arbitrary")),
        name="dsa_attn",
    )(qT, qiT, wT, kidx, k, vT)


def _layer_norm(v, g, b):
    mu = jnp.mean(v, axis=-1, keepdims=True)
    var = jnp.mean(jnp.square(v - mu), axis=-1, keepdims=True)
    return (v - mu) * lax.rsqrt(var + LN_EPS) * g + b


def _out_ln_kernel(oa_ref, ob_ref, x_ref, wa_ref, wb_ref, g_ref, b_ref, h_ref, hb_ref):
    mix = _dot(oa_ref[...], wa_ref[...]) + _dot(ob_ref[...], wb_ref[...])
    h = _layer_norm(ALPHA * x_ref[...] + mix, g_ref[...], b_ref[...])
    h_ref[...] = h
    hb_ref[...] = h.astype(BF16)


def _out_ln(oa, ob, x2, wa, wb, g, b, tm=512):
    t, d = x2.shape
    row = lambda w: pl.BlockSpec((tm, w), lambda i: (i, 0))
    full = lambda a: pl.BlockSpec(a.shape, lambda i: (0, 0))
    return pl.pallas_call(
        _out_ln_kernel,
        grid=(t // tm,),
        in_specs=[row(oa.shape[1]), row(ob.shape[1]), row(d), full(wa), full(wb), full(g), full(b)],
        out_specs=[row(d), row(d)],
        out_shape=[jax.ShapeDtypeStruct((t, d), F32), jax.ShapeDtypeStruct((t, d), BF16)],
        compiler_params=_params(("parallel",)),
        name="out_ln",
    )(oa, ob, x2, wa, wb, g, b)


def _first_max(v, idx, big):
    m = jnp.max(v, axis=0, keepdims=True)
    first = jnp.min(jnp.where(v == m, idx, big), axis=0, keepdims=True)
    return m, first


def _router_kernel(h_ref, wrT_ref, bias_ref, eidx_ref, gT_ref, cnt_ref):
    tm = h_ref.shape[0]
    per_group = N_EXPERTS // N_GROUPS
    logits = _dot_nt(_cat_hlh(wrT_ref[...], 1), _cat_hhl(h_ref[...], 1))
    scores = _sigmoid(logits)
    sel = scores + bias_ref[...]
    iota_g = lax.broadcasted_iota(I32, (per_group, tm), 0)
    neg = -jnp.inf

    grp = jnp.zeros((N_GROUPS, tm), F32)
    iota_grp = lax.broadcasted_iota(I32, (N_GROUPS, tm), 0)
    for g in range(N_GROUPS):
        blk = sel[g * per_group:(g + 1) * per_group, :]
        m1, i1 = _first_max(blk, iota_g, per_group)
        m2 = jnp.max(jnp.where(iota_g == i1, neg, blk), axis=0, keepdims=True)
        grp = jnp.where(iota_grp == g, m1 + m2, grp)

    gmask = jnp.zeros((N_GROUPS, tm), F32)
    work = grp
    for _ in range(TOPK_GROUPS):
        _, gi = _first_max(work, iota_grp, N_GROUPS)
        hit = iota_grp == gi
        gmask = jnp.where(hit, 1.0, gmask)
        work = jnp.where(hit, neg, work)

    e_mask = jnp.concatenate(
        [jnp.broadcast_to(gmask[g:g + 1, :], (per_group, tm)) for g in range(N_GROUPS)], axis=0)
    iota_e = lax.broadcasted_iota(I32, (N_EXPERTS, tm), 0)
    work = jnp.where(e_mask > 0.5, sel, neg)
    iota_k = lax.broadcasted_iota(I32, (TOP_K, tm), 0)
    eidx = jnp.zeros((TOP_K, tm), I32)
    w = jnp.zeros((N_EXPERTS, tm), F32)
    chosen = jnp.zeros((N_EXPERTS, tm), F32)
    for k in range(TOP_K):
        _, ei = _first_max(work, iota_e, N_EXPERTS)
        hit = iota_e == ei
        eidx = jnp.where(iota_k == k, ei, eidx)
        w = jnp.where(hit, scores, w)
        chosen = jnp.where(hit, 1.0, chosen)
        work = jnp.where(hit, neg, work)

    gates = w / jnp.sum(w, axis=0, keepdims=True) * ROUTED_SCALE
    eidx_ref[...] = eidx
    gates = jnp.concatenate([gates, jnp.zeros((LANES - N_EXPERTS, tm), F32)], axis=0)
    gT_ref[...] = gates.T
    cnt_ref[0] = jnp.broadcast_to(jnp.sum(chosen, axis=1, keepdims=True), (N_EXPERTS, LANES))


def _router(h1, wrT, bias_col, tm):
    t, d = h1.shape
    nt = t // tm
    return pl.pallas_call(
        _router_kernel,
        grid=(nt,),
        in_specs=[pl.BlockSpec((tm, d), lambda i: (i, 0)),
                  pl.BlockSpec(wrT.shape, lambda i: (0, 0)),
                  pl.BlockSpec(bias_col.shape, lambda i: (0, 0))],
        out_specs=[pl.BlockSpec((TOP_K, tm), lambda i: (0, i)),
                   pl.BlockSpec((tm, LANES), lambda i: (i, 0)),
                   pl.BlockSpec((1, N_EXPERTS, LANES), lambda i: (i, 0, 0))],
        out_shape=[jax.ShapeDtypeStruct((TOP_K, t), I32),
                   jax.ShapeDtypeStruct((t, LANES), F32),
                   jax.ShapeDtypeStruct((nt, N_EXPERTS, LANES), F32)],
        compiler_params=_params(("parallel",)),
        name="router",
    )(h1, wrT, bias_col)


MOE_TM = 512
SEG = 16
EXP_TM = 512
ROW_BUF = TOP_K * MOE_TM + N_EXPERTS * SEG
P_BLK = 256
G_BLK = 512


def _moe_plan(cnt, n_exp_tiles):
    nt = cnt.shape[0]
    n16 = (cnt + (SEG - 1)) // SEG
    so16 = jnp.cumsum(n16, axis=1) - n16
    r16 = jnp.sum(n16, axis=0)
    per = EXP_TM // SEG
    rp16 = (r16 + (per - 1)) // per * per
    ends = jnp.cumsum(rp16)
    off16 = ends - rp16
    go16 = off16[None, :] + jnp.cumsum(n16, axis=0) - n16
    n_et = (ends[-1] // per).astype(I32).reshape(1)
    tiles = jnp.arange(n_exp_tiles, dtype=I32)
    tile_expert = jnp.sum((ends // per)[None, :] <= tiles[:, None], axis=1)
    tile_expert = jnp.minimum(tile_expert, N_EXPERTS - 1).astype(I32)
    flat = lambda a: a.reshape(nt * N_EXPERTS).astype(I32)
    twice = lambda a: jnp.concatenate([a, a], axis=1).astype(F32)
    seg_lo, seg_hi = twice(so16 * SEG), twice((so16 + n16) * SEG)
    as_rows = lambda a: jnp.broadcast_to(a[:, None, :], (nt, SUBLANES, 2 * N_EXPERTS))
    as_cols = lambda a: jnp.broadcast_to(a[:, :, None], (nt, 2 * N_EXPERTS, LANES))
    return dict(n16=flat(n16), so16=flat(so16), go16=flat(go16),
                rtot16=jnp.sum(n16, axis=1).astype(I32), r16=r16.astype(I32), rp16=rp16.astype(I32),
                off16=off16.astype(I32), n_et=n_et, tile_expert=tile_expert,
                lo_rows=as_rows(seg_lo), hi_rows=as_rows(seg_hi),
                lo_cols=as_cols(seg_lo), hi_cols=as_cols(seg_hi))


def _chunk(ref, c):
    return ref.at[pl.ds(pl.multiple_of(c * SEG, SEG), SEG)]


def _segment_copies(i, n16_ref, so16_ref, go16_ref, make_copy):
    def expert_body(e, total):
        n = n16_ref[i * N_EXPERTS + e]
        so = so16_ref[i * N_EXPERTS + e]
        go = go16_ref[i * N_EXPERTS + e]

        def chunk_body(c, carry):
            make_copy(so + c, go + c).start()
            return carry

        lax.fori_loop(0, n, chunk_body, 0)
        return total + n

    return lax.fori_loop(0, N_EXPERTS, expert_body, 0)


def _wait_copies(count, make_copy):
    def body(c, carry):
        make_copy(0, 0).wait()
        return carry

    lax.fori_loop(0, count, body, 0)


POS_SPLIT = 64


def _dispatch_kernel(n16_ref, so16_ref, go16_ref, rtot_ref, r16_ref, rp16_ref, off16_ref,
                     x_ref, eidx_ref, locol_ref, lorow_ref, hirow_ref,
                     xs_hbm, posT_ref, buf_ref, zero_ref, pend_ref, sem):
    i = pl.program_id(0)
    slot = lax.rem(i, 2)
    buf = buf_ref.at[slot]
    tm = x_ref.shape[0]
    eidx = eidx_ref[...]
    iota_e = lax.broadcasted_iota(I32, (N_EXPERTS, tm), 0)
    onehot = jnp.zeros((N_EXPERTS, tm), F32)
    for k in range(TOP_K):
        onehot = jnp.where(iota_e == eidx[k:k + 1, :], 1.0, onehot)
    earlier = lax.broadcasted_iota(I32, (tm, tm), 0) < lax.broadcasted_iota(I32, (tm, tm), 1)
    rank = _dot(onehot.astype(BF16), jnp.where(earlier, 1.0, 0.0).astype(BF16))
    pos1 = jnp.where(onehot > 0.5, locol_ref[0][:N_EXPERTS, 0:1] + rank + 1.0, 0.0)
    pos_hi = jnp.floor(pos1 * (1.0 / POS_SPLIT)) * POS_SPLIT
    pos2 = jnp.concatenate([pos_hi, pos1 - pos_hi], axis=0)
    posT_ref[...] = pos2.T
    pos2b = pos2.astype(BF16)

    x = x_ref[...]
    seg_lo = lorow_ref[0][0:1, :]
    seg_hi = hirow_ref[0][0:1, :]
    n_blk = (rtot_ref[i] * SEG + (P_BLK - 1)) // P_BLK

    def blk_body(rb, carry):
        r0 = pl.multiple_of(rb * P_BLK, P_BLK)
        r_e = (lax.broadcasted_iota(I32, (P_BLK, 2 * N_EXPERTS), 0) + r0).astype(F32)
        owner = jnp.where((r_e >= seg_lo) & (r_e < seg_hi), 1.0, 0.0).astype(BF16)
        want = _dot(owner, pos2b)
        r_t = (lax.broadcasted_iota(I32, (P_BLK, tm), 0) + (r0 + 1)).astype(F32)
        p = jnp.where(want == r_t, 1.0, 0.0).astype(BF16)
        buf[pl.ds(r0, P_BLK), :] = _dot(p, x).astype(BF16)
        return carry

    lax.fori_loop(0, n_blk, blk_body, 0)

    @pl.when(i > 0)
    def _():
        prev_copy = lambda src, dst: pltpu.make_async_copy(
            _chunk(buf_ref.at[1 - slot], src), _chunk(xs_hbm, dst), sem.at[1 - slot])
        _wait_copies(pend_ref[0], prev_copy)

    out_copy = lambda src, dst: pltpu.make_async_copy(_chunk(buf, src), _chunk(xs_hbm, dst),
                                                      sem.at[slot])
    pend_ref[0] = _segment_copies(i, n16_ref, so16_ref, go16_ref, out_copy)

    @pl.when(i == pl.num_programs(0) - 1)
    def _():
        _wait_copies(pend_ref[0], out_copy)
        zero_ref[...] = jnp.zeros_like(zero_ref)
        zero_copy = lambda src, dst: pltpu.make_async_copy(_chunk(zero_ref, 0), _chunk(xs_hbm, dst),
                                                           sem.at[slot])

        def zero_chunk(c, carry):
            zero_copy(0, c).start()
            return carry

        def tail_body(e, total):
            lax.fori_loop(off16_ref[e] + r16_ref[e], off16_ref[e] + rp16_ref[e], zero_chunk, 0)
            return total + rp16_ref[e] - r16_ref[e]

        _wait_copies(lax.fori_loop(0, N_EXPERTS, tail_body, 0), zero_copy)

        per = EXP_TM // SEG
        first = (off16_ref[N_EXPERTS - 1] + rp16_ref[N_EXPERTS - 1]) // per
        n_tiles = xs_hbm.shape[0] // EXP_TM
        tile_copy = lambda tile: pltpu.make_async_copy(
            zero_ref, xs_hbm.at[pl.ds(pl.multiple_of(tile * EXP_TM, EXP_TM), EXP_TM)], sem.at[slot])

        def zero_tile(tile, carry):
            tile_copy(tile).start()
            return carry

        def wait_tile(tile, carry):
            tile_copy(0).wait()
            return carry

        lax.fori_loop(first, n_tiles, zero_tile, 0)
        lax.fori_loop(first, n_tiles, wait_tile, 0)


def _dispatch(hb, eidx, plan, n_rows):
    t, d = hb.shape
    nt = t // MOE_TM
    grid_spec = pltpu.PrefetchScalarGridSpec(
        num_scalar_prefetch=7,
        grid=(nt,),
        in_specs=[pl.BlockSpec((MOE_TM, d), lambda i, *_: (i, 0)),
                  pl.BlockSpec((TOP_K, MOE_TM), lambda i, *_: (0, i)),
                  pl.BlockSpec((1, 2 * N_EXPERTS, LANES), lambda i, *_: (i, 0, 0)),
                  pl.BlockSpec((1, SUBLANES, 2 * N_EXPERTS), lambda i, *_: (i, 0, 0)),
                  pl.BlockSpec((1, SUBLANES, 2 * N_EXPERTS), lambda i, *_: (i, 0, 0))],
        out_specs=[pl.BlockSpec(memory_space=pl.ANY),
                   pl.BlockSpec((MOE_TM, LANES), lambda i, *_: (i, 0))],
        scratch_shapes=[pltpu.VMEM((2, ROW_BUF, d), BF16), pltpu.VMEM((EXP_TM, d), BF16),
                        pltpu.SMEM((1,), I32), pltpu.SemaphoreType.DMA((2,))])
    return pl.pallas_call(
        _dispatch_kernel,
        grid_spec=grid_spec,
        out_shape=[jax.ShapeDtypeStruct((n_rows, d), BF16),
                   jax.ShapeDtypeStruct((t, LANES), F32)],
        compiler_params=_params(("arbitrary",)),
        name="moe_dispatch",
    )(plan["n16"], plan["so16"], plan["go16"], plan["rtot16"], plan["r16"], plan["rp16"],
      plan["off16"], hb, eidx, plan["lo_cols"], plan["lo_rows"], plan["hi_rows"])


def _experts_kernel(te_ref, net_ref, x_ref, wg_ref, wu_ref, wd_ref, y_ref, wgb_ref, wub_ref, wdb_ref):
    j = pl.program_id(0)
    used = j < net_ref[0]

    @pl.when(used & ((j == 0) | (te_ref[j] != te_ref[jnp.maximum(j - 1, 0)])))
    def _():
        wgb_ref[...] = wg_ref[0].astype(BF16)
        wub_ref[...] = wu_ref[0].astype(BF16)
        wdb_ref[...] = wd_ref[0].astype(BF16)

    @pl.when(used)
    def _():
        x = x_ref[...]
        hmid = _silu(_dot(x, wgb_ref[...])) * _dot(x, wub_ref[...])
        y_ref[...] = _dot(hmid.astype(BF16), wdb_ref[...]).astype(y_ref.dtype)


def _experts(xs, wg, wu, wd, plan):
    n_rows, d = xs.shape
    row_map = lambda j, te, net: (jnp.minimum(j, net[0] - 1), 0)
    w_map = lambda j, te, net: (te[jnp.minimum(j, net[0] - 1)], 0, 0)
    grid_spec = pltpu.PrefetchScalarGridSpec(
        num_scalar_prefetch=2,
        grid=(n_rows // EXP_TM,),
        in_specs=[pl.BlockSpec((EXP_TM, d), row_map),
                  pl.BlockSpec((1, d, D_EXPERT), w_map),
                  pl.BlockSpec((1, d, D_EXPERT), w_map),
                  pl.BlockSpec((1, D_EXPERT, d), w_map)],
        out_specs=pl.BlockSpec((EXP_TM, d), row_map),
        scratch_shapes=[pltpu.VMEM((d, D_EXPERT), BF16), pltpu.VMEM((d, D_EXPERT), BF16),
                        pltpu.VMEM((D_EXPERT, d), BF16)])
    return pl.pallas_call(
        _experts_kernel,
        grid_spec=grid_spec,
        out_shape=jax.ShapeDtypeStruct((n_rows, d), BF16),
        input_output_aliases={2: 0},
        compiler_params=_params(("arbitrary",)),
        name="moe_experts",
    )(plan["tile_expert"], plan["n_et"], xs, wg, wu, wd)


def _combine_kernel(n16_ref, so16_ref, go16_ref, rtot_ref,
                    ys_hbm, posT_ref, gT_ref, locol_ref, hicol_ref, h_ref,
                    sg_ref, su_ref, sd_ref, lg_ref, lb_ref, o_ref, ybuf_ref, acc_ref, pend_ref, sem):
    i = pl.program_id(0)
    slot = lax.rem(i, 2)
    ybuf = ybuf_ref.at[slot]
    tm = h_ref.shape[0]

    def fetch(tile, to_slot):
        copy = lambda dst, src: pltpu.make_async_copy(
            _chunk(ys_hbm, src), _chunk(ybuf_ref.at[to_slot], dst), sem.at[to_slot])
        pend_ref[to_slot] = _segment_copies(tile, n16_ref, so16_ref, go16_ref, copy)

    @pl.when(i == 0)
    def _():
        ybuf_ref[...] = jnp.zeros_like(ybuf_ref)
        fetch(0, 0)

    @pl.when(i + 1 < pl.num_programs(0))
    def _():
        fetch(i + 1, 1 - slot)

    h = h_ref[...]
    xb = h.astype(BF16)
    hs = _silu(_dot(xb, sg_ref[...])) * _dot(xb, su_ref[...])
    acc_ref[...] = ALPHA * h + _dot(hs.astype(BF16), sd_ref[...])
    _wait_copies(pend_ref[slot], lambda dst, src: pltpu.make_async_copy(
        _chunk(ys_hbm, src), _chunk(ybuf, dst), sem.at[slot]))

    pos2b = posT_ref[...].astype(BF16)
    g_hi, g_lo = _split(gT_ref[...])
    gate2b = (g_hi.astype(F32) + pltpu.roll(g_lo.astype(F32), N_EXPERTS, 1)).astype(BF16)
    seg_lo = jnp.broadcast_to(locol_ref[0][:, 0:1], (2 * N_EXPERTS, G_BLK))
    seg_hi = jnp.broadcast_to(hicol_ref[0][:, 0:1], (2 * N_EXPERTS, G_BLK))
    n_blk = (rtot_ref[i] * SEG + (G_BLK - 1)) // G_BLK

    def blk_body(cb, carry):
        c0 = pl.multiple_of(cb * G_BLK, G_BLK)
        c_e = (lax.broadcasted_iota(I32, (2 * N_EXPERTS, G_BLK), 1) + c0).astype(F32)
        owner = jnp.where((c_e >= seg_lo) & (c_e < seg_hi), 1.0, 0.0).astype(BF16)
        want = _dot(pos2b, owner)
        gate = _dot(gate2b, owner)
        c_t = (lax.broadcasted_iota(I32, (tm, G_BLK), 1) + (c0 + 1)).astype(F32)
        g = jnp.where(want == c_t, gate, 0.0).astype(BF16)
        acc_ref[...] += _dot(g, ybuf[pl.ds(c0, G_BLK), :])
        return carry

    lax.fori_loop(0, n_blk, blk_body, 0)
    o_ref[...] = _layer_norm(acc_ref[...], lg_ref[...], lb_ref[...])


def _combine(ys, posT, gT, h1, sg, su, sd, lg, lb, plan):
    t, d = h1.shape
    nt = t // MOE_TM
    row = lambda w: pl.BlockSpec((MOE_TM, w), lambda i, *_: (i, 0))
    full = lambda a: pl.BlockSpec(a.shape, lambda i, *_: (0, 0))
    seg = pl.BlockSpec((1, 2 * N_EXPERTS, LANES), lambda i, *_: (i, 0, 0))
    grid_spec = pltpu.PrefetchScalarGridSpec(
        num_scalar_prefetch=4,
        grid=(nt,),
        in_specs=[pl.BlockSpec(memory_space=pl.ANY), row(LANES), row(LANES), seg, seg, row(d),
                  full(sg), full(su), full(sd), full(lg), full(lb)],
        out_specs=row(d),
        scratch_shapes=[pltpu.VMEM((2, ROW_BUF, d), BF16), pltpu.VMEM((MOE_TM, d), F32),
                        pltpu.SMEM((2,), I32), pltpu.SemaphoreType.DMA((2,))])
    return pl.pallas_call(
        _combine_kernel,
        grid_spec=grid_spec,
        out_shape=jax.ShapeDtypeStruct((t, d), F32),
        compiler_params=_params(("arbitrary",)),
        name="moe_combine",
    )(plan["n16"], plan["so16"], plan["go16"], plan["rtot16"], ys, posT, gT,
      plan["lo_cols"], plan["hi_cols"], h1, sg, su, sd, lg, lb)


def _moe(h1, hb, eidx, gT, cnt, wg, wu, wd, sg, su, sd, lg, lb):
    t = h1.shape[0]
    nt = t // MOE_TM
    max_rows = nt * ROW_BUF + N_EXPERTS * EXP_TM
    n_exp_tiles = pl.cdiv(max_rows, EXP_TM)
    plan = _moe_plan(cnt[:, :, 0].astype(I32), n_exp_tiles)
    xs, posT = _dispatch(hb, eidx, plan, n_exp_tiles * EXP_TM)
    ys = _experts(xs, wg, wu, wd, plan)
    return _combine(ys, posT, gT, h1, sg, su, sd, lg, lb, plan)


def kernel(x, w_in, hg_lb_logits, hg_norm_g, q_norm_g, w_q_up, w_qidx_up, kv_norm_g, w_kv_up,
           idx_ln_g, idx_ln_b, w_out, ln1_g, ln1_b, w_router, router_bias, w_e_gate, w_e_up,
           w_e_down, w_s_gate, w_s_up, w_s_down, ln2_g, ln2_b):
    b, s, d = x.shape
    h = x
    for l in range(DEPTH):
        h2 = h.reshape(b * s, d)
        n_main = 4 * HG_WIDTH
        w_main = w_in[l, :, :n_main].astype(BF16)
        c_kv, c_idx = n_main + Q_RANK, n_main + Q_RANK + KV_RANK
        w_sel = jnp.concatenate([w_in[l, :, n_main:c_kv], w_in[l, :, c_idx:]], axis=1)
        w_sel = jnp.pad(w_sel, ((0, 0), (0, SEL_W - w_sel.shape[1])))
        w_ckv = w_in[l, :, c_kv:c_idx].astype(BF16)

        proj = _proj_main(h2, w_main).reshape(b, s, n_main)
        o_a = _hgrn(proj, hg_lb_logits, hg_norm_g[l], l)

        wqT = w_q_up[l].T.astype(BF16)
        wqiT = w_qidx_up[l].T
        wk = w_kv_up[l][:, :AT_WIDTH].astype(BF16)
        wvT = w_kv_up[l][:, AT_WIDTH:].T.astype(BF16)
        qT, qiT, k, vT, kidx, wT = _dsa_prep(h, w_sel, w_ckv, q_norm_g[l], kv_norm_g[l],
                                             idx_ln_g[l], idx_ln_b[l], wqT, wqiT, wk, wvT)
        o_b = _dsa_attn(qT, qiT, wT, kidx, k, vT)

        h1, hb = _out_ln(o_a.reshape(b * s, HG_WIDTH), o_b.reshape(b * s, AT_WIDTH), h2,
                         w_out[l, :HG_WIDTH].astype(BF16), w_out[l, HG_WIDTH:].astype(BF16),
                         ln1_g[l].reshape(1, d), ln1_b[l].reshape(1, d))
        eidx, gwT, cnt = _router(h1, w_router[l].T, router_bias[l].reshape(N_EXPERTS, 1), MOE_TM)
        out = _moe(h1, hb, eidx, gwT, cnt, w_e_gate[l], w_e_up[l], w_e_down[l],
                   w_s_gate[l].astype(BF16), w_s_up[l].astype(BF16), w_s_down[l].astype(BF16),
                   ln2_g[l].reshape(1, d), ln2_b[l].reshape(1, d))
        h = out.reshape(b, s, d)
    return h
```

```python
import functools

import jax
import jax.numpy as jnp
from jax import lax
from jax.experimental import pallas as pl
from jax.experimental.pallas import tpu as pltpu

F32 = jnp.float32
BF16 = jnp.bfloat16
I32 = jnp.int32
HIGHEST = lax.Precision.HIGHEST

CHUNK = 64
HG_HEADS = 4
HG_DK = 128
HG_DV = 128
HG_WIDTH = HG_HEADS * HG_DV
AT_HEADS = 8
AT_DH = 64
AT_WIDTH = AT_HEADS * AT_DH
Q_RANK = 256
KV_RANK = 128
IDX_HEADS = 4
IDX_DH = 64
TOPK_MAX = 256
N_EXPERTS = 64
TOP_K = 8
N_GROUPS = 8
TOPK_GROUPS = 4
D_EXPERT = 256
ROUTED_SCALE = 2.5
DEPTH = 1
ALPHA = (2.0 * DEPTH) ** 0.25
LN_EPS = 1e-5
RMS_EPS = 1e-6
LOG2E = 1.4426950408889634

LANES = 128
SUBLANES = 8
QBLK = 4 * CHUNK
KBLK = 256
VMEM_LIMIT = 56 * 1024 * 1024
INT_MIN = -2 ** 31


def _params(sem, vmem=VMEM_LIMIT):
    return pltpu.CompilerParams(dimension_semantics=sem, vmem_limit_bytes=vmem)


def _dot(a, b, precision=None):
    return jnp.dot(a, b, preferred_element_type=F32, precision=precision)


def _dot_nt(a, b, precision=None):
    return lax.dot_general(a, b, (((1,), (1,)), ((), ())), preferred_element_type=F32,
                           precision=precision)


def _dot_tn(a, b, precision=None):
    return lax.dot_general(a, b, (((0,), (0,)), ((), ())), preferred_element_type=F32,
                           precision=precision)


def _split(x):
    hi = x.astype(BF16)
    return hi, (x - hi.astype(F32)).astype(BF16)


def _cat_hhl(x, axis):
    hi, lo = _split(x)
    return jnp.concatenate([hi, hi, lo], axis=axis)


def _cat_hlh(x, axis):
    hi, lo = _split(x)
    return jnp.concatenate([hi, lo, hi], axis=axis)


def _split3(x):
    hi = x.astype(BF16)
    r = x - hi.astype(F32)
    mid = r.astype(BF16)
    return hi, mid, (r - mid.astype(F32)).astype(BF16)


def _cat_act6(x, axis):
    h, m, l = _split3(x)
    return jnp.concatenate([h, h, m, h, l, m], axis=axis)


def _cat_wgt6(x, axis):
    h, m, l = _split3(x)
    return jnp.concatenate([h, m, h, l, h, m], axis=axis)


def _sigmoid(x):
    return 1.0 / (1.0 + jnp.exp(-x))


def _silu(x):
    return x * _sigmoid(x)


def _proj_main_kernel(x_ref, w_ref, o_ref):
    o_ref[...] = _dot(x_ref[...].astype(BF16), w_ref[...])


def _proj_main(x2, w_bf, tm=256):
    t, d = x2.shape
    n = w_bf.shape[1]
    return pl.pallas_call(
        _proj_main_kernel,
        grid=(t // tm,),
        in_specs=[pl.BlockSpec((tm, d), lambda i: (i, 0)),
                  pl.BlockSpec((d, n), lambda i: (0, 0))],
        out_specs=pl.BlockSpec((tm, n), lambda i: (i, 0)),
        out_shape=jax.ShapeDtypeStruct((t, n), F32),
        compiler_params=_params(("parallel",)),
        name="proj_main",
    )(x2, w_bf)


def _hgrn_kernel(p_ref, lbl_ref, ng_ref, o_ref, st_ref, *, n_chunks, layer):
    @pl.when(pl.program_id(1) == 0)
    def _():
        st_ref[...] = jnp.zeros_like(st_ref)

    lg = lbl_ref[...]
    ex = jnp.exp(lg - jnp.max(lg, axis=0, keepdims=True))
    lb_all = jnp.sum(ex[: layer + 1], axis=0, keepdims=True) / jnp.sum(ex, axis=0, keepdims=True)
    ng = ng_ref[...]

    r = lax.broadcasted_iota(I32, (CHUNK, CHUNK), 0)
    c = lax.broadcasted_iota(I32, (CHUNK, CHUNK), 1)
    causal = c <= r
    tri = jnp.where(causal, 1.0, 0.0).astype(BF16)
    tri2 = jnp.concatenate([tri, tri], axis=1)

    units = [(sq, h) for sq in range(p_ref.shape[0]) for h in range(HG_HEADS)]

    def chunk_body(j, carry):
        rows = pl.ds(pl.multiple_of(j * CHUNK, CHUNK), CHUNK)
        ks, bs = [], []
        for sq, h in units:
            lo = h * HG_DK
            lb = lb_all[:, lo:lo + HG_DK]
            f = lb + (1.0 - lb) * _sigmoid(p_ref[sq, rows, HG_WIDTH + lo:HG_WIDTH + lo + HG_DK])
            lf_hi, lf_lo = _split(jnp.log(f))
            ks.append(1.0 - f)
            bs.append(_dot(tri2, jnp.concatenate([lf_hi, lf_lo], axis=0)))
        scs, ois, vbs = [], [], []
        for (sq, h), k, b in zip(units, ks, bs):
            lo = h * HG_DK
            b_last = b[CHUNK - 1:CHUNK, :]
            q_dec = (_silu(p_ref[sq, rows, lo:lo + HG_DK]) * jnp.exp(b)).astype(BF16)
            k_inv = (k * jnp.exp(-b)).astype(BF16)
            k_dec = (k * jnp.exp(b_last - b)).astype(BF16)
            vb = p_ref[sq, rows, 2 * HG_WIDTH + lo:2 * HG_WIDTH + lo + HG_DV].astype(BF16)
            st = st_ref[sq, h]
            scs.append(jnp.where(causal, _dot_nt(q_dec, k_inv), 0.0).astype(BF16))
            ois.append(_dot_nt(q_dec, st.astype(BF16)))
            st_ref[sq, h] = st * jnp.exp(b_last) + _dot_tn(vb, k_dec)
            vbs.append(vb)
        for (sq, h), sc, oi, vb in zip(units, scs, ois, vbs):
            lo = h * HG_DK
            o = _dot(sc, vb) + oi
            o = o * lax.rsqrt(jnp.mean(o * o, axis=-1, keepdims=True) + RMS_EPS) * ng
            gate = p_ref[sq, rows, 3 * HG_WIDTH + lo:3 * HG_WIDTH + lo + HG_DV]
            o_ref[sq, rows, lo:lo + HG_DV] = (o * _silu(gate)).astype(o_ref.dtype)
        return carry

    lax.fori_loop(0, n_chunks, chunk_body, 0)


def _hgrn(proj, lb_logits, norm_g, layer, ct=512, n_seq=2):
    b, s, w = proj.shape
    n_seq = min(n_seq, b)
    n_chunks = ct // CHUNK
    return pl.pallas_call(
        functools.partial(_hgrn_kernel, n_chunks=n_chunks, layer=layer),
        grid=(b // n_seq, s // ct),
        in_specs=[pl.BlockSpec((n_seq, ct, w), lambda i, j: (i, j, 0)),
                  pl.BlockSpec(lb_logits.shape, lambda i, j: (0, 0)),
                  pl.BlockSpec((1, HG_DV), lambda i, j: (0, 0))],
        out_specs=pl.BlockSpec((n_seq, ct, HG_WIDTH), lambda i, j: (i, j, 0)),
        out_shape=jax.ShapeDtypeStruct((b, s, HG_WIDTH), BF16),
        scratch_shapes=[pltpu.VMEM((n_seq, HG_HEADS, HG_DV, HG_DK), F32)],
        compiler_params=_params(("parallel", "arbitrary")),
        name="hgrn",
    )(proj, lb_logits, norm_g.reshape(1, HG_DV))


SEL_W = Q_RANK + LANES
KIDX_W = 6 * IDX_DH


def _dsa_prep_kernel(x_ref, wsel_ref, wckv_ref, qg_ref, kvg_ref, lng_ref, lnb_ref, wqT_ref, wqiT_ref,
                     wk_ref, wvT_ref, qT_ref, qiT_ref, k_ref, vT_ref, kidx_ref, wT_ref, w6_ref):
    @pl.when((pl.program_id(0) == 0) & (pl.program_id(1) == 0))
    def _():
        w6_ref[...] = _cat_wgt6(wsel_ref[...], 0)

    x = x_ref[...]
    sel = _dot(_cat_act6(x, 1), w6_ref[...])
    cq = sel[:, :Q_RANK]
    cqn = cq * lax.rsqrt(jnp.mean(cq * cq, axis=-1, keepdims=True) + RMS_EPS) * qg_ref[...]
    qT_ref[0] = (_dot_nt(wqT_ref[...], cqn.astype(BF16)) * (AT_DH ** -0.5 * LOG2E)).astype(qT_ref.dtype)
    qiT_ref[0] = _dot_nt(_cat_wgt6(wqiT_ref[...], 1), _cat_act6(cqn, 1))
    ckv = _dot(x.astype(BF16), wckv_ref[...])
    ckvn = (ckv * lax.rsqrt(jnp.mean(ckv * ckv, axis=-1, keepdims=True) + RMS_EPS)
            * kvg_ref[...]).astype(BF16)
    k_ref[0] = _dot(ckvn, wk_ref[...]).astype(k_ref.dtype)
    for kb in range(vT_ref.shape[1]):
        vT_ref[0, kb] = _dot_nt(wvT_ref[...], ckvn[kb * KBLK:(kb + 1) * KBLK]).astype(vT_ref.dtype)
    k128 = sel[:, Q_RANK:]
    in_k = lax.broadcasted_iota(I32, (1, LANES), 1) < IDX_DH
    mu = jnp.sum(jnp.where(in_k, k128, 0.0), axis=-1, keepdims=True) * (1.0 / IDX_DH)
    dev = jnp.where(in_k, k128 - mu, 0.0)
    var = jnp.sum(dev * dev, axis=-1, keepdims=True) * (1.0 / IDX_DH)
    y = dev * lax.rsqrt(var + LN_EPS) * lng_ref[...] + lnb_ref[...]
    h, m, l = (p.astype(F32) for p in _split3(y))
    up = lambda p: pltpu.roll(p, IDX_DH, 1)
    kidx_ref[0] = jnp.concatenate([h + up(h), m + up(h), l + up(m)], axis=1).astype(BF16)
    tT = k128.T
    wT_ref[0] = tT[IDX_DH:IDX_DH + SUBLANES, :] * (IDX_HEADS ** -0.5 * IDX_DH ** -0.5)


def _dsa_prep(x, w_sel, w_ckv, q_norm_g, kv_norm_g, idx_ln_g, idx_ln_b, wqT, wqiT, wk, wvT):
    b, s, d = x.shape
    tm = min(2 * KBLK, s)
    kb_per = tm // KBLK
    nb = s // KBLK
    full = lambda a: pl.BlockSpec(a.shape, lambda i, j: (0,) * a.ndim)
    pad_lanes = lambda v: jnp.pad(v.reshape(1, -1), ((0, 0), (0, LANES - v.shape[0])))
    args = (w_sel, w_ckv, q_norm_g.reshape(1, -1), kv_norm_g.reshape(1, -1),
            pad_lanes(idx_ln_g), pad_lanes(idx_ln_b), wqT, wqiT, wk, wvT)
    return pl.pallas_call(
        _dsa_prep_kernel,
        grid=(b, s // tm),
        in_specs=[pl.BlockSpec((None, tm, d), lambda i, j: (i, j, 0))] + [full(a) for a in args],
        out_specs=[pl.BlockSpec((1, AT_WIDTH, tm), lambda i, j: (i, 0, j)),
                   pl.BlockSpec((1, IDX_HEADS * IDX_DH, tm), lambda i, j: (i, 0, j)),
                   pl.BlockSpec((1, tm, AT_WIDTH), lambda i, j: (i, j, 0)),
                   pl.BlockSpec((1, kb_per, AT_WIDTH, KBLK), lambda i, j: (i, j, 0, 0)),
                   pl.BlockSpec((1, tm, KIDX_W), lambda i, j: (i, j, 0)),
                   pl.BlockSpec((1, SUBLANES, tm), lambda i, j: (i, 0, j))],
        out_shape=[jax.ShapeDtypeStruct((b, AT_WIDTH, s), BF16),
                   jax.ShapeDtypeStruct((b, IDX_HEADS * IDX_DH, s), F32),
                   jax.ShapeDtypeStruct((b, s, AT_WIDTH), BF16),
                   jax.ShapeDtypeStruct((b, nb, AT_WIDTH, KBLK), BF16),
                   jax.ShapeDtypeStruct((b, s, KIDX_W), BF16),
                   jax.ShapeDtypeStruct((b, SUBLANES, s), F32)],
        scratch_shapes=[pltpu.VMEM((6 * d, SEL_W), BF16)],
        compiler_params=_params(("arbitrary", "arbitrary")),
        name="dsa_prep",
    )(x, *args)


def _sortable(x):
    bits = lax.bitcast_convert_type(x, I32)
    return jnp.where(bits < 0, bits ^ jnp.int32(0x7FFFFFFF), bits)


def _dsa_attn_kernel(qT_ref, qiT_ref, wT_ref, kidx_ref, k_ref, vT_ref, o_ref,
                     keys_ref, q2_ref, qbd_ref, acc_ref, m_ref, l_ref, *, k_top):
    qb = pl.program_id(1)
    t0 = qb * QBLK
    e_max = t0 + QBLK
    nkb = (e_max + KBLK - 1) >> 8

    lane = lax.broadcasted_iota(I32, (1, QBLK), 1)
    t_idx = t0 + lane
    end_t = ((t_idx >> 6) + 1) * CHUNK
    srow = lax.broadcasted_iota(I32, (KBLK, QBLK), 0)

    for h in range(IDX_HEADS):
        q2_ref[:, h * QBLK:(h + 1) * QBLK] = _cat_wgt6(qiT_ref[0, h * IDX_DH:(h + 1) * IDX_DH, :], 0)
    wrow = jnp.concatenate([wT_ref[0, h:h + 1, :] for h in range(IDX_HEADS)], axis=1)

    def score_body(kb, carry):
        rows = pl.ds(pl.multiple_of(kb * KBLK, KBLK), KBLK)
        logits = _dot(kidx_ref[0, rows, :], q2_ref[...])
        r = jnp.maximum(logits, 0.0) * wrow
        sc = r[:, :QBLK]
        for h in range(1, IDX_HEADS):
            sc = sc + r[:, h * QBLK:(h + 1) * QBLK]
        valid = (srow + kb * KBLK) < end_t
        keys_ref[rows, :] = _sortable(jnp.where(valid, sc, -jnp.inf))
        return carry

    lax.fori_loop(0, nkb, score_body, 0)

    def count_ge(cand):
        def body(kb, acc):
            rows = pl.ds(pl.multiple_of(kb * KBLK, KBLK), KBLK)
            hit = jnp.where(keys_ref[rows, :] >= cand, 1, 0).astype(I32)
            return acc + jnp.sum(hit.reshape(KBLK // SUBLANES, SUBLANES, QBLK), axis=0)
        acc = lax.fori_loop(0, nkb, body, jnp.zeros((SUBLANES, QBLK), I32))
        return jnp.sum(acc, axis=0, keepdims=True)

    zero = jnp.zeros((1, QBLK), I32)
    thr = jnp.where(count_ge(zero) >= k_top, zero, jnp.full((1, QBLK), INT_MIN, I32))

    def bit_body(i, thr):
        cand = thr | (jnp.int32(1) << (30 - i))
        return jnp.where(count_ge(cand) >= k_top, cand, thr)

    thr = lax.fori_loop(0, 31, bit_body, thr)
    need = (k_top - (count_ge(thr + 1))).astype(F32)

    for j in range(AT_HEADS // 2):
        pair = qT_ref[0, j * LANES:(j + 1) * LANES, :]
        rr = lax.broadcasted_iota(I32, pair.shape, 0)
        zeros = jnp.zeros_like(pair)
        qbd_ref[j, :, :QBLK] = jnp.where(rr < AT_DH, pair, zeros)
        qbd_ref[j, :, QBLK:] = jnp.where(rr >= AT_DH, pair, zeros)
    acc_ref[...] = jnp.zeros_like(acc_ref)
    m_ref[...] = jnp.full_like(m_ref, -jnp.inf)
    l_ref[...] = jnp.zeros_like(l_ref)

    ri = lax.broadcasted_iota(I32, (KBLK, KBLK), 0)
    ci = lax.broadcasted_iota(I32, (KBLK, KBLK), 1)
    lstrict = jnp.where(ci < ri, 1.0, 0.0).astype(BF16)

    def attn_body(kb, carry):
        rows = pl.ds(pl.multiple_of(kb * KBLK, KBLK), KBLK)
        kblk = keys_ref[rows, :]
        s_idx = srow + kb * KBLK
        eq = kblk == thr
        eqf = jnp.where(eq, 1.0, 0.0)
        before = _dot(lstrict, eqf.astype(BF16)) + carry
        sel = ((kblk > thr) | (eq & (before < need))) & (s_idx < end_t)
        bias = jnp.where(sel, 0.0, -jnp.inf)
        dist = jnp.abs(t_idx - s_idx).astype(F32)
        s2s = [_dot(k_ref[0, rows, j * LANES:(j + 1) * LANES], qbd_ref[j])
               for j in range(AT_HEADS // 2)]
        for j in range(AT_HEADS // 2):
            ps, alphas = [], []
            for half in range(2):
                h = 2 * j + half
                slope = LOG2E * 2.0 ** (-8.0 * (h + 1) / AT_HEADS)
                st = s2s[j][:, half * QBLK:(half + 1) * QBLK] - slope * dist + bias
                m_old = m_ref[h:h + 1, :]
                m_new = jnp.maximum(m_old, jnp.max(st, axis=0, keepdims=True))
                m_safe = jnp.where(m_new == -jnp.inf, 0.0, m_new)
                alpha = jnp.exp2(m_old - m_safe)
                p = jnp.exp2(st - m_safe)
                l_ref[h:h + 1, :] = alpha * l_ref[h:h + 1, :] + jnp.sum(p, axis=0, keepdims=True)
                m_ref[h:h + 1, :] = m_new
                ps.append(p.astype(BF16))
                alphas.append(alpha)
            o2 = _dot(vT_ref[0, kb, j * LANES:(j + 1) * LANES, :], jnp.concatenate(ps, axis=1))
            for half in range(2):
                hs = slice((2 * j + half) * AT_DH, (2 * j + half + 1) * AT_DH)
                acc_ref[hs, :] = (alphas[half] * acc_ref[hs, :]
                                  + o2[half * AT_DH:(half + 1) * AT_DH, half * QBLK:(half + 1) * QBLK])
        return carry + jnp.sum(eqf, axis=0, keepdims=True)

    lax.fori_loop(0, nkb, attn_body, jnp.zeros((1, QBLK), F32))

    for h in range(AT_HEADS):
        hs = slice(h * AT_DH, (h + 1) * AT_DH)
        acc_ref[hs, :] = acc_ref[hs, :] / l_ref[h:h + 1, :]
    o_ref[0] = acc_ref[...].T.astype(o_ref.dtype)


def _dsa_attn(qT, qiT, wT, kidx, k, vT):
    b, _, s = qT.shape
    nb = s // KBLK
    k_top = min(TOPK_MAX, s // 4)
    return pl.pallas_call(
        functools.partial(_dsa_attn_kernel, k_top=k_top),
        grid=(b, s // QBLK),
        in_specs=[pl.BlockSpec((1, AT_WIDTH, QBLK), lambda i, j: (i, 0, j)),
                  pl.BlockSpec((1, IDX_HEADS * IDX_DH, QBLK), lambda i, j: (i, 0, j)),
                  pl.BlockSpec((1, SUBLANES, QBLK), lambda i, j: (i, 0, j)),
                  pl.BlockSpec((1, s, KIDX_W), lambda i, j: (i, 0, 0)),
                  pl.BlockSpec((1, s, AT_WIDTH), lambda i, j: (i, 0, 0)),
                  pl.BlockSpec((1, nb, AT_WIDTH, KBLK), lambda i, j: (i, 0, 0, 0))],
        out_specs=pl.BlockSpec((1, QBLK, AT_WIDTH), lambda i, j: (i, j, 0)),
        out_shape=jax.ShapeDtypeStruct((b, s, AT_WIDTH), BF16),
        scratch_shapes=[pltpu.VMEM((s, QBLK), I32),
                        pltpu.VMEM((KIDX_W, IDX_HEADS * QBLK), BF16),
                        pltpu.VMEM((AT_HEADS // 2, LANES, 2 * QBLK), BF16),
                        pltpu.VMEM((AT_WIDTH, QBLK), F32),
                        pltpu.VMEM((AT_HEADS, QBLK), F32),
                        pltpu.VMEM((AT_HEADS, QBLK), F32)],
        compiler_params=_params(("parallel", "arbitrary")),
        name="dsa_attn",
    )(qT, qiT, wT, kidx, k, vT)


def _layer_norm(v, g, b):
    mu = jnp.mean(v, axis=-1, keepdims=True)
    var = jnp.mean(jnp.square(v - mu), axis=-1, keepdims=True)
    return (v - mu) * lax.rsqrt(var + LN_EPS) * g + b


def _out_ln_kernel(oa_ref, ob_ref, x_ref, wa_ref, wb_ref, g_ref, b_ref, wrT_ref, bias_ref,
                   h_ref, hb_ref, eidx_ref, gT_ref, cnt_ref):
    mix = _dot(oa_ref[...], wa_ref[...]) + _dot(ob_ref[...], wb_ref[...])
    h = _layer_norm(ALPHA * x_ref[...] + mix, g_ref[...], b_ref[...])
    h_ref[...] = h
    hb_ref[...] = h.astype(BF16)
    eidx_ref[...], gT_ref[...], cnt_ref[0] = _route(h, wrT_ref[...], bias_ref[...])


def _out_ln(oa, ob, x2, wa, wb, g, b, wrT, bias_col, tm):
    t, d = x2.shape
    nt = t // tm
    row = lambda w: pl.BlockSpec((tm, w), lambda i: (i, 0))
    full = lambda a: pl.BlockSpec(a.shape, lambda i: (0, 0))
    return pl.pallas_call(
        _out_ln_kernel,
        grid=(nt,),
        in_specs=[row(oa.shape[1]), row(ob.shape[1]), row(d), full(wa), full(wb), full(g), full(b),
                  full(wrT), full(bias_col)],
        out_specs=[row(d), row(d),
                   pl.BlockSpec((TOP_K, tm), lambda i: (0, i)),
                   row(LANES),
                   pl.BlockSpec((1, N_EXPERTS, LANES), lambda i: (i, 0, 0))],
        out_shape=[jax.ShapeDtypeStruct((t, d), F32), jax.ShapeDtypeStruct((t, d), BF16),
                   jax.ShapeDtypeStruct((TOP_K, t), I32),
                   jax.ShapeDtypeStruct((t, LANES), F32),
                   jax.ShapeDtypeStruct((nt, N_EXPERTS, LANES), F32)],
        compiler_params=_params(("parallel",)),
        name="out_ln_route",
    )(oa, ob, x2, wa, wb, g, b, wrT, bias_col)


def _first_max(v, idx, big):
    m = jnp.max(v, axis=0, keepdims=True)
    first = jnp.min(jnp.where(v == m, idx, big), axis=0, keepdims=True)
    return m, first


def _route(h, wrT, bias_col):
    tm = h.shape[0]
    per_group = N_EXPERTS // N_GROUPS
    logits = _dot_nt(_cat_hlh(wrT, 1), _cat_hhl(h, 1))
    scores = _sigmoid(logits)
    sel = scores + bias_col
    iota_g = lax.broadcasted_iota(I32, (per_group, tm), 0)
    neg = -jnp.inf

    grp = jnp.zeros((N_GROUPS, tm), F32)
    iota_grp = lax.broadcasted_iota(I32, (N_GROUPS, tm), 0)
    for g in range(N_GROUPS):
        blk = sel[g * per_group:(g + 1) * per_group, :]
        m1, i1 = _first_max(blk, iota_g, per_group)
        m2 = jnp.max(jnp.where(iota_g == i1, neg, blk), axis=0, keepdims=True)
        grp = jnp.where(iota_grp == g, m1 + m2, grp)

    gmask = jnp.zeros((N_GROUPS, tm), F32)
    work = grp
    for _ in range(TOPK_GROUPS):
        _, gi = _first_max(work, iota_grp, N_GROUPS)
        hit = iota_grp == gi
        gmask = jnp.where(hit, 1.0, gmask)
        work = jnp.where(hit, neg, work)

    e_mask = jnp.concatenate(
        [jnp.broadcast_to(gmask[g:g + 1, :], (per_group, tm)) for g in range(N_GROUPS)], axis=0)
    iota_e = lax.broadcasted_iota(I32, (N_EXPERTS, tm), 0)
    work = jnp.where(e_mask > 0.5, sel, neg)
    iota_k = lax.broadcasted_iota(I32, (TOP_K, tm), 0)
    eidx = jnp.zeros((TOP_K, tm), I32)
    w = jnp.zeros((N_EXPERTS, tm), F32)
    chosen = jnp.zeros((N_EXPERTS, tm), F32)
    for k in range(TOP_K):
        _, ei = _first_max(work, iota_e, N_EXPERTS)
        hit = iota_e == ei
        eidx = jnp.where(iota_k == k, ei, eidx)
        w = jnp.where(hit, scores, w)
        chosen = jnp.where(hit, 1.0, chosen)
        work = jnp.where(hit, neg, work)

    gates = w / jnp.sum(w, axis=0, keepdims=True) * ROUTED_SCALE
    gates = jnp.concatenate([gates, jnp.zeros((LANES - N_EXPERTS, tm), F32)], axis=0)
    cnt = jnp.broadcast_to(jnp.sum(chosen, axis=1, keepdims=True), (N_EXPERTS, LANES))
    return eidx, gates.T, cnt


MOE_TM = 512
SEG = 16
EXP_TM = 512
ROW_BUF = TOP_K * MOE_TM + N_EXPERTS * SEG
P_BLK = 256
G_BLK = 512


def _moe_plan(cnt, n_exp_tiles):
    nt = cnt.shape[0]
    n16 = (cnt + (SEG - 1)) // SEG
    so16 = jnp.cumsum(n16, axis=1) - n16
    r16 = jnp.sum(n16, axis=0)
    per = EXP_TM // SEG
    rp16 = (r16 + (per - 1)) // per * per
    ends = jnp.cumsum(rp16)
    off16 = ends - rp16
    go16 = off16[None, :] + jnp.cumsum(n16, axis=0) - n16
    n_et = (ends[-1] // per).astype(I32).reshape(1)
    tiles = jnp.arange(n_exp_tiles, dtype=I32)
    tile_expert = jnp.sum((ends // per)[None, :] <= tiles[:, None], axis=1)
    tile_expert = jnp.minimum(tile_expert, N_EXPERTS - 1).astype(I32)
    flat = lambda a: a.reshape(nt * N_EXPERTS).astype(I32)
    twice = lambda a: jnp.concatenate([a, a], axis=1).astype(F32)
    seg_lo, seg_hi = twice(so16 * SEG), twice((so16 + n16) * SEG)
    as_rows = lambda a: jnp.broadcast_to(a[:, None, :], (nt, SUBLANES, 2 * N_EXPERTS))
    as_cols = lambda a: jnp.broadcast_to(a[:, :, None], (nt, 2 * N_EXPERTS, LANES))
    return dict(n16=flat(n16), so16=flat(so16), go16=flat(go16),
                rtot16=jnp.sum(n16, axis=1).astype(I32), r16=r16.astype(I32), rp16=rp16.astype(I32),
                off16=off16.astype(I32), n_et=n_et, tile_expert=tile_expert,
                lo_rows=as_rows(seg_lo), hi_rows=as_rows(seg_hi),
                lo_cols=as_cols(seg_lo), hi_cols=as_cols(seg_hi))


def _chunk(ref, c):
    return ref.at[pl.ds(pl.multiple_of(c * SEG, SEG), SEG)]


def _segment_copies(i, n16_ref, so16_ref, go16_ref, make_copy):
    def expert_body(e, total):
        n = n16_ref[i * N_EXPERTS + e]
        so = so16_ref[i * N_EXPERTS + e]
        go = go16_ref[i * N_EXPERTS + e]

        def chunk_body(c, carry):
            make_copy(so + c, go + c).start()
            return carry

        lax.fori_loop(0, n, chunk_body, 0)
        return total + n

    return lax.fori_loop(0, N_EXPERTS, expert_body, 0)


def _wait_copies(count, make_copy):
    def body(c, carry):
        make_copy(0, 0).wait()
        return carry

    lax.fori_loop(0, count, body, 0)


POS_SPLIT = 64


def _dispatch_kernel(n16_ref, so16_ref, go16_ref, rtot_ref, r16_ref, rp16_ref, off16_ref,
                     x_ref, eidx_ref, locol_ref, lorow_ref, hirow_ref,
                     xs_hbm, posT_ref, buf_ref, zero_ref, pend_ref, sem):
    i = pl.program_id(0)
    slot = lax.rem(i, 2)
    buf = buf_ref.at[slot]
    tm = x_ref.shape[0]
    eidx = eidx_ref[...]
    iota_e = lax.broadcasted_iota(I32, (N_EXPERTS, tm), 0)
    onehot = jnp.zeros((N_EXPERTS, tm), F32)
    for k in range(TOP_K):
        onehot = jnp.where(iota_e == eidx[k:k + 1, :], 1.0, onehot)
    earlier = lax.broadcasted_iota(I32, (tm, tm), 0) < lax.broadcasted_iota(I32, (tm, tm), 1)
    rank = _dot(onehot.astype(BF16), jnp.where(earlier, 1.0, 0.0).astype(BF16))
    pos1 = jnp.where(onehot > 0.5, locol_ref[0][:N_EXPERTS, 0:1] + rank + 1.0, 0.0)
    pos_hi = jnp.floor(pos1 * (1.0 / POS_SPLIT)) * POS_SPLIT
    pos2 = jnp.concatenate([pos_hi, pos1 - pos_hi], axis=0)
    posT_ref[...] = pos2.T
    pos2b = pos2.astype(BF16)

    x = x_ref[...]
    seg_lo = lorow_ref[0][0:1, :]
    seg_hi = hirow_ref[0][0:1, :]
    n_blk = (rtot_ref[i] * SEG + (P_BLK - 1)) // P_BLK

    def blk_body(rb, carry):
        r0 = pl.multiple_of(rb * P_BLK, P_BLK)
        r_e = (lax.broadcasted_iota(I32, (P_BLK, 2 * N_EXPERTS), 0) + r0).astype(F32)
        owner = jnp.where((r_e >= seg_lo) & (r_e < seg_hi), 1.0, 0.0).astype(BF16)
        want = _dot(owner, pos2b)
        r_t = (lax.broadcasted_iota(I32, (P_BLK, tm), 0) + (r0 + 1)).astype(F32)
        p = jnp.where(want == r_t, 1.0, 0.0).astype(BF16)
        buf[pl.ds(r0, P_BLK), :] = _dot(p, x).astype(BF16)
        return carry

    lax.fori_loop(0, n_blk, blk_body, 0)

    @pl.when(i > 0)
    def _():
        prev_copy = lambda src, dst: pltpu.make_async_copy(
            _chunk(buf_ref.at[1 - slot], src), _chunk(xs_hbm, dst), sem.at[1 - slot])
        _wait_copies(pend_ref[0], prev_copy)

    out_copy = lambda src, dst: pltpu.make_async_copy(_chunk(buf, src), _chunk(xs_hbm, dst),
                                                      sem.at[slot])
    pend_ref[0] = _segment_copies(i, n16_ref, so16_ref, go16_ref, out_copy)

    @pl.when(i == pl.num_programs(0) - 1)
    def _():
        _wait_copies(pend_ref[0], out_copy)
        zero_ref[...] = jnp.zeros_like(zero_ref)
        zero_copy = lambda src, dst: pltpu.make_async_copy(_chunk(zero_ref, 0), _chunk(xs_hbm, dst),
                                                           sem.at[slot])

        def zero_chunk(c, carry):
            zero_copy(0, c).start()
            return carry

        def tail_body(e, total):
            lax.fori_loop(off16_ref[e] + r16_ref[e], off16_ref[e] + rp16_ref[e], zero_chunk, 0)
            return total + rp16_ref[e] - r16_ref[e]

        _wait_copies(lax.fori_loop(0, N_EXPERTS, tail_body, 0), zero_copy)

        per = EXP_TM // SEG
        first = (off16_ref[N_EXPERTS - 1] + rp16_ref[N_EXPERTS - 1]) // per
        n_tiles = xs_hbm.shape[0] // EXP_TM
        tile_copy = lambda tile: pltpu.make_async_copy(
            zero_ref, xs_hbm.at[pl.ds(pl.multiple_of(tile * EXP_TM, EXP_TM), EXP_TM)], sem.at[slot])

        def zero_tile(tile, carry):
            tile_copy(tile).start()
            return carry

        def wait_tile(tile, carry):
            tile_copy(0).wait()
            return carry

        lax.fori_loop(first, n_tiles, zero_tile, 0)
        lax.fori_loop(first, n_tiles, wait_tile, 0)


def _dispatch(hb, eidx, plan, n_rows):
    t, d = hb.shape
    nt = t // MOE_TM
    grid_spec = pltpu.PrefetchScalarGridSpec(
        num_scalar_prefetch=7,
        grid=(nt,),
        in_specs=[pl.BlockSpec((MOE_TM, d), lambda i, *_: (i, 0)),
                  pl.BlockSpec((TOP_K, MOE_TM), lambda i, *_: (0, i)),
                  pl.BlockSpec((1, 2 * N_EXPERTS, LANES), lambda i, *_: (i, 0, 0)),
                  pl.BlockSpec((1, SUBLANES, 2 * N_EXPERTS), lambda i, *_: (i, 0, 0)),
                  pl.BlockSpec((1, SUBLANES, 2 * N_EXPERTS), lambda i, *_: (i, 0, 0))],
        out_specs=[pl.BlockSpec(memory_space=pl.ANY),
                   pl.BlockSpec((MOE_TM, LANES), lambda i, *_: (i, 0))],
        scratch_shapes=[pltpu.VMEM((2, ROW_BUF, d), BF16), pltpu.VMEM((EXP_TM, d), BF16),
                        pltpu.SMEM((1,), I32), pltpu.SemaphoreType.DMA((2,))])
    return pl.pallas_call(
        _dispatch_kernel,
        grid_spec=grid_spec,
        out_shape=[jax.ShapeDtypeStruct((n_rows, d), BF16),
                   jax.ShapeDtypeStruct((t, LANES), F32)],
        compiler_params=_params(("arbitrary",)),
        name="moe_dispatch",
    )(plan["n16"], plan["so16"], plan["go16"], plan["rtot16"], plan["r16"], plan["rp16"],
      plan["off16"], hb, eidx, plan["lo_cols"], plan["lo_rows"], plan["hi_rows"])


def _experts_kernel(te_ref, net_ref, x_ref, wg_ref, wu_ref, wd_ref, y_ref, wgb_ref, wub_ref, wdb_ref):
    j = pl.program_id(0)
    used = j < net_ref[0]

    @pl.when(used & ((j == 0) | (te_ref[j] != te_ref[jnp.maximum(j - 1, 0)])))
    def _():
        wgb_ref[...] = wg_ref[0].astype(BF16)
        wub_ref[...] = wu_ref[0].astype(BF16)
        wdb_ref[...] = wd_ref[0].astype(BF16)

    @pl.when(used)
    def _():
        x = x_ref[...]
        hmid = _silu(_dot(x, wgb_ref[...])) * _dot(x, wub_ref[...])
        y_ref[...] = _dot(hmid.astype(BF16), wdb_ref[...]).astype(y_ref.dtype)


def _experts(xs, wg, wu, wd, plan):
    n_rows, d = xs.shape
    row_map = lambda j, te, net: (jnp.minimum(j, net[0] - 1), 0)
    w_map = lambda j, te, net: (te[jnp.minimum(j, net[0] - 1)], 0, 0)
    grid_spec = pltpu.PrefetchScalarGridSpec(
        num_scalar_prefetch=2,
        grid=(n_rows // EXP_TM,),
        in_specs=[pl.BlockSpec((EXP_TM, d), row_map),
                  pl.BlockSpec((1, d, D_EXPERT), w_map),
                  pl.BlockSpec((1, d, D_EXPERT), w_map),
                  pl.BlockSpec((1, D_EXPERT, d), w_map)],
        out_specs=pl.BlockSpec((EXP_TM, d), row_map),
        scratch_shapes=[pltpu.VMEM((d, D_EXPERT), BF16), pltpu.VMEM((d, D_EXPERT), BF16),
                        pltpu.VMEM((D_EXPERT, d), BF16)])
    return pl.pallas_call(
        _experts_kernel,
        grid_spec=grid_spec,
        out_shape=jax.ShapeDtypeStruct((n_rows, d), BF16),
        input_output_aliases={2: 0},
        compiler_params=_params(("arbitrary",)),
        name="moe_experts",
    )(plan["tile_expert"], plan["n_et"], xs, wg, wu, wd)


def _combine_kernel(n16_ref, so16_ref, go16_ref, rtot_ref,
                    ys_hbm, posT_ref, gT_ref, locol_ref, hicol_ref, h_ref,
                    sg_ref, su_ref, sd_ref, lg_ref, lb_ref, o_ref, ybuf_ref, acc_ref, pend_ref, sem):
    i = pl.program_id(0)
    slot = lax.rem(i, 2)
    ybuf = ybuf_ref.at[slot]
    tm = h_ref.shape[0]

    def fetch(tile, to_slot):
        copy = lambda dst, src: pltpu.make_async_copy(
            _chunk(ys_hbm, src), _chunk(ybuf_ref.at[to_slot], dst), sem.at[to_slot])
        pend_ref[to_slot] = _segment_copies(tile, n16_ref, so16_ref, go16_ref, copy)

    @pl.when(i == 0)
    def _():
        ybuf_ref[...] = jnp.zeros_like(ybuf_ref)
        fetch(0, 0)

    @pl.when(i + 1 < pl.num_programs(0))
    def _():
        fetch(i + 1, 1 - slot)

    h = h_ref[...]
    xb = h.astype(BF16)
    hs = _silu(_dot(xb, sg_ref[...])) * _dot(xb, su_ref[...])
    acc_ref[...] = ALPHA * h + _dot(hs.astype(BF16), sd_ref[...])
    _wait_copies(pend_ref[slot], lambda dst, src: pltpu.make_async_copy(
        _chunk(ys_hbm, src), _chunk(ybuf, dst), sem.at[slot]))

    pos2b = posT_ref[...].astype(BF16)
    g_hi, g_lo = _split(gT_ref[...])
    gate2b = (g_hi.astype(F32) + pltpu.roll(g_lo.astype(F32), N_EXPERTS, 1)).astype(BF16)
    seg_lo = jnp.broadcast_to(locol_ref[0][:, 0:1], (2 * N_EXPERTS, G_BLK))
    seg_hi = jnp.broadcast_to(hicol_ref[0][:, 0:1], (2 * N_EXPERTS, G_BLK))
    n_blk = (rtot_ref[i] * SEG + (G_BLK - 1)) // G_BLK

    def blk_body(cb, carry):
        c0 = pl.multiple_of(cb * G_BLK, G_BLK)
        c_e = (lax.broadcasted_iota(I32, (2 * N_EXPERTS, G_BLK), 1) + c0).astype(F32)
        owner = jnp.where((c_e >= seg_lo) & (c_e < seg_hi), 1.0, 0.0).astype(BF16)
        want = _dot(pos2b, owner)
        gate = _dot(gate2b, owner)
        c_t = (lax.broadcasted_iota(I32, (tm, G_BLK), 1) + (c0 + 1)).astype(F32)
        g = jnp.where(want == c_t, gate, 0.0).astype(BF16)
        acc_ref[...] += _dot(g, ybuf[pl.ds(c0, G_BLK), :])
        return carry

    lax.fori_loop(0, n_blk, blk_body, 0)
    o_ref[...] = _layer_norm(acc_ref[...], lg_ref[...], lb_ref[...])


def _combine(ys, posT, gT, h1, sg, su, sd, lg, lb, plan):
    t, d = h1.shape
    nt = t // MOE_TM
    row = lambda w: pl.BlockSpec((MOE_TM, w), lambda i, *_: (i, 0))
    full = lambda a: pl.BlockSpec(a.shape, lambda i, *_: (0, 0))
    seg = pl.BlockSpec((1, 2 * N_EXPERTS, LANES), lambda i, *_: (i, 0, 0))
    grid_spec = pltpu.PrefetchScalarGridSpec(
        num_scalar_prefetch=4,
        grid=(nt,),
        in_specs=[pl.BlockSpec(memory_space=pl.ANY), row(LANES), row(LANES), seg, seg, row(d),
                  full(sg), full(su), full(sd), full(lg), full(lb)],
        out_specs=row(d),
        scratch_shapes=[pltpu.VMEM((2, ROW_BUF, d), BF16), pltpu.VMEM((MOE_TM, d), F32),
                        pltpu.SMEM((2,), I32), pltpu.SemaphoreType.DMA((2,))])
    return pl.pallas_call(
        _combine_kernel,
        grid_spec=grid_spec,
        out_shape=jax.ShapeDtypeStruct((t, d), F32),
        compiler_params=_params(("arbitrary",)),
        name="moe_combine",
    )(plan["n16"], plan["so16"], plan["go16"], plan["rtot16"], ys, posT, gT,
      plan["lo_cols"], plan["hi_cols"], h1, sg, su, sd, lg, lb)


def _moe(h1, hb, eidx, gT, cnt, wg, wu, wd, sg, su, sd, lg, lb):
    t = h1.shape[0]
    nt = t // MOE_TM
    max_rows = nt * ROW_BUF + N_EXPERTS * EXP_TM
    n_exp_tiles = pl.cdiv(max_rows, EXP_TM)
    plan = _moe_plan(cnt[:, :, 0].astype(I32), n_exp_tiles)
    xs, posT = _dispatch(hb, eidx, plan, n_exp_tiles * EXP_TM)
    ys = _experts(xs, wg, wu, wd, plan)
    return _combine(ys, posT, gT, h1, sg, su, sd, lg, lb, plan)


def kernel(x, w_in, hg_lb_logits, hg_norm_g, q_norm_g, w_q_up, w_qidx_up, kv_norm_g, w_kv_up,
           idx_ln_g, idx_ln_b, w_out, ln1_g, ln1_b, w_router, router_bias, w_e_gate, w_e_up,
           w_e_down, w_s_gate, w_s_up, w_s_down, ln2_g, ln2_b):
    b, s, d = x.shape
    h = x
    for l in range(DEPTH):
        h2 = h.reshape(b * s, d)
        n_main = 4 * HG_WIDTH
        w_main = w_in[l, :, :n_main].astype(BF16)
        c_kv, c_idx = n_main + Q_RANK, n_main + Q_RANK + KV_RANK
        w_sel = jnp.concatenate([w_in[l, :, n_main:c_kv], w_in[l, :, c_idx:]], axis=1)
        w_sel = jnp.pad(w_sel, ((0, 0), (0, SEL_W - w_sel.shape[1])))
        w_ckv = w_in[l, :, c_kv:c_idx].astype(BF16)

        proj = _proj_main(h2, w_main).reshape(b, s, n_main)
        o_a = _hgrn(proj, hg_lb_logits, hg_norm_g[l], l)

        wqT = w_q_up[l].T.astype(BF16)
        wqiT = w_qidx_up[l].T
        wk = w_kv_up[l][:, :AT_WIDTH].astype(BF16)
        wvT = w_kv_up[l][:, AT_WIDTH:].T.astype(BF16)
        qT, qiT, k, vT, kidx, wT = _dsa_prep(h, w_sel, w_ckv, q_norm_g[l], kv_norm_g[l],
                                             idx_ln_g[l], idx_ln_b[l], wqT, wqiT, wk, wvT)
        o_b = _dsa_attn(qT, qiT, wT, kidx, k, vT)

        h1, hb, eidx, gT, cnt = _out_ln(
            o_a.reshape(b * s, HG_WIDTH), o_b.reshape(b * s, AT_WIDTH), h2,
            w_out[l, :HG_WIDTH].astype(BF16), w_out[l, HG_WIDTH:].astype(BF16),
            ln1_g[l].reshape(1, d), ln1_b[l].reshape(1, d),
            w_router[l].T, router_bias[l].reshape(N_EXPERTS, 1), MOE_TM)
        out = _moe(h1, hb, eidx, gT, cnt, w_e_gate[l], w_e_up[l], w_e_down[l],
                   w_s_gate[l].astype(BF16), w_s_up[l].astype(BF16), w_s_down[l].astype(BF16),
                   ln2_g[l].reshape(1, d), ln2_b[l].reshape(1, d))
        h = out.reshape(b, s, d)
    return h
```

```python
import functools

import jax
import jax.numpy as jnp
from jax import lax
from jax.experimental import pallas as pl
from jax.experimental.pallas import tpu as pltpu

F32 = jnp.float32
BF16 = jnp.bfloat16
I32 = jnp.int32
HIGHEST = lax.Precision.HIGHEST

CHUNK = 64
HG_HEADS = 4
HG_DK = 128
HG_DV = 128
HG_WIDTH = HG_HEADS * HG_DV
AT_HEADS = 8
AT_DH = 64
AT_WIDTH = AT_HEADS * AT_DH
Q_RANK = 256
KV_RANK = 128
IDX_HEADS = 4
IDX_DH = 64
TOPK_MAX = 256
N_EXPERTS = 64
TOP_K = 8
N_GROUPS = 8
TOPK_GROUPS = 4
D_EXPERT = 256
ROUTED_SCALE = 2.5
DEPTH = 1
ALPHA = (2.0 * DEPTH) ** 0.25
LN_EPS = 1e-5
RMS_EPS = 1e-6
LOG2E = 1.4426950408889634

LANES = 128
SUBLANES = 8
QBLK = 4 * CHUNK
KBLK = 256
VMEM_LIMIT = 56 * 1024 * 1024
INT_MIN = -2 ** 31


def _params(sem, vmem=VMEM_LIMIT):
    return pltpu.CompilerParams(dimension_semantics=sem, vmem_limit_bytes=vmem)


def _dot(a, b, precision=None):
    return jnp.dot(a, b, preferred_element_type=F32, precision=precision)


def _dot_nt(a, b, precision=None):
    return lax.dot_general(a, b, (((1,), (1,)), ((), ())), preferred_element_type=F32,
                           precision=precision)


def _dot_tn(a, b, precision=None):
    return lax.dot_general(a, b, (((0,), (0,)), ((), ())), preferred_element_type=F32,
                           precision=precision)


def _split(x):
    hi = x.astype(BF16)
    return hi, (x - hi.astype(F32)).astype(BF16)


def _cat_hhl(x, axis):
    hi, lo = _split(x)
    return jnp.concatenate([hi, hi, lo], axis=axis)


def _cat_hlh(x, axis):
    hi, lo = _split(x)
    return jnp.concatenate([hi, lo, hi], axis=axis)


def _split3(x):
    hi = x.astype(BF16)
    r = x - hi.astype(F32)
    mid = r.astype(BF16)
    return hi, mid, (r - mid.astype(F32)).astype(BF16)


def _cat_act6(x, axis):
    h, m, l = _split3(x)
    return jnp.concatenate([h, h, m, h, l, m], axis=axis)


def _cat_wgt6(x, axis):
    h, m, l = _split3(x)
    return jnp.concatenate([h, m, h, l, h, m], axis=axis)


def _sigmoid(x):
    return 1.0 / (1.0 + jnp.exp(-x))


def _silu(x):
    return x * _sigmoid(x)


def _proj_main_kernel(x_ref, w_ref, o_ref):
    o_ref[...] = _dot(x_ref[...].astype(BF16), w_ref[...])


def _proj_main(x2, w_bf, tm=256):
    t, d = x2.shape
    n = w_bf.shape[1]
    return pl.pallas_call(
        _proj_main_kernel,
        grid=(t // tm,),
        in_specs=[pl.BlockSpec((tm, d), lambda i: (i, 0)),
                  pl.BlockSpec((d, n), lambda i: (0, 0))],
        out_specs=pl.BlockSpec((tm, n), lambda i: (i, 0)),
        out_shape=jax.ShapeDtypeStruct((t, n), F32),
        compiler_params=_params(("parallel",)),
        name="proj_main",
    )(x2, w_bf)


def _hgrn_kernel(p_ref, lbl_ref, ng_ref, o_ref, st_ref, *, n_chunks, layer):
    @pl.when(pl.program_id(1) == 0)
    def _():
        st_ref[...] = jnp.zeros_like(st_ref)

    lg = lbl_ref[...]
    ex = jnp.exp(lg - jnp.max(lg, axis=0, keepdims=True))
    lb_all = jnp.sum(ex[: layer + 1], axis=0, keepdims=True) / jnp.sum(ex, axis=0, keepdims=True)
    ng = ng_ref[...]

    r = lax.broadcasted_iota(I32, (CHUNK, CHUNK), 0)
    c = lax.broadcasted_iota(I32, (CHUNK, CHUNK), 1)
    causal = c <= r
    tri = jnp.where(causal, 1.0, 0.0).astype(BF16)
    tri2 = jnp.concatenate([tri, tri], axis=1)

    units = [(sq, h) for sq in range(p_ref.shape[0]) for h in range(HG_HEADS)]

    def chunk_body(j, carry):
        rows = pl.ds(pl.multiple_of(j * CHUNK, CHUNK), CHUNK)
        ks, bs = [], []
        for sq, h in units:
            lo = h * HG_DK
            lb = lb_all[:, lo:lo + HG_DK]
            f = lb + (1.0 - lb) * _sigmoid(p_ref[sq, rows, HG_WIDTH + lo:HG_WIDTH + lo + HG_DK])
            lf_hi, lf_lo = _split(jnp.log(f))
            ks.append(1.0 - f)
            bs.append(_dot(tri2, jnp.concatenate([lf_hi, lf_lo], axis=0)))
        scs, ois, vbs = [], [], []
        for (sq, h), k, b in zip(units, ks, bs):
            lo = h * HG_DK
            b_last = b[CHUNK - 1:CHUNK, :]
            q_dec = (_silu(p_ref[sq, rows, lo:lo + HG_DK]) * jnp.exp(b)).astype(BF16)
            k_inv = (k * jnp.exp(-b)).astype(BF16)
            k_dec = (k * jnp.exp(b_last - b)).astype(BF16)
            vb = p_ref[sq, rows, 2 * HG_WIDTH + lo:2 * HG_WIDTH + lo + HG_DV].astype(BF16)
            st = st_ref[sq, h]
            scs.append(jnp.where(causal, _dot_nt(q_dec, k_inv), 0.0).astype(BF16))
            ois.append(_dot_nt(q_dec, st.astype(BF16)))
            st_ref[sq, h] = st * jnp.exp(b_last) + _dot_tn(vb, k_dec)
            vbs.append(vb)
        for (sq, h), sc, oi, vb in zip(units, scs, ois, vbs):
            lo = h * HG_DK
            o = _dot(sc, vb) + oi
            o = o * lax.rsqrt(jnp.mean(o * o, axis=-1, keepdims=True) + RMS_EPS) * ng
            gate = p_ref[sq, rows, 3 * HG_WIDTH + lo:3 * HG_WIDTH + lo + HG_DV]
            o_ref[sq, rows, lo:lo + HG_DV] = (o * _silu(gate)).astype(o_ref.dtype)
        return carry

    lax.fori_loop(0, n_chunks, chunk_body, 0)


def _hgrn(proj, lb_logits, norm_g, layer, ct=512, n_seq=2):
    b, s, w = proj.shape
    n_seq = min(n_seq, b)
    n_chunks = ct // CHUNK
    return pl.pallas_call(
        functools.partial(_hgrn_kernel, n_chunks=n_chunks, layer=layer),
        grid=(b // n_seq, s // ct),
        in_specs=[pl.BlockSpec((n_seq, ct, w), lambda i, j: (i, j, 0)),
                  pl.BlockSpec(lb_logits.shape, lambda i, j: (0, 0)),
                  pl.BlockSpec((1, HG_DV), lambda i, j: (0, 0))],
        out_specs=pl.BlockSpec((n_seq, ct, HG_WIDTH), lambda i, j: (i, j, 0)),
        out_shape=jax.ShapeDtypeStruct((b, s, HG_WIDTH), BF16),
        scratch_shapes=[pltpu.VMEM((n_seq, HG_HEADS, HG_DV, HG_DK), F32)],
        compiler_params=_params(("parallel", "arbitrary")),
        name="hgrn",
    )(proj, lb_logits, norm_g.reshape(1, HG_DV))


POS_SHIFT = 4
POS_RADIX = 1 << POS_SHIFT
KP_W = 2 * LANES
K_W = (AT_HEADS // 2) * KP_W
SEL_W = Q_RANK + LANES
KIDX_W = 6 * IDX_DH


def _dsa_prep_kernel(x_ref, wsel_ref, wckv_ref, qg_ref, kvg_ref, lng_ref, lnb_ref, wqT_ref, wqiT_ref,
                     wk_ref, wvT_ref, qT_ref, qiT_ref, k_ref, vT_ref, kidx_ref, wT_ref, w6_ref):
    @pl.when((pl.program_id(0) == 0) & (pl.program_id(1) == 0))
    def _():
        w6_ref[...] = _cat_wgt6(wsel_ref[...], 0)

    x = x_ref[...]
    sel = _dot(_cat_act6(x, 1), w6_ref[...])
    cq = sel[:, :Q_RANK]
    cqn = cq * lax.rsqrt(jnp.mean(cq * cq, axis=-1, keepdims=True) + RMS_EPS) * qg_ref[...]
    qT_ref[0] = (_dot_nt(wqT_ref[...], cqn.astype(BF16)) * (AT_DH ** -0.5 * LOG2E)).astype(qT_ref.dtype)
    qiT_ref[0] = _dot_nt(_cat_wgt6(wqiT_ref[...], 1), _cat_act6(cqn, 1))
    ckv = _dot(x.astype(BF16), wckv_ref[...])
    ckvn = (ckv * lax.rsqrt(jnp.mean(ckv * ckv, axis=-1, keepdims=True) + RMS_EPS)
            * kvg_ref[...]).astype(BF16)
    kmat = _dot(ckvn, wk_ref[...])
    tm = kmat.shape[0]
    s_abs = pl.program_id(1) * tm + lax.broadcasted_iota(I32, (tm, LANES), 0)
    lane = lax.broadcasted_iota(I32, (tm, LANES), 1)
    pos = jnp.where(lane < 3, s_abs >> POS_SHIFT,
                    jnp.where(lane < 6, s_abs & (POS_RADIX - 1), 0)).astype(F32)
    k_ref[0] = jnp.concatenate(
        [blk for j in range(AT_HEADS // 2) for blk in (kmat[:, j * LANES:(j + 1) * LANES], pos)],
        axis=1).astype(k_ref.dtype)
    for kb in range(vT_ref.shape[1]):
        vT_ref[0, kb] = _dot_nt(wvT_ref[...], ckvn[kb * KBLK:(kb + 1) * KBLK]).astype(vT_ref.dtype)
    k128 = sel[:, Q_RANK:]
    in_k = lax.broadcasted_iota(I32, (1, LANES), 1) < IDX_DH
    mu = jnp.sum(jnp.where(in_k, k128, 0.0), axis=-1, keepdims=True) * (1.0 / IDX_DH)
    dev = jnp.where(in_k, k128 - mu, 0.0)
    var = jnp.sum(dev * dev, axis=-1, keepdims=True) * (1.0 / IDX_DH)
    y = dev * lax.rsqrt(var + LN_EPS) * lng_ref[...] + lnb_ref[...]
    h, m, l = (p.astype(F32) for p in _split3(y))
    up = lambda p: pltpu.roll(p, IDX_DH, 1)
    kidx_ref[0] = jnp.concatenate([h + up(h), m + up(h), l + up(m)], axis=1).astype(BF16)
    tT = k128.T
    wT_ref[0] = tT[IDX_DH:IDX_DH + SUBLANES, :] * (IDX_HEADS ** -0.5 * IDX_DH ** -0.5)


def _dsa_prep(x, w_sel, w_ckv, q_norm_g, kv_norm_g, idx_ln_g, idx_ln_b, wqT, wqiT, wk, wvT):
    b, s, d = x.shape
    tm = min(2 * KBLK, s)
    kb_per = tm // KBLK
    nb = s // KBLK
    full = lambda a: pl.BlockSpec(a.shape, lambda i, j: (0,) * a.ndim)
    pad_lanes = lambda v: jnp.pad(v.reshape(1, -1), ((0, 0), (0, LANES - v.shape[0])))
    args = (w_sel, w_ckv, q_norm_g.reshape(1, -1), kv_norm_g.reshape(1, -1),
            pad_lanes(idx_ln_g), pad_lanes(idx_ln_b), wqT, wqiT, wk, wvT)
    return pl.pallas_call(
        _dsa_prep_kernel,
        grid=(b, s // tm),
        in_specs=[pl.BlockSpec((None, tm, d), lambda i, j: (i, j, 0))] + [full(a) for a in args],
        out_specs=[pl.BlockSpec((1, AT_WIDTH, tm), lambda i, j: (i, 0, j)),
                   pl.BlockSpec((1, IDX_HEADS * IDX_DH, tm), lambda i, j: (i, 0, j)),
                   pl.BlockSpec((1, tm, K_W), lambda i, j: (i, j, 0)),
                   pl.BlockSpec((1, kb_per, AT_WIDTH, KBLK), lambda i, j: (i, j, 0, 0)),
                   pl.BlockSpec((1, tm, KIDX_W), lambda i, j: (i, j, 0)),
                   pl.BlockSpec((1, SUBLANES, tm), lambda i, j: (i, 0, j))],
        out_shape=[jax.ShapeDtypeStruct((b, AT_WIDTH, s), BF16),
                   jax.ShapeDtypeStruct((b, IDX_HEADS * IDX_DH, s), F32),
                   jax.ShapeDtypeStruct((b, s, K_W), BF16),
                   jax.ShapeDtypeStruct((b, nb, AT_WIDTH, KBLK), BF16),
                   jax.ShapeDtypeStruct((b, s, KIDX_W), BF16),
                   jax.ShapeDtypeStruct((b, SUBLANES, s), F32)],
        scratch_shapes=[pltpu.VMEM((6 * d, SEL_W), BF16)],
        compiler_params=_params(("arbitrary", "arbitrary")),
        name="dsa_prep",
    )(x, *args)


def _sortable(x):
    bits = lax.bitcast_convert_type(x, I32)
    return jnp.where(bits < 0, bits ^ jnp.int32(0x7FFFFFFF), bits)


def _dsa_attn_kernel(qT_ref, qiT_ref, wT_ref, kidx_ref, k_ref, vT_ref, o_ref,
                     keys_ref, q2_ref, qbd_ref, acc_ref, m_ref, l_ref, *, k_top):
    qb = pl.program_id(1)
    t0 = qb * QBLK
    e_max = t0 + QBLK
    nkb = (e_max + KBLK - 1) >> 8

    lane = lax.broadcasted_iota(I32, (1, QBLK), 1)
    t_idx = t0 + lane
    end_t = ((t_idx >> 6) + 1) * CHUNK
    srow = lax.broadcasted_iota(I32, (KBLK, QBLK), 0)

    for h in range(IDX_HEADS):
        q2_ref[:, h * QBLK:(h + 1) * QBLK] = _cat_wgt6(qiT_ref[0, h * IDX_DH:(h + 1) * IDX_DH, :], 0)
    wrow = jnp.concatenate([wT_ref[0, h:h + 1, :] for h in range(IDX_HEADS)], axis=1)

    def score_body(kb, carry):
        rows = pl.ds(pl.multiple_of(kb * KBLK, KBLK), KBLK)
        logits = _dot(kidx_ref[0, rows, :], q2_ref[...])
        r = jnp.maximum(logits, 0.0) * wrow
        sc = r[:, :QBLK]
        for h in range(1, IDX_HEADS):
            sc = sc + r[:, h * QBLK:(h + 1) * QBLK]
        valid = (srow + kb * KBLK) < end_t
        keys_ref[rows, :] = _sortable(jnp.where(valid, sc, -jnp.inf))
        return carry

    lax.fori_loop(0, nkb, score_body, 0)

    def count_ge(cand):
        def body(kb, acc):
            rows = pl.ds(pl.multiple_of(kb * KBLK, KBLK), KBLK)
            hit = jnp.where(keys_ref[rows, :] >= cand, 1, 0).astype(I32)
            return acc + jnp.sum(hit.reshape(KBLK // SUBLANES, SUBLANES, QBLK), axis=0)
        acc = lax.fori_loop(0, nkb, body, jnp.zeros((SUBLANES, QBLK), I32))
        return jnp.sum(acc, axis=0, keepdims=True)

    zero = jnp.zeros((1, QBLK), I32)
    thr = jnp.where(count_ge(zero) >= k_top, zero, jnp.full((1, QBLK), INT_MIN, I32))

    def bit_body(i, thr):
        cand = thr | (jnp.int32(1) << (30 - i))
        return jnp.where(count_ge(cand) >= k_top, cand, thr)

    thr = lax.fori_loop(0, 31, bit_body, thr)
    need = (k_top - (count_ge(thr + 1))).astype(F32)

    log2e = [p.astype(F32) for p in _split3(jnp.full((1, QBLK), LOG2E, F32))]
    ci16 = lax.broadcasted_iota(I32, (2 * SUBLANES, QBLK), 0)

    def pos_coef(h):
        slope = 2.0 ** (-8.0 * (h + 1) / AT_HEADS)
        blk = jnp.zeros((2 * SUBLANES, QBLK), F32)
        for i in range(3):
            blk = jnp.where(ci16 == i, log2e[i] * (POS_RADIX * slope), blk)
            blk = jnp.where(ci16 == 3 + i, log2e[i] * slope, blk)
        return blk.astype(BF16)

    for j in range(AT_HEADS // 2):
        pair = qT_ref[0, j * LANES:(j + 1) * LANES, :]
        rr = lax.broadcasted_iota(I32, pair.shape, 0)
        zeros = jnp.zeros_like(pair)
        qbd_ref[j, :LANES, :QBLK] = jnp.where(rr < AT_DH, pair, zeros)
        qbd_ref[j, :LANES, QBLK:] = jnp.where(rr >= AT_DH, pair, zeros)
        qbd_ref[j, LANES:LANES + 2 * SUBLANES, :QBLK] = pos_coef(2 * j)
        qbd_ref[j, LANES:LANES + 2 * SUBLANES, QBLK:] = pos_coef(2 * j + 1)
        qbd_ref[j, LANES + 2 * SUBLANES:, :] = jnp.zeros((KP_W - LANES - 2 * SUBLANES, 2 * QBLK), BF16)
    acc_ref[...] = jnp.zeros_like(acc_ref)
    m_ref[...] = jnp.full_like(m_ref, -jnp.inf)
    l_ref[...] = jnp.zeros_like(l_ref)

    ri = lax.broadcasted_iota(I32, (KBLK, KBLK), 0)
    ci = lax.broadcasted_iota(I32, (KBLK, KBLK), 1)
    lstrict = jnp.where(ci < ri, 1.0, 0.0).astype(BF16)

    def attn_body(kb, carry, last=False):
        rows = pl.ds(pl.multiple_of(kb * KBLK, KBLK), KBLK)
        kblk = keys_ref[rows, :]
        s_idx = srow + kb * KBLK
        eq = kblk == thr
        eqf = jnp.where(eq, 1.0, 0.0)
        before = _dot(lstrict, eqf.astype(BF16)) + carry
        sel = ((kblk > thr) | (eq & (before < need))) & (s_idx < end_t)
        bias = jnp.where(sel, 0.0, -jnp.inf)
        if last:
            ahead = jnp.maximum(s_idx - t_idx, 0).astype(F32)
        s2s = [_dot(k_ref[0, rows, j * KP_W:(j + 1) * KP_W], qbd_ref[j])
               for j in range(AT_HEADS // 2)]
        for j in range(AT_HEADS // 2):
            ps, alphas = [], []
            for half in range(2):
                h = 2 * j + half
                st = s2s[j][:, half * QBLK:(half + 1) * QBLK] + bias
                if last:
                    st = st - (2.0 * LOG2E * 2.0 ** (-8.0 * (h + 1) / AT_HEADS)) * ahead
                m_old = m_ref[h:h + 1, :]
                m_new = jnp.maximum(m_old, jnp.max(st, axis=0, keepdims=True))
                m_safe = jnp.where(m_new == -jnp.inf, 0.0, m_new)
                alpha = jnp.exp2(m_old - m_safe)
                p = jnp.exp2(st - m_safe)
                l_ref[h:h + 1, :] = alpha * l_ref[h:h + 1, :] + jnp.sum(p, axis=0, keepdims=True)
                m_ref[h:h + 1, :] = m_new
                ps.append(p.astype(BF16))
                alphas.append(alpha)
            o2 = _dot(vT_ref[0, kb, j * LANES:(j + 1) * LANES, :], jnp.concatenate(ps, axis=1))
            for half in range(2):
                hs = slice((2 * j + half) * AT_DH, (2 * j + half + 1) * AT_DH)
                acc_ref[hs, :] = (alphas[half] * acc_ref[hs, :]
                                  + o2[half * AT_DH:(half + 1) * AT_DH, half * QBLK:(half + 1) * QBLK])
        return carry + jnp.sum(eqf, axis=0, keepdims=True)

    ties_seen = lax.fori_loop(0, nkb - 1, attn_body, jnp.zeros((1, QBLK), F32))
    attn_body(nkb - 1, ties_seen, last=True)

    for h in range(AT_HEADS):
        hs = slice(h * AT_DH, (h + 1) * AT_DH)
        acc_ref[hs, :] = acc_ref[hs, :] / l_ref[h:h + 1, :]
    o_ref[0] = acc_ref[...].T.astype(o_ref.dtype)


def _dsa_attn(qT, qiT, wT, kidx, k, vT):
    b, _, s = qT.shape
    nb = s // KBLK
    k_top = min(TOPK_MAX, s // 4)
    return pl.pallas_call(
        functools.partial(_dsa_attn_kernel, k_top=k_top),
        grid=(b, s // QBLK),
        in_specs=[pl.BlockSpec((1, AT_WIDTH, QBLK), lambda i, j: (i, 0, j)),
                  pl.BlockSpec((1, IDX_HEADS * IDX_DH, QBLK), lambda i, j: (i, 0, j)),
                  pl.BlockSpec((1, SUBLANES, QBLK), lambda i, j: (i, 0, j)),
                  pl.BlockSpec((1, s, KIDX_W), lambda i, j: (i, 0, 0)),
                  pl.BlockSpec((1, s, K_W), lambda i, j: (i, 0, 0)),
                  pl.BlockSpec((1, nb, AT_WIDTH, KBLK), lambda i, j: (i, 0, 0, 0))],
        out_specs=pl.BlockSpec((1, QBLK, AT_WIDTH), lambda i, j: (i, j, 0)),
        out_shape=jax.ShapeDtypeStruct((b, s, AT_WIDTH), BF16),
        scratch_shapes=[pltpu.VMEM((s, QBLK), I32),
                        pltpu.VMEM((KIDX_W, IDX_HEADS * QBLK), BF16),
                        pltpu.VMEM((AT_HEADS // 2, KP_W, 2 * QBLK), BF16),
                        pltpu.VMEM((AT_WIDTH, QBLK), F32),
                        pltpu.VMEM((AT_HEADS, QBLK), F32),
                        pltpu.VMEM((AT_HEADS, QBLK), F32)],
        compiler_params=_params(("parallel", "arbitrary")),
        name="dsa_attn",
    )(qT, qiT, wT, kidx, k, vT)


def _layer_norm(v, g, b):
    mu = jnp.mean(v, axis=-1, keepdims=True)
    var = jnp.mean(jnp.square(v - mu), axis=-1, keepdims=True)
    return (v - mu) * lax.rsqrt(var + LN_EPS) * g + b


def _out_ln_kernel(oa_ref, ob_ref, x_ref, wa_ref, wb_ref, g_ref, b_ref, wrT_ref, bias_ref,
                   h_ref, hb_ref, eidx_ref, gT_ref, cnt_ref):
    mix = _dot(oa_ref[...], wa_ref[...]) + _dot(ob_ref[...], wb_ref[...])
    h = _layer_norm(ALPHA * x_ref[...] + mix, g_ref[...], b_ref[...])
    h_ref[...] = h
    hb_ref[...] = h.astype(BF16)
    eidx_ref[...], gT_ref[...], cnt_ref[0] = _route(h, wrT_ref[...], bias_ref[...])


def _out_ln(oa, ob, x2, wa, wb, g, b, wrT, bias_col, tm):
    t, d = x2.shape
    nt = t // tm
    row = lambda w: pl.BlockSpec((tm, w), lambda i: (i, 0))
    full = lambda a: pl.BlockSpec(a.shape, lambda i: (0, 0))
    return pl.pallas_call(
        _out_ln_kernel,
        grid=(nt,),
        in_specs=[row(oa.shape[1]), row(ob.shape[1]), row(d), full(wa), full(wb), full(g), full(b),
                  full(wrT), full(bias_col)],
        out_specs=[row(d), row(d),
                   pl.BlockSpec((TOP_K, tm), lambda i: (0, i)),
                   row(LANES),
                   pl.BlockSpec((1, N_EXPERTS, LANES), lambda i: (i, 0, 0))],
        out_shape=[jax.ShapeDtypeStruct((t, d), F32), jax.ShapeDtypeStruct((t, d), BF16),
                   jax.ShapeDtypeStruct((TOP_K, t), I32),
                   jax.ShapeDtypeStruct((t, LANES), F32),
                   jax.ShapeDtypeStruct((nt, N_EXPERTS, LANES), F32)],
        compiler_params=_params(("parallel",)),
        name="out_ln_route",
    )(oa, ob, x2, wa, wb, g, b, wrT, bias_col)


def _first_max(v, idx, big):
    m = jnp.max(v, axis=0, keepdims=True)
    first = jnp.min(jnp.where(v == m, idx, big), axis=0, keepdims=True)
    return m, first


def _route(h, wrT, bias_col):
    tm = h.shape[0]
    per_group = N_EXPERTS // N_GROUPS
    logits = _dot_nt(_cat_hlh(wrT, 1), _cat_hhl(h, 1))
    scores = _sigmoid(logits)
    sel = scores + bias_col
    iota_g = lax.broadcasted_iota(I32, (per_group, tm), 0)
    neg = -jnp.inf

    grp = jnp.zeros((N_GROUPS, tm), F32)
    iota_grp = lax.broadcasted_iota(I32, (N_GROUPS, tm), 0)
    for g in range(N_GROUPS):
        blk = sel[g * per_group:(g + 1) * per_group, :]
        m1, i1 = _first_max(blk, iota_g, per_group)
        m2 = jnp.max(jnp.where(iota_g == i1, neg, blk), axis=0, keepdims=True)
        grp = jnp.where(iota_grp == g, m1 + m2, grp)

    gmask = jnp.zeros((N_GROUPS, tm), F32)
    work = grp
    for _ in range(TOPK_GROUPS):
        _, gi = _first_max(work, iota_grp, N_GROUPS)
        hit = iota_grp == gi
        gmask = jnp.where(hit, 1.0, gmask)
        work = jnp.where(hit, neg, work)

    e_mask = jnp.concatenate(
        [jnp.broadcast_to(gmask[g:g + 1, :], (per_group, tm)) for g in range(N_GROUPS)], axis=0)
    iota_e = lax.broadcasted_iota(I32, (N_EXPERTS, tm), 0)
    work = jnp.where(e_mask > 0.5, sel, neg)
    iota_k = lax.broadcasted_iota(I32, (TOP_K, tm), 0)
    eidx = jnp.zeros((TOP_K, tm), I32)
    w = jnp.zeros((N_EXPERTS, tm), F32)
    chosen = jnp.zeros((N_EXPERTS, tm), F32)
    for k in range(TOP_K):
        _, ei = _first_max(work, iota_e, N_EXPERTS)
        hit = iota_e == ei
        eidx = jnp.where(iota_k == k, ei, eidx)
        w = jnp.where(hit, scores, w)
        chosen = jnp.where(hit, 1.0, chosen)
        work = jnp.where(hit, neg, work)

    gates = w / jnp.sum(w, axis=0, keepdims=True) * ROUTED_SCALE
    gates = jnp.concatenate([gates, jnp.zeros((LANES - N_EXPERTS, tm), F32)], axis=0)
    cnt = jnp.broadcast_to(jnp.sum(chosen, axis=1, keepdims=True), (N_EXPERTS, LANES))
    return eidx, gates.T, cnt


MOE_TM = 512
SEG = 16
EXP_TM = 512
ROW_BUF = TOP_K * MOE_TM + N_EXPERTS * SEG
P_BLK = 512
G_BLK = 512


def _moe_plan(cnt, n_exp_tiles):
    nt = cnt.shape[0]
    n16 = (cnt + (SEG - 1)) // SEG
    so16 = jnp.cumsum(n16, axis=1) - n16
    r16 = jnp.sum(n16, axis=0)
    per = EXP_TM // SEG
    rp16 = (r16 + (per - 1)) // per * per
    ends = jnp.cumsum(rp16)
    off16 = ends - rp16
    go16 = off16[None, :] + jnp.cumsum(n16, axis=0) - n16
    n_et = (ends[-1] // per).astype(I32).reshape(1)
    tiles = jnp.arange(n_exp_tiles, dtype=I32)
    tile_expert = jnp.sum((ends // per)[None, :] <= tiles[:, None], axis=1)
    tile_expert = jnp.minimum(tile_expert, N_EXPERTS - 1).astype(I32)
    flat = lambda a: a.reshape(nt * N_EXPERTS).astype(I32)
    twice = lambda a: jnp.concatenate([a, a], axis=1).astype(F32)
    seg_lo, seg_hi = twice(so16 * SEG), twice((so16 + n16) * SEG)
    as_rows = lambda a: jnp.broadcast_to(a[:, None, :], (nt, SUBLANES, 2 * N_EXPERTS))
    as_cols = lambda a: jnp.broadcast_to(a[:, :, None], (nt, 2 * N_EXPERTS, LANES))
    return dict(n16=flat(n16), so16=flat(so16), go16=flat(go16),
                rtot16=jnp.sum(n16, axis=1).astype(I32), r16=r16.astype(I32), rp16=rp16.astype(I32),
                off16=off16.astype(I32), n_et=n_et, tile_expert=tile_expert,
                lo_rows=as_rows(seg_lo), hi_rows=as_rows(seg_hi),
                lo_cols=as_cols(seg_lo), hi_cols=as_cols(seg_hi))


def _chunk(ref, c):
    return ref.at[pl.ds(pl.multiple_of(c * SEG, SEG), SEG)]


def _segment_copies(i, n16_ref, so16_ref, go16_ref, make_copy):
    def expert_body(e, total):
        n = n16_ref[i * N_EXPERTS + e]
        so = so16_ref[i * N_EXPERTS + e]
        go = go16_ref[i * N_EXPERTS + e]

        def chunk_body(c, carry):
            make_copy(so + c, go + c).start()
            return carry

        lax.fori_loop(0, n, chunk_body, 0)
        return total + n

    return lax.fori_loop(0, N_EXPERTS, expert_body, 0)


def _wait_copies(count, make_copy):
    def body(c, carry):
        make_copy(0, 0).wait()
        return carry

    lax.fori_loop(0, count, body, 0)


POS_SPLIT = 64


def _dispatch_kernel(n16_ref, so16_ref, go16_ref, rtot_ref, r16_ref, rp16_ref, off16_ref,
                     x_ref, eidx_ref, locol_ref, lorow_ref, hirow_ref,
                     xs_hbm, posT_ref, buf_ref, zero_ref, pend_ref, sem):
    i = pl.program_id(0)
    slot = lax.rem(i, 2)
    buf = buf_ref.at[slot]
    tm = x_ref.shape[0]
    eidx = eidx_ref[...]
    iota_e = lax.broadcasted_iota(I32, (N_EXPERTS, tm), 0)
    onehot = jnp.zeros((N_EXPERTS, tm), F32)
    for k in range(TOP_K):
        onehot = jnp.where(iota_e == eidx[k:k + 1, :], 1.0, onehot)
    earlier = lax.broadcasted_iota(I32, (tm, tm), 0) < lax.broadcasted_iota(I32, (tm, tm), 1)
    rank = _dot(onehot.astype(BF16), jnp.where(earlier, 1.0, 0.0).astype(BF16))
    pos1 = jnp.where(onehot > 0.5, locol_ref[0][:N_EXPERTS, 0:1] + rank + 1.0, 0.0)
    pos_hi = jnp.floor(pos1 * (1.0 / POS_SPLIT)) * POS_SPLIT
    pos2 = jnp.concatenate([pos_hi, pos1 - pos_hi], axis=0)
    posT_ref[...] = pos2.T
    pos2b = pos2.astype(BF16)

    x = x_ref[...]
    seg_lo = lorow_ref[0][0:1, :]
    seg_hi = hirow_ref[0][0:1, :]
    n_blk = (rtot_ref[i] * SEG + (P_BLK - 1)) // P_BLK

    def blk_body(rb, carry):
        r0 = pl.multiple_of(rb * P_BLK, P_BLK)
        r_e = (lax.broadcasted_iota(I32, (P_BLK, 2 * N_EXPERTS), 0) + r0).astype(F32)
        owner = jnp.where((r_e >= seg_lo) & (r_e < seg_hi), 1.0, 0.0).astype(BF16)
        want = _dot(owner, pos2b)
        r_t = (lax.broadcasted_iota(I32, (P_BLK, tm), 0) + (r0 + 1)).astype(F32)
        p = jnp.where(want == r_t, 1.0, 0.0).astype(BF16)
        buf[pl.ds(r0, P_BLK), :] = _dot(p, x).astype(BF16)
        return carry

    lax.fori_loop(0, n_blk, blk_body, 0)

    @pl.when(i > 0)
    def _():
        prev_copy = lambda src, dst: pltpu.make_async_copy(
            _chunk(buf_ref.at[1 - slot], src), _chunk(xs_hbm, dst), sem.at[1 - slot])
        _wait_copies(pend_ref[0], prev_copy)

    out_copy = lambda src, dst: pltpu.make_async_copy(_chunk(buf, src), _chunk(xs_hbm, dst),
                                                      sem.at[slot])
    pend_ref[0] = _segment_copies(i, n16_ref, so16_ref, go16_ref, out_copy)

    @pl.when(i == pl.num_programs(0) - 1)
    def _():
        _wait_copies(pend_ref[0], out_copy)
        zero_ref[...] = jnp.zeros_like(zero_ref)
        zero_copy = lambda src, dst: pltpu.make_async_copy(_chunk(zero_ref, 0), _chunk(xs_hbm, dst),
                                                           sem.at[slot])

        def zero_chunk(c, carry):
            zero_copy(0, c).start()
            return carry

        def tail_body(e, total):
            lax.fori_loop(off16_ref[e] + r16_ref[e], off16_ref[e] + rp16_ref[e], zero_chunk, 0)
            return total + rp16_ref[e] - r16_ref[e]

        _wait_copies(lax.fori_loop(0, N_EXPERTS, tail_body, 0), zero_copy)

        per = EXP_TM // SEG
        first = (off16_ref[N_EXPERTS - 1] + rp16_ref[N_EXPERTS - 1]) // per
        n_tiles = xs_hbm.shape[0] // EXP_TM
        tile_copy = lambda tile: pltpu.make_async_copy(
            zero_ref, xs_hbm.at[pl.ds(pl.multiple_of(tile * EXP_TM, EXP_TM), EXP_TM)], sem.at[slot])

        def zero_tile(tile, carry):
            tile_copy(tile).start()
            return carry

        def wait_tile(tile, carry):
            tile_copy(0).wait()
            return carry

        lax.fori_loop(first, n_tiles, zero_tile, 0)
        lax.fori_loop(first, n_tiles, wait_tile, 0)


def _dispatch(hb, eidx, plan, n_rows):
    t, d = hb.shape
    nt = t // MOE_TM
    grid_spec = pltpu.PrefetchScalarGridSpec(
        num_scalar_prefetch=7,
        grid=(nt,),
        in_specs=[pl.BlockSpec((MOE_TM, d), lambda i, *_: (i, 0)),
                  pl.BlockSpec((TOP_K, MOE_TM), lambda i, *_: (0, i)),
                  pl.BlockSpec((1, 2 * N_EXPERTS, LANES), lambda i, *_: (i, 0, 0)),
                  pl.BlockSpec((1, SUBLANES, 2 * N_EXPERTS), lambda i, *_: (i, 0, 0)),
                  pl.BlockSpec((1, SUBLANES, 2 * N_EXPERTS), lambda i, *_: (i, 0, 0))],
        out_specs=[pl.BlockSpec(memory_space=pl.ANY),
                   pl.BlockSpec((MOE_TM, LANES), lambda i, *_: (i, 0))],
        scratch_shapes=[pltpu.VMEM((2, ROW_BUF, d), BF16), pltpu.VMEM((EXP_TM, d), BF16),
                        pltpu.SMEM((1,), I32), pltpu.SemaphoreType.DMA((2,))])
    return pl.pallas_call(
        _dispatch_kernel,
        grid_spec=grid_spec,
        out_shape=[jax.ShapeDtypeStruct((n_rows, d), BF16),
                   jax.ShapeDtypeStruct((t, LANES), F32)],
        compiler_params=_params(("arbitrary",)),
        name="moe_dispatch",
    )(plan["n16"], plan["so16"], plan["go16"], plan["rtot16"], plan["r16"], plan["rp16"],
      plan["off16"], hb, eidx, plan["lo_cols"], plan["lo_rows"], plan["hi_rows"])


def _experts_kernel(te_ref, net_ref, x_ref, wg_ref, wu_ref, wd_ref, y_ref, wgb_ref, wub_ref, wdb_ref):
    j = pl.program_id(0)
    used = j < net_ref[0]

    @pl.when(used & ((j == 0) | (te_ref[j] != te_ref[jnp.maximum(j - 1, 0)])))
    def _():
        wgb_ref[...] = wg_ref[0].astype(BF16)
        wub_ref[...] = wu_ref[0].astype(BF16)
        wdb_ref[...] = wd_ref[0].astype(BF16)

    @pl.when(used)
    def _():
        x = x_ref[...]
        hmid = _silu(_dot(x, wgb_ref[...])) * _dot(x, wub_ref[...])
        y_ref[...] = _dot(hmid.astype(BF16), wdb_ref[...]).astype(y_ref.dtype)


def _experts(xs, wg, wu, wd, plan):
    n_rows, d = xs.shape
    row_map = lambda j, te, net: (jnp.minimum(j, net[0] - 1), 0)
    w_map = lambda j, te, net: (te[jnp.minimum(j, net[0] - 1)], 0, 0)
    grid_spec = pltpu.PrefetchScalarGridSpec(
        num_scalar_prefetch=2,
        grid=(n_rows // EXP_TM,),
        in_specs=[pl.BlockSpec((EXP_TM, d), row_map),
                  pl.BlockSpec((1, d, D_EXPERT), w_map),
                  pl.BlockSpec((1, d, D_EXPERT), w_map),
                  pl.BlockSpec((1, D_EXPERT, d), w_map)],
        out_specs=pl.BlockSpec((EXP_TM, d), row_map),
        scratch_shapes=[pltpu.VMEM((d, D_EXPERT), BF16), pltpu.VMEM((d, D_EXPERT), BF16),
                        pltpu.VMEM((D_EXPERT, d), BF16)])
    return pl.pallas_call(
        _experts_kernel,
        grid_spec=grid_spec,
        out_shape=jax.ShapeDtypeStruct((n_rows, d), BF16),
        input_output_aliases={2: 0},
        compiler_params=_params(("arbitrary",)),
        name="moe_experts",
    )(plan["tile_expert"], plan["n_et"], xs, wg, wu, wd)


def _combine_kernel(n16_ref, so16_ref, go16_ref, rtot_ref,
                    ys_hbm, posT_ref, gT_ref, locol_ref, hicol_ref, h_ref,
                    sg_ref, su_ref, sd_ref, lg_ref, lb_ref, o_ref, ybuf_ref, acc_ref, pend_ref, sem):
    i = pl.program_id(0)
    slot = lax.rem(i, 2)
    ybuf = ybuf_ref.at[slot]
    tm = h_ref.shape[0]

    def fetch(tile, to_slot):
        copy = lambda dst, src: pltpu.make_async_copy(
            _chunk(ys_hbm, src), _chunk(ybuf_ref.at[to_slot], dst), sem.at[to_slot])
        pend_ref[to_slot] = _segment_copies(tile, n16_ref, so16_ref, go16_ref, copy)

    @pl.when(i == 0)
    def _():
        ybuf_ref[...] = jnp.zeros_like(ybuf_ref)
        fetch(0, 0)

    @pl.when(i + 1 < pl.num_programs(0))
    def _():
        fetch(i + 1, 1 - slot)

    h = h_ref[...]
    xb = h.astype(BF16)
    hs = _silu(_dot(xb, sg_ref[...])) * _dot(xb, su_ref[...])
    acc_ref[...] = ALPHA * h + _dot(hs.astype(BF16), sd_ref[...])
    _wait_copies(pend_ref[slot], lambda dst, src: pltpu.make_async_copy(
        _chunk(ys_hbm, src), _chunk(ybuf, dst), sem.at[slot]))

    pos2b = posT_ref[...].astype(BF16)
    g_hi, g_lo = _split(gT_ref[...])
    gate2b = (g_hi.astype(F32) + pltpu.roll(g_lo.astype(F32), N_EXPERTS, 1)).astype(BF16)
    seg_lo = jnp.broadcast_to(locol_ref[0][:, 0:1], (2 * N_EXPERTS, G_BLK))
    seg_hi = jnp.broadcast_to(hicol_ref[0][:, 0:1], (2 * N_EXPERTS, G_BLK))
    n_blk = (rtot_ref[i] * SEG + (G_BLK - 1)) // G_BLK

    def blk_body(cb, carry):
        c0 = pl.multiple_of(cb * G_BLK, G_BLK)
        c_e = (lax.broadcasted_iota(I32, (2 * N_EXPERTS, G_BLK), 1) + c0).astype(F32)
        owner = jnp.where((c_e >= seg_lo) & (c_e < seg_hi), 1.0, 0.0).astype(BF16)
        want = _dot(pos2b, owner)
        gate = _dot(gate2b, owner)
        c_t = (lax.broadcasted_iota(I32, (tm, G_BLK), 1) + (c0 + 1)).astype(F32)
        g = jnp.where(want == c_t, gate, 0.0).astype(BF16)
        acc_ref[...] += _dot(g, ybuf[pl.ds(c0, G_BLK), :])
        return carry

    lax.fori_loop(0, n_blk, blk_body, 0)
    o_ref[...] = _layer_norm(acc_ref[...], lg_ref[...], lb_ref[...])


def _combine(ys, posT, gT, h1, sg, su, sd, lg, lb, plan):
    t, d = h1.shape
    nt = t // MOE_TM
    row = lambda w: pl.BlockSpec((MOE_TM, w), lambda i, *_: (i, 0))
    full = lambda a: pl.BlockSpec(a.shape, lambda i, *_: (0, 0))
    seg = pl.BlockSpec((1, 2 * N_EXPERTS, LANES), lambda i, *_: (i, 0, 0))
    grid_spec = pltpu.PrefetchScalarGridSpec(
        num_scalar_prefetch=4,
        grid=(nt,),
        in_specs=[pl.BlockSpec(memory_space=pl.ANY), row(LANES), row(LANES), seg, seg, row(d),
                  full(sg), full(su), full(sd), full(lg), full(lb)],
        out_specs=row(d),
        scratch_shapes=[pltpu.VMEM((2, ROW_BUF, d), BF16), pltpu.VMEM((MOE_TM, d), F32),
                        pltpu.SMEM((2,), I32), pltpu.SemaphoreType.DMA((2,))])
    return pl.pallas_call(
        _combine_kernel,
        grid_spec=grid_spec,
        out_shape=jax.ShapeDtypeStruct((t, d), F32),
        compiler_params=_params(("arbitrary",)),
        name="moe_combine",
    )(plan["n16"], plan["so16"], plan["go16"], plan["rtot16"], ys, posT, gT,
      plan["lo_cols"], plan["hi_cols"], h1, sg, su, sd, lg, lb)


def _moe(h1, hb, eidx, gT, cnt, wg, wu, wd, sg, su, sd, lg, lb):
    t = h1.shape[0]
    nt = t // MOE_TM
    max_rows = nt * ROW_BUF + N_EXPERTS * EXP_TM
    n_exp_tiles = pl.cdiv(max_rows, EXP_TM)
    plan = _moe_plan(cnt[:, :, 0].astype(I32), n_exp_tiles)
    xs, posT = _dispatch(hb, eidx, plan, n_exp_tiles * EXP_TM)
    ys = _experts(xs, wg, wu, wd, plan)
    return _combine(ys, posT, gT, h1, sg, su, sd, lg, lb, plan)


def kernel(x, w_in, hg_lb_logits, hg_norm_g, q_norm_g, w_q_up, w_qidx_up, kv_norm_g, w_kv_up,
           idx_ln_g, idx_ln_b, w_out, ln1_g, ln1_b, w_router, router_bias, w_e_gate, w_e_up,
           w_e_down, w_s_gate, w_s_up, w_s_down, ln2_g, ln2_b):
    b, s, d = x.shape
    h = x
    for l in range(DEPTH):
        h2 = h.reshape(b * s, d)
        n_main = 4 * HG_WIDTH
        w_main = w_in[l, :, :n_main].astype(BF16)
        c_kv, c_idx = n_main + Q_RANK, n_main + Q_RANK + KV_RANK
        w_sel = jnp.concatenate([w_in[l, :, n_main:c_kv], w_in[l, :, c_idx:]], axis=1)
        w_sel = jnp.pad(w_sel, ((0, 0), (0, SEL_W - w_sel.shape[1])))
        w_ckv = w_in[l, :, c_kv:c_idx].astype(BF16)

        proj = _proj_main(h2, w_main).reshape(b, s, n_main)
        o_a = _hgrn(proj, hg_lb_logits, hg_norm_g[l], l)

        wqT = w_q_up[l].T.astype(BF16)
        wqiT = w_qidx_up[l].T
        wk = w_kv_up[l][:, :AT_WIDTH].astype(BF16)
        wvT = w_kv_up[l][:, AT_WIDTH:].T.astype(BF16)
        qT, qiT, k, vT, kidx, wT = _dsa_prep(h, w_sel, w_ckv, q_norm_g[l], kv_norm_g[l],
                                             idx_ln_g[l], idx_ln_b[l], wqT, wqiT, wk, wvT)
        o_b = _dsa_attn(qT, qiT, wT, kidx, k, vT)

        h1, hb, eidx, gT, cnt = _out_ln(
            o_a.reshape(b * s, HG_WIDTH), o_b.reshape(b * s, AT_WIDTH), h2,
            w_out[l, :HG_WIDTH].astype(BF16), w_out[l, HG_WIDTH:].astype(BF16),
            ln1_g[l].reshape(1, d), ln1_b[l].reshape(1, d),
            w_router[l].T, router_bias[l].reshape(N_EXPERTS, 1), MOE_TM)
        out = _moe(h1, hb, eidx, gT, cnt, w_e_gate[l], w_e_up[l], w_e_down[l],
                   w_s_gate[l].astype(BF16), w_s_up[l].astype(BF16), w_s_down[l].astype(BF16),
                   ln2_g[l].reshape(1, d), ln2_b[l].reshape(1, d))
        h = out.reshape(b, s, d)
    return h
```

```python
import functools

import jax
import jax.numpy as jnp
from jax import lax
from jax.experimental import pallas as pl
from jax.experimental.pallas import tpu as pltpu

F32 = jnp.float32
BF16 = jnp.bfloat16
I32 = jnp.int32
HIGHEST = lax.Precision.HIGHEST

CHUNK = 64
HG_HEADS = 4
HG_DK = 128
HG_DV = 128
HG_WIDTH = HG_HEADS * HG_DV
AT_HEADS = 8
AT_DH = 64
AT_WIDTH = AT_HEADS * AT_DH
Q_RANK = 256
KV_RANK = 128
IDX_HEADS = 4
IDX_DH = 64
TOPK_MAX = 256
N_EXPERTS = 64
TOP_K = 8
N_GROUPS = 8
TOPK_GROUPS = 4
D_EXPERT = 256
ROUTED_SCALE = 2.5
DEPTH = 1
ALPHA = (2.0 * DEPTH) ** 0.25
LN_EPS = 1e-5
RMS_EPS = 1e-6
LOG2E = 1.4426950408889634

LANES = 128
SUBLANES = 8
QBLK = 4 * CHUNK
KBLK = 256
VMEM_LIMIT = 56 * 1024 * 1024
INT_MIN = -2 ** 31


def _params(sem, vmem=VMEM_LIMIT):
    return pltpu.CompilerParams(dimension_semantics=sem, vmem_limit_bytes=vmem)


def _dot(a, b, precision=None):
    return jnp.dot(a, b, preferred_element_type=F32, precision=precision)


def _dot_nt(a, b, precision=None):
    return lax.dot_general(a, b, (((1,), (1,)), ((), ())), preferred_element_type=F32,
                           precision=precision)


def _dot_tn(a, b, precision=None):
    return lax.dot_general(a, b, (((0,), (0,)), ((), ())), preferred_element_type=F32,
                           precision=precision)


def _split(x):
    hi = x.astype(BF16)
    return hi, (x - hi.astype(F32)).astype(BF16)


def _cat_hhl(x, axis):
    hi, lo = _split(x)
    return jnp.concatenate([hi, hi, lo], axis=axis)


def _cat_hlh(x, axis):
    hi, lo = _split(x)
    return jnp.concatenate([hi, lo, hi], axis=axis)


def _split3(x):
    hi = x.astype(BF16)
    r = x - hi.astype(F32)
    mid = r.astype(BF16)
    return hi, mid, (r - mid.astype(F32)).astype(BF16)


def _cat_act6(x, axis):
    h, m, l = _split3(x)
    return jnp.concatenate([h, h, m, h, l, m], axis=axis)


def _cat_wgt6(x, axis):
    h, m, l = _split3(x)
    return jnp.concatenate([h, m, h, l, h, m], axis=axis)


def _sigmoid(x):
    return 1.0 / (1.0 + jnp.exp(-x))


def _silu(x):
    return x * _sigmoid(x)


def _proj_main_kernel(x_ref, w_ref, o_ref):
    o_ref[...] = _dot(x_ref[...].astype(BF16), w_ref[...])


def _proj_main(x2, w_bf, tm=256):
    t, d = x2.shape
    n = w_bf.shape[1]
    return pl.pallas_call(
        _proj_main_kernel,
        grid=(t // tm,),
        in_specs=[pl.BlockSpec((tm, d), lambda i: (i, 0)),
                  pl.BlockSpec((d, n), lambda i: (0, 0))],
        out_specs=pl.BlockSpec((tm, n), lambda i: (i, 0)),
        out_shape=jax.ShapeDtypeStruct((t, n), F32),
        compiler_params=_params(("parallel",)),
        name="proj_main",
    )(x2, w_bf)


def _hgrn_kernel(p_ref, lbl_ref, ng_ref, o_ref, st_ref, *, n_chunks, layer):
    @pl.when(pl.program_id(1) == 0)
    def _():
        st_ref[...] = jnp.zeros_like(st_ref)

    lg = lbl_ref[...]
    ex = jnp.exp(lg - jnp.max(lg, axis=0, keepdims=True))
    lb_all = jnp.sum(ex[: layer + 1], axis=0, keepdims=True) / jnp.sum(ex, axis=0, keepdims=True)
    ng = ng_ref[...]

    r = lax.broadcasted_iota(I32, (CHUNK, CHUNK), 0)
    c = lax.broadcasted_iota(I32, (CHUNK, CHUNK), 1)
    causal = c <= r
    tri = jnp.where(causal, 1.0, 0.0).astype(BF16)
    tri2 = jnp.concatenate([tri, tri], axis=1)

    units = [(sq, h) for sq in range(p_ref.shape[0]) for h in range(HG_HEADS)]

    def chunk_body(j, carry):
        rows = pl.ds(pl.multiple_of(j * CHUNK, CHUNK), CHUNK)
        ks, bs = [], []
        for sq, h in units:
            lo = h * HG_DK
            lb = lb_all[:, lo:lo + HG_DK]
            f = lb + (1.0 - lb) * _sigmoid(p_ref[sq, rows, HG_WIDTH + lo:HG_WIDTH + lo + HG_DK])
            lf_hi, lf_lo = _split(jnp.log(f))
            ks.append(1.0 - f)
            bs.append(_dot(tri2, jnp.concatenate([lf_hi, lf_lo], axis=0)))
        scs, ois, vbs = [], [], []
        for (sq, h), k, b in zip(units, ks, bs):
            lo = h * HG_DK
            b_last = b[CHUNK - 1:CHUNK, :]
            q_dec = (_silu(p_ref[sq, rows, lo:lo + HG_DK]) * jnp.exp(b)).astype(BF16)
            k_inv = (k * jnp.exp(-b)).astype(BF16)
            k_dec = (k * jnp.exp(b_last - b)).astype(BF16)
            vb = p_ref[sq, rows, 2 * HG_WIDTH + lo:2 * HG_WIDTH + lo + HG_DV].astype(BF16)
            st = st_ref[sq, h]
            scs.append(jnp.where(causal, _dot_nt(q_dec, k_inv), 0.0).astype(BF16))
            ois.append(_dot_nt(q_dec, st.astype(BF16)))
            st_ref[sq, h] = st * jnp.exp(b_last) + _dot_tn(vb, k_dec)
            vbs.append(vb)
        for (sq, h), sc, oi, vb in zip(units, scs, ois, vbs):
            lo = h * HG_DK
            o = _dot(sc, vb) + oi
            o = o * lax.rsqrt(jnp.mean(o * o, axis=-1, keepdims=True) + RMS_EPS) * ng
            gate = p_ref[sq, rows, 3 * HG_WIDTH + lo:3 * HG_WIDTH + lo + HG_DV]
            o_ref[sq, rows, lo:lo + HG_DV] = (o * _silu(gate)).astype(o_ref.dtype)
        return carry

    lax.fori_loop(0, n_chunks, chunk_body, 0)


def _hgrn(proj, lb_logits, norm_g, layer, ct=512, n_seq=2):
    b, s, w = proj.shape
    n_seq = min(n_seq, b)
    n_chunks = ct // CHUNK
    return pl.pallas_call(
        functools.partial(_hgrn_kernel, n_chunks=n_chunks, layer=layer),
        grid=(b // n_seq, s // ct),
        in_specs=[pl.BlockSpec((n_seq, ct, w), lambda i, j: (i, j, 0)),
                  pl.BlockSpec(lb_logits.shape, lambda i, j: (0, 0)),
                  pl.BlockSpec((1, HG_DV), lambda i, j: (0, 0))],
        out_specs=pl.BlockSpec((n_seq, ct, HG_WIDTH), lambda i, j: (i, j, 0)),
        out_shape=jax.ShapeDtypeStruct((b, s, HG_WIDTH), BF16),
        scratch_shapes=[pltpu.VMEM((n_seq, HG_HEADS, HG_DV, HG_DK), F32)],
        compiler_params=_params(("parallel", "arbitrary")),
        name="hgrn",
    )(proj, lb_logits, norm_g.reshape(1, HG_DV))


POS_SHIFT = 4
POS_RADIX = 1 << POS_SHIFT
KP_W = 2 * LANES
K_W = (AT_HEADS // 2) * KP_W
SEL_W = Q_RANK + LANES
KIDX_W = 6 * IDX_DH


def _dsa_prep_kernel(x_ref, wsel_ref, wckv_ref, qg_ref, kvg_ref, lng_ref, lnb_ref, wqT_ref, wqiT_ref,
                     wk_ref, wvT_ref, qT_ref, qiT_ref, k_ref, vT_ref, kidx_ref, wT_ref, w6_ref):
    @pl.when((pl.program_id(0) == 0) & (pl.program_id(1) == 0))
    def _():
        w6_ref[...] = _cat_wgt6(wsel_ref[...], 0)

    x = x_ref[...]
    sel = _dot(_cat_act6(x, 1), w6_ref[...])
    cq = sel[:, :Q_RANK]
    cqn = cq * lax.rsqrt(jnp.mean(cq * cq, axis=-1, keepdims=True) + RMS_EPS) * qg_ref[...]
    qT_ref[0] = (_dot_nt(wqT_ref[...], cqn.astype(BF16)) * (AT_DH ** -0.5 * LOG2E)).astype(qT_ref.dtype)
    qiT_ref[0] = _dot_nt(_cat_wgt6(wqiT_ref[...], 1), _cat_act6(cqn, 1))
    ckv = _dot(x.astype(BF16), wckv_ref[...])
    ckvn = (ckv * lax.rsqrt(jnp.mean(ckv * ckv, axis=-1, keepdims=True) + RMS_EPS)
            * kvg_ref[...]).astype(BF16)
    kmat = _dot(ckvn, wk_ref[...])
    tm = kmat.shape[0]
    s_abs = pl.program_id(1) * tm + lax.broadcasted_iota(I32, (tm, LANES), 0)
    lane = lax.broadcasted_iota(I32, (tm, LANES), 1)
    pos = jnp.where(lane < 3, s_abs >> POS_SHIFT,
                    jnp.where(lane < 6, s_abs & (POS_RADIX - 1), 0)).astype(F32)
    k_ref[0] = jnp.concatenate(
        [blk for j in range(AT_HEADS // 2) for blk in (kmat[:, j * LANES:(j + 1) * LANES], pos)],
        axis=1).astype(k_ref.dtype)
    for kb in range(vT_ref.shape[1]):
        vT_ref[0, kb] = _dot_nt(wvT_ref[...], ckvn[kb * KBLK:(kb + 1) * KBLK]).astype(vT_ref.dtype)
    k128 = sel[:, Q_RANK:]
    in_k = lax.broadcasted_iota(I32, (1, LANES), 1) < IDX_DH
    mu = jnp.sum(jnp.where(in_k, k128, 0.0), axis=-1, keepdims=True) * (1.0 / IDX_DH)
    dev = jnp.where(in_k, k128 - mu, 0.0)
    var = jnp.sum(dev * dev, axis=-1, keepdims=True) * (1.0 / IDX_DH)
    y = dev * lax.rsqrt(var + LN_EPS) * lng_ref[...] + lnb_ref[...]
    h, m, l = (p.astype(F32) for p in _split3(y))
    up = lambda p: pltpu.roll(p, IDX_DH, 1)
    kidx_ref[0] = jnp.concatenate([h + up(h), m + up(h), l + up(m)], axis=1).astype(BF16)
    tT = k128.T
    wT_ref[0] = tT[IDX_DH:IDX_DH + SUBLANES, :] * (IDX_HEADS ** -0.5 * IDX_DH ** -0.5)


def _dsa_prep(x, w_sel, w_ckv, q_norm_g, kv_norm_g, idx_ln_g, idx_ln_b, wqT, wqiT, wk, wvT):
    b, s, d = x.shape
    tm = min(2 * KBLK, s)
    kb_per = tm // KBLK
    nb = s // KBLK
    full = lambda a: pl.BlockSpec(a.shape, lambda i, j: (0,) * a.ndim)
    pad_lanes = lambda v: jnp.pad(v.reshape(1, -1), ((0, 0), (0, LANES - v.shape[0])))
    args = (w_sel, w_ckv, q_norm_g.reshape(1, -1), kv_norm_g.reshape(1, -1),
            pad_lanes(idx_ln_g), pad_lanes(idx_ln_b), wqT, wqiT, wk, wvT)
    return pl.pallas_call(
        _dsa_prep_kernel,
        grid=(b, s // tm),
        in_specs=[pl.BlockSpec((None, tm, d), lambda i, j: (i, j, 0))] + [full(a) for a in args],
        out_specs=[pl.BlockSpec((1, AT_WIDTH, tm), lambda i, j: (i, 0, j)),
                   pl.BlockSpec((1, IDX_HEADS * IDX_DH, tm), lambda i, j: (i, 0, j)),
                   pl.BlockSpec((1, tm, K_W), lambda i, j: (i, j, 0)),
                   pl.BlockSpec((1, kb_per, AT_WIDTH, KBLK), lambda i, j: (i, j, 0, 0)),
                   pl.BlockSpec((1, tm, KIDX_W), lambda i, j: (i, j, 0)),
                   pl.BlockSpec((1, SUBLANES, tm), lambda i, j: (i, 0, j))],
        out_shape=[jax.ShapeDtypeStruct((b, AT_WIDTH, s), BF16),
                   jax.ShapeDtypeStruct((b, IDX_HEADS * IDX_DH, s), F32),
                   jax.ShapeDtypeStruct((b, s, K_W), BF16),
                   jax.ShapeDtypeStruct((b, nb, AT_WIDTH, KBLK), BF16),
                   jax.ShapeDtypeStruct((b, s, KIDX_W), BF16),
                   jax.ShapeDtypeStruct((b, SUBLANES, s), F32)],
        scratch_shapes=[pltpu.VMEM((6 * d, SEL_W), BF16)],
        compiler_params=_params(("arbitrary", "arbitrary")),
        name="dsa_prep",
    )(x, *args)


def _sortable(x):
    bits = lax.bitcast_convert_type(x, I32)
    return jnp.where(bits < 0, bits ^ jnp.int32(0x7FFFFFFF), bits)


def _dsa_attn_kernel(qT_ref, qiT_ref, wT_ref, kidx_ref, k_ref, vT_ref, o_ref,
                     keys_ref, q2_ref, qbd_ref, acc_ref, m_ref, l_ref, *, k_top):
    qb = pl.program_id(1)
    t0 = qb * QBLK
    e_max = t0 + QBLK
    nkb = (e_max + KBLK - 1) >> 8

    lane = lax.broadcasted_iota(I32, (1, QBLK), 1)
    t_idx = t0 + lane
    end_t = ((t_idx >> 6) + 1) * CHUNK
    srow = lax.broadcasted_iota(I32, (KBLK, QBLK), 0)

    for h in range(IDX_HEADS):
        q2_ref[:, h * QBLK:(h + 1) * QBLK] = _cat_wgt6(qiT_ref[0, h * IDX_DH:(h + 1) * IDX_DH, :], 0)
    wrow = jnp.concatenate([wT_ref[0, h:h + 1, :] for h in range(IDX_HEADS)], axis=1)

    def score_body(kb, carry):
        rows = pl.ds(pl.multiple_of(kb * KBLK, KBLK), KBLK)
        logits = _dot(kidx_ref[0, rows, :], q2_ref[...])
        r = jnp.maximum(logits, 0.0) * wrow
        sc = r[:, :QBLK]
        for h in range(1, IDX_HEADS):
            sc = sc + r[:, h * QBLK:(h + 1) * QBLK]
        valid = (srow + kb * KBLK) < end_t
        keys_ref[rows, :] = _sortable(jnp.where(valid, sc, -jnp.inf))
        return carry

    lax.fori_loop(0, nkb, score_body, 0)

    def count_ge(cand):
        def body(kb, acc):
            rows = pl.ds(pl.multiple_of(kb * KBLK, KBLK), KBLK)
            hit = jnp.where(keys_ref[rows, :] >= cand, 1, 0).astype(I32)
            return acc + jnp.sum(hit.reshape(KBLK // SUBLANES, SUBLANES, QBLK), axis=0)
        acc = lax.fori_loop(0, nkb, body, jnp.zeros((SUBLANES, QBLK), I32))
        return jnp.sum(acc, axis=0, keepdims=True)

    zero = jnp.zeros((1, QBLK), I32)
    thr = jnp.where(count_ge(zero) >= k_top, zero, jnp.full((1, QBLK), INT_MIN, I32))

    def bit_body(i, thr):
        cand = thr | (jnp.int32(1) << (30 - i))
        return jnp.where(count_ge(cand) >= k_top, cand, thr)

    thr = lax.fori_loop(0, 31, bit_body, thr)
    need = (k_top - (count_ge(thr + 1))).astype(F32)

    log2e = [p.astype(F32) for p in _split3(jnp.full((1, QBLK), LOG2E, F32))]
    ci16 = lax.broadcasted_iota(I32, (2 * SUBLANES, QBLK), 0)

    def pos_coef(h):
        slope = 2.0 ** (-8.0 * (h + 1) / AT_HEADS)
        blk = jnp.zeros((2 * SUBLANES, QBLK), F32)
        for i in range(3):
            blk = jnp.where(ci16 == i, log2e[i] * (POS_RADIX * slope), blk)
            blk = jnp.where(ci16 == 3 + i, log2e[i] * slope, blk)
        return blk.astype(BF16)

    for j in range(AT_HEADS // 2):
        pair = qT_ref[0, j * LANES:(j + 1) * LANES, :]
        rr = lax.broadcasted_iota(I32, pair.shape, 0)
        zeros = jnp.zeros_like(pair)
        qbd_ref[j, :LANES, :QBLK] = jnp.where(rr < AT_DH, pair, zeros)
        qbd_ref[j, :LANES, QBLK:] = jnp.where(rr >= AT_DH, pair, zeros)
        qbd_ref[j, LANES:LANES + 2 * SUBLANES, :QBLK] = pos_coef(2 * j)
        qbd_ref[j, LANES:LANES + 2 * SUBLANES, QBLK:] = pos_coef(2 * j + 1)
        qbd_ref[j, LANES + 2 * SUBLANES:, :] = jnp.zeros((KP_W - LANES - 2 * SUBLANES, 2 * QBLK), BF16)
    acc_ref[...] = jnp.zeros_like(acc_ref)
    m_ref[...] = jnp.full_like(m_ref, -jnp.inf)
    l_ref[...] = jnp.zeros_like(l_ref)

    ri = lax.broadcasted_iota(I32, (KBLK, KBLK), 0)
    ci = lax.broadcasted_iota(I32, (KBLK, KBLK), 1)
    lstrict = jnp.where(ci < ri, 1.0, 0.0).astype(BF16)

    def attn_body(kb, carry, last=False):
        rows = pl.ds(pl.multiple_of(kb * KBLK, KBLK), KBLK)
        kblk = keys_ref[rows, :]
        s_idx = srow + kb * KBLK
        eq = kblk == thr
        eqf = jnp.where(eq, 1.0, 0.0)
        before = _dot(lstrict, eqf.astype(BF16)) + carry
        sel = ((kblk > thr) | (eq & (before < need))) & (s_idx < end_t)
        bias = jnp.where(sel, 0.0, -jnp.inf)
        if last:
            ahead = jnp.maximum(s_idx - t_idx, 0).astype(F32)
        s2s = [_dot(k_ref[0, rows, j * KP_W:(j + 1) * KP_W], qbd_ref[j])
               for j in range(AT_HEADS // 2)]
        for j in range(AT_HEADS // 2):
            ps, alphas = [], []
            for half in range(2):
                h = 2 * j + half
                st = s2s[j][:, half * QBLK:(half + 1) * QBLK] + bias
                if last:
                    st = st - (2.0 * LOG2E * 2.0 ** (-8.0 * (h + 1) / AT_HEADS)) * ahead
                m_old = m_ref[h:h + 1, :]
                m_new = jnp.maximum(m_old, jnp.max(st, axis=0, keepdims=True))
                m_safe = jnp.where(m_new == -jnp.inf, 0.0, m_new)
                alpha = jnp.exp2(m_old - m_safe)
                p = jnp.exp2(st - m_safe)
                l_ref[h:h + 1, :] = alpha * l_ref[h:h + 1, :] + jnp.sum(p, axis=0, keepdims=True)
                m_ref[h:h + 1, :] = m_new
                ps.append(p.astype(BF16))
                alphas.append(alpha)
            o2 = _dot(vT_ref[0, kb, j * LANES:(j + 1) * LANES, :], jnp.concatenate(ps, axis=1))
            for half in range(2):
                hs = slice((2 * j + half) * AT_DH, (2 * j + half + 1) * AT_DH)
                acc_ref[hs, :] = (alphas[half] * acc_ref[hs, :]
                                  + o2[half * AT_DH:(half + 1) * AT_DH, half * QBLK:(half + 1) * QBLK])
        return carry + jnp.sum(eqf, axis=0, keepdims=True)

    ties_seen = lax.fori_loop(0, nkb - 1, attn_body, jnp.zeros((1, QBLK), F32))
    attn_body(nkb - 1, ties_seen, last=True)

    for h in range(AT_HEADS):
        hs = slice(h * AT_DH, (h + 1) * AT_DH)
        acc_ref[hs, :] = acc_ref[hs, :] / l_ref[h:h + 1, :]
    o_ref[0] = acc_ref[...].T.astype(o_ref.dtype)


def _dsa_attn(qT, qiT, wT, kidx, k, vT):
    b, _, s = qT.shape
    nb = s // KBLK
    k_top = min(TOPK_MAX, s // 4)
    return pl.pallas_call(
        functools.partial(_dsa_attn_kernel, k_top=k_top),
        grid=(b, s // QBLK),
        in_specs=[pl.BlockSpec((1, AT_WIDTH, QBLK), lambda i, j: (i, 0, j)),
                  pl.BlockSpec((1, IDX_HEADS * IDX_DH, QBLK), lambda i, j: (i, 0, j)),
                  pl.BlockSpec((1, SUBLANES, QBLK), lambda i, j: (i, 0, j)),
                  pl.BlockSpec((1, s, KIDX_W), lambda i, j: (i, 0, 0)),
                  pl.BlockSpec((1, s, K_W), lambda i, j: (i, 0, 0)),
                  pl.BlockSpec((1, nb, AT_WIDTH, KBLK), lambda i, j: (i, 0, 0, 0))],
        out_specs=pl.BlockSpec((1, QBLK, AT_WIDTH), lambda i, j: (i, j, 0)),
        out_shape=jax.ShapeDtypeStruct((b, s, AT_WIDTH), BF16),
        scratch_shapes=[pltpu.VMEM((s, QBLK), I32),
                        pltpu.VMEM((KIDX_W, IDX_HEADS * QBLK), BF16),
                        pltpu.VMEM((AT_HEADS // 2, KP_W, 2 * QBLK), BF16),
                        pltpu.VMEM((AT_WIDTH, QBLK), F32),
                        pltpu.VMEM((AT_HEADS, QBLK), F32),
                        pltpu.VMEM((AT_HEADS, QBLK), F32)],
        compiler_params=_params(("parallel", "arbitrary")),
        name="dsa_attn",
    )(qT, qiT, wT, kidx, k, vT)


def _layer_norm(v, g, b):
    mu = jnp.mean(v, axis=-1, keepdims=True)
    var = jnp.mean(jnp.square(v - mu), axis=-1, keepdims=True)
    return (v - mu) * lax.rsqrt(var + LN_EPS) * g + b


def _out_ln_kernel(oa_ref, ob_ref, x_ref, wa_ref, wb_ref, g_ref, b_ref, wrT_ref, bias_ref,
                   h_ref, hb_ref, eidx_ref, gT_ref, cnt_ref):
    mix = _dot(oa_ref[...], wa_ref[...]) + _dot(ob_ref[...], wb_ref[...])
    h = _layer_norm(ALPHA * x_ref[...] + mix, g_ref[...], b_ref[...])
    h_ref[...] = h
    hb_ref[...] = h.astype(BF16)
    eidx_ref[...], gT_ref[...], cnt_ref[0] = _route(h, wrT_ref[...], bias_ref[...])


def _out_ln(oa, ob, x2, wa, wb, g, b, wrT, bias_col, tm):
    t, d = x2.shape
    nt = t // tm
    row = lambda w: pl.BlockSpec((tm, w), lambda i: (i, 0))
    full = lambda a: pl.BlockSpec(a.shape, lambda i: (0, 0))
    return pl.pallas_call(
        _out_ln_kernel,
        grid=(nt,),
        in_specs=[row(oa.shape[1]), row(ob.shape[1]), row(d), full(wa), full(wb), full(g), full(b),
                  full(wrT), full(bias_col)],
        out_specs=[row(d), row(d),
                   pl.BlockSpec((TOP_K, tm), lambda i: (0, i)),
                   row(LANES),
                   pl.BlockSpec((1, N_EXPERTS, LANES), lambda i: (i, 0, 0))],
        out_shape=[jax.ShapeDtypeStruct((t, d), F32), jax.ShapeDtypeStruct((t, d), BF16),
                   jax.ShapeDtypeStruct((TOP_K, t), I32),
                   jax.ShapeDtypeStruct((t, LANES), F32),
                   jax.ShapeDtypeStruct((nt, N_EXPERTS, LANES), F32)],
        compiler_params=_params(("parallel",)),
        name="out_ln_route",
    )(oa, ob, x2, wa, wb, g, b, wrT, bias_col)


def _first_max(v, idx, big):
    m = jnp.max(v, axis=0, keepdims=True)
    first = jnp.min(jnp.where(v == m, idx, big), axis=0, keepdims=True)
    return m, first


def _route(h, wrT, bias_col):
    tm = h.shape[0]
    per_group = N_EXPERTS // N_GROUPS
    logits = _dot_nt(_cat_hlh(wrT, 1), _cat_hhl(h, 1))
    scores = _sigmoid(logits)
    sel = scores + bias_col
    iota_g = lax.broadcasted_iota(I32, (per_group, tm), 0)
    neg = -jnp.inf

    grp = jnp.zeros((N_GROUPS, tm), F32)
    iota_grp = lax.broadcasted_iota(I32, (N_GROUPS, tm), 0)
    for g in range(N_GROUPS):
        blk = sel[g * per_group:(g + 1) * per_group, :]
        m1, i1 = _first_max(blk, iota_g, per_group)
        m2 = jnp.max(jnp.where(iota_g == i1, neg, blk), axis=0, keepdims=True)
        grp = jnp.where(iota_grp == g, m1 + m2, grp)

    gmask = jnp.zeros((N_GROUPS, tm), F32)
    work = grp
    for _ in range(TOPK_GROUPS):
        _, gi = _first_max(work, iota_grp, N_GROUPS)
        hit = iota_grp == gi
        gmask = jnp.where(hit, 1.0, gmask)
        work = jnp.where(hit, neg, work)

    e_mask = jnp.concatenate(
        [jnp.broadcast_to(gmask[g:g + 1, :], (per_group, tm)) for g in range(N_GROUPS)], axis=0)
    iota_e = lax.broadcasted_iota(I32, (N_EXPERTS, tm), 0)
    work = jnp.where(e_mask > 0.5, sel, neg)
    iota_k = lax.broadcasted_iota(I32, (TOP_K, tm), 0)
    eidx = jnp.zeros((TOP_K, tm), I32)
    w = jnp.zeros((N_EXPERTS, tm), F32)
    chosen = jnp.zeros((N_EXPERTS, tm), F32)
    for k in range(TOP_K):
        _, ei = _first_max(work, iota_e, N_EXPERTS)
        hit = iota_e == ei
        eidx = jnp.where(iota_k == k, ei, eidx)
        w = jnp.where(hit, scores, w)
        chosen = jnp.where(hit, 1.0, chosen)
        work = jnp.where(hit, neg, work)

    gates = w / jnp.sum(w, axis=0, keepdims=True) * ROUTED_SCALE
    gates = jnp.concatenate([gates, jnp.zeros((LANES - N_EXPERTS, tm), F32)], axis=0)
    cnt = jnp.broadcast_to(jnp.sum(chosen, axis=1, keepdims=True), (N_EXPERTS, LANES))
    return eidx, gates.T, cnt


MOE_TM = 512
SEG = 16
EXP_TM = 512
ROW_BUF = TOP_K * MOE_TM + N_EXPERTS * SEG
CH_MAX = ROW_BUF // SEG
P_BLK = 512
G_BLK = 512


def _moe_plan(cnt, n_exp_tiles):
    nt = cnt.shape[0]
    n16 = (cnt + (SEG - 1)) // SEG
    so16 = jnp.cumsum(n16, axis=1) - n16
    r16 = jnp.sum(n16, axis=0)
    per = EXP_TM // SEG
    rp16 = (r16 + (per - 1)) // per * per
    ends = jnp.cumsum(rp16)
    off16 = ends - rp16
    go16 = off16[None, :] + jnp.cumsum(n16, axis=0) - n16
    n_et = (ends[-1] // per).astype(I32).reshape(1)
    tiles = jnp.arange(n_exp_tiles, dtype=I32)
    tile_expert = jnp.sum((ends // per)[None, :] <= tiles[:, None], axis=1)
    tile_expert = jnp.minimum(tile_expert, N_EXPERTS - 1).astype(I32)
    c = jnp.arange(CH_MAX, dtype=I32)
    owner = jnp.sum((so16 + n16)[:, None, :] <= c[None, :, None], axis=2)
    shift = jnp.take_along_axis(go16 - so16, jnp.minimum(owner, N_EXPERTS - 1), axis=1)
    dst16 = (shift + c[None, :]).reshape(nt * CH_MAX).astype(I32)
    twice = lambda a: jnp.concatenate([a, a], axis=1).astype(F32)
    seg_lo, seg_hi = twice(so16 * SEG), twice((so16 + n16) * SEG)
    as_rows = lambda a: jnp.broadcast_to(a[:, None, :], (nt, SUBLANES, 2 * N_EXPERTS))
    as_cols = lambda a: jnp.broadcast_to(a[:, :, None], (nt, 2 * N_EXPERTS, LANES))
    return dict(dst16=dst16,
                rtot16=jnp.sum(n16, axis=1).astype(I32), r16=r16.astype(I32), rp16=rp16.astype(I32),
                off16=off16.astype(I32), n_et=n_et, tile_expert=tile_expert,
                lo_rows=as_rows(seg_lo), hi_rows=as_rows(seg_hi),
                lo_cols=as_cols(seg_lo), hi_cols=as_cols(seg_hi))


def _chunk(ref, c):
    return ref.at[pl.ds(pl.multiple_of(c * SEG, SEG), SEG)]


def _segment_copies(i, dst16_ref, rtot_ref, make_copy):
    n = rtot_ref[i]

    def chunk_body(c, carry):
        make_copy(c, dst16_ref[i * CH_MAX + c]).start()
        return carry

    lax.fori_loop(0, n, chunk_body, 0)
    return n


def _wait_copies(count, make_copy):
    def body(c, carry):
        make_copy(0, 0).wait()
        return carry

    lax.fori_loop(0, count, body, 0)


POS_SPLIT = 64


def _dispatch_kernel(dst16_ref, rtot_ref, r16_ref, rp16_ref, off16_ref,
                     x_ref, eidx_ref, locol_ref, lorow_ref, hirow_ref,
                     xs_hbm, posT_ref, buf_ref, zero_ref, pend_ref, sem):
    i = pl.program_id(0)
    slot = lax.rem(i, 2)
    buf = buf_ref.at[slot]
    tm = x_ref.shape[0]
    eidx = eidx_ref[...]
    iota_e = lax.broadcasted_iota(I32, (N_EXPERTS, tm), 0)
    onehot = jnp.zeros((N_EXPERTS, tm), F32)
    for k in range(TOP_K):
        onehot = jnp.where(iota_e == eidx[k:k + 1, :], 1.0, onehot)
    earlier = lax.broadcasted_iota(I32, (tm, tm), 0) < lax.broadcasted_iota(I32, (tm, tm), 1)
    rank = _dot(onehot.astype(BF16), jnp.where(earlier, 1.0, 0.0).astype(BF16))
    pos1 = jnp.where(onehot > 0.5, locol_ref[0][:N_EXPERTS, 0:1] + rank + 1.0, 0.0)
    pos_hi = jnp.floor(pos1 * (1.0 / POS_SPLIT)) * POS_SPLIT
    pos2 = jnp.concatenate([pos_hi, pos1 - pos_hi], axis=0)
    posT_ref[...] = pos2.T
    pos2b = pos2.astype(BF16)

    x = x_ref[...]
    seg_lo = lorow_ref[0][0:1, :]
    seg_hi = hirow_ref[0][0:1, :]
    n_blk = (rtot_ref[i] * SEG + (P_BLK - 1)) // P_BLK

    def blk_body(rb, carry):
        r0 = pl.multiple_of(rb * P_BLK, P_BLK)
        r_e = (lax.broadcasted_iota(I32, (P_BLK, 2 * N_EXPERTS), 0) + r0).astype(F32)
        owner = jnp.where((r_e >= seg_lo) & (r_e < seg_hi), 1.0, 0.0).astype(BF16)
        want = _dot(owner, pos2b)
        r_t = (lax.broadcasted_iota(I32, (P_BLK, tm), 0) + (r0 + 1)).astype(F32)
        p = jnp.where(want == r_t, 1.0, 0.0).astype(BF16)
        buf[pl.ds(r0, P_BLK), :] = _dot(p, x).astype(BF16)
        return carry

    lax.fori_loop(0, n_blk, blk_body, 0)

    @pl.when(i > 0)
    def _():
        prev_copy = lambda src, dst: pltpu.make_async_copy(
            _chunk(buf_ref.at[1 - slot], src), _chunk(xs_hbm, dst), sem.at[1 - slot])
        _wait_copies(pend_ref[0], prev_copy)

    out_copy = lambda src, dst: pltpu.make_async_copy(_chunk(buf, src), _chunk(xs_hbm, dst),
                                                      sem.at[slot])
    pend_ref[0] = _segment_copies(i, dst16_ref, rtot_ref, out_copy)

    @pl.when(i == pl.num_programs(0) - 1)
    def _():
        _wait_copies(pend_ref[0], out_copy)
        zero_ref[...] = jnp.zeros_like(zero_ref)
        zero_copy = lambda src, dst: pltpu.make_async_copy(_chunk(zero_ref, 0), _chunk(xs_hbm, dst),
                                                           sem.at[slot])

        def zero_chunk(c, carry):
            zero_copy(0, c).start()
            return carry

        def tail_body(e, total):
            lax.fori_loop(off16_ref[e] + r16_ref[e], off16_ref[e] + rp16_ref[e], zero_chunk, 0)
            return total + rp16_ref[e] - r16_ref[e]

        _wait_copies(lax.fori_loop(0, N_EXPERTS, tail_body, 0), zero_copy)

        per = EXP_TM // SEG
        first = (off16_ref[N_EXPERTS - 1] + rp16_ref[N_EXPERTS - 1]) // per
        n_tiles = xs_hbm.shape[0] // EXP_TM
        tile_copy = lambda tile: pltpu.make_async_copy(
            zero_ref, xs_hbm.at[pl.ds(pl.multiple_of(tile * EXP_TM, EXP_TM), EXP_TM)], sem.at[slot])

        def zero_tile(tile, carry):
            tile_copy(tile).start()
            return carry

        def wait_tile(tile, carry):
            tile_copy(0).wait()
            return carry

        lax.fori_loop(first, n_tiles, zero_tile, 0)
        lax.fori_loop(first, n_tiles, wait_tile, 0)


def _dispatch(hb, eidx, plan, n_rows):
    t, d = hb.shape
    nt = t // MOE_TM
    grid_spec = pltpu.PrefetchScalarGridSpec(
        num_scalar_prefetch=5,
        grid=(nt,),
        in_specs=[pl.BlockSpec((MOE_TM, d), lambda i, *_: (i, 0)),
                  pl.BlockSpec((TOP_K, MOE_TM), lambda i, *_: (0, i)),
                  pl.BlockSpec((1, 2 * N_EXPERTS, LANES), lambda i, *_: (i, 0, 0)),
                  pl.BlockSpec((1, SUBLANES, 2 * N_EXPERTS), lambda i, *_: (i, 0, 0)),
                  pl.BlockSpec((1, SUBLANES, 2 * N_EXPERTS), lambda i, *_: (i, 0, 0))],
        out_specs=[pl.BlockSpec(memory_space=pl.ANY),
                   pl.BlockSpec((MOE_TM, LANES), lambda i, *_: (i, 0))],
        scratch_shapes=[pltpu.VMEM((2, ROW_BUF, d), BF16), pltpu.VMEM((EXP_TM, d), BF16),
                        pltpu.SMEM((1,), I32), pltpu.SemaphoreType.DMA((2,))])
    return pl.pallas_call(
        _dispatch_kernel,
        grid_spec=grid_spec,
        out_shape=[jax.ShapeDtypeStruct((n_rows, d), BF16),
                   jax.ShapeDtypeStruct((t, LANES), F32)],
        compiler_params=_params(("arbitrary",)),
        name="moe_dispatch",
    )(plan["dst16"], plan["rtot16"], plan["r16"], plan["rp16"],
      plan["off16"], hb, eidx, plan["lo_cols"], plan["lo_rows"], plan["hi_rows"])


def _experts_kernel(te_ref, net_ref, x_ref, wg_ref, wu_ref, wd_ref, y_ref, wgb_ref, wub_ref, wdb_ref):
    j = pl.program_id(0)
    used = j < net_ref[0]

    @pl.when(used & ((j == 0) | (te_ref[j] != te_ref[jnp.maximum(j - 1, 0)])))
    def _():
        wgb_ref[...] = wg_ref[0].astype(BF16)
        wub_ref[...] = wu_ref[0].astype(BF16)
        wdb_ref[...] = wd_ref[0].astype(BF16)

    @pl.when(used)
    def _():
        x = x_ref[...]
        hmid = _silu(_dot(x, wgb_ref[...])) * _dot(x, wub_ref[...])
        y_ref[...] = _dot(hmid.astype(BF16), wdb_ref[...]).astype(y_ref.dtype)


def _experts(xs, wg, wu, wd, plan):
    n_rows, d = xs.shape
    row_map = lambda j, te, net: (jnp.minimum(j, net[0] - 1), 0)
    w_map = lambda j, te, net: (te[jnp.minimum(j, net[0] - 1)], 0, 0)
    grid_spec = pltpu.PrefetchScalarGridSpec(
        num_scalar_prefetch=2,
        grid=(n_rows // EXP_TM,),
        in_specs=[pl.BlockSpec((EXP_TM, d), row_map),
                  pl.BlockSpec((1, d, D_EXPERT), w_map),
                  pl.BlockSpec((1, d, D_EXPERT), w_map),
                  pl.BlockSpec((1, D_EXPERT, d), w_map)],
        out_specs=pl.BlockSpec((EXP_TM, d), row_map),
        scratch_shapes=[pltpu.VMEM((d, D_EXPERT), BF16), pltpu.VMEM((d, D_EXPERT), BF16),
                        pltpu.VMEM((D_EXPERT, d), BF16)])
    return pl.pallas_call(
        _experts_kernel,
        grid_spec=grid_spec,
        out_shape=jax.ShapeDtypeStruct((n_rows, d), BF16),
        input_output_aliases={2: 0},
        compiler_params=_params(("arbitrary",)),
        name="moe_experts",
    )(plan["tile_expert"], plan["n_et"], xs, wg, wu, wd)


def _combine_kernel(dst16_ref, rtot_ref,
                    ys_hbm, posT_ref, gT_ref, locol_ref, hicol_ref, h_ref,
                    sg_ref, su_ref, sd_ref, lg_ref, lb_ref, o_ref, ybuf_ref, acc_ref, pend_ref, sem):
    i = pl.program_id(0)
    slot = lax.rem(i, 2)
    ybuf = ybuf_ref.at[slot]
    tm = h_ref.shape[0]

    def fetch(tile, to_slot):
        copy = lambda dst, src: pltpu.make_async_copy(
            _chunk(ys_hbm, src), _chunk(ybuf_ref.at[to_slot], dst), sem.at[to_slot])
        pend_ref[to_slot] = _segment_copies(tile, dst16_ref, rtot_ref, copy)

    @pl.when(i == 0)
    def _():
        ybuf_ref[...] = jnp.zeros_like(ybuf_ref)
        fetch(0, 0)

    @pl.when(i + 1 < pl.num_programs(0))
    def _():
        fetch(i + 1, 1 - slot)

    h = h_ref[...]
    xb = h.astype(BF16)
    hs = _silu(_dot(xb, sg_ref[...])) * _dot(xb, su_ref[...])
    acc_ref[...] = ALPHA * h + _dot(hs.astype(BF16), sd_ref[...])
    _wait_copies(pend_ref[slot], lambda dst, src: pltpu.make_async_copy(
        _chunk(ys_hbm, src), _chunk(ybuf, dst), sem.at[slot]))

    pos2b = posT_ref[...].astype(BF16)
    g_hi, g_lo = _split(gT_ref[...])
    gate2b = (g_hi.astype(F32) + pltpu.roll(g_lo.astype(F32), N_EXPERTS, 1)).astype(BF16)
    seg_lo = jnp.broadcast_to(locol_ref[0][:, 0:1], (2 * N_EXPERTS, G_BLK))
    seg_hi = jnp.broadcast_to(hicol_ref[0][:, 0:1], (2 * N_EXPERTS, G_BLK))
    n_blk = (rtot_ref[i] * SEG + (G_BLK - 1)) // G_BLK

    def blk_body(cb, carry):
        c0 = pl.multiple_of(cb * G_BLK, G_BLK)
        c_e = (lax.broadcasted_iota(I32, (2 * N_EXPERTS, G_BLK), 1) + c0).astype(F32)
        owner = jnp.where((c_e >= seg_lo) & (c_e < seg_hi), 1.0, 0.0).astype(BF16)
        want = _dot(pos2b, owner)
        gate = _dot(gate2b, owner)
        c_t = (lax.broadcasted_iota(I32, (tm, G_BLK), 1) + (c0 + 1)).astype(F32)
        g = jnp.where(want == c_t, gate, 0.0).astype(BF16)
        acc_ref[...] += _dot(g, ybuf[pl.ds(c0, G_BLK), :])
        return carry

    lax.fori_loop(0, n_blk, blk_body, 0)
    o_ref[...] = _layer_norm(acc_ref[...], lg_ref[...], lb_ref[...])


def _combine(ys, posT, gT, h1, sg, su, sd, lg, lb, plan):
    t, d = h1.shape
    nt = t // MOE_TM
    row = lambda w: pl.BlockSpec((MOE_TM, w), lambda i, *_: (i, 0))
    full = lambda a: pl.BlockSpec(a.shape, lambda i, *_: (0, 0))
    seg = pl.BlockSpec((1, 2 * N_EXPERTS, LANES), lambda i, *_: (i, 0, 0))
    grid_spec = pltpu.PrefetchScalarGridSpec(
        num_scalar_prefetch=2,
        grid=(nt,),
        in_specs=[pl.BlockSpec(memory_space=pl.ANY), row(LANES), row(LANES), seg, seg, row(d),
                  full(sg), full(su), full(sd), full(lg), full(lb)],
        out_specs=row(d),
        scratch_shapes=[pltpu.VMEM((2, ROW_BUF, d), BF16), pltpu.VMEM((MOE_TM, d), F32),
                        pltpu.SMEM((2,), I32), pltpu.SemaphoreType.DMA((2,))])
    return pl.pallas_call(
        _combine_kernel,
        grid_spec=grid_spec,
        out_shape=jax.ShapeDtypeStruct((t, d), F32),
        compiler_params=_params(("arbitrary",)),
        name="moe_combine",
    )(plan["dst16"], plan["rtot16"], ys, posT, gT,
      plan["lo_cols"], plan["hi_cols"], h1, sg, su, sd, lg, lb)


def _moe(h1, hb, eidx, gT, cnt, wg, wu, wd, sg, su, sd, lg, lb):
    t = h1.shape[0]
    nt = t // MOE_TM
    max_rows = nt * ROW_BUF + N_EXPERTS * EXP_TM
    n_exp_tiles = pl.cdiv(max_rows, EXP_TM)
    plan = _moe_plan(cnt[:, :, 0].astype(I32), n_exp_tiles)
    xs, posT = _dispatch(hb, eidx, plan, n_exp_tiles * EXP_TM)
    ys = _experts(xs, wg, wu, wd, plan)
    return _combine(ys, posT, gT, h1, sg, su, sd, lg, lb, plan)


def kernel(x, w_in, hg_lb_logits, hg_norm_g, q_norm_g, w_q_up, w_qidx_up, kv_norm_g, w_kv_up,
           idx_ln_g, idx_ln_b, w_out, ln1_g, ln1_b, w_router, router_bias, w_e_gate, w_e_up,
           w_e_down, w_s_gate, w_s_up, w_s_down, ln2_g, ln2_b):
    b, s, d = x.shape
    h = x
    for l in range(DEPTH):
        h2 = h.reshape(b * s, d)
        n_main = 4 * HG_WIDTH
        w_main = w_in[l, :, :n_main].astype(BF16)
        c_kv, c_idx = n_main + Q_RANK, n_main + Q_RANK + KV_RANK
        w_sel = jnp.concatenate([w_in[l, :, n_main:c_kv], w_in[l, :, c_idx:]], axis=1)
        w_sel = jnp.pad(w_sel, ((0, 0), (0, SEL_W - w_sel.shape[1])))
        w_ckv = w_in[l, :, c_kv:c_idx].astype(BF16)

        proj = _proj_main(h2, w_main).reshape(b, s, n_main)
        o_a = _hgrn(proj, hg_lb_logits, hg_norm_g[l], l)

        wqT = w_q_up[l].T.astype(BF16)
        wqiT = w_qidx_up[l].T
        wk = w_kv_up[l][:, :AT_WIDTH].astype(BF16)
        wvT = w_kv_up[l][:, AT_WIDTH:].T.astype(BF16)
        qT, qiT, k, vT, kidx, wT = _dsa_prep(h, w_sel, w_ckv, q_norm_g[l], kv_norm_g[l],
                                             idx_ln_g[l], idx_ln_b[l], wqT, wqiT, wk, wvT)
        o_b = _dsa_attn(qT, qiT, wT, kidx, k, vT)

        h1, hb, eidx, gT, cnt = _out_ln(
            o_a.reshape(b * s, HG_WIDTH), o_b.reshape(b * s, AT_WIDTH), h2,
            w_out[l, :HG_WIDTH].astype(BF16), w_out[l, HG_WIDTH:].astype(BF16),
            ln1_g[l].reshape(1, d), ln1_b[l].reshape(1, d),
            w_router[l].T, router_bias[l].reshape(N_EXPERTS, 1), MOE_TM)
        out = _moe(h1, hb, eidx, gT, cnt, w_e_gate[l], w_e_up[l], w_e_down[l],
                   w_s_gate[l].astype(BF16), w_s_up[l].astype(BF16), w_s_down[l].astype(BF16),
                   ln2_g[l].reshape(1, d), ln2_b[l].reshape(1, d))
        h = out.reshape(b, s, d)
    return h
```

```python
import functools

import jax
import jax.numpy as jnp
from jax import lax
from jax.experimental import pallas as pl
from jax.experimental.pallas import tpu as pltpu

F32 = jnp.float32
BF16 = jnp.bfloat16
I32 = jnp.int32
HIGHEST = lax.Precision.HIGHEST

CHUNK = 64
HG_HEADS = 4
HG_DK = 128
HG_DV = 128
HG_WIDTH = HG_HEADS * HG_DV
AT_HEADS = 8
AT_DH = 64
AT_WIDTH = AT_HEADS * AT_DH
Q_RANK = 256
KV_RANK = 128
IDX_HEADS = 4
IDX_DH = 64
TOPK_MAX = 256
N_EXPERTS = 64
TOP_K = 8
N_GROUPS = 8
TOPK_GROUPS = 4
D_EXPERT = 256
ROUTED_SCALE = 2.5
DEPTH = 1
ALPHA = (2.0 * DEPTH) ** 0.25
LN_EPS = 1e-5
RMS_EPS = 1e-6
LOG2E = 1.4426950408889634

LANES = 128
SUBLANES = 8
QBLK = 4 * CHUNK
KBLK = 256
VMEM_LIMIT = 56 * 1024 * 1024
INT_MIN = -2 ** 31


def _params(sem, vmem=VMEM_LIMIT):
    return pltpu.CompilerParams(dimension_semantics=sem, vmem_limit_bytes=vmem)


def _dot(a, b, precision=None):
    return jnp.dot(a, b, preferred_element_type=F32, precision=precision)


def _dot_nt(a, b, precision=None):
    return lax.dot_general(a, b, (((1,), (1,)), ((), ())), preferred_element_type=F32,
                           precision=precision)


def _dot_tn(a, b, precision=None):
    return lax.dot_general(a, b, (((0,), (0,)), ((), ())), preferred_element_type=F32,
                           precision=precision)


def _split(x):
    hi = x.astype(BF16)
    return hi, (x - hi.astype(F32)).astype(BF16)


def _cat_hhl(x, axis):
    hi, lo = _split(x)
    return jnp.concatenate([hi, hi, lo], axis=axis)


def _cat_hlh(x, axis):
    hi, lo = _split(x)
    return jnp.concatenate([hi, lo, hi], axis=axis)


def _split3(x):
    hi = x.astype(BF16)
    r = x - hi.astype(F32)
    mid = r.astype(BF16)
    return hi, mid, (r - mid.astype(F32)).astype(BF16)


def _cat_act6(x, axis):
    h, m, l = _split3(x)
    return jnp.concatenate([h, h, m, h, l, m], axis=axis)


def _cat_wgt6(x, axis):
    h, m, l = _split3(x)
    return jnp.concatenate([h, m, h, l, h, m], axis=axis)


def _sigmoid(x):
    return 1.0 / (1.0 + jnp.exp(-x))


def _silu(x):
    return x * _sigmoid(x)


def _proj_main_kernel(x_ref, w_ref, o_ref):
    o_ref[...] = _dot(x_ref[...].astype(BF16), w_ref[...])


def _proj_main(x2, w_bf, tm=256):
    t, d = x2.shape
    n = w_bf.shape[1]
    return pl.pallas_call(
        _proj_main_kernel,
        grid=(t // tm,),
        in_specs=[pl.BlockSpec((tm, d), lambda i: (i, 0)),
                  pl.BlockSpec((d, n), lambda i: (0, 0))],
        out_specs=pl.BlockSpec((tm, n), lambda i: (i, 0)),
        out_shape=jax.ShapeDtypeStruct((t, n), F32),
        compiler_params=_params(("parallel",)),
        name="proj_main",
    )(x2, w_bf)


def _hgrn_kernel(p_ref, lbl_ref, ng_ref, o_ref, st_ref, *, n_chunks, layer):
    @pl.when(pl.program_id(1) == 0)
    def _():
        st_ref[...] = jnp.zeros_like(st_ref)

    lg = lbl_ref[...]
    ex = jnp.exp(lg - jnp.max(lg, axis=0, keepdims=True))
    lb_all = jnp.sum(ex[: layer + 1], axis=0, keepdims=True) / jnp.sum(ex, axis=0, keepdims=True)
    ng = ng_ref[...]

    r = lax.broadcasted_iota(I32, (CHUNK, CHUNK), 0)
    c = lax.broadcasted_iota(I32, (CHUNK, CHUNK), 1)
    causal = c <= r
    tri = jnp.where(causal, 1.0, 0.0).astype(BF16)
    tri2 = jnp.concatenate([tri, tri], axis=1)

    units = [(sq, h) for sq in range(p_ref.shape[0]) for h in range(HG_HEADS)]

    def chunk_body(j, carry):
        rows = pl.ds(pl.multiple_of(j * CHUNK, CHUNK), CHUNK)
        ks, bs = [], []
        for sq, h in units:
            lo = h * HG_DK
            lb = lb_all[:, lo:lo + HG_DK]
            f = lb + (1.0 - lb) * _sigmoid(p_ref[sq, rows, HG_WIDTH + lo:HG_WIDTH + lo + HG_DK])
            lf_hi, lf_lo = _split(jnp.log(f))
            ks.append(1.0 - f)
            bs.append(_dot(tri2, jnp.concatenate([lf_hi, lf_lo], axis=0)))
        scs, ois, vbs = [], [], []
        for (sq, h), k, b in zip(units, ks, bs):
            lo = h * HG_DK
            b_last = b[CHUNK - 1:CHUNK, :]
            q_dec = (_silu(p_ref[sq, rows, lo:lo + HG_DK]) * jnp.exp(b)).astype(BF16)
            k_inv = (k * jnp.exp(-b)).astype(BF16)
            k_dec = (k * jnp.exp(b_last - b)).astype(BF16)
            vb = p_ref[sq, rows, 2 * HG_WIDTH + lo:2 * HG_WIDTH + lo + HG_DV].astype(BF16)
            st = st_ref[sq, h]
            scs.append(jnp.where(causal, _dot_nt(q_dec, k_inv), 0.0).astype(BF16))
            ois.append(_dot_nt(q_dec, st.astype(BF16)))
            st_ref[sq, h] = st * jnp.exp(b_last) + _dot_tn(vb, k_dec)
            vbs.append(vb)
        for (sq, h), sc, oi, vb in zip(units, scs, ois, vbs):
            lo = h * HG_DK
            o = _dot(sc, vb) + oi
            o = o * lax.rsqrt(jnp.mean(o * o, axis=-1, keepdims=True) + RMS_EPS) * ng
            gate = p_ref[sq, rows, 3 * HG_WIDTH + lo:3 * HG_WIDTH + lo + HG_DV]
            o_ref[sq, rows, lo:lo + HG_DV] = (o * _silu(gate)).astype(o_ref.dtype)
        return carry

    lax.fori_loop(0, n_chunks, chunk_body, 0)


def _hgrn(proj, lb_logits, norm_g, layer, ct=512, n_seq=2):
    b, s, w = proj.shape
    n_seq = min(n_seq, b)
    n_chunks = ct // CHUNK
    return pl.pallas_call(
        functools.partial(_hgrn_kernel, n_chunks=n_chunks, layer=layer),
        grid=(b // n_seq, s // ct),
        in_specs=[pl.BlockSpec((n_seq, ct, w), lambda i, j: (i, j, 0)),
                  pl.BlockSpec(lb_logits.shape, lambda i, j: (0, 0)),
                  pl.BlockSpec((1, HG_DV), lambda i, j: (0, 0))],
        out_specs=pl.BlockSpec((n_seq, ct, HG_WIDTH), lambda i, j: (i, j, 0)),
        out_shape=jax.ShapeDtypeStruct((b, s, HG_WIDTH), BF16),
        scratch_shapes=[pltpu.VMEM((n_seq, HG_HEADS, HG_DV, HG_DK), F32)],
        compiler_params=_params(("parallel", "arbitrary")),
        name="hgrn",
    )(proj, lb_logits, norm_g.reshape(1, HG_DV))


POS_SHIFT = 4
POS_RADIX = 1 << POS_SHIFT
KP_W = 2 * LANES
K_W = (AT_HEADS // 2) * KP_W
SEL_W = Q_RANK + LANES
KIDX_W = 6 * IDX_DH


def _dsa_prep_kernel(x_ref, wsel_ref, wckv_ref, qg_ref, kvg_ref, lng_ref, lnb_ref, wqT_ref, wqiT_ref,
                     wk_ref, wvT_ref, qT_ref, qiT_ref, k_ref, vT_ref, kidx_ref, wT_ref, w6_ref):
    @pl.when((pl.program_id(0) == 0) & (pl.program_id(1) == 0))
    def _():
        w6_ref[...] = _cat_wgt6(wsel_ref[...], 0)

    x = x_ref[...]
    sel = _dot(_cat_act6(x, 1), w6_ref[...])
    cq = sel[:, :Q_RANK]
    cqn = cq * lax.rsqrt(jnp.mean(cq * cq, axis=-1, keepdims=True) + RMS_EPS) * qg_ref[...]
    qT_ref[0] = (_dot_nt(wqT_ref[...], cqn.astype(BF16)) * (AT_DH ** -0.5 * LOG2E)).astype(qT_ref.dtype)
    qiT_ref[0] = _dot_nt(_cat_wgt6(wqiT_ref[...], 1), _cat_act6(cqn, 1))
    ckv = _dot(x.astype(BF16), wckv_ref[...])
    ckvn = (ckv * lax.rsqrt(jnp.mean(ckv * ckv, axis=-1, keepdims=True) + RMS_EPS)
            * kvg_ref[...]).astype(BF16)
    kmat = _dot(ckvn, wk_ref[...])
    tm = kmat.shape[0]
    s_abs = pl.program_id(1) * tm + lax.broadcasted_iota(I32, (tm, LANES), 0)
    lane = lax.broadcasted_iota(I32, (tm, LANES), 1)
    pos = jnp.where(lane < 3, s_abs >> POS_SHIFT,
                    jnp.where(lane < 6, s_abs & (POS_RADIX - 1), 0)).astype(F32)
    k_ref[0] = jnp.concatenate(
        [blk for j in range(AT_HEADS // 2) for blk in (kmat[:, j * LANES:(j + 1) * LANES], pos)],
        axis=1).astype(k_ref.dtype)
    for kb in range(vT_ref.shape[1]):
        vT_ref[0, kb] = _dot_nt(wvT_ref[...], ckvn[kb * KBLK:(kb + 1) * KBLK]).astype(vT_ref.dtype)
    k128 = sel[:, Q_RANK:]
    in_k = lax.broadcasted_iota(I32, (1, LANES), 1) < IDX_DH
    mu = jnp.sum(jnp.where(in_k, k128, 0.0), axis=-1, keepdims=True) * (1.0 / IDX_DH)
    dev = jnp.where(in_k, k128 - mu, 0.0)
    var = jnp.sum(dev * dev, axis=-1, keepdims=True) * (1.0 / IDX_DH)
    y = dev * lax.rsqrt(var + LN_EPS) * lng_ref[...] + lnb_ref[...]
    h, m, l = (p.astype(F32) for p in _split3(y))
    up = lambda p: pltpu.roll(p, IDX_DH, 1)
    kidx_ref[0] = jnp.concatenate([h + up(h), m + up(h), l + up(m)], axis=1).astype(BF16)
    tT = k128.T
    wT_ref[0] = tT[IDX_DH:IDX_DH + SUBLANES, :] * (IDX_HEADS ** -0.5 * IDX_DH ** -0.5)


def _dsa_prep(x, w_sel, w_ckv, q_norm_g, kv_norm_g, idx_ln_g, idx_ln_b, wqT, wqiT, wk, wvT):
    b, s, d = x.shape
    tm = min(2 * KBLK, s)
    kb_per = tm // KBLK
    nb = s // KBLK
    full = lambda a: pl.BlockSpec(a.shape, lambda i, j: (0,) * a.ndim)
    pad_lanes = lambda v: jnp.pad(v.reshape(1, -1), ((0, 0), (0, LANES - v.shape[0])))
    args = (w_sel, w_ckv, q_norm_g.reshape(1, -1), kv_norm_g.reshape(1, -1),
            pad_lanes(idx_ln_g), pad_lanes(idx_ln_b), wqT, wqiT, wk, wvT)
    return pl.pallas_call(
        _dsa_prep_kernel,
        grid=(b, s // tm),
        in_specs=[pl.BlockSpec((None, tm, d), lambda i, j: (i, j, 0))] + [full(a) for a in args],
        out_specs=[pl.BlockSpec((1, AT_WIDTH, tm), lambda i, j: (i, 0, j)),
                   pl.BlockSpec((1, IDX_HEADS * IDX_DH, tm), lambda i, j: (i, 0, j)),
                   pl.BlockSpec((1, tm, K_W), lambda i, j: (i, j, 0)),
                   pl.BlockSpec((1, kb_per, AT_WIDTH, KBLK), lambda i, j: (i, j, 0, 0)),
                   pl.BlockSpec((1, tm, KIDX_W), lambda i, j: (i, j, 0)),
                   pl.BlockSpec((1, SUBLANES, tm), lambda i, j: (i, 0, j))],
        out_shape=[jax.ShapeDtypeStruct((b, AT_WIDTH, s), BF16),
                   jax.ShapeDtypeStruct((b, IDX_HEADS * IDX_DH, s), F32),
                   jax.ShapeDtypeStruct((b, s, K_W), BF16),
                   jax.ShapeDtypeStruct((b, nb, AT_WIDTH, KBLK), BF16),
                   jax.ShapeDtypeStruct((b, s, KIDX_W), BF16),
                   jax.ShapeDtypeStruct((b, SUBLANES, s), F32)],
        scratch_shapes=[pltpu.VMEM((6 * d, SEL_W), BF16)],
        compiler_params=_params(("arbitrary", "arbitrary")),
        name="dsa_prep",
    )(x, *args)


def _sortable(x):
    bits = lax.bitcast_convert_type(x, I32)
    return jnp.where(bits < 0, bits ^ jnp.int32(0x7FFFFFFF), bits)


def _dsa_attn_kernel(qT_ref, qiT_ref, wT_ref, kidx_ref, k_ref, vT_ref, o_ref,
                     keys_ref, q2_ref, qbd_ref, acc_ref, m_ref, l_ref, *, k_top):
    qb = pl.program_id(1)
    t0 = qb * QBLK
    e_max = t0 + QBLK
    nkb = (e_max + KBLK - 1) >> 8

    lane = lax.broadcasted_iota(I32, (1, QBLK), 1)
    t_idx = t0 + lane
    end_t = ((t_idx >> 6) + 1) * CHUNK
    srow = lax.broadcasted_iota(I32, (KBLK, QBLK), 0)

    for h in range(IDX_HEADS):
        q2_ref[:, h * QBLK:(h + 1) * QBLK] = _cat_wgt6(qiT_ref[0, h * IDX_DH:(h + 1) * IDX_DH, :], 0)
    wrow = jnp.concatenate([wT_ref[0, h:h + 1, :] for h in range(IDX_HEADS)], axis=1)

    def score_body(kb, carry):
        rows = pl.ds(pl.multiple_of(kb * KBLK, KBLK), KBLK)
        logits = _dot(kidx_ref[0, rows, :], q2_ref[...])
        r = jnp.maximum(logits, 0.0) * wrow
        sc = r[:, :QBLK]
        for h in range(1, IDX_HEADS):
            sc = sc + r[:, h * QBLK:(h + 1) * QBLK]
        valid = (srow + kb * KBLK) < end_t
        keys_ref[rows, :] = _sortable(jnp.where(valid, sc, -jnp.inf))
        return carry

    lax.fori_loop(0, nkb, score_body, 0)

    def count_ge(cand):
        def body(kb, acc):
            rows = pl.ds(pl.multiple_of(kb * KBLK, KBLK), KBLK)
            hit = jnp.where(keys_ref[rows, :] >= cand, 1, 0).astype(I32)
            return acc + jnp.sum(hit.reshape(KBLK // SUBLANES, SUBLANES, QBLK), axis=0)
        acc = lax.fori_loop(0, nkb, body, jnp.zeros((SUBLANES, QBLK), I32))
        return jnp.sum(acc, axis=0, keepdims=True)

    zero = jnp.zeros((1, QBLK), I32)
    thr = jnp.where(count_ge(zero) >= k_top, zero, jnp.full((1, QBLK), INT_MIN, I32))

    def bit_body(i, thr):
        cand = thr | (jnp.int32(1) << (30 - i))
        return jnp.where(count_ge(cand) >= k_top, cand, thr)

    thr = lax.fori_loop(0, 31, bit_body, thr)
    need = (k_top - (count_ge(thr + 1))).astype(F32)

    log2e = [p.astype(F32) for p in _split3(jnp.full((1, QBLK), LOG2E, F32))]
    ci16 = lax.broadcasted_iota(I32, (2 * SUBLANES, QBLK), 0)

    def pos_coef(h):
        slope = 2.0 ** (-8.0 * (h + 1) / AT_HEADS)
        blk = jnp.zeros((2 * SUBLANES, QBLK), F32)
        for i in range(3):
            blk = jnp.where(ci16 == i, log2e[i] * (POS_RADIX * slope), blk)
            blk = jnp.where(ci16 == 3 + i, log2e[i] * slope, blk)
        return blk.astype(BF16)

    for j in range(AT_HEADS // 2):
        pair = qT_ref[0, j * LANES:(j + 1) * LANES, :]
        rr = lax.broadcasted_iota(I32, pair.shape, 0)
        zeros = jnp.zeros_like(pair)
        qbd_ref[j, :LANES, :QBLK] = jnp.where(rr < AT_DH, pair, zeros)
        qbd_ref[j, :LANES, QBLK:] = jnp.where(rr >= AT_DH, pair, zeros)
        qbd_ref[j, LANES:LANES + 2 * SUBLANES, :QBLK] = pos_coef(2 * j)
        qbd_ref[j, LANES:LANES + 2 * SUBLANES, QBLK:] = pos_coef(2 * j + 1)
        qbd_ref[j, LANES + 2 * SUBLANES:, :] = jnp.zeros((KP_W - LANES - 2 * SUBLANES, 2 * QBLK), BF16)
    acc_ref[...] = jnp.zeros_like(acc_ref)
    m_ref[...] = jnp.full_like(m_ref, -jnp.inf)
    l_ref[...] = jnp.zeros_like(l_ref)

    ri = lax.broadcasted_iota(I32, (KBLK, KBLK), 0)
    ci = lax.broadcasted_iota(I32, (KBLK, KBLK), 1)
    lstrict = jnp.where(ci < ri, 1.0, 0.0).astype(BF16)

    def attn_body(kb, carry, last=False):
        rows = pl.ds(pl.multiple_of(kb * KBLK, KBLK), KBLK)
        kblk = keys_ref[rows, :]
        s_idx = srow + kb * KBLK
        eq = kblk == thr
        eqf = jnp.where(eq, 1.0, 0.0)
        before = _dot(lstrict, eqf.astype(BF16)) + carry
        sel = ((kblk > thr) | (eq & (before < need))) & (s_idx < end_t)
        bias = jnp.where(sel, 0.0, -jnp.inf)
        if last:
            ahead = jnp.maximum(s_idx - t_idx, 0).astype(F32)
        s2s = [_dot(k_ref[0, rows, j * KP_W:(j + 1) * KP_W], qbd_ref[j])
               for j in range(AT_HEADS // 2)]
        for j in range(AT_HEADS // 2):
            ps, alphas = [], []
            for half in range(2):
                h = 2 * j + half
                st = s2s[j][:, half * QBLK:(half + 1) * QBLK] + bias
                if last:
                    st = st - (2.0 * LOG2E * 2.0 ** (-8.0 * (h + 1) / AT_HEADS)) * ahead
                m_old = m_ref[h:h + 1, :]
                m_new = jnp.maximum(m_old, jnp.max(st, axis=0, keepdims=True))
                m_safe = jnp.where(m_new == -jnp.inf, 0.0, m_new)
                alpha = jnp.exp2(m_old - m_safe)
                p = jnp.exp2(st - m_safe)
                l_ref[h:h + 1, :] = alpha * l_ref[h:h + 1, :] + jnp.sum(p, axis=0, keepdims=True)
                m_ref[h:h + 1, :] = m_new
                ps.append(p.astype(BF16))
                alphas.append(alpha)
            o2 = _dot(vT_ref[0, kb, j * LANES:(j + 1) * LANES, :], jnp.concatenate(ps, axis=1))
            for half in range(2):
                hs = slice((2 * j + half) * AT_DH, (2 * j + half + 1) * AT_DH)
                acc_ref[hs, :] = (alphas[half] * acc_ref[hs, :]
                                  + o2[half * AT_DH:(half + 1) * AT_DH, half * QBLK:(half + 1) * QBLK])
        return carry + jnp.sum(eqf, axis=0, keepdims=True)

    ties_seen = lax.fori_loop(0, nkb - 1, attn_body, jnp.zeros((1, QBLK), F32))
    attn_body(nkb - 1, ties_seen, last=True)

    for h in range(AT_HEADS):
        hs = slice(h * AT_DH, (h + 1) * AT_DH)
        acc_ref[hs, :] = acc_ref[hs, :] / l_ref[h:h + 1, :]
    o_ref[0] = acc_ref[...].T.astype(o_ref.dtype)


def _dsa_attn(qT, qiT, wT, kidx, k, vT):
    b, _, s = qT.shape
    nb = s // KBLK
    k_top = min(TOPK_MAX, s // 4)
    return pl.pallas_call(
        functools.partial(_dsa_attn_kernel, k_top=k_top),
        grid=(b, s // QBLK),
        in_specs=[pl.BlockSpec((1, AT_WIDTH, QBLK), lambda i, j: (i, 0, j)),
                  pl.BlockSpec((1, IDX_HEADS * IDX_DH, QBLK), lambda i, j: (i, 0, j)),
                  pl.BlockSpec((1, SUBLANES, QBLK), lambda i, j: (i, 0, j)),
                  pl.BlockSpec((1, s, KIDX_W), lambda i, j: (i, 0, 0)),
                  pl.BlockSpec((1, s, K_W), lambda i, j: (i, 0, 0)),
                  pl.BlockSpec((1, nb, AT_WIDTH, KBLK), lambda i, j: (i, 0, 0, 0))],
        out_specs=pl.BlockSpec((1, QBLK, AT_WIDTH), lambda i, j: (i, j, 0)),
        out_shape=jax.ShapeDtypeStruct((b, s, AT_WIDTH), BF16),
        scratch_shapes=[pltpu.VMEM((s, QBLK), I32),
                        pltpu.VMEM((KIDX_W, IDX_HEADS * QBLK), BF16),
                        pltpu.VMEM((AT_HEADS // 2, KP_W, 2 * QBLK), BF16),
                        pltpu.VMEM((AT_WIDTH, QBLK), F32),
                        pltpu.VMEM((AT_HEADS, QBLK), F32),
                        pltpu.VMEM((AT_HEADS, QBLK), F32)],
        compiler_params=_params(("parallel", "arbitrary")),
        name="dsa_attn",
    )(qT, qiT, wT, kidx, k, vT)


def _layer_norm(v, g, b):
    mu = jnp.mean(v, axis=-1, keepdims=True)
    var = jnp.mean(jnp.square(v - mu), axis=-1, keepdims=True)
    return (v - mu) * lax.rsqrt(var + LN_EPS) * g + b


def _out_ln_kernel(oa_ref, ob_ref, x_ref, wa_ref, wb_ref, g_ref, b_ref, wrT_ref, bias_ref,
                   h_ref, hb_ref, eidx_ref, gT_ref, cnt_ref):
    mix = _dot(oa_ref[...], wa_ref[...]) + _dot(ob_ref[...], wb_ref[...])
    h = _layer_norm(ALPHA * x_ref[...] + mix, g_ref[...], b_ref[...])
    h_ref[...] = h
    hb_ref[...] = h.astype(BF16)
    eidx_ref[...], gT_ref[...], cnt_ref[0] = _route(h, wrT_ref[...], bias_ref[...])


def _out_ln(oa, ob, x2, wa, wb, g, b, wrT, bias_col, tm):
    t, d = x2.shape
    nt = t // tm
    row = lambda w: pl.BlockSpec((tm, w), lambda i: (i, 0))
    full = lambda a: pl.BlockSpec(a.shape, lambda i: (0, 0))
    return pl.pallas_call(
        _out_ln_kernel,
        grid=(nt,),
        in_specs=[row(oa.shape[1]), row(ob.shape[1]), row(d), full(wa), full(wb), full(g), full(b),
                  full(wrT), full(bias_col)],
        out_specs=[row(d), row(d),
                   pl.BlockSpec((TOP_K, tm), lambda i: (0, i)),
                   row(LANES),
                   pl.BlockSpec((1, N_EXPERTS, LANES), lambda i: (i, 0, 0))],
        out_shape=[jax.ShapeDtypeStruct((t, d), F32), jax.ShapeDtypeStruct((t, d), BF16),
                   jax.ShapeDtypeStruct((TOP_K, t), I32),
                   jax.ShapeDtypeStruct((t, LANES), F32),
                   jax.ShapeDtypeStruct((nt, N_EXPERTS, LANES), F32)],
        compiler_params=_params(("parallel",)),
        name="out_ln_route",
    )(oa, ob, x2, wa, wb, g, b, wrT, bias_col)


def _first_max(v, idx, big):
    m = jnp.max(v, axis=0, keepdims=True)
    first = jnp.min(jnp.where(v == m, idx, big), axis=0, keepdims=True)
    return m, first


def _route(h, wrT, bias_col):
    tm = h.shape[0]
    per_group = N_EXPERTS // N_GROUPS
    logits = _dot_nt(_cat_hlh(wrT, 1), _cat_hhl(h, 1))
    scores = _sigmoid(logits)
    sel = scores + bias_col
    iota_g = lax.broadcasted_iota(I32, (per_group, tm), 0)
    neg = -jnp.inf

    grp = jnp.zeros((N_GROUPS, tm), F32)
    iota_grp = lax.broadcasted_iota(I32, (N_GROUPS, tm), 0)
    for g in range(N_GROUPS):
        blk = sel[g * per_group:(g + 1) * per_group, :]
        m1, i1 = _first_max(blk, iota_g, per_group)
        m2 = jnp.max(jnp.where(iota_g == i1, neg, blk), axis=0, keepdims=True)
        grp = jnp.where(iota_grp == g, m1 + m2, grp)

    gmask = jnp.zeros((N_GROUPS, tm), F32)
    work = grp
    for _ in range(TOPK_GROUPS):
        _, gi = _first_max(work, iota_grp, N_GROUPS)
        hit = iota_grp == gi
        gmask = jnp.where(hit, 1.0, gmask)
        work = jnp.where(hit, neg, work)

    e_mask = jnp.concatenate(
        [jnp.broadcast_to(gmask[g:g + 1, :], (per_group, tm)) for g in range(N_GROUPS)], axis=0)
    iota_e = lax.broadcasted_iota(I32, (N_EXPERTS, tm), 0)
    work = jnp.where(e_mask > 0.5, sel, neg)
    iota_k = lax.broadcasted_iota(I32, (TOP_K, tm), 0)
    eidx = jnp.zeros((TOP_K, tm), I32)
    w = jnp.zeros((N_EXPERTS, tm), F32)
    chosen = jnp.zeros((N_EXPERTS, tm), F32)
    for k in range(TOP_K):
        _, ei = _first_max(work, iota_e, N_EXPERTS)
        hit = iota_e == ei
        eidx = jnp.where(iota_k == k, ei, eidx)
        w = jnp.where(hit, scores, w)
        chosen = jnp.where(hit, 1.0, chosen)
        work = jnp.where(hit, neg, work)

    gates = w / jnp.sum(w, axis=0, keepdims=True) * ROUTED_SCALE
    gates = jnp.concatenate([gates, jnp.zeros((LANES - N_EXPERTS, tm), F32)], axis=0)
    cnt = jnp.broadcast_to(jnp.sum(chosen, axis=1, keepdims=True), (N_EXPERTS, LANES))
    return eidx, gates.T, cnt


MOE_TM = 512
SEG = 16
EXP_TM = 768
ROW_BUF = TOP_K * MOE_TM + N_EXPERTS * SEG
CH_MAX = ROW_BUF // SEG
P_BLK = 512
G_BLK = 512


def _moe_plan(cnt, n_exp_tiles):
    nt = cnt.shape[0]
    n16 = (cnt + (SEG - 1)) // SEG
    so16 = jnp.cumsum(n16, axis=1) - n16
    r16 = jnp.sum(n16, axis=0)
    per = EXP_TM // SEG
    rp16 = (r16 + (per - 1)) // per * per
    ends = jnp.cumsum(rp16)
    off16 = ends - rp16
    go16 = off16[None, :] + jnp.cumsum(n16, axis=0) - n16
    n_et = (ends[-1] // per).astype(I32).reshape(1)
    tiles = jnp.arange(n_exp_tiles, dtype=I32)
    tile_expert = jnp.sum((ends // per)[None, :] <= tiles[:, None], axis=1)
    tile_expert = jnp.minimum(tile_expert, N_EXPERTS - 1).astype(I32)
    c = jnp.arange(CH_MAX, dtype=I32)[None, :, None]
    lo, hi = so16[:, None, :], (so16 + n16)[:, None, :]
    inside = (lo <= c) & (c < hi)
    dst16 = jnp.sum(jnp.where(inside, go16[:, None, :] + c - lo, 0), axis=2)
    dst16 = dst16.reshape(nt * CH_MAX).astype(I32)
    twice = lambda a: jnp.concatenate([a, a], axis=1).astype(F32)
    seg_lo, seg_hi = twice(so16 * SEG), twice((so16 + n16) * SEG)
    as_rows = lambda a: jnp.broadcast_to(a[:, None, :], (nt, SUBLANES, 2 * N_EXPERTS))
    as_cols = lambda a: jnp.broadcast_to(a[:, :, None], (nt, 2 * N_EXPERTS, LANES))
    return dict(dst16=dst16,
                rtot16=jnp.sum(n16, axis=1).astype(I32), r16=r16.astype(I32), rp16=rp16.astype(I32),
                off16=off16.astype(I32), n_et=n_et, tile_expert=tile_expert,
                lo_rows=as_rows(seg_lo), hi_rows=as_rows(seg_hi),
                lo_cols=as_cols(seg_lo), hi_cols=as_cols(seg_hi))


def _chunk(ref, c):
    return ref.at[pl.ds(pl.multiple_of(c * SEG, SEG), SEG)]


def _segment_copies(i, dst16_ref, rtot_ref, make_copy):
    n = rtot_ref[i]

    def chunk_body(c, carry):
        make_copy(c, dst16_ref[i * CH_MAX + c]).start()
        return carry

    lax.fori_loop(0, n, chunk_body, 0)
    return n


def _wait_copies(count, make_copy):
    def body(c, carry):
        make_copy(0, 0).wait()
        return carry

    lax.fori_loop(0, count, body, 0)


POS_SPLIT = 64


def _dispatch_kernel(dst16_ref, rtot_ref, r16_ref, rp16_ref, off16_ref,
                     x_ref, eidx_ref, locol_ref, lorow_ref, hirow_ref,
                     xs_hbm, posT_ref, buf_ref, zero_ref, pend_ref, sem):
    i = pl.program_id(0)
    slot = lax.rem(i, 2)
    buf = buf_ref.at[slot]
    tm = x_ref.shape[0]
    eidx = eidx_ref[...]
    iota_e = lax.broadcasted_iota(I32, (N_EXPERTS, tm), 0)
    onehot = jnp.zeros((N_EXPERTS, tm), F32)
    for k in range(TOP_K):
        onehot = jnp.where(iota_e == eidx[k:k + 1, :], 1.0, onehot)
    earlier = lax.broadcasted_iota(I32, (tm, tm), 0) < lax.broadcasted_iota(I32, (tm, tm), 1)
    rank = _dot(onehot.astype(BF16), jnp.where(earlier, 1.0, 0.0).astype(BF16))
    pos1 = jnp.where(onehot > 0.5, locol_ref[0][:N_EXPERTS, 0:1] + rank + 1.0, 0.0)
    pos_hi = jnp.floor(pos1 * (1.0 / POS_SPLIT)) * POS_SPLIT
    pos2 = jnp.concatenate([pos_hi, pos1 - pos_hi], axis=0)
    posT_ref[...] = pos2.T
    pos2b = pos2.astype(BF16)

    x = x_ref[...]
    seg_lo = lorow_ref[0][0:1, :]
    seg_hi = hirow_ref[0][0:1, :]
    n_blk = (rtot_ref[i] * SEG + (P_BLK - 1)) // P_BLK

    def blk_body(rb, carry):
        r0 = pl.multiple_of(rb * P_BLK, P_BLK)
        r_e = (lax.broadcasted_iota(I32, (P_BLK, 2 * N_EXPERTS), 0) + r0).astype(F32)
        owner = jnp.where((r_e >= seg_lo) & (r_e < seg_hi), 1.0, 0.0).astype(BF16)
        want = _dot(owner, pos2b)
        r_t = (lax.broadcasted_iota(I32, (P_BLK, tm), 0) + (r0 + 1)).astype(F32)
        p = jnp.where(want == r_t, 1.0, 0.0).astype(BF16)
        buf[pl.ds(r0, P_BLK), :] = _dot(p, x).astype(BF16)
        return carry

    lax.fori_loop(0, n_blk, blk_body, 0)

    @pl.when(i > 0)
    def _():
        prev_copy = lambda src, dst: pltpu.make_async_copy(
            _chunk(buf_ref.at[1 - slot], src), _chunk(xs_hbm, dst), sem.at[1 - slot])
        _wait_copies(pend_ref[0], prev_copy)

    out_copy = lambda src, dst: pltpu.make_async_copy(_chunk(buf, src), _chunk(xs_hbm, dst),
                                                      sem.at[slot])
    pend_ref[0] = _segment_copies(i, dst16_ref, rtot_ref, out_copy)

    @pl.when(i == pl.num_programs(0) - 1)
    def _():
        _wait_copies(pend_ref[0], out_copy)
        zero_ref[...] = jnp.zeros_like(zero_ref)
        zero_copy = lambda src, dst: pltpu.make_async_copy(_chunk(zero_ref, 0), _chunk(xs_hbm, dst),
                                                           sem.at[slot])

        def zero_chunk(c, carry):
            zero_copy(0, c).start()
            return carry

        def tail_body(e, total):
            lax.fori_loop(off16_ref[e] + r16_ref[e], off16_ref[e] + rp16_ref[e], zero_chunk, 0)
            return total + rp16_ref[e] - r16_ref[e]

        _wait_copies(lax.fori_loop(0, N_EXPERTS, tail_body, 0), zero_copy)

        per = EXP_TM // SEG
        first = (off16_ref[N_EXPERTS - 1] + rp16_ref[N_EXPERTS - 1]) // per
        n_tiles = xs_hbm.shape[0] // EXP_TM
        tile_copy = lambda tile: pltpu.make_async_copy(
            zero_ref, xs_hbm.at[pl.ds(pl.multiple_of(tile * EXP_TM, EXP_TM), EXP_TM)], sem.at[slot])

        def zero_tile(tile, carry):
            tile_copy(tile).start()
            return carry

        def wait_tile(tile, carry):
            tile_copy(0).wait()
            return carry

        lax.fori_loop(first, n_tiles, zero_tile, 0)
        lax.fori_loop(first, n_tiles, wait_tile, 0)


def _dispatch(hb, eidx, plan, n_rows):
    t, d = hb.shape
    nt = t // MOE_TM
    grid_spec = pltpu.PrefetchScalarGridSpec(
        num_scalar_prefetch=5,
        grid=(nt,),
        in_specs=[pl.BlockSpec((MOE_TM, d), lambda i, *_: (i, 0)),
                  pl.BlockSpec((TOP_K, MOE_TM), lambda i, *_: (0, i)),
                  pl.BlockSpec((1, 2 * N_EXPERTS, LANES), lambda i, *_: (i, 0, 0)),
                  pl.BlockSpec((1, SUBLANES, 2 * N_EXPERTS), lambda i, *_: (i, 0, 0)),
                  pl.BlockSpec((1, SUBLANES, 2 * N_EXPERTS), lambda i, *_: (i, 0, 0))],
        out_specs=[pl.BlockSpec(memory_space=pl.ANY),
                   pl.BlockSpec((MOE_TM, LANES), lambda i, *_: (i, 0))],
        scratch_shapes=[pltpu.VMEM((2, ROW_BUF, d), BF16), pltpu.VMEM((EXP_TM, d), BF16),
                        pltpu.SMEM((1,), I32), pltpu.SemaphoreType.DMA((2,))])
    return pl.pallas_call(
        _dispatch_kernel,
        grid_spec=grid_spec,
        out_shape=[jax.ShapeDtypeStruct((n_rows, d), BF16),
                   jax.ShapeDtypeStruct((t, LANES), F32)],
        compiler_params=_params(("arbitrary",)),
        name="moe_dispatch",
    )(plan["dst16"], plan["rtot16"], plan["r16"], plan["rp16"],
      plan["off16"], hb, eidx, plan["lo_cols"], plan["lo_rows"], plan["hi_rows"])


def _experts_kernel(te_ref, net_ref, x_ref, wg_ref, wu_ref, wd_ref, y_ref, wgb_ref, wub_ref, wdb_ref):
    j = pl.program_id(0)
    used = j < net_ref[0]

    @pl.when(used & ((j == 0) | (te_ref[j] != te_ref[jnp.maximum(j - 1, 0)])))
    def _():
        wgb_ref[...] = wg_ref[0].astype(BF16)
        wub_ref[...] = wu_ref[0].astype(BF16)
        wdb_ref[...] = wd_ref[0].astype(BF16)

    @pl.when(used)
    def _():
        x = x_ref[...]
        hmid = _silu(_dot(x, wgb_ref[...])) * _dot(x, wub_ref[...])
        y_ref[...] = _dot(hmid.astype(BF16), wdb_ref[...]).astype(y_ref.dtype)


def _experts(xs, wg, wu, wd, plan):
    n_rows, d = xs.shape
    row_map = lambda j, te, net: (jnp.minimum(j, net[0] - 1), 0)
    w_map = lambda j, te, net: (te[jnp.minimum(j, net[0] - 1)], 0, 0)
    grid_spec = pltpu.PrefetchScalarGridSpec(
        num_scalar_prefetch=2,
        grid=(n_rows // EXP_TM,),
        in_specs=[pl.BlockSpec((EXP_TM, d), row_map),
                  pl.BlockSpec((1, d, D_EXPERT), w_map),
                  pl.BlockSpec((1, d, D_EXPERT), w_map),
                  pl.BlockSpec((1, D_EXPERT, d), w_map)],
        out_specs=pl.BlockSpec((EXP_TM, d), row_map),
        scratch_shapes=[pltpu.VMEM((d, D_EXPERT), BF16), pltpu.VMEM((d, D_EXPERT), BF16),
                        pltpu.VMEM((D_EXPERT, d), BF16)])
    return pl.pallas_call(
        _experts_kernel,
        grid_spec=grid_spec,
        out_shape=jax.ShapeDtypeStruct((n_rows, d), BF16),
        input_output_aliases={2: 0},
        compiler_params=_params(("arbitrary",)),
        name="moe_experts",
    )(plan["tile_expert"], plan["n_et"], xs, wg, wu, wd)


def _combine_kernel(dst16_ref, rtot_ref,
                    ys_hbm, posT_ref, gT_ref, locol_ref, hicol_ref, h_ref,
                    sg_ref, su_ref, sd_ref, lg_ref, lb_ref, o_ref, ybuf_ref, acc_ref, pend_ref, sem):
    i = pl.program_id(0)
    slot = lax.rem(i, 2)
    ybuf = ybuf_ref.at[slot]
    tm = h_ref.shape[0]

    def fetch(tile, to_slot):
        copy = lambda dst, src: pltpu.make_async_copy(
            _chunk(ys_hbm, src), _chunk(ybuf_ref.at[to_slot], dst), sem.at[to_slot])
        pend_ref[to_slot] = _segment_copies(tile, dst16_ref, rtot_ref, copy)

    @pl.when(i == 0)
    def _():
        ybuf_ref[...] = jnp.zeros_like(ybuf_ref)
        fetch(0, 0)

    @pl.when(i + 1 < pl.num_programs(0))
    def _():
        fetch(i + 1, 1 - slot)

    h = h_ref[...]
    xb = h.astype(BF16)
    hs = _silu(_dot(xb, sg_ref[...])) * _dot(xb, su_ref[...])
    acc_ref[...] = ALPHA * h + _dot(hs.astype(BF16), sd_ref[...])
    _wait_copies(pend_ref[slot], lambda dst, src: pltpu.make_async_copy(
        _chunk(ys_hbm, src), _chunk(ybuf, dst), sem.at[slot]))

    pos2b = posT_ref[...].astype(BF16)
    g_hi, g_lo = _split(gT_ref[...])
    gate2b = (g_hi.astype(F32) + pltpu.roll(g_lo.astype(F32), N_EXPERTS, 1)).astype(BF16)
    seg_lo = jnp.broadcast_to(locol_ref[0][:, 0:1], (2 * N_EXPERTS, G_BLK))
    seg_hi = jnp.broadcast_to(hicol_ref[0][:, 0:1], (2 * N_EXPERTS, G_BLK))
    n_blk = (rtot_ref[i] * SEG + (G_BLK - 1)) // G_BLK

    def blk_body(cb, carry):
        c0 = pl.multiple_of(cb * G_BLK, G_BLK)
        c_e = (lax.broadcasted_iota(I32, (2 * N_EXPERTS, G_BLK), 1) + c0).astype(F32)
        owner = jnp.where((c_e >= seg_lo) & (c_e < seg_hi), 1.0, 0.0).astype(BF16)
        want = _dot(pos2b, owner)
        gate = _dot(gate2b, owner)
        c_t = (lax.broadcasted_iota(I32, (tm, G_BLK), 1) + (c0 + 1)).astype(F32)
        g = jnp.where(want == c_t, gate, 0.0).astype(BF16)
        acc_ref[...] += _dot(g, ybuf[pl.ds(c0, G_BLK), :])
        return carry

    lax.fori_loop(0, n_blk, blk_body, 0)
    o_ref[...] = _layer_norm(acc_ref[...], lg_ref[...], lb_ref[...])


def _combine(ys, posT, gT, h1, sg, su, sd, lg, lb, plan):
    t, d = h1.shape
    nt = t // MOE_TM
    row = lambda w: pl.BlockSpec((MOE_TM, w), lambda i, *_: (i, 0))
    full = lambda a: pl.BlockSpec(a.shape, lambda i, *_: (0, 0))
    seg = pl.BlockSpec((1, 2 * N_EXPERTS, LANES), lambda i, *_: (i, 0, 0))
    grid_spec = pltpu.PrefetchScalarGridSpec(
        num_scalar_prefetch=2,
        grid=(nt,),
        in_specs=[pl.BlockSpec(memory_space=pl.ANY), row(LANES), row(LANES), seg, seg, row(d),
                  full(sg), full(su), full(sd), full(lg), full(lb)],
        out_specs=row(d),
        scratch_shapes=[pltpu.VMEM((2, ROW_BUF, d), BF16), pltpu.VMEM((MOE_TM, d), F32),
                        pltpu.SMEM((2,), I32), pltpu.SemaphoreType.DMA((2,))])
    return pl.pallas_call(
        _combine_kernel,
        grid_spec=grid_spec,
        out_shape=jax.ShapeDtypeStruct((t, d), F32),
        compiler_params=_params(("arbitrary",)),
        name="moe_combine",
    )(plan["dst16"], plan["rtot16"], ys, posT, gT,
      plan["lo_cols"], plan["hi_cols"], h1, sg, su, sd, lg, lb)


def _moe(h1, hb, eidx, gT, cnt, wg, wu, wd, sg, su, sd, lg, lb):
    t = h1.shape[0]
    nt = t // MOE_TM
    max_rows = nt * ROW_BUF + N_EXPERTS * EXP_TM
    n_exp_tiles = pl.cdiv(max_rows, EXP_TM)
    plan = _moe_plan(cnt[:, :, 0].astype(I32), n_exp_tiles)
    xs, posT = _dispatch(hb, eidx, plan, n_exp_tiles * EXP_TM)
    ys = _experts(xs, wg, wu, wd, plan)
    return _combine(ys, posT, gT, h1, sg, su, sd, lg, lb, plan)


def kernel(x, w_in, hg_lb_logits, hg_norm_g, q_norm_g, w_q_up, w_qidx_up, kv_norm_g, w_kv_up,
           idx_ln_g, idx_ln_b, w_out, ln1_g, ln1_b, w_router, router_bias, w_e_gate, w_e_up,
           w_e_down, w_s_gate, w_s_up, w_s_down, ln2_g, ln2_b):
    b, s, d = x.shape
    h = x
    for l in range(DEPTH):
        h2 = h.reshape(b * s, d)
        n_main = 4 * HG_WIDTH
        w_main = w_in[l, :, :n_main].astype(BF16)
        c_kv, c_idx = n_main + Q_RANK, n_main + Q_RANK + KV_RANK
        w_sel = jnp.concatenate([w_in[l, :, n_main:c_kv], w_in[l, :, c_idx:]], axis=1)
        w_sel = jnp.pad(w_sel, ((0, 0), (0, SEL_W - w_sel.shape[1])))
        w_ckv = w_in[l, :, c_kv:c_idx].astype(BF16)

        proj = _proj_main(h2, w_main).reshape(b, s, n_main)
        o_a = _hgrn(proj, hg_lb_logits, hg_norm_g[l], l)

        wqT = w_q_up[l].T.astype(BF16)
        wqiT = w_qidx_up[l].T
        wk = w_kv_up[l][:, :AT_WIDTH].astype(BF16)
        wvT = w_kv_up[l][:, AT_WIDTH:].T.astype(BF16)
        qT, qiT, k, vT, kidx, wT = _dsa_prep(h, w_sel, w_ckv, q_norm_g[l], kv_norm_g[l],
                                             idx_ln_g[l], idx_ln_b[l], wqT, wqiT, wk, wvT)
        o_b = _dsa_attn(qT, qiT, wT, kidx, k, vT)

        h1, hb, eidx, gT, cnt = _out_ln(
            o_a.reshape(b * s, HG_WIDTH), o_b.reshape(b * s, AT_WIDTH), h2,
            w_out[l, :HG_WIDTH].astype(BF16), w_out[l, HG_WIDTH:].astype(BF16),
            ln1_g[l].reshape(1, d), ln1_b[l].reshape(1, d),
            w_router[l].T, router_bias[l].reshape(N_EXPERTS, 1), MOE_TM)
        out = _moe(h1, hb, eidx, gT, cnt, w_e_gate[l], w_e_up[l], w_e_down[l],
                   w_s_gate[l].astype(BF16), w_s_up[l].astype(BF16), w_s_down[l].astype(BF16),
                   ln2_g[l].reshape(1, d), ln2_b[l].reshape(1, d))
        h = out.reshape(b, s, d)
    return h
```

```python
import functools

import jax
import jax.numpy as jnp
from jax import lax
from jax.experimental import pallas as pl
from jax.experimental.pallas import tpu as pltpu

F32 = jnp.float32
BF16 = jnp.bfloat16
I32 = jnp.int32
HIGHEST = lax.Precision.HIGHEST

CHUNK = 64
HG_HEADS = 4
HG_DK = 128
HG_DV = 128
HG_WIDTH = HG_HEADS * HG_DV
AT_HEADS = 8
AT_DH = 64
AT_WIDTH = AT_HEADS * AT_DH
Q_RANK = 256
KV_RANK = 128
IDX_HEADS = 4
IDX_DH = 64
TOPK_MAX = 256
N_EXPERTS = 64
TOP_K = 8
N_GROUPS = 8
TOPK_GROUPS = 4
D_EXPERT = 256
ROUTED_SCALE = 2.5
DEPTH = 1
ALPHA = (2.0 * DEPTH) ** 0.25
LN_EPS = 1e-5
RMS_EPS = 1e-6
LOG2E = 1.4426950408889634

LANES = 128
SUBLANES = 8
QBLK = 4 * CHUNK
KBLK = 256
VMEM_LIMIT = 56 * 1024 * 1024
INT_MIN = -2 ** 31


def _params(sem, vmem=VMEM_LIMIT):
    return pltpu.CompilerParams(dimension_semantics=sem, vmem_limit_bytes=vmem)


def _dot(a, b, precision=None):
    return jnp.dot(a, b, preferred_element_type=F32, precision=precision)


def _dot_nt(a, b, precision=None):
    return lax.dot_general(a, b, (((1,), (1,)), ((), ())), preferred_element_type=F32,
                           precision=precision)


def _dot_tn(a, b, precision=None):
    return lax.dot_general(a, b, (((0,), (0,)), ((), ())), preferred_element_type=F32,
                           precision=precision)


def _split(x):
    hi = x.astype(BF16)
    return hi, (x - hi.astype(F32)).astype(BF16)


def _cat_hhl(x, axis):
    hi, lo = _split(x)
    return jnp.concatenate([hi, hi, lo], axis=axis)


def _cat_hlh(x, axis):
    hi, lo = _split(x)
    return jnp.concatenate([hi, lo, hi], axis=axis)


def _split3(x):
    hi = x.astype(BF16)
    r = x - hi.astype(F32)
    mid = r.astype(BF16)
    return hi, mid, (r - mid.astype(F32)).astype(BF16)


def _cat_act6(x, axis):
    h, m, l = _split3(x)
    return jnp.concatenate([h, h, m, h, l, m], axis=axis)


def _cat_wgt6(x, axis):
    h, m, l = _split3(x)
    return jnp.concatenate([h, m, h, l, h, m], axis=axis)


def _sigmoid(x):
    return 1.0 / (1.0 + jnp.exp(-x))


def _silu(x):
    return x * _sigmoid(x)


def _hgrn_kernel(x_ref, w_ref, lbl_ref, ng_ref, o_ref, p_ref, st_ref, *, n_chunks, layer):
    @pl.when(pl.program_id(1) == 0)
    def _():
        st_ref[...] = jnp.zeros_like(st_ref)

    for sq in range(x_ref.shape[0]):
        p_ref[sq] = _dot(x_ref[sq].astype(BF16), w_ref[...])

    lg = lbl_ref[...]
    ex = jnp.exp(lg - jnp.max(lg, axis=0, keepdims=True))
    lb_all = jnp.sum(ex[: layer + 1], axis=0, keepdims=True) / jnp.sum(ex, axis=0, keepdims=True)
    ng = ng_ref[...]

    r = lax.broadcasted_iota(I32, (CHUNK, CHUNK), 0)
    c = lax.broadcasted_iota(I32, (CHUNK, CHUNK), 1)
    causal = c <= r
    tri = jnp.where(causal, 1.0, 0.0).astype(BF16)
    tri2 = jnp.concatenate([tri, tri], axis=1)

    units = [(sq, h) for sq in range(p_ref.shape[0]) for h in range(HG_HEADS)]

    def chunk_body(j, carry):
        rows = pl.ds(pl.multiple_of(j * CHUNK, CHUNK), CHUNK)
        ks, bs = [], []
        for sq, h in units:
            lo = h * HG_DK
            lb = lb_all[:, lo:lo + HG_DK]
            f = lb + (1.0 - lb) * _sigmoid(p_ref[sq, rows, HG_WIDTH + lo:HG_WIDTH + lo + HG_DK])
            lf_hi, lf_lo = _split(jnp.log(f))
            ks.append(1.0 - f)
            bs.append(_dot(tri2, jnp.concatenate([lf_hi, lf_lo], axis=0)))
        scs, ois, vbs = [], [], []
        for (sq, h), k, b in zip(units, ks, bs):
            lo = h * HG_DK
            b_last = b[CHUNK - 1:CHUNK, :]
            q_dec = (_silu(p_ref[sq, rows, lo:lo + HG_DK]) * jnp.exp(b)).astype(BF16)
            k_inv = (k * jnp.exp(-b)).astype(BF16)
            k_dec = (k * jnp.exp(b_last - b)).astype(BF16)
            vb = p_ref[sq, rows, 2 * HG_WIDTH + lo:2 * HG_WIDTH + lo + HG_DV].astype(BF16)
            st = st_ref[sq, h]
            scs.append(jnp.where(causal, _dot_nt(q_dec, k_inv), 0.0).astype(BF16))
            ois.append(_dot_nt(q_dec, st.astype(BF16)))
            st_ref[sq, h] = st * jnp.exp(b_last) + _dot_tn(vb, k_dec)
            vbs.append(vb)
        for (sq, h), sc, oi, vb in zip(units, scs, ois, vbs):
            lo = h * HG_DK
            o = _dot(sc, vb) + oi
            o = o * lax.rsqrt(jnp.mean(o * o, axis=-1, keepdims=True) + RMS_EPS) * ng
            gate = p_ref[sq, rows, 3 * HG_WIDTH + lo:3 * HG_WIDTH + lo + HG_DV]
            o_ref[sq, rows, lo:lo + HG_DV] = (o * _silu(gate)).astype(o_ref.dtype)
        return carry

    lax.fori_loop(0, n_chunks, chunk_body, 0)


def _hgrn(x, w_main, lb_logits, norm_g, layer, ct=512, n_seq=2):
    b, s, d = x.shape
    w = w_main.shape[1]
    n_seq = min(n_seq, b)
    ct = min(ct, s)
    n_chunks = ct // CHUNK
    return pl.pallas_call(
        functools.partial(_hgrn_kernel, n_chunks=n_chunks, layer=layer),
        grid=(b // n_seq, s // ct),
        in_specs=[pl.BlockSpec((n_seq, ct, d), lambda i, j: (i, j, 0)),
                  pl.BlockSpec(w_main.shape, lambda i, j: (0, 0)),
                  pl.BlockSpec(lb_logits.shape, lambda i, j: (0, 0)),
                  pl.BlockSpec((1, HG_DV), lambda i, j: (0, 0))],
        out_specs=pl.BlockSpec((n_seq, ct, HG_WIDTH), lambda i, j: (i, j, 0)),
        out_shape=jax.ShapeDtypeStruct((b, s, HG_WIDTH), BF16),
        scratch_shapes=[pltpu.VMEM((n_seq, ct, w), F32),
                        pltpu.VMEM((n_seq, HG_HEADS, HG_DV, HG_DK), F32)],
        compiler_params=_params(("parallel", "arbitrary")),
        name="hgrn",
    )(x, w_main, lb_logits, norm_g.reshape(1, HG_DV))


POS_SHIFT = 4
POS_RADIX = 1 << POS_SHIFT
KP_W = 2 * LANES
K_W = (AT_HEADS // 2) * KP_W
SEL_W = Q_RANK + LANES
KIDX_W = 6 * IDX_DH


def _dsa_prep_kernel(x_ref, wsel_ref, wckv_ref, qg_ref, kvg_ref, lng_ref, lnb_ref, wqT_ref, wqiT_ref,
                     wk_ref, wvT_ref, qT_ref, qiT_ref, k_ref, vT_ref, kidx_ref, wT_ref, w6_ref):
    @pl.when((pl.program_id(0) == 0) & (pl.program_id(1) == 0))
    def _():
        w6_ref[...] = _cat_wgt6(wsel_ref[...], 0)

    x = x_ref[...]
    sel = _dot(_cat_act6(x, 1), w6_ref[...])
    cq = sel[:, :Q_RANK]
    cqn = cq * lax.rsqrt(jnp.mean(cq * cq, axis=-1, keepdims=True) + RMS_EPS) * qg_ref[...]
    qT_ref[0] = (_dot_nt(wqT_ref[...], cqn.astype(BF16)) * (AT_DH ** -0.5 * LOG2E)).astype(qT_ref.dtype)
    qiT_ref[0] = _dot_nt(_cat_wgt6(wqiT_ref[...], 1), _cat_act6(cqn, 1))
    ckv = _dot(x.astype(BF16), wckv_ref[...])
    ckvn = (ckv * lax.rsqrt(jnp.mean(ckv * ckv, axis=-1, keepdims=True) + RMS_EPS)
            * kvg_ref[...]).astype(BF16)
    kmat = _dot(ckvn, wk_ref[...])
    tm = kmat.shape[0]
    s_abs = pl.program_id(1) * tm + lax.broadcasted_iota(I32, (tm, LANES), 0)
    lane = lax.broadcasted_iota(I32, (tm, LANES), 1)
    pos = jnp.where(lane < 3, s_abs >> POS_SHIFT,
                    jnp.where(lane < 6, s_abs & (POS_RADIX - 1), 0)).astype(F32)
    k_ref[0] = jnp.concatenate(
        [blk for j in range(AT_HEADS // 2) for blk in (kmat[:, j * LANES:(j + 1) * LANES], pos)],
        axis=1).astype(k_ref.dtype)
    for kb in range(vT_ref.shape[1]):
        vT_ref[0, kb] = _dot_nt(wvT_ref[...], ckvn[kb * KBLK:(kb + 1) * KBLK]).astype(vT_ref.dtype)
    k128 = sel[:, Q_RANK:]
    in_k = lax.broadcasted_iota(I32, (1, LANES), 1) < IDX_DH
    mu = jnp.sum(jnp.where(in_k, k128, 0.0), axis=-1, keepdims=True) * (1.0 / IDX_DH)
    dev = jnp.where(in_k, k128 - mu, 0.0)
    var = jnp.sum(dev * dev, axis=-1, keepdims=True) * (1.0 / IDX_DH)
    y = dev * lax.rsqrt(var + LN_EPS) * lng_ref[...] + lnb_ref[...]
    h, m, l = (p.astype(F32) for p in _split3(y))
    up = lambda p: pltpu.roll(p, IDX_DH, 1)
    kidx_ref[0] = jnp.concatenate([h + up(h), m + up(h), l + up(m)], axis=1).astype(BF16)
    tT = k128.T
    wT_ref[0] = tT[IDX_DH:IDX_DH + SUBLANES, :] * (IDX_HEADS ** -0.5 * IDX_DH ** -0.5)


def _dsa_prep(x, w_sel, w_ckv, q_norm_g, kv_norm_g, idx_ln_g, idx_ln_b, wqT, wqiT, wk, wvT):
    b, s, d = x.shape
    tm = min(2 * KBLK, s)
    kb_per = tm // KBLK
    nb = s // KBLK
    full = lambda a: pl.BlockSpec(a.shape, lambda i, j: (0,) * a.ndim)
    pad_lanes = lambda v: jnp.pad(v.reshape(1, -1), ((0, 0), (0, LANES - v.shape[0])))
    args = (w_sel, w_ckv, q_norm_g.reshape(1, -1), kv_norm_g.reshape(1, -1),
            pad_lanes(idx_ln_g), pad_lanes(idx_ln_b), wqT, wqiT, wk, wvT)
    return pl.pallas_call(
        _dsa_prep_kernel,
        grid=(b, s // tm),
        in_specs=[pl.BlockSpec((None, tm, d), lambda i, j: (i, j, 0))] + [full(a) for a in args],
        out_specs=[pl.BlockSpec((1, AT_WIDTH, tm), lambda i, j: (i, 0, j)),
                   pl.BlockSpec((1, IDX_HEADS * IDX_DH, tm), lambda i, j: (i, 0, j)),
                   pl.BlockSpec((1, tm, K_W), lambda i, j: (i, j, 0)),
                   pl.BlockSpec((1, kb_per, AT_WIDTH, KBLK), lambda i, j: (i, j, 0, 0)),
                   pl.BlockSpec((1, tm, KIDX_W), lambda i, j: (i, j, 0)),
                   pl.BlockSpec((1, SUBLANES, tm), lambda i, j: (i, 0, j))],
        out_shape=[jax.ShapeDtypeStruct((b, AT_WIDTH, s), BF16),
                   jax.ShapeDtypeStruct((b, IDX_HEADS * IDX_DH, s), F32),
                   jax.ShapeDtypeStruct((b, s, K_W), BF16),
                   jax.ShapeDtypeStruct((b, nb, AT_WIDTH, KBLK), BF16),
                   jax.ShapeDtypeStruct((b, s, KIDX_W), BF16),
                   jax.ShapeDtypeStruct((b, SUBLANES, s), F32)],
        scratch_shapes=[pltpu.VMEM((6 * d, SEL_W), BF16)],
        compiler_params=_params(("arbitrary", "arbitrary")),
        name="dsa_prep",
    )(x, *args)


def _sortable(x):
    bits = lax.bitcast_convert_type(x, I32)
    return jnp.where(bits < 0, bits ^ jnp.int32(0x7FFFFFFF), bits)


def _dsa_attn_kernel(qT_ref, qiT_ref, wT_ref, kidx_ref, k_ref, vT_ref, o_ref,
                     keys_ref, q2_ref, qbd_ref, acc_ref, m_ref, l_ref, *, k_top):
    qb = pl.program_id(1)
    t0 = qb * QBLK
    e_max = t0 + QBLK
    nkb = (e_max + KBLK - 1) >> 8

    lane = lax.broadcasted_iota(I32, (1, QBLK), 1)
    t_idx = t0 + lane
    end_t = ((t_idx >> 6) + 1) * CHUNK
    srow = lax.broadcasted_iota(I32, (KBLK, QBLK), 0)

    for h in range(IDX_HEADS):
        q2_ref[:, h * QBLK:(h + 1) * QBLK] = _cat_wgt6(qiT_ref[0, h * IDX_DH:(h + 1) * IDX_DH, :], 0)
    wrow = jnp.concatenate([wT_ref[0, h:h + 1, :] for h in range(IDX_HEADS)], axis=1)

    def score_body(kb, carry):
        rows = pl.ds(pl.multiple_of(kb * KBLK, KBLK), KBLK)
        logits = _dot(kidx_ref[0, rows, :], q2_ref[...])
        r = jnp.maximum(logits, 0.0) * wrow
        sc = r[:, :QBLK]
        for h in range(1, IDX_HEADS):
            sc = sc + r[:, h * QBLK:(h + 1) * QBLK]
        valid = (srow + kb * KBLK) < end_t
        keys_ref[rows, :] = _sortable(jnp.where(valid, sc, -jnp.inf))
        return carry

    lax.fori_loop(0, nkb, score_body, 0)

    def count_ge(cand):
        def body(kb, acc):
            rows = pl.ds(pl.multiple_of(kb * KBLK, KBLK), KBLK)
            hit = jnp.where(keys_ref[rows, :] >= cand, 1, 0).astype(I32)
            return acc + jnp.sum(hit.reshape(KBLK // SUBLANES, SUBLANES, QBLK), axis=0)
        acc = lax.fori_loop(0, nkb, body, jnp.zeros((SUBLANES, QBLK), I32))
        return jnp.sum(acc, axis=0, keepdims=True)

    zero = jnp.zeros((1, QBLK), I32)
    thr = jnp.where(count_ge(zero) >= k_top, zero, jnp.full((1, QBLK), INT_MIN, I32))

    def bit_body(i, thr):
        cand = thr | (jnp.int32(1) << (30 - i))
        return jnp.where(count_ge(cand) >= k_top, cand, thr)

    thr = lax.fori_loop(0, 31, bit_body, thr)
    need = (k_top - (count_ge(thr + 1))).astype(F32)

    log2e = [p.astype(F32) for p in _split3(jnp.full((1, QBLK), LOG2E, F32))]
    ci16 = lax.broadcasted_iota(I32, (2 * SUBLANES, QBLK), 0)

    def pos_coef(h):
        slope = 2.0 ** (-8.0 * (h + 1) / AT_HEADS)
        blk = jnp.zeros((2 * SUBLANES, QBLK), F32)
        for i in range(3):
            blk = jnp.where(ci16 == i, log2e[i] * (POS_RADIX * slope), blk)
            blk = jnp.where(ci16 == 3 + i, log2e[i] * slope, blk)
        return blk.astype(BF16)

    for j in range(AT_HEADS // 2):
        pair = qT_ref[0, j * LANES:(j + 1) * LANES, :]
        rr = lax.broadcasted_iota(I32, pair.shape, 0)
        zeros = jnp.zeros_like(pair)
        qbd_ref[j, :LANES, :QBLK] = jnp.where(rr < AT_DH, pair, zeros)
        qbd_ref[j, :LANES, QBLK:] = jnp.where(rr >= AT_DH, pair, zeros)
        qbd_ref[j, LANES:LANES + 2 * SUBLANES, :QBLK] = pos_coef(2 * j)
        qbd_ref[j, LANES:LANES + 2 * SUBLANES, QBLK:] = pos_coef(2 * j + 1)
        qbd_ref[j, LANES + 2 * SUBLANES:, :] = jnp.zeros((KP_W - LANES - 2 * SUBLANES, 2 * QBLK), BF16)
    acc_ref[...] = jnp.zeros_like(acc_ref)
    m_ref[...] = jnp.full_like(m_ref, -jnp.inf)
    l_ref[...] = jnp.zeros_like(l_ref)

    ri = lax.broadcasted_iota(I32, (KBLK, KBLK), 0)
    ci = lax.broadcasted_iota(I32, (KBLK, KBLK), 1)
    lstrict = jnp.where(ci < ri, 1.0, 0.0).astype(BF16)

    def attn_body(kb, carry, last=False):
        rows = pl.ds(pl.multiple_of(kb * KBLK, KBLK), KBLK)
        kblk = keys_ref[rows, :]
        s_idx = srow + kb * KBLK
        eq = kblk == thr
        eqf = jnp.where(eq, 1.0, 0.0)
        before = _dot(lstrict, eqf.astype(BF16)) + carry
        sel = ((kblk > thr) | (eq & (before < need))) & (s_idx < end_t)
        bias = jnp.where(sel, 0.0, -jnp.inf)
        if last:
            ahead = jnp.maximum(s_idx - t_idx, 0).astype(F32)
        s2s = [_dot(k_ref[0, rows, j * KP_W:(j + 1) * KP_W], qbd_ref[j])
               for j in range(AT_HEADS // 2)]
        for j in range(AT_HEADS // 2):
            ps, alphas = [], []
            for half in range(2):
                h = 2 * j + half
                st = s2s[j][:, half * QBLK:(half + 1) * QBLK] + bias
                if last:
                    st = st - (2.0 * LOG2E * 2.0 ** (-8.0 * (h + 1) / AT_HEADS)) * ahead
                m_old = m_ref[h:h + 1, :]
                m_new = jnp.maximum(m_old, jnp.max(st, axis=0, keepdims=True))
                m_safe = jnp.where(m_new == -jnp.inf, 0.0, m_new)
                alpha = jnp.exp2(m_old - m_safe)
                p = jnp.exp2(st - m_safe)
                l_ref[h:h + 1, :] = alpha * l_ref[h:h + 1, :] + jnp.sum(p, axis=0, keepdims=True)
                m_ref[h:h + 1, :] = m_new
                ps.append(p.astype(BF16))
                alphas.append(alpha)
            o2 = _dot(vT_ref[0, kb, j * LANES:(j + 1) * LANES, :], jnp.concatenate(ps, axis=1))
            for half in range(2):
                hs = slice((2 * j + half) * AT_DH, (2 * j + half + 1) * AT_DH)
                acc_ref[hs, :] = (alphas[half] * acc_ref[hs, :]
                                  + o2[half * AT_DH:(half + 1) * AT_DH, half * QBLK:(half + 1) * QBLK])
        return carry + jnp.sum(eqf, axis=0, keepdims=True)

    ties_seen = lax.fori_loop(0, nkb - 1, attn_body, jnp.zeros((1, QBLK), F32))
    attn_body(nkb - 1, ties_seen, last=True)

    for h in range(AT_HEADS):
        hs = slice(h * AT_DH, (h + 1) * AT_DH)
        acc_ref[hs, :] = acc_ref[hs, :] / l_ref[h:h + 1, :]
    o_ref[0] = acc_ref[...].T.astype(o_ref.dtype)


def _dsa_attn(qT, qiT, wT, kidx, k, vT):
    b, _, s = qT.shape
    nb = s // KBLK
    k_top = min(TOPK_MAX, s // 4)
    return pl.pallas_call(
        functools.partial(_dsa_attn_kernel, k_top=k_top),
        grid=(b, s // QBLK),
        in_specs=[pl.BlockSpec((1, AT_WIDTH, QBLK), lambda i, j: (i, 0, j)),
                  pl.BlockSpec((1, IDX_HEADS * IDX_DH, QBLK), lambda i, j: (i, 0, j)),
                  pl.BlockSpec((1, SUBLANES, QBLK), lambda i, j: (i, 0, j)),
                  pl.BlockSpec((1, s, KIDX_W), lambda i, j: (i, 0, 0)),
                  pl.BlockSpec((1, s, K_W), lambda i, j: (i, 0, 0)),
                  pl.BlockSpec((1, nb, AT_WIDTH, KBLK), lambda i, j: (i, 0, 0, 0))],
        out_specs=pl.BlockSpec((1, QBLK, AT_WIDTH), lambda i, j: (i, j, 0)),
        out_shape=jax.ShapeDtypeStruct((b, s, AT_WIDTH), BF16),
        scratch_shapes=[pltpu.VMEM((s, QBLK), I32),
                        pltpu.VMEM((KIDX_W, IDX_HEADS * QBLK), BF16),
                        pltpu.VMEM((AT_HEADS // 2, KP_W, 2 * QBLK), BF16),
                        pltpu.VMEM((AT_WIDTH, QBLK), F32),
                        pltpu.VMEM((AT_HEADS, QBLK), F32),
                        pltpu.VMEM((AT_HEADS, QBLK), F32)],
        compiler_params=_params(("parallel", "arbitrary")),
        name="dsa_attn",
    )(qT, qiT, wT, kidx, k, vT)


def _layer_norm(v, g, b):
    mu = jnp.mean(v, axis=-1, keepdims=True)
    var = jnp.mean(jnp.square(v - mu), axis=-1, keepdims=True)
    return (v - mu) * lax.rsqrt(var + LN_EPS) * g + b


def _out_ln_kernel(oa_ref, ob_ref, x_ref, wa_ref, wb_ref, g_ref, b_ref, wrT_ref, bias_ref,
                   h_ref, hb_ref, eidx_ref, gT_ref, cnt_ref):
    mix = _dot(oa_ref[...], wa_ref[...]) + _dot(ob_ref[...], wb_ref[...])
    h = _layer_norm(ALPHA * x_ref[...] + mix, g_ref[...], b_ref[...])
    h_ref[...] = h
    hb_ref[...] = h.astype(BF16)
    eidx_ref[...], gT_ref[...], cnt_ref[0] = _route(h, wrT_ref[...], bias_ref[...])


def _out_ln(oa, ob, x2, wa, wb, g, b, wrT, bias_col, tm):
    t, d = x2.shape
    nt = t // tm
    row = lambda w: pl.BlockSpec((tm, w), lambda i: (i, 0))
    full = lambda a: pl.BlockSpec(a.shape, lambda i: (0, 0))
    return pl.pallas_call(
        _out_ln_kernel,
        grid=(nt,),
        in_specs=[row(oa.shape[1]), row(ob.shape[1]), row(d), full(wa), full(wb), full(g), full(b),
                  full(wrT), full(bias_col)],
        out_specs=[row(d), row(d),
                   pl.BlockSpec((TOP_K, tm), lambda i: (0, i)),
                   row(LANES),
                   pl.BlockSpec((1, N_EXPERTS, LANES), lambda i: (i, 0, 0))],
        out_shape=[jax.ShapeDtypeStruct((t, d), F32), jax.ShapeDtypeStruct((t, d), BF16),
                   jax.ShapeDtypeStruct((TOP_K, t), I32),
                   jax.ShapeDtypeStruct((t, LANES), F32),
                   jax.ShapeDtypeStruct((nt, N_EXPERTS, LANES), F32)],
        compiler_params=_params(("parallel",)),
        name="out_ln_route",
    )(oa, ob, x2, wa, wb, g, b, wrT, bias_col)


def _first_max(v, idx, big):
    m = jnp.max(v, axis=0, keepdims=True)
    first = jnp.min(jnp.where(v == m, idx, big), axis=0, keepdims=True)
    return m, first


def _route(h, wrT, bias_col):
    tm = h.shape[0]
    per_group = N_EXPERTS // N_GROUPS
    logits = _dot_nt(_cat_hlh(wrT, 1), _cat_hhl(h, 1))
    scores = _sigmoid(logits)
    sel = scores + bias_col
    iota_g = lax.broadcasted_iota(I32, (per_group, tm), 0)
    neg = -jnp.inf

    grp = jnp.zeros((N_GROUPS, tm), F32)
    iota_grp = lax.broadcasted_iota(I32, (N_GROUPS, tm), 0)
    for g in range(N_GROUPS):
        blk = sel[g * per_group:(g + 1) * per_group, :]
        m1, i1 = _first_max(blk, iota_g, per_group)
        m2 = jnp.max(jnp.where(iota_g == i1, neg, blk), axis=0, keepdims=True)
        grp = jnp.where(iota_grp == g, m1 + m2, grp)

    gmask = jnp.zeros((N_GROUPS, tm), F32)
    work = grp
    for _ in range(TOPK_GROUPS):
        _, gi = _first_max(work, iota_grp, N_GROUPS)
        hit = iota_grp == gi
        gmask = jnp.where(hit, 1.0, gmask)
        work = jnp.where(hit, neg, work)

    e_mask = jnp.concatenate(
        [jnp.broadcast_to(gmask[g:g + 1, :], (per_group, tm)) for g in range(N_GROUPS)], axis=0)
    iota_e = lax.broadcasted_iota(I32, (N_EXPERTS, tm), 0)
    work = jnp.where(e_mask > 0.5, sel, neg)
    iota_k = lax.broadcasted_iota(I32, (TOP_K, tm), 0)
    eidx = jnp.zeros((TOP_K, tm), I32)
    w = jnp.zeros((N_EXPERTS, tm), F32)
    chosen = jnp.zeros((N_EXPERTS, tm), F32)
    for k in range(TOP_K):
        _, ei = _first_max(work, iota_e, N_EXPERTS)
        hit = iota_e == ei
        eidx = jnp.where(iota_k == k, ei, eidx)
        w = jnp.where(hit, scores, w)
        chosen = jnp.where(hit, 1.0, chosen)
        work = jnp.where(hit, neg, work)

    gates = w / jnp.sum(w, axis=0, keepdims=True) * ROUTED_SCALE
    gates = jnp.concatenate([gates, jnp.zeros((LANES - N_EXPERTS, tm), F32)], axis=0)
    cnt = jnp.broadcast_to(jnp.sum(chosen, axis=1, keepdims=True), (N_EXPERTS, LANES))
    return eidx, gates.T, cnt


MOE_TM = 512
SEG = 16
EXP_TM = 768
ROW_BUF = TOP_K * MOE_TM + N_EXPERTS * SEG
CH_MAX = ROW_BUF // SEG
P_BLK = 512
G_BLK = 512


def _moe_plan(cnt, n_exp_tiles):
    nt = cnt.shape[0]
    n16 = (cnt + (SEG - 1)) // SEG
    so16 = jnp.cumsum(n16, axis=1) - n16
    r16 = jnp.sum(n16, axis=0)
    per = EXP_TM // SEG
    rp16 = (r16 + (per - 1)) // per * per
    ends = jnp.cumsum(rp16)
    off16 = ends - rp16
    go16 = off16[None, :] + jnp.cumsum(n16, axis=0) - n16
    n_et = (ends[-1] // per).astype(I32).reshape(1)
    tiles = jnp.arange(n_exp_tiles, dtype=I32)
    tile_expert = jnp.sum((ends // per)[None, :] <= tiles[:, None], axis=1)
    tile_expert = jnp.minimum(tile_expert, N_EXPERTS - 1).astype(I32)
    c = jnp.arange(CH_MAX, dtype=I32)[None, :, None]
    lo, hi = so16[:, None, :], (so16 + n16)[:, None, :]
    inside = (lo <= c) & (c < hi)
    dst16 = jnp.sum(jnp.where(inside, go16[:, None, :] + c - lo, 0), axis=2)
    dst16 = dst16.reshape(nt * CH_MAX).astype(I32)
    twice = lambda a: jnp.concatenate([a, a], axis=1).astype(F32)
    seg_lo, seg_hi = twice(so16 * SEG), twice((so16 + n16) * SEG)
    as_rows = lambda a: jnp.broadcast_to(a[:, None, :], (nt, SUBLANES, 2 * N_EXPERTS))
    as_cols = lambda a: jnp.broadcast_to(a[:, :, None], (nt, 2 * N_EXPERTS, LANES))
    return dict(dst16=dst16,
                rtot16=jnp.sum(n16, axis=1).astype(I32), r16=r16.astype(I32), rp16=rp16.astype(I32),
                off16=off16.astype(I32), n_et=n_et, tile_expert=tile_expert,
                lo_rows=as_rows(seg_lo), hi_rows=as_rows(seg_hi),
                lo_cols=as_cols(seg_lo), hi_cols=as_cols(seg_hi))


def _chunk(ref, c):
    return ref.at[pl.ds(pl.multiple_of(c * SEG, SEG), SEG)]


def _segment_copies(i, dst16_ref, rtot_ref, make_copy):
    n = rtot_ref[i]

    def chunk_body(c, carry):
        make_copy(c, dst16_ref[i * CH_MAX + c]).start()
        return carry

    lax.fori_loop(0, n, chunk_body, 0)
    return n


def _wait_copies(count, make_copy):
    def body(c, carry):
        make_copy(0, 0).wait()
        return carry

    lax.fori_loop(0, count, body, 0)


POS_SPLIT = 64


def _dispatch_kernel(dst16_ref, rtot_ref, r16_ref, rp16_ref, off16_ref,
                     x_ref, eidx_ref, locol_ref, lorow_ref, hirow_ref,
                     xs_hbm, posT_ref, buf_ref, zero_ref, pend_ref, sem):
    i = pl.program_id(0)
    slot = lax.rem(i, 2)
    buf = buf_ref.at[slot]
    tm = x_ref.shape[0]
    eidx = eidx_ref[...]
    iota_e = lax.broadcasted_iota(I32, (N_EXPERTS, tm), 0)
    onehot = jnp.zeros((N_EXPERTS, tm), F32)
    for k in range(TOP_K):
        onehot = jnp.where(iota_e == eidx[k:k + 1, :], 1.0, onehot)
    earlier = lax.broadcasted_iota(I32, (tm, tm), 0) < lax.broadcasted_iota(I32, (tm, tm), 1)
    rank = _dot(onehot.astype(BF16), jnp.where(earlier, 1.0, 0.0).astype(BF16))
    pos1 = jnp.where(onehot > 0.5, locol_ref[0][:N_EXPERTS, 0:1] + rank + 1.0, 0.0)
    pos_hi = jnp.floor(pos1 * (1.0 / POS_SPLIT)) * POS_SPLIT
    pos2 = jnp.concatenate([pos_hi, pos1 - pos_hi], axis=0)
    posT_ref[...] = pos2.T
    pos2b = pos2.astype(BF16)

    x = x_ref[...]
    seg_lo = lorow_ref[0][0:1, :]
    seg_hi = hirow_ref[0][0:1, :]
    n_blk = (rtot_ref[i] * SEG + (P_BLK - 1)) // P_BLK

    def blk_body(rb, carry):
        r0 = pl.multiple_of(rb * P_BLK, P_BLK)
        r_e = (lax.broadcasted_iota(I32, (P_BLK, 2 * N_EXPERTS), 0) + r0).astype(F32)
        owner = jnp.where((r_e >= seg_lo) & (r_e < seg_hi), 1.0, 0.0).astype(BF16)
        want = _dot(owner, pos2b)
        r_t = (lax.broadcasted_iota(I32, (P_BLK, tm), 0) + (r0 + 1)).astype(F32)
        p = jnp.where(want == r_t, 1.0, 0.0).astype(BF16)
        buf[pl.ds(r0, P_BLK), :] = _dot(p, x).astype(BF16)
        return carry

    lax.fori_loop(0, n_blk, blk_body, 0)

    @pl.when(i > 0)
    def _():
        prev_copy = lambda src, dst: pltpu.make_async_copy(
            _chunk(buf_ref.at[1 - slot], src), _chunk(xs_hbm, dst), sem.at[1 - slot])
        _wait_copies(pend_ref[0], prev_copy)

    out_copy = lambda src, dst: pltpu.make_async_copy(_chunk(buf, src), _chunk(xs_hbm, dst),
                                                      sem.at[slot])
    pend_ref[0] = _segment_copies(i, dst16_ref, rtot_ref, out_copy)

    @pl.when(i == pl.num_programs(0) - 1)
    def _():
        _wait_copies(pend_ref[0], out_copy)
        zero_ref[...] = jnp.zeros_like(zero_ref)
        zero_copy = lambda src, dst: pltpu.make_async_copy(_chunk(zero_ref, 0), _chunk(xs_hbm, dst),
                                                           sem.at[slot])

        def zero_chunk(c, carry):
            zero_copy(0, c).start()
            return carry

        def tail_body(e, total):
            lax.fori_loop(off16_ref[e] + r16_ref[e], off16_ref[e] + rp16_ref[e], zero_chunk, 0)
            return total + rp16_ref[e] - r16_ref[e]

        _wait_copies(lax.fori_loop(0, N_EXPERTS, tail_body, 0), zero_copy)

        per = EXP_TM // SEG
        first = (off16_ref[N_EXPERTS - 1] + rp16_ref[N_EXPERTS - 1]) // per
        n_tiles = xs_hbm.shape[0] // EXP_TM
        tile_copy = lambda tile: pltpu.make_async_copy(
            zero_ref, xs_hbm.at[pl.ds(pl.multiple_of(tile * EXP_TM, EXP_TM), EXP_TM)], sem.at[slot])

        def zero_tile(tile, carry):
            tile_copy(tile).start()
            return carry

        def wait_tile(tile, carry):
            tile_copy(0).wait()
            return carry

        lax.fori_loop(first, n_tiles, zero_tile, 0)
        lax.fori_loop(first, n_tiles, wait_tile, 0)


def _dispatch(hb, eidx, plan, n_rows):
    t, d = hb.shape
    nt = t // MOE_TM
    grid_spec = pltpu.PrefetchScalarGridSpec(
        num_scalar_prefetch=5,
        grid=(nt,),
        in_specs=[pl.BlockSpec((MOE_TM, d), lambda i, *_: (i, 0)),
                  pl.BlockSpec((TOP_K, MOE_TM), lambda i, *_: (0, i)),
                  pl.BlockSpec((1, 2 * N_EXPERTS, LANES), lambda i, *_: (i, 0, 0)),
                  pl.BlockSpec((1, SUBLANES, 2 * N_EXPERTS), lambda i, *_: (i, 0, 0)),
                  pl.BlockSpec((1, SUBLANES, 2 * N_EXPERTS), lambda i, *_: (i, 0, 0))],
        out_specs=[pl.BlockSpec(memory_space=pl.ANY),
                   pl.BlockSpec((MOE_TM, LANES), lambda i, *_: (i, 0))],
        scratch_shapes=[pltpu.VMEM((2, ROW_BUF, d), BF16), pltpu.VMEM((EXP_TM, d), BF16),
                        pltpu.SMEM((1,), I32), pltpu.SemaphoreType.DMA((2,))])
    return pl.pallas_call(
        _dispatch_kernel,
        grid_spec=grid_spec,
        out_shape=[jax.ShapeDtypeStruct((n_rows, d), BF16),
                   jax.ShapeDtypeStruct((t, LANES), F32)],
        compiler_params=_params(("arbitrary",)),
        name="moe_dispatch",
    )(plan["dst16"], plan["rtot16"], plan["r16"], plan["rp16"],
      plan["off16"], hb, eidx, plan["lo_cols"], plan["lo_rows"], plan["hi_rows"])


def _experts_kernel(te_ref, net_ref, x_ref, wg_ref, wu_ref, wd_ref, y_ref, wgb_ref, wub_ref, wdb_ref):
    j = pl.program_id(0)
    used = j < net_ref[0]

    @pl.when(used & ((j == 0) | (te_ref[j] != te_ref[jnp.maximum(j - 1, 0)])))
    def _():
        wgb_ref[...] = wg_ref[0].astype(BF16)
        wub_ref[...] = wu_ref[0].astype(BF16)
        wdb_ref[...] = wd_ref[0].astype(BF16)

    @pl.when(used)
    def _():
        x = x_ref[...]
        hmid = _silu(_dot(x, wgb_ref[...])) * _dot(x, wub_ref[...])
        y_ref[...] = _dot(hmid.astype(BF16), wdb_ref[...]).astype(y_ref.dtype)


def _experts(xs, wg, wu, wd, plan):
    n_rows, d = xs.shape
    row_map = lambda j, te, net: (jnp.minimum(j, net[0] - 1), 0)
    w_map = lambda j, te, net: (te[jnp.minimum(j, net[0] - 1)], 0, 0)
    grid_spec = pltpu.PrefetchScalarGridSpec(
        num_scalar_prefetch=2,
        grid=(n_rows // EXP_TM,),
        in_specs=[pl.BlockSpec((EXP_TM, d), row_map),
                  pl.BlockSpec((1, d, D_EXPERT), w_map),
                  pl.BlockSpec((1, d, D_EXPERT), w_map),
                  pl.BlockSpec((1, D_EXPERT, d), w_map)],
        out_specs=pl.BlockSpec((EXP_TM, d), row_map),
        scratch_shapes=[pltpu.VMEM((d, D_EXPERT), BF16), pltpu.VMEM((d, D_EXPERT), BF16),
                        pltpu.VMEM((D_EXPERT, d), BF16)])
    return pl.pallas_call(
        _experts_kernel,
        grid_spec=grid_spec,
        out_shape=jax.ShapeDtypeStruct((n_rows, d), BF16),
        input_output_aliases={2: 0},
        compiler_params=_params(("arbitrary",)),
        name="moe_experts",
    )(plan["tile_expert"], plan["n_et"], xs, wg, wu, wd)


def _combine_kernel(dst16_ref, rtot_ref,
                    ys_hbm, posT_ref, gT_ref, locol_ref, hicol_ref, h_ref,
                    sg_ref, su_ref, sd_ref, lg_ref, lb_ref, o_ref, ybuf_ref, acc_ref, pend_ref, sem):
    i = pl.program_id(0)
    slot = lax.rem(i, 2)
    ybuf = ybuf_ref.at[slot]
    tm = h_ref.shape[0]

    def fetch(tile, to_slot):
        copy = lambda dst, src: pltpu.make_async_copy(
            _chunk(ys_hbm, src), _chunk(ybuf_ref.at[to_slot], dst), sem.at[to_slot])
        pend_ref[to_slot] = _segment_copies(tile, dst16_ref, rtot_ref, copy)

    @pl.when(i == 0)
    def _():
        ybuf_ref[...] = jnp.zeros_like(ybuf_ref)
        fetch(0, 0)

    @pl.when(i + 1 < pl.num_programs(0))
    def _():
        fetch(i + 1, 1 - slot)

    h = h_ref[...]
    xb = h.astype(BF16)
    hs = _silu(_dot(xb, sg_ref[...])) * _dot(xb, su_ref[...])
    acc_ref[...] = ALPHA * h + _dot(hs.astype(BF16), sd_ref[...])
    _wait_copies(pend_ref[slot], lambda dst, src: pltpu.make_async_copy(
        _chunk(ys_hbm, src), _chunk(ybuf, dst), sem.at[slot]))

    pos2b = posT_ref[...].astype(BF16)
    g_hi, g_lo = _split(gT_ref[...])
    gate2b = (g_hi.astype(F32) + pltpu.roll(g_lo.astype(F32), N_EXPERTS, 1)).astype(BF16)
    seg_lo = jnp.broadcast_to(locol_ref[0][:, 0:1], (2 * N_EXPERTS, G_BLK))
    seg_hi = jnp.broadcast_to(hicol_ref[0][:, 0:1], (2 * N_EXPERTS, G_BLK))
    n_blk = (rtot_ref[i] * SEG + (G_BLK - 1)) // G_BLK

    def blk_body(cb, carry):
        c0 = pl.multiple_of(cb * G_BLK, G_BLK)
        c_e = (lax.broadcasted_iota(I32, (2 * N_EXPERTS, G_BLK), 1) + c0).astype(F32)
        owner = jnp.where((c_e >= seg_lo) & (c_e < seg_hi), 1.0, 0.0).astype(BF16)
        want = _dot(pos2b, owner)
        gate = _dot(gate2b, owner)
        c_t = (lax.broadcasted_iota(I32, (tm, G_BLK), 1) + (c0 + 1)).astype(F32)
        g = jnp.where(want == c_t, gate, 0.0).astype(BF16)
        acc_ref[...] += _dot(g, ybuf[pl.ds(c0, G_BLK), :])
        return carry

    lax.fori_loop(0, n_blk, blk_body, 0)
    o_ref[...] = _layer_norm(acc_ref[...], lg_ref[...], lb_ref[...])


def _combine(ys, posT, gT, h1, sg, su, sd, lg, lb, plan):
    t, d = h1.shape
    nt = t // MOE_TM
    row = lambda w: pl.BlockSpec((MOE_TM, w), lambda i, *_: (i, 0))
    full = lambda a: pl.BlockSpec(a.shape, lambda i, *_: (0, 0))
    seg = pl.BlockSpec((1, 2 * N_EXPERTS, LANES), lambda i, *_: (i, 0, 0))
    grid_spec = pltpu.PrefetchScalarGridSpec(
        num_scalar_prefetch=2,
        grid=(nt,),
        in_specs=[pl.BlockSpec(memory_space=pl.ANY), row(LANES), row(LANES), seg, seg, row(d),
                  full(sg), full(su), full(sd), full(lg), full(lb)],
        out_specs=row(d),
        scratch_shapes=[pltpu.VMEM((2, ROW_BUF, d), BF16), pltpu.VMEM((MOE_TM, d), F32),
                        pltpu.SMEM((2,), I32), pltpu.SemaphoreType.DMA((2,))])
    return pl.pallas_call(
        _combine_kernel,
        grid_spec=grid_spec,
        out_shape=jax.ShapeDtypeStruct((t, d), F32),
        compiler_params=_params(("arbitrary",)),
        name="moe_combine",
    )(plan["dst16"], plan["rtot16"], ys, posT, gT,
      plan["lo_cols"], plan["hi_cols"], h1, sg, su, sd, lg, lb)


def _moe(h1, hb, eidx, gT, cnt, wg, wu, wd, sg, su, sd, lg, lb):
    t = h1.shape[0]
    nt = t // MOE_TM
    max_rows = nt * ROW_BUF + N_EXPERTS * EXP_TM
    n_exp_tiles = pl.cdiv(max_rows, EXP_TM)
    plan = _moe_plan(cnt[:, :, 0].astype(I32), n_exp_tiles)
    xs, posT = _dispatch(hb, eidx, plan, n_exp_tiles * EXP_TM)
    ys = _experts(xs, wg, wu, wd, plan)
    return _combine(ys, posT, gT, h1, sg, su, sd, lg, lb, plan)


def kernel(x, w_in, hg_lb_logits, hg_norm_g, q_norm_g, w_q_up, w_qidx_up, kv_norm_g, w_kv_up,
           idx_ln_g, idx_ln_b, w_out, ln1_g, ln1_b, w_router, router_bias, w_e_gate, w_e_up,
           w_e_down, w_s_gate, w_s_up, w_s_down, ln2_g, ln2_b):
    b, s, d = x.shape
    h = x
    for l in range(DEPTH):
        h2 = h.reshape(b * s, d)
        n_main = 4 * HG_WIDTH
        w_main = w_in[l, :, :n_main].astype(BF16)
        c_kv, c_idx = n_main + Q_RANK, n_main + Q_RANK + KV_RANK
        w_sel = jnp.concatenate([w_in[l, :, n_main:c_kv], w_in[l, :, c_idx:]], axis=1)
        w_sel = jnp.pad(w_sel, ((0, 0), (0, SEL_W - w_sel.shape[1])))
        w_ckv = w_in[l, :, c_kv:c_idx].astype(BF16)

        o_a = _hgrn(h, w_main, hg_lb_logits, hg_norm_g[l], l)

        wqT = w_q_up[l].T.astype(BF16)
        wqiT = w_qidx_up[l].T
        wk = w_kv_up[l][:, :AT_WIDTH].astype(BF16)
        wvT = w_kv_up[l][:, AT_WIDTH:].T.astype(BF16)
        qT, qiT, k, vT, kidx, wT = _dsa_prep(h, w_sel, w_ckv, q_norm_g[l], kv_norm_g[l],
                                             idx_ln_g[l], idx_ln_b[l], wqT, wqiT, wk, wvT)
        o_b = _dsa_attn(qT, qiT, wT, kidx, k, vT)

        h1, hb, eidx, gT, cnt = _out_ln(
            o_a.reshape(b * s, HG_WIDTH), o_b.reshape(b * s, AT_WIDTH), h2,
            w_out[l, :HG_WIDTH].astype(BF16), w_out[l, HG_WIDTH:].astype(BF16),
            ln1_g[l].reshape(1, d), ln1_b[l].reshape(1, d),
            w_router[l].T, router_bias[l].reshape(N_EXPERTS, 1), MOE_TM)
        out = _moe(h1, hb, eidx, gT, cnt, w_e_gate[l], w_e_up[l], w_e_down[l],
                   w_s_gate[l].astype(BF16), w_s_up[l].astype(BF16), w_s_down[l].astype(BF16),
                   ln2_g[l].reshape(1, d), ln2_b[l].reshape(1, d))
        h = out.reshape(b, s, d)
    return h
```

```python
import functools

import jax
import jax.numpy as jnp
from jax import lax
from jax.experimental import pallas as pl
from jax.experimental.pallas import tpu as pltpu

F32 = jnp.float32
BF16 = jnp.bfloat16
I32 = jnp.int32
HIGHEST = lax.Precision.HIGHEST

CHUNK = 64
HG_HEADS = 4
HG_DK = 128
HG_DV = 128
HG_WIDTH = HG_HEADS * HG_DV
AT_HEADS = 8
AT_DH = 64
AT_WIDTH = AT_HEADS * AT_DH
Q_RANK = 256
KV_RANK = 128
IDX_HEADS = 4
IDX_DH = 64
TOPK_MAX = 256
N_EXPERTS = 64
TOP_K = 8
N_GROUPS = 8
TOPK_GROUPS = 4
D_EXPERT = 256
ROUTED_SCALE = 2.5
DEPTH = 1
ALPHA = (2.0 * DEPTH) ** 0.25
LN_EPS = 1e-5
RMS_EPS = 1e-6
LOG2E = 1.4426950408889634

LANES = 128
SUBLANES = 8
QBLK = 4 * CHUNK
KBLK = 256
VMEM_LIMIT = 56 * 1024 * 1024
INT_MIN = -2 ** 31


def _params(sem, vmem=VMEM_LIMIT):
    return pltpu.CompilerParams(dimension_semantics=sem, vmem_limit_bytes=vmem)


def _dot(a, b, precision=None):
    return jnp.dot(a, b, preferred_element_type=F32, precision=precision)


def _dot_nt(a, b, precision=None):
    return lax.dot_general(a, b, (((1,), (1,)), ((), ())), preferred_element_type=F32,
                           precision=precision)


def _dot_tn(a, b, precision=None):
    return lax.dot_general(a, b, (((0,), (0,)), ((), ())), preferred_element_type=F32,
                           precision=precision)


def _split(x):
    hi = x.astype(BF16)
    return hi, (x - hi.astype(F32)).astype(BF16)


def _cat_hhl(x, axis):
    hi, lo = _split(x)
    return jnp.concatenate([hi, hi, lo], axis=axis)


def _cat_hlh(x, axis):
    hi, lo = _split(x)
    return jnp.concatenate([hi, lo, hi], axis=axis)


def _split3(x):
    hi = x.astype(BF16)
    r = x - hi.astype(F32)
    mid = r.astype(BF16)
    return hi, mid, (r - mid.astype(F32)).astype(BF16)


def _cat_act6(x, axis):
    h, m, l = _split3(x)
    return jnp.concatenate([h, h, m, h, l, m], axis=axis)


def _cat_wgt6(x, axis):
    h, m, l = _split3(x)
    return jnp.concatenate([h, m, h, l, h, m], axis=axis)


def _sigmoid(x):
    return 1.0 / (1.0 + jnp.exp(-x))


def _silu(x):
    return x * _sigmoid(x)


def _hgrn_kernel(x_ref, w_ref, lbl_ref, ng_ref, o_ref, p_ref, st_ref, *, n_chunks, layer):
    @pl.when(pl.program_id(1) == 0)
    def _():
        st_ref[...] = jnp.zeros_like(st_ref)

    for sq in range(x_ref.shape[0]):
        p_ref[sq] = _dot(x_ref[sq].astype(BF16), w_ref[...])

    lg = lbl_ref[...]
    ex = jnp.exp(lg - jnp.max(lg, axis=0, keepdims=True))
    lb_all = jnp.sum(ex[: layer + 1], axis=0, keepdims=True) / jnp.sum(ex, axis=0, keepdims=True)
    ng = ng_ref[...]

    r = lax.broadcasted_iota(I32, (CHUNK, CHUNK), 0)
    c = lax.broadcasted_iota(I32, (CHUNK, CHUNK), 1)
    causal = c <= r
    tri = jnp.where(causal, 1.0, 0.0).astype(BF16)
    tri2 = jnp.concatenate([tri, tri], axis=1)

    units = [(sq, h) for sq in range(p_ref.shape[0]) for h in range(HG_HEADS)]

    def chunk_body(j, carry):
        rows = pl.ds(pl.multiple_of(j * CHUNK, CHUNK), CHUNK)
        ks, bs = [], []
        for sq, h in units:
            lo = h * HG_DK
            lb = lb_all[:, lo:lo + HG_DK]
            f = lb + (1.0 - lb) * _sigmoid(p_ref[sq, rows, HG_WIDTH + lo:HG_WIDTH + lo + HG_DK])
            lf_hi, lf_lo = _split(jnp.log(f))
            ks.append(1.0 - f)
            bs.append(_dot(tri2, jnp.concatenate([lf_hi, lf_lo], axis=0)))
        scs, ois, vbs = [], [], []
        for (sq, h), k, b in zip(units, ks, bs):
            lo = h * HG_DK
            b_last = b[CHUNK - 1:CHUNK, :]
            q_dec = (_silu(p_ref[sq, rows, lo:lo + HG_DK]) * jnp.exp(b)).astype(BF16)
            k_inv = (k * jnp.exp(-b)).astype(BF16)
            k_dec = (k * jnp.exp(b_last - b)).astype(BF16)
            vb = p_ref[sq, rows, 2 * HG_WIDTH + lo:2 * HG_WIDTH + lo + HG_DV].astype(BF16)
            st = st_ref[sq, h]
            scs.append(jnp.where(causal, _dot_nt(q_dec, k_inv), 0.0).astype(BF16))
            ois.append(_dot_nt(q_dec, st.astype(BF16)))
            st_ref[sq, h] = st * jnp.exp(b_last) + _dot_tn(vb, k_dec)
            vbs.append(vb)
        for (sq, h), sc, oi, vb in zip(units, scs, ois, vbs):
            lo = h * HG_DK
            o = _dot(sc, vb) + oi
            o = o * lax.rsqrt(jnp.mean(o * o, axis=-1, keepdims=True) + RMS_EPS) * ng
            gate = p_ref[sq, rows, 3 * HG_WIDTH + lo:3 * HG_WIDTH + lo + HG_DV]
            o_ref[sq, rows, lo:lo + HG_DV] = (o * _silu(gate)).astype(o_ref.dtype)
        return carry

    lax.fori_loop(0, n_chunks, chunk_body, 0)


def _hgrn(x, w_main, lb_logits, norm_g, layer, ct=512, n_seq=2):
    b, s, d = x.shape
    w = w_main.shape[1]
    n_seq = min(n_seq, b)
    ct = min(ct, s)
    n_chunks = ct // CHUNK
    return pl.pallas_call(
        functools.partial(_hgrn_kernel, n_chunks=n_chunks, layer=layer),
        grid=(b // n_seq, s // ct),
        in_specs=[pl.BlockSpec((n_seq, ct, d), lambda i, j: (i, j, 0)),
                  pl.BlockSpec(w_main.shape, lambda i, j: (0, 0)),
                  pl.BlockSpec(lb_logits.shape, lambda i, j: (0, 0)),
                  pl.BlockSpec((1, HG_DV), lambda i, j: (0, 0))],
        out_specs=pl.BlockSpec((n_seq, ct, HG_WIDTH), lambda i, j: (i, j, 0)),
        out_shape=jax.ShapeDtypeStruct((b, s, HG_WIDTH), BF16),
        scratch_shapes=[pltpu.VMEM((n_seq, ct, w), F32),
                        pltpu.VMEM((n_seq, HG_HEADS, HG_DV, HG_DK), F32)],
        compiler_params=_params(("parallel", "arbitrary")),
        name="hgrn",
    )(x, w_main, lb_logits, norm_g.reshape(1, HG_DV))


POS_SHIFT = 4
POS_RADIX = 1 << POS_SHIFT
KP_W = 2 * LANES
K_W = (AT_HEADS // 2) * KP_W
SEL_W = Q_RANK + LANES
KIDX_W = 6 * IDX_DH


def _dsa_prep_kernel(x_ref, wsel_ref, wckv_ref, qg_ref, kvg_ref, lng_ref, lnb_ref, wqT_ref, wqiT_ref,
                     wk_ref, wvT_ref, qT_ref, qiT_ref, k_ref, vT_ref, kidx_ref, wT_ref, w6_ref):
    @pl.when((pl.program_id(0) == 0) & (pl.program_id(1) == 0))
    def _():
        w6_ref[...] = _cat_wgt6(wsel_ref[...], 0)

    x = x_ref[...]
    sel = _dot(_cat_act6(x, 1), w6_ref[...])
    cq = sel[:, :Q_RANK]
    cqn = cq * lax.rsqrt(jnp.mean(cq * cq, axis=-1, keepdims=True) + RMS_EPS) * qg_ref[...]
    qT_ref[0] = (_dot_nt(wqT_ref[...], cqn.astype(BF16)) * (AT_DH ** -0.5 * LOG2E)).astype(qT_ref.dtype)
    qiT_ref[0] = _dot_nt(_cat_wgt6(wqiT_ref[...], 1), _cat_act6(cqn, 1))
    ckv = _dot(x.astype(BF16), wckv_ref[...])
    ckvn = (ckv * lax.rsqrt(jnp.mean(ckv * ckv, axis=-1, keepdims=True) + RMS_EPS)
            * kvg_ref[...]).astype(BF16)
    kmat = _dot(ckvn, wk_ref[...])
    tm = kmat.shape[0]
    s_abs = pl.program_id(1) * tm + lax.broadcasted_iota(I32, (tm, LANES), 0)
    lane = lax.broadcasted_iota(I32, (tm, LANES), 1)
    pos = jnp.where(lane < 3, s_abs >> POS_SHIFT,
                    jnp.where(lane < 6, s_abs & (POS_RADIX - 1), 0)).astype(F32)
    k_ref[0] = jnp.concatenate(
        [blk for j in range(AT_HEADS // 2) for blk in (kmat[:, j * LANES:(j + 1) * LANES], pos)],
        axis=1).astype(k_ref.dtype)
    for kb in range(vT_ref.shape[1]):
        vT_ref[0, kb] = _dot_nt(wvT_ref[...], ckvn[kb * KBLK:(kb + 1) * KBLK]).astype(vT_ref.dtype)
    k128 = sel[:, Q_RANK:]
    in_k = lax.broadcasted_iota(I32, (1, LANES), 1) < IDX_DH
    mu = jnp.sum(jnp.where(in_k, k128, 0.0), axis=-1, keepdims=True) * (1.0 / IDX_DH)
    dev = jnp.where(in_k, k128 - mu, 0.0)
    var = jnp.sum(dev * dev, axis=-1, keepdims=True) * (1.0 / IDX_DH)
    y = dev * lax.rsqrt(var + LN_EPS) * lng_ref[...] + lnb_ref[...]
    h, m, l = (p.astype(F32) for p in _split3(y))
    up = lambda p: pltpu.roll(p, IDX_DH, 1)
    kidx_ref[0] = jnp.concatenate([h + up(h), m + up(h), l + up(m)], axis=1).astype(BF16)
    tT = k128.T
    wT_ref[0] = tT[IDX_DH:IDX_DH + SUBLANES, :] * (IDX_HEADS ** -0.5 * IDX_DH ** -0.5)


def _dsa_prep(x, w_sel, w_ckv, q_norm_g, kv_norm_g, idx_ln_g, idx_ln_b, wqT, wqiT, wk, wvT):
    b, s, d = x.shape
    tm = min(2 * KBLK, s)
    kb_per = tm // KBLK
    nb = s // KBLK
    full = lambda a: pl.BlockSpec(a.shape, lambda i, j: (0,) * a.ndim)
    pad_lanes = lambda v: jnp.pad(v.reshape(1, -1), ((0, 0), (0, LANES - v.shape[0])))
    args = (w_sel, w_ckv, q_norm_g.reshape(1, -1), kv_norm_g.reshape(1, -1),
            pad_lanes(idx_ln_g), pad_lanes(idx_ln_b), wqT, wqiT, wk, wvT)
    return pl.pallas_call(
        _dsa_prep_kernel,
        grid=(b, s // tm),
        in_specs=[pl.BlockSpec((None, tm, d), lambda i, j: (i, j, 0))] + [full(a) for a in args],
        out_specs=[pl.BlockSpec((1, AT_WIDTH, tm), lambda i, j: (i, 0, j)),
                   pl.BlockSpec((1, IDX_HEADS * IDX_DH, tm), lambda i, j: (i, 0, j)),
                   pl.BlockSpec((1, tm, K_W), lambda i, j: (i, j, 0)),
                   pl.BlockSpec((1, kb_per, AT_WIDTH, KBLK), lambda i, j: (i, j, 0, 0)),
                   pl.BlockSpec((1, tm, KIDX_W), lambda i, j: (i, j, 0)),
                   pl.BlockSpec((1, SUBLANES, tm), lambda i, j: (i, 0, j))],
        out_shape=[jax.ShapeDtypeStruct((b, AT_WIDTH, s), BF16),
                   jax.ShapeDtypeStruct((b, IDX_HEADS * IDX_DH, s), F32),
                   jax.ShapeDtypeStruct((b, s, K_W), BF16),
                   jax.ShapeDtypeStruct((b, nb, AT_WIDTH, KBLK), BF16),
                   jax.ShapeDtypeStruct((b, s, KIDX_W), BF16),
                   jax.ShapeDtypeStruct((b, SUBLANES, s), F32)],
        scratch_shapes=[pltpu.VMEM((6 * d, SEL_W), BF16)],
        compiler_params=_params(("arbitrary", "arbitrary")),
        name="dsa_prep",
    )(x, *args)


def _sortable(x):
    bits = lax.bitcast_convert_type(x, I32)
    return jnp.where(bits < 0, bits ^ jnp.int32(0x7FFFFFFF), bits)


def _dsa_attn_kernel(qT_ref, qiT_ref, wT_ref, kidx_ref, k_ref, vT_ref, o_ref,
                     keys_ref, q2_ref, qbd_ref, acc_ref, m_ref, l_ref, *, k_top):
    qb = pl.program_id(1)
    t0 = qb * QBLK
    e_max = t0 + QBLK
    nkb = (e_max + KBLK - 1) >> 8

    lane = lax.broadcasted_iota(I32, (1, QBLK), 1)
    t_idx = t0 + lane
    end_t = ((t_idx >> 6) + 1) * CHUNK
    srow = lax.broadcasted_iota(I32, (KBLK, QBLK), 0)

    for h in range(IDX_HEADS):
        q2_ref[:, h * QBLK:(h + 1) * QBLK] = _cat_wgt6(qiT_ref[0, h * IDX_DH:(h + 1) * IDX_DH, :], 0)
    wrow = jnp.concatenate([wT_ref[0, h:h + 1, :] for h in range(IDX_HEADS)], axis=1)

    def score_body(kb, carry):
        rows = pl.ds(pl.multiple_of(kb * KBLK, KBLK), KBLK)
        logits = _dot(kidx_ref[0, rows, :], q2_ref[...])
        r = jnp.maximum(logits, 0.0) * wrow
        sc = r[:, :QBLK]
        for h in range(1, IDX_HEADS):
            sc = sc + r[:, h * QBLK:(h + 1) * QBLK]
        valid = (srow + kb * KBLK) < end_t
        keys_ref[rows, :] = _sortable(jnp.where(valid, sc, -jnp.inf))
        return carry

    lax.fori_loop(0, nkb, score_body, 0)

    def count_ge(cand):
        def body(kb, acc):
            rows = pl.ds(pl.multiple_of(kb * KBLK, KBLK), KBLK)
            hit = jnp.where(keys_ref[rows, :] >= cand, 1, 0).astype(I32)
            return acc + jnp.sum(hit.reshape(KBLK // SUBLANES, SUBLANES, QBLK), axis=0)
        acc = lax.fori_loop(0, nkb, body, jnp.zeros((SUBLANES, QBLK), I32))
        return jnp.sum(acc, axis=0, keepdims=True)

    zero = jnp.zeros((1, QBLK), I32)
    thr = jnp.where(count_ge(zero) >= k_top, zero, jnp.full((1, QBLK), INT_MIN, I32))

    def bit_body(i, thr):
        cand = thr | (jnp.int32(1) << (30 - i))
        return jnp.where(count_ge(cand) >= k_top, cand, thr)

    thr = lax.fori_loop(0, 31, bit_body, thr)
    need = (k_top - (count_ge(thr + 1))).astype(F32)

    log2e = [p.astype(F32) for p in _split3(jnp.full((1, QBLK), LOG2E, F32))]
    ci16 = lax.broadcasted_iota(I32, (2 * SUBLANES, QBLK), 0)

    def pos_coef(h):
        slope = 2.0 ** (-8.0 * (h + 1) / AT_HEADS)
        blk = jnp.zeros((2 * SUBLANES, QBLK), F32)
        for i in range(3):
            blk = jnp.where(ci16 == i, log2e[i] * (POS_RADIX * slope), blk)
            blk = jnp.where(ci16 == 3 + i, log2e[i] * slope, blk)
        return blk.astype(BF16)

    for j in range(AT_HEADS // 2):
        pair = qT_ref[0, j * LANES:(j + 1) * LANES, :]
        rr = lax.broadcasted_iota(I32, pair.shape, 0)
        zeros = jnp.zeros_like(pair)
        qbd_ref[j, :LANES, :QBLK] = jnp.where(rr < AT_DH, pair, zeros)
        qbd_ref[j, :LANES, QBLK:] = jnp.where(rr >= AT_DH, pair, zeros)
        qbd_ref[j, LANES:LANES + 2 * SUBLANES, :QBLK] = pos_coef(2 * j)
        qbd_ref[j, LANES:LANES + 2 * SUBLANES, QBLK:] = pos_coef(2 * j + 1)
        qbd_ref[j, LANES + 2 * SUBLANES:, :] = jnp.zeros((KP_W - LANES - 2 * SUBLANES, 2 * QBLK), BF16)
    acc_ref[...] = jnp.zeros_like(acc_ref)
    m_ref[...] = jnp.full_like(m_ref, -jnp.inf)
    l_ref[...] = jnp.zeros_like(l_ref)

    ri = lax.broadcasted_iota(I32, (KBLK, KBLK), 0)
    ci = lax.broadcasted_iota(I32, (KBLK, KBLK), 1)
    lstrict = jnp.where(ci < ri, 1.0, 0.0).astype(BF16)

    def attn_body(kb, carry, last=False):
        rows = pl.ds(pl.multiple_of(kb * KBLK, KBLK), KBLK)
        kblk = keys_ref[rows, :]
        s_idx = srow + kb * KBLK
        eq = kblk == thr
        eqf = jnp.where(eq, 1.0, 0.0)
        before = _dot(lstrict, eqf.astype(BF16)) + carry
        sel = ((kblk > thr) | (eq & (before < need))) & (s_idx < end_t)
        bias = jnp.where(sel, 0.0, -jnp.inf)
        if last:
            ahead = jnp.maximum(s_idx - t_idx, 0).astype(F32)
        s2s = [_dot(k_ref[0, rows, j * KP_W:(j + 1) * KP_W], qbd_ref[j])
               for j in range(AT_HEADS // 2)]
        for j in range(AT_HEADS // 2):
            ps, alphas = [], []
            for half in range(2):
                h = 2 * j + half
                st = s2s[j][:, half * QBLK:(half + 1) * QBLK] + bias
                if last:
                    st = st - (2.0 * LOG2E * 2.0 ** (-8.0 * (h + 1) / AT_HEADS)) * ahead
                m_old = m_ref[h:h + 1, :]
                m_new = jnp.maximum(m_old, jnp.max(st, axis=0, keepdims=True))
                m_safe = jnp.where(m_new == -jnp.inf, 0.0, m_new)
                alpha = jnp.exp2(m_old - m_safe)
                p = jnp.exp2(st - m_safe)
                l_ref[h:h + 1, :] = alpha * l_ref[h:h + 1, :] + jnp.sum(p, axis=0, keepdims=True)
                m_ref[h:h + 1, :] = m_new
                ps.append(p.astype(BF16))
                alphas.append(alpha)
            o2 = _dot(vT_ref[0, kb, j * LANES:(j + 1) * LANES, :], jnp.concatenate(ps, axis=1))
            for half in range(2):
                hs = slice((2 * j + half) * AT_DH, (2 * j + half + 1) * AT_DH)
                acc_ref[hs, :] = (alphas[half] * acc_ref[hs, :]
                                  + o2[half * AT_DH:(half + 1) * AT_DH, half * QBLK:(half + 1) * QBLK])
        return carry + jnp.sum(eqf, axis=0, keepdims=True)

    ties_seen = lax.fori_loop(0, nkb - 1, attn_body, jnp.zeros((1, QBLK), F32))
    attn_body(nkb - 1, ties_seen, last=True)

    for h in range(AT_HEADS):
        hs = slice(h * AT_DH, (h + 1) * AT_DH)
        acc_ref[hs, :] = acc_ref[hs, :] / l_ref[h:h + 1, :]
    o_ref[0] = acc_ref[...].T.astype(o_ref.dtype)


def _dsa_attn(qT, qiT, wT, kidx, k, vT):
    b, _, s = qT.shape
    nb = s // KBLK
    k_top = min(TOPK_MAX, s // 4)
    return pl.pallas_call(
        functools.partial(_dsa_attn_kernel, k_top=k_top),
        grid=(b, s // QBLK),
        in_specs=[pl.BlockSpec((1, AT_WIDTH, QBLK), lambda i, j: (i, 0, j)),
                  pl.BlockSpec((1, IDX_HEADS * IDX_DH, QBLK), lambda i, j: (i, 0, j)),
                  pl.BlockSpec((1, SUBLANES, QBLK), lambda i, j: (i, 0, j)),
                  pl.BlockSpec((1, s, KIDX_W), lambda i, j: (i, 0, 0)),
                  pl.BlockSpec((1, s, K_W), lambda i, j: (i, 0, 0)),
                  pl.BlockSpec((1, nb, AT_WIDTH, KBLK), lambda i, j: (i, 0, 0, 0))],
        out_specs=pl.BlockSpec((1, QBLK, AT_WIDTH), lambda i, j: (i, j, 0)),
        out_shape=jax.ShapeDtypeStruct((b, s, AT_WIDTH), BF16),
        scratch_shapes=[pltpu.VMEM((s, QBLK), I32),
                        pltpu.VMEM((KIDX_W, IDX_HEADS * QBLK), BF16),
                        pltpu.VMEM((AT_HEADS // 2, KP_W, 2 * QBLK), BF16),
                        pltpu.VMEM((AT_WIDTH, QBLK), F32),
                        pltpu.VMEM((AT_HEADS, QBLK), F32),
                        pltpu.VMEM((AT_HEADS, QBLK), F32)],
        compiler_params=_params(("parallel", "arbitrary")),
        name="dsa_attn",
    )(qT, qiT, wT, kidx, k, vT)


def _layer_norm(v, g, b):
    mu = jnp.mean(v, axis=-1, keepdims=True)
    var = jnp.mean(jnp.square(v - mu), axis=-1, keepdims=True)
    return (v - mu) * lax.rsqrt(var + LN_EPS) * g + b


def _out_ln_kernel(oa_ref, ob_ref, x_ref, wa_ref, wb_ref, g_ref, b_ref, wrT_ref, bias_ref,
                   h_ref, hb_ref, eidx_ref, gT_ref, cnt_ref):
    mix = _dot(oa_ref[...], wa_ref[...]) + _dot(ob_ref[...], wb_ref[...])
    h = _layer_norm(ALPHA * x_ref[...] + mix, g_ref[...], b_ref[...])
    h_ref[...] = h
    hb_ref[...] = h.astype(BF16)
    eidx_ref[...], gT_ref[...], cnt_ref[0] = _route(h, wrT_ref[...], bias_ref[...])


def _out_ln(oa, ob, x2, wa, wb, g, b, wrT, bias_col, tm):
    t, d = x2.shape
    nt = t // tm
    row = lambda w: pl.BlockSpec((tm, w), lambda i: (i, 0))
    full = lambda a: pl.BlockSpec(a.shape, lambda i: (0, 0))
    return pl.pallas_call(
        _out_ln_kernel,
        grid=(nt,),
        in_specs=[row(oa.shape[1]), row(ob.shape[1]), row(d), full(wa), full(wb), full(g), full(b),
                  full(wrT), full(bias_col)],
        out_specs=[row(d), row(d),
                   pl.BlockSpec((TOP_K, tm), lambda i: (0, i)),
                   row(LANES),
                   pl.BlockSpec((1, N_EXPERTS, LANES), lambda i: (i, 0, 0))],
        out_shape=[jax.ShapeDtypeStruct((t, d), F32), jax.ShapeDtypeStruct((t, d), BF16),
                   jax.ShapeDtypeStruct((TOP_K, t), I32),
                   jax.ShapeDtypeStruct((t, LANES), F32),
                   jax.ShapeDtypeStruct((nt, N_EXPERTS, LANES), F32)],
        compiler_params=_params(("parallel",)),
        name="out_ln_route",
    )(oa, ob, x2, wa, wb, g, b, wrT, bias_col)


def _first_max(v, idx, big):
    m = jnp.max(v, axis=0, keepdims=True)
    first = jnp.min(jnp.where(v == m, idx, big), axis=0, keepdims=True)
    return m, first


def _route(h, wrT, bias_col):
    tm = h.shape[0]
    per_group = N_EXPERTS // N_GROUPS
    logits = _dot_nt(_cat_hlh(wrT, 1), _cat_hhl(h, 1))
    scores = _sigmoid(logits)
    sel = scores + bias_col
    iota_g = lax.broadcasted_iota(I32, (per_group, tm), 0)
    neg = -jnp.inf

    grp = jnp.zeros((N_GROUPS, tm), F32)
    iota_grp = lax.broadcasted_iota(I32, (N_GROUPS, tm), 0)
    for g in range(N_GROUPS):
        blk = sel[g * per_group:(g + 1) * per_group, :]
        m1, i1 = _first_max(blk, iota_g, per_group)
        m2 = jnp.max(jnp.where(iota_g == i1, neg, blk), axis=0, keepdims=True)
        grp = jnp.where(iota_grp == g, m1 + m2, grp)

    gmask = jnp.zeros((N_GROUPS, tm), F32)
    work = grp
    for _ in range(TOPK_GROUPS):
        _, gi = _first_max(work, iota_grp, N_GROUPS)
        hit = iota_grp == gi
        gmask = jnp.where(hit, 1.0, gmask)
        work = jnp.where(hit, neg, work)

    e_mask = jnp.concatenate(
        [jnp.broadcast_to(gmask[g:g + 1, :], (per_group, tm)) for g in range(N_GROUPS)], axis=0)
    iota_e = lax.broadcasted_iota(I32, (N_EXPERTS, tm), 0)
    work = jnp.where(e_mask > 0.5, sel, neg)
    iota_k = lax.broadcasted_iota(I32, (TOP_K, tm), 0)
    eidx = jnp.zeros((TOP_K, tm), I32)
    w = jnp.zeros((N_EXPERTS, tm), F32)
    chosen = jnp.zeros((N_EXPERTS, tm), F32)
    for k in range(TOP_K):
        _, ei = _first_max(work, iota_e, N_EXPERTS)
        hit = iota_e == ei
        eidx = jnp.where(iota_k == k, ei, eidx)
        w = jnp.where(hit, scores, w)
        chosen = jnp.where(hit, 1.0, chosen)
        work = jnp.where(hit, neg, work)

    gates = w / jnp.sum(w, axis=0, keepdims=True) * ROUTED_SCALE
    gates = jnp.concatenate([gates, jnp.zeros((LANES - N_EXPERTS, tm), F32)], axis=0)
    cnt = jnp.broadcast_to(jnp.sum(chosen, axis=1, keepdims=True), (N_EXPERTS, LANES))
    return eidx, gates.T, cnt


MOE_TM = 512
SEG = 16
EXP_TM = 768
ROW_BUF = TOP_K * MOE_TM + N_EXPERTS * SEG
CH_MAX = ROW_BUF // SEG
P_BLK = 512
G_BLK = 512


def _moe_plan(cnt):
    nt = cnt.shape[0]
    n16 = (cnt + (SEG - 1)) // SEG
    so16 = jnp.cumsum(n16, axis=1) - n16
    r16 = jnp.sum(n16, axis=0)
    per = EXP_TM // SEG
    rp16 = (r16 + (per - 1)) // per * per
    ends = jnp.cumsum(rp16)
    off16 = ends - rp16
    go16 = off16[None, :] + jnp.cumsum(n16, axis=0) - n16
    c = jnp.arange(CH_MAX, dtype=I32)[None, :, None]
    lo, hi = so16[:, None, :], (so16 + n16)[:, None, :]
    inside = (lo <= c) & (c < hi)
    dst16 = jnp.sum(jnp.where(inside, go16[:, None, :] + c - lo, 0), axis=2)
    dst16 = dst16.reshape(nt * CH_MAX).astype(I32)
    twice = lambda a: jnp.concatenate([a, a], axis=1).astype(F32)
    seg_lo, seg_hi = twice(so16 * SEG), twice((so16 + n16) * SEG)
    as_rows = lambda a: jnp.broadcast_to(a[:, None, :], (nt, SUBLANES, 2 * N_EXPERTS))
    as_cols = lambda a: jnp.broadcast_to(a[:, :, None], (nt, 2 * N_EXPERTS, LANES))
    return dict(dst16=dst16,
                rtot16=jnp.sum(n16, axis=1).astype(I32), r16=r16.astype(I32), rp16=rp16.astype(I32),
                off16=off16.astype(I32),
                tile_first=(off16 // per).astype(I32), tile_count=(rp16 // per).astype(I32),
                lo_rows=as_rows(seg_lo), hi_rows=as_rows(seg_hi),
                lo_cols=as_cols(seg_lo), hi_cols=as_cols(seg_hi))


def _chunk(ref, c):
    return ref.at[pl.ds(pl.multiple_of(c * SEG, SEG), SEG)]


def _segment_copies(i, dst16_ref, rtot_ref, make_copy):
    n = rtot_ref[i]

    def chunk_body(c, carry):
        make_copy(c, dst16_ref[i * CH_MAX + c]).start()
        return carry

    lax.fori_loop(0, n, chunk_body, 0)
    return n


def _wait_copies(count, make_copy):
    def body(c, carry):
        make_copy(0, 0).wait()
        return carry

    lax.fori_loop(0, count, body, 0)


POS_SPLIT = 64


def _dispatch_kernel(dst16_ref, rtot_ref, r16_ref, rp16_ref, off16_ref,
                     x_ref, eidx_ref, locol_ref, lorow_ref, hirow_ref,
                     xs_hbm, posT_ref, buf_ref, zero_ref, pend_ref, sem):
    i = pl.program_id(0)
    slot = lax.rem(i, 2)
    buf = buf_ref.at[slot]
    tm = x_ref.shape[0]
    eidx = eidx_ref[...]
    iota_e = lax.broadcasted_iota(I32, (N_EXPERTS, tm), 0)
    onehot = jnp.zeros((N_EXPERTS, tm), F32)
    for k in range(TOP_K):
        onehot = jnp.where(iota_e == eidx[k:k + 1, :], 1.0, onehot)
    earlier = lax.broadcasted_iota(I32, (tm, tm), 0) < lax.broadcasted_iota(I32, (tm, tm), 1)
    rank = _dot(onehot.astype(BF16), jnp.where(earlier, 1.0, 0.0).astype(BF16))
    pos1 = jnp.where(onehot > 0.5, locol_ref[0][:N_EXPERTS, 0:1] + rank + 1.0, 0.0)
    pos_hi = jnp.floor(pos1 * (1.0 / POS_SPLIT)) * POS_SPLIT
    pos2 = jnp.concatenate([pos_hi, pos1 - pos_hi], axis=0)
    posT_ref[...] = pos2.T
    pos2b = pos2.astype(BF16)

    x = x_ref[...]
    seg_lo = lorow_ref[0][0:1, :]
    seg_hi = hirow_ref[0][0:1, :]
    n_blk = (rtot_ref[i] * SEG + (P_BLK - 1)) // P_BLK

    def blk_body(rb, carry):
        r0 = pl.multiple_of(rb * P_BLK, P_BLK)
        r_e = (lax.broadcasted_iota(I32, (P_BLK, 2 * N_EXPERTS), 0) + r0).astype(F32)
        owner = jnp.where((r_e >= seg_lo) & (r_e < seg_hi), 1.0, 0.0).astype(BF16)
        want = _dot(owner, pos2b)
        r_t = (lax.broadcasted_iota(I32, (P_BLK, tm), 0) + (r0 + 1)).astype(F32)
        p = jnp.where(want == r_t, 1.0, 0.0).astype(BF16)
        buf[pl.ds(r0, P_BLK), :] = _dot(p, x).astype(BF16)
        return carry

    lax.fori_loop(0, n_blk, blk_body, 0)

    @pl.when(i > 0)
    def _():
        prev_copy = lambda src, dst: pltpu.make_async_copy(
            _chunk(buf_ref.at[1 - slot], src), _chunk(xs_hbm, dst), sem.at[1 - slot])
        _wait_copies(pend_ref[0], prev_copy)

    out_copy = lambda src, dst: pltpu.make_async_copy(_chunk(buf, src), _chunk(xs_hbm, dst),
                                                      sem.at[slot])
    pend_ref[0] = _segment_copies(i, dst16_ref, rtot_ref, out_copy)

    @pl.when(i == pl.num_programs(0) - 1)
    def _():
        _wait_copies(pend_ref[0], out_copy)
        zero_ref[...] = jnp.zeros_like(zero_ref)
        zero_copy = lambda src, dst: pltpu.make_async_copy(_chunk(zero_ref, 0), _chunk(xs_hbm, dst),
                                                           sem.at[slot])

        def zero_chunk(c, carry):
            zero_copy(0, c).start()
            return carry

        def tail_body(e, total):
            lax.fori_loop(off16_ref[e] + r16_ref[e], off16_ref[e] + rp16_ref[e], zero_chunk, 0)
            return total + rp16_ref[e] - r16_ref[e]

        _wait_copies(lax.fori_loop(0, N_EXPERTS, tail_body, 0), zero_copy)

        per = EXP_TM // SEG
        first = (off16_ref[N_EXPERTS - 1] + rp16_ref[N_EXPERTS - 1]) // per
        n_tiles = xs_hbm.shape[0] // EXP_TM
        tile_copy = lambda tile: pltpu.make_async_copy(
            zero_ref, xs_hbm.at[pl.ds(pl.multiple_of(tile * EXP_TM, EXP_TM), EXP_TM)], sem.at[slot])

        def zero_tile(tile, carry):
            tile_copy(tile).start()
            return carry

        def wait_tile(tile, carry):
            tile_copy(0).wait()
            return carry

        lax.fori_loop(first, n_tiles, zero_tile, 0)
        lax.fori_loop(first, n_tiles, wait_tile, 0)


def _dispatch(hb, eidx, plan, n_rows):
    t, d = hb.shape
    nt = t // MOE_TM
    grid_spec = pltpu.PrefetchScalarGridSpec(
        num_scalar_prefetch=5,
        grid=(nt,),
        in_specs=[pl.BlockSpec((MOE_TM, d), lambda i, *_: (i, 0)),
                  pl.BlockSpec((TOP_K, MOE_TM), lambda i, *_: (0, i)),
                  pl.BlockSpec((1, 2 * N_EXPERTS, LANES), lambda i, *_: (i, 0, 0)),
                  pl.BlockSpec((1, SUBLANES, 2 * N_EXPERTS), lambda i, *_: (i, 0, 0)),
                  pl.BlockSpec((1, SUBLANES, 2 * N_EXPERTS), lambda i, *_: (i, 0, 0))],
        out_specs=[pl.BlockSpec(memory_space=pl.ANY),
                   pl.BlockSpec((MOE_TM, LANES), lambda i, *_: (i, 0))],
        scratch_shapes=[pltpu.VMEM((2, ROW_BUF, d), BF16), pltpu.VMEM((EXP_TM, d), BF16),
                        pltpu.SMEM((1,), I32), pltpu.SemaphoreType.DMA((2,))])
    return pl.pallas_call(
        _dispatch_kernel,
        grid_spec=grid_spec,
        out_shape=[jax.ShapeDtypeStruct((n_rows, d), BF16),
                   jax.ShapeDtypeStruct((t, LANES), F32)],
        compiler_params=_params(("arbitrary",)),
        name="moe_dispatch",
    )(plan["dst16"], plan["rtot16"], plan["r16"], plan["rp16"],
      plan["off16"], hb, eidx, plan["lo_cols"], plan["lo_rows"], plan["hi_rows"])


def _experts_kernel(first_ref, count_ref, xs_hbm, wg_ref, wu_ref, wd_ref, ys_hbm,
                    xbuf_ref, ybuf_ref, wgb_ref, wub_ref, wdb_ref, sem_in, sem_out):
    e = pl.program_id(0)
    first, n = first_ref[e], count_ref[e]

    def tile(ref, t):
        return ref.at[pl.ds(pl.multiple_of((first + t) * EXP_TM, EXP_TM), EXP_TM)]

    fetch = lambda t, slot: pltpu.make_async_copy(tile(xs_hbm, t), xbuf_ref.at[slot], sem_in.at[slot])
    store = lambda t, slot: pltpu.make_async_copy(ybuf_ref.at[slot], tile(ys_hbm, t), sem_out.at[slot])
    last = pl.num_programs(0) - 1
    handed_over = (e > 0) & (count_ref[jnp.maximum(e - 1, 0)] > 0)
    hands_over = (e < last) & (count_ref[jnp.minimum(e + 1, last)] > 0)

    @pl.when(n > 0)
    def _():
        @pl.when(jnp.logical_not(handed_over))
        def _():
            fetch(0, 0).start()

        wgb_ref[...] = wg_ref[0].astype(BF16)
        wub_ref[...] = wu_ref[0].astype(BF16)
        wdb_ref[...] = wd_ref[0].astype(BF16)

        def body(t, carry):
            slot = lax.rem(t, 2)
            fetch(t, slot).wait()

            @pl.when(t + 1 < n)
            def _():
                fetch(t + 1, 1 - slot).start()

            @pl.when(t >= 2)
            def _():
                store(t - 2, slot).wait()

            x = xbuf_ref[slot]
            hmid = _silu(_dot(x, wgb_ref[...])) * _dot(x, wub_ref[...])
            ybuf_ref[slot] = _dot(hmid.astype(BF16), wdb_ref[...]).astype(ybuf_ref.dtype)
            store(t, slot).start()
            return carry

        lax.fori_loop(0, n, body, 0)

        @pl.when(hands_over)
        def _():
            fetch(n, 0).start()

        store(n - 1, lax.rem(n - 1, 2)).wait()

        @pl.when(n >= 2)
        def _():
            store(n - 2, lax.rem(n, 2)).wait()


def _experts(xs, wg, wu, wd, plan):
    n_rows, d = xs.shape
    w_map = lambda e, *_: (e, 0, 0)
    grid_spec = pltpu.PrefetchScalarGridSpec(
        num_scalar_prefetch=2,
        grid=(wg.shape[0],),
        in_specs=[pl.BlockSpec(memory_space=pl.ANY),
                  pl.BlockSpec((1, d, D_EXPERT), w_map),
                  pl.BlockSpec((1, d, D_EXPERT), w_map),
                  pl.BlockSpec((1, D_EXPERT, d), w_map)],
        out_specs=pl.BlockSpec(memory_space=pl.ANY),
        scratch_shapes=[pltpu.VMEM((2, EXP_TM, d), BF16), pltpu.VMEM((2, EXP_TM, d), BF16),
                        pltpu.VMEM((d, D_EXPERT), BF16), pltpu.VMEM((d, D_EXPERT), BF16),
                        pltpu.VMEM((D_EXPERT, d), BF16),
                        pltpu.SemaphoreType.DMA((2,)), pltpu.SemaphoreType.DMA((2,))])
    return pl.pallas_call(
        _experts_kernel,
        grid_spec=grid_spec,
        out_shape=jax.ShapeDtypeStruct((n_rows, d), BF16),
        input_output_aliases={2: 0},
        compiler_params=_params(("arbitrary",)),
        name="moe_experts",
    )(plan["tile_first"], plan["tile_count"], xs, wg, wu, wd)


def _combine_kernel(dst16_ref, rtot_ref,
                    ys_hbm, posT_ref, gT_ref, locol_ref, hicol_ref, h_ref,
                    sg_ref, su_ref, sd_ref, lg_ref, lb_ref, o_ref, ybuf_ref, acc_ref, pend_ref, sem):
    i = pl.program_id(0)
    slot = lax.rem(i, 2)
    ybuf = ybuf_ref.at[slot]
    tm = h_ref.shape[0]

    def fetch(tile, to_slot):
        copy = lambda dst, src: pltpu.make_async_copy(
            _chunk(ys_hbm, src), _chunk(ybuf_ref.at[to_slot], dst), sem.at[to_slot])
        pend_ref[to_slot] = _segment_copies(tile, dst16_ref, rtot_ref, copy)

    @pl.when(i == 0)
    def _():
        ybuf_ref[...] = jnp.zeros_like(ybuf_ref)
        fetch(0, 0)

    @pl.when(i + 1 < pl.num_programs(0))
    def _():
        fetch(i + 1, 1 - slot)

    h = h_ref[...]
    xb = h.astype(BF16)
    hs = _silu(_dot(xb, sg_ref[...])) * _dot(xb, su_ref[...])
    acc_ref[...] = ALPHA * h + _dot(hs.astype(BF16), sd_ref[...])
    _wait_copies(pend_ref[slot], lambda dst, src: pltpu.make_async_copy(
        _chunk(ys_hbm, src), _chunk(ybuf, dst), sem.at[slot]))

    pos2b = posT_ref[...].astype(BF16)
    g_hi, g_lo = _split(gT_ref[...])
    gate2b = (g_hi.astype(F32) + pltpu.roll(g_lo.astype(F32), N_EXPERTS, 1)).astype(BF16)
    seg_lo = jnp.broadcast_to(locol_ref[0][:, 0:1], (2 * N_EXPERTS, G_BLK))
    seg_hi = jnp.broadcast_to(hicol_ref[0][:, 0:1], (2 * N_EXPERTS, G_BLK))
    n_blk = (rtot_ref[i] * SEG + (G_BLK - 1)) // G_BLK

    def blk_body(cb, carry):
        c0 = pl.multiple_of(cb * G_BLK, G_BLK)
        c_e = (lax.broadcasted_iota(I32, (2 * N_EXPERTS, G_BLK), 1) + c0).astype(F32)
        owner = jnp.where((c_e >= seg_lo) & (c_e < seg_hi), 1.0, 0.0).astype(BF16)
        want = _dot(pos2b, owner)
        gate = _dot(gate2b, owner)
        c_t = (lax.broadcasted_iota(I32, (tm, G_BLK), 1) + (c0 + 1)).astype(F32)
        g = jnp.where(want == c_t, gate, 0.0).astype(BF16)
        acc_ref[...] += _dot(g, ybuf[pl.ds(c0, G_BLK), :])
        return carry

    lax.fori_loop(0, n_blk, blk_body, 0)
    o_ref[...] = _layer_norm(acc_ref[...], lg_ref[...], lb_ref[...])


def _combine(ys, posT, gT, h1, sg, su, sd, lg, lb, plan):
    t, d = h1.shape
    nt = t // MOE_TM
    row = lambda w: pl.BlockSpec((MOE_TM, w), lambda i, *_: (i, 0))
    full = lambda a: pl.BlockSpec(a.shape, lambda i, *_: (0, 0))
    seg = pl.BlockSpec((1, 2 * N_EXPERTS, LANES), lambda i, *_: (i, 0, 0))
    grid_spec = pltpu.PrefetchScalarGridSpec(
        num_scalar_prefetch=2,
        grid=(nt,),
        in_specs=[pl.BlockSpec(memory_space=pl.ANY), row(LANES), row(LANES), seg, seg, row(d),
                  full(sg), full(su), full(sd), full(lg), full(lb)],
        out_specs=row(d),
        scratch_shapes=[pltpu.VMEM((2, ROW_BUF, d), BF16), pltpu.VMEM((MOE_TM, d), F32),
                        pltpu.SMEM((2,), I32), pltpu.SemaphoreType.DMA((2,))])
    return pl.pallas_call(
        _combine_kernel,
        grid_spec=grid_spec,
        out_shape=jax.ShapeDtypeStruct((t, d), F32),
        compiler_params=_params(("arbitrary",)),
        name="moe_combine",
    )(plan["dst16"], plan["rtot16"], ys, posT, gT,
      plan["lo_cols"], plan["hi_cols"], h1, sg, su, sd, lg, lb)


def _moe(h1, hb, eidx, gT, cnt, wg, wu, wd, sg, su, sd, lg, lb):
    t = h1.shape[0]
    nt = t // MOE_TM
    max_rows = nt * ROW_BUF + N_EXPERTS * EXP_TM
    n_exp_tiles = pl.cdiv(max_rows, EXP_TM)
    plan = _moe_plan(cnt[:, :, 0].astype(I32))
    xs, posT = _dispatch(hb, eidx, plan, n_exp_tiles * EXP_TM)
    ys = _experts(xs, wg, wu, wd, plan)
    return _combine(ys, posT, gT, h1, sg, su, sd, lg, lb, plan)


def kernel(x, w_in, hg_lb_logits, hg_norm_g, q_norm_g, w_q_up, w_qidx_up, kv_norm_g, w_kv_up,
           idx_ln_g, idx_ln_b, w_out, ln1_g, ln1_b, w_router, router_bias, w_e_gate, w_e_up,
           w_e_down, w_s_gate, w_s_up, w_s_down, ln2_g, ln2_b):
    b, s, d = x.shape
    h = x
    for l in range(DEPTH):
        h2 = h.reshape(b * s, d)
        n_main = 4 * HG_WIDTH
        w_main = w_in[l, :, :n_main].astype(BF16)
        c_kv, c_idx = n_main + Q_RANK, n_main + Q_RANK + KV_RANK
        w_sel = jnp.concatenate([w_in[l, :, n_main:c_kv], w_in[l, :, c_idx:]], axis=1)
        w_sel = jnp.pad(w_sel, ((0, 0), (0, SEL_W - w_sel.shape[1])))
        w_ckv = w_in[l, :, c_kv:c_idx].astype(BF16)

        o_a = _hgrn(h, w_main, hg_lb_logits, hg_norm_g[l], l)

        wqT = w_q_up[l].T.astype(BF16)
        wqiT = w_qidx_up[l].T
        wk = w_kv_up[l][:, :AT_WIDTH].astype(BF16)
        wvT = w_kv_up[l][:, AT_WIDTH:].T.astype(BF16)
        qT, qiT, k, vT, kidx, wT = _dsa_prep(h, w_sel, w_ckv, q_norm_g[l], kv_norm_g[l],
                                             idx_ln_g[l], idx_ln_b[l], wqT, wqiT, wk, wvT)
        o_b = _dsa_attn(qT, qiT, wT, kidx, k, vT)

        h1, hb, eidx, gT, cnt = _out_ln(
            o_a.reshape(b * s, HG_WIDTH), o_b.reshape(b * s, AT_WIDTH), h2,
            w_out[l, :HG_WIDTH].astype(BF16), w_out[l, HG_WIDTH:].astype(BF16),
            ln1_g[l].reshape(1, d), ln1_b[l].reshape(1, d),
            w_router[l].T, router_bias[l].reshape(N_EXPERTS, 1), MOE_TM)
        out = _moe(h1, hb, eidx, gT, cnt, w_e_gate[l], w_e_up[l], w_e_down[l],
                   w_s_gate[l].astype(BF16), w_s_up[l].astype(BF16), w_s_down[l].astype(BF16),
                   ln2_g[l].reshape(1, d), ln2_b[l].reshape(1, d))
        h = out.reshape(b, s, d)
    return h
```

```python
import functools

import jax
import jax.numpy as jnp
from jax import lax
from jax.experimental import pallas as pl
from jax.experimental.pallas import tpu as pltpu

F32 = jnp.float32
BF16 = jnp.bfloat16
I32 = jnp.int32
HIGHEST = lax.Precision.HIGHEST

CHUNK = 64
HG_HEADS = 4
HG_DK = 128
HG_DV = 128
HG_WIDTH = HG_HEADS * HG_DV
AT_HEADS = 8
AT_DH = 64
AT_WIDTH = AT_HEADS * AT_DH
Q_RANK = 256
KV_RANK = 128
IDX_HEADS = 4
IDX_DH = 64
TOPK_MAX = 256
N_EXPERTS = 64
TOP_K = 8
N_GROUPS = 8
TOPK_GROUPS = 4
D_EXPERT = 256
ROUTED_SCALE = 2.5
DEPTH = 1
ALPHA = (2.0 * DEPTH) ** 0.25
LN_EPS = 1e-5
RMS_EPS = 1e-6
LOG2E = 1.4426950408889634

LANES = 128
SUBLANES = 8
QBLK = 4 * CHUNK
KBLK = 256
VMEM_LIMIT = 56 * 1024 * 1024
INT_MIN = -2 ** 31


def _params(sem, vmem=VMEM_LIMIT):
    return pltpu.CompilerParams(dimension_semantics=sem, vmem_limit_bytes=vmem)


def _dot(a, b, precision=None):
    return jnp.dot(a, b, preferred_element_type=F32, precision=precision)


def _dot_nt(a, b, precision=None):
    return lax.dot_general(a, b, (((1,), (1,)), ((), ())), preferred_element_type=F32,
                           precision=precision)


def _dot_tn(a, b, precision=None):
    return lax.dot_general(a, b, (((0,), (0,)), ((), ())), preferred_element_type=F32,
                           precision=precision)


def _split(x):
    hi = x.astype(BF16)
    return hi, (x - hi.astype(F32)).astype(BF16)


def _cat_hhl(x, axis):
    hi, lo = _split(x)
    return jnp.concatenate([hi, hi, lo], axis=axis)


def _cat_hlh(x, axis):
    hi, lo = _split(x)
    return jnp.concatenate([hi, lo, hi], axis=axis)


def _split3(x):
    hi = x.astype(BF16)
    r = x - hi.astype(F32)
    mid = r.astype(BF16)
    return hi, mid, (r - mid.astype(F32)).astype(BF16)


def _cat_act6(x, axis):
    h, m, l = _split3(x)
    return jnp.concatenate([h, h, m, h, l, m], axis=axis)


def _cat_wgt6(x, axis):
    h, m, l = _split3(x)
    return jnp.concatenate([h, m, h, l, h, m], axis=axis)


def _sigmoid(x):
    return 1.0 / (1.0 + jnp.exp(-x))


def _silu(x):
    return x * _sigmoid(x)


def _hgrn_kernel(x_ref, w_ref, lbl_ref, ng_ref, o_ref, p_ref, st_ref, *, n_chunks, layer):
    @pl.when(pl.program_id(1) == 0)
    def _():
        st_ref[...] = jnp.zeros_like(st_ref)

    for sq in range(x_ref.shape[0]):
        p_ref[sq] = _dot(x_ref[sq].astype(BF16), w_ref[...])

    lg = lbl_ref[...]
    ex = jnp.exp(lg - jnp.max(lg, axis=0, keepdims=True))
    lb_all = jnp.sum(ex[: layer + 1], axis=0, keepdims=True) / jnp.sum(ex, axis=0, keepdims=True)
    ng = ng_ref[...]

    r = lax.broadcasted_iota(I32, (CHUNK, CHUNK), 0)
    c = lax.broadcasted_iota(I32, (CHUNK, CHUNK), 1)
    causal = c <= r
    tri = jnp.where(causal, 1.0, 0.0).astype(BF16)
    tri2 = jnp.concatenate([tri, tri], axis=1)

    units = [(sq, h) for sq in range(p_ref.shape[0]) for h in range(HG_HEADS)]

    def chunk_body(j, carry):
        rows = pl.ds(pl.multiple_of(j * CHUNK, CHUNK), CHUNK)
        ks, bs = [], []
        for sq, h in units:
            lo = h * HG_DK
            lb = lb_all[:, lo:lo + HG_DK]
            f = lb + (1.0 - lb) * _sigmoid(p_ref[sq, rows, HG_WIDTH + lo:HG_WIDTH + lo + HG_DK])
            lf_hi, lf_lo = _split(jnp.log(f))
            ks.append(1.0 - f)
            bs.append(_dot(tri2, jnp.concatenate([lf_hi, lf_lo], axis=0)))
        scs, ois, vbs = [], [], []
        for (sq, h), k, b in zip(units, ks, bs):
            lo = h * HG_DK
            b_last = b[CHUNK - 1:CHUNK, :]
            q_dec = (_silu(p_ref[sq, rows, lo:lo + HG_DK]) * jnp.exp(b)).astype(BF16)
            k_inv = (k * jnp.exp(-b)).astype(BF16)
            k_dec = (k * jnp.exp(b_last - b)).astype(BF16)
            vb = p_ref[sq, rows, 2 * HG_WIDTH + lo:2 * HG_WIDTH + lo + HG_DV].astype(BF16)
            st = st_ref[sq, h]
            scs.append(jnp.where(causal, _dot_nt(q_dec, k_inv), 0.0).astype(BF16))
            ois.append(_dot_nt(q_dec, st.astype(BF16)))
            st_ref[sq, h] = st * jnp.exp(b_last) + _dot_tn(vb, k_dec)
            vbs.append(vb)
        for (sq, h), sc, oi, vb in zip(units, scs, ois, vbs):
            lo = h * HG_DK
            o = _dot(sc, vb) + oi
            o = o * lax.rsqrt(jnp.mean(o * o, axis=-1, keepdims=True) + RMS_EPS) * ng
            gate = p_ref[sq, rows, 3 * HG_WIDTH + lo:3 * HG_WIDTH + lo + HG_DV]
            o_ref[sq, rows, lo:lo + HG_DV] = (o * _silu(gate)).astype(o_ref.dtype)
        return carry

    lax.fori_loop(0, n_chunks, chunk_body, 0)


def _hgrn(x, w_main, lb_logits, norm_g, layer, ct=512, n_seq=2):
    b, s, d = x.shape
    w = w_main.shape[1]
    n_seq = min(n_seq, b)
    ct = min(ct, s)
    n_chunks = ct // CHUNK
    return pl.pallas_call(
        functools.partial(_hgrn_kernel, n_chunks=n_chunks, layer=layer),
        grid=(b // n_seq, s // ct),
        in_specs=[pl.BlockSpec((n_seq, ct, d), lambda i, j: (i, j, 0)),
                  pl.BlockSpec(w_main.shape, lambda i, j: (0, 0)),
                  pl.BlockSpec(lb_logits.shape, lambda i, j: (0, 0)),
                  pl.BlockSpec((1, HG_DV), lambda i, j: (0, 0))],
        out_specs=pl.BlockSpec((n_seq, ct, HG_WIDTH), lambda i, j: (i, j, 0)),
        out_shape=jax.ShapeDtypeStruct((b, s, HG_WIDTH), BF16),
        scratch_shapes=[pltpu.VMEM((n_seq, ct, w), F32),
                        pltpu.VMEM((n_seq, HG_HEADS, HG_DV, HG_DK), F32)],
        compiler_params=_params(("parallel", "arbitrary")),
        name="hgrn",
    )(x, w_main, lb_logits, norm_g.reshape(1, HG_DV))


POS_SHIFT = 4
POS_RADIX = 1 << POS_SHIFT
KP_W = 2 * LANES
K_W = (AT_HEADS // 2) * KP_W
SEL_W = Q_RANK + LANES
KIDX_W = 6 * IDX_DH


def _dsa_prep_kernel(x_ref, wsel_ref, wckv_ref, qg_ref, kvg_ref, lng_ref, lnb_ref, wqT_ref, wqiT_ref,
                     wk_ref, wvT_ref, qT_ref, qiT_ref, k_ref, vT_ref, kidx_ref, wT_ref, w6_ref):
    @pl.when((pl.program_id(0) == 0) & (pl.program_id(1) == 0))
    def _():
        w6_ref[...] = _cat_wgt6(wsel_ref[...], 0)

    x = x_ref[...]
    sel = _dot(_cat_act6(x, 1), w6_ref[...])
    cq = sel[:, :Q_RANK]
    cqn = cq * lax.rsqrt(jnp.mean(cq * cq, axis=-1, keepdims=True) + RMS_EPS) * qg_ref[...]
    qT_ref[0] = (_dot_nt(wqT_ref[...], cqn.astype(BF16)) * (AT_DH ** -0.5 * LOG2E)).astype(qT_ref.dtype)
    qiT_ref[0] = _dot_nt(_cat_wgt6(wqiT_ref[...], 1), _cat_act6(cqn, 1))
    ckv = _dot(x.astype(BF16), wckv_ref[...])
    ckvn = (ckv * lax.rsqrt(jnp.mean(ckv * ckv, axis=-1, keepdims=True) + RMS_EPS)
            * kvg_ref[...]).astype(BF16)
    kmat = _dot(ckvn, wk_ref[...])
    tm = kmat.shape[0]
    s_abs = pl.program_id(1) * tm + lax.broadcasted_iota(I32, (tm, LANES), 0)
    lane = lax.broadcasted_iota(I32, (tm, LANES), 1)
    pos = jnp.where(lane < 3, s_abs >> POS_SHIFT,
                    jnp.where(lane < 6, s_abs & (POS_RADIX - 1), 0)).astype(F32)
    k_ref[0] = jnp.concatenate(
        [blk for j in range(AT_HEADS // 2) for blk in (kmat[:, j * LANES:(j + 1) * LANES], pos)],
        axis=1).astype(k_ref.dtype)
    for kb in range(vT_ref.shape[1]):
        vT_ref[0, kb] = _dot_nt(wvT_ref[...], ckvn[kb * KBLK:(kb + 1) * KBLK]).astype(vT_ref.dtype)
    k128 = sel[:, Q_RANK:]
    in_k = lax.broadcasted_iota(I32, (1, LANES), 1) < IDX_DH
    mu = jnp.sum(jnp.where(in_k, k128, 0.0), axis=-1, keepdims=True) * (1.0 / IDX_DH)
    dev = jnp.where(in_k, k128 - mu, 0.0)
    var = jnp.sum(dev * dev, axis=-1, keepdims=True) * (1.0 / IDX_DH)
    y = dev * lax.rsqrt(var + LN_EPS) * lng_ref[...] + lnb_ref[...]
    h, m, l = (p.astype(F32) for p in _split3(y))
    up = lambda p: pltpu.roll(p, IDX_DH, 1)
    kidx_ref[0] = jnp.concatenate([h + up(h), m + up(h), l + up(m)], axis=1).astype(BF16)
    tT = k128.T
    wT_ref[0] = tT[IDX_DH:IDX_DH + SUBLANES, :] * (IDX_HEADS ** -0.5 * IDX_DH ** -0.5)


def _dsa_prep(x, w_sel, w_ckv, q_norm_g, kv_norm_g, idx_ln_g, idx_ln_b, wqT, wqiT, wk, wvT):
    b, s, d = x.shape
    tm = min(2 * KBLK, s)
    kb_per = tm // KBLK
    nb = s // KBLK
    full = lambda a: pl.BlockSpec(a.shape, lambda i, j: (0,) * a.ndim)
    pad_lanes = lambda v: jnp.pad(v.reshape(1, -1), ((0, 0), (0, LANES - v.shape[0])))
    args = (w_sel, w_ckv, q_norm_g.reshape(1, -1), kv_norm_g.reshape(1, -1),
            pad_lanes(idx_ln_g), pad_lanes(idx_ln_b), wqT, wqiT, wk, wvT)
    return pl.pallas_call(
        _dsa_prep_kernel,
        grid=(b, s // tm),
        in_specs=[pl.BlockSpec((None, tm, d), lambda i, j: (i, j, 0))] + [full(a) for a in args],
        out_specs=[pl.BlockSpec((1, AT_WIDTH, tm), lambda i, j: (i, 0, j)),
                   pl.BlockSpec((1, IDX_HEADS * IDX_DH, tm), lambda i, j: (i, 0, j)),
                   pl.BlockSpec((1, tm, K_W), lambda i, j: (i, j, 0)),
                   pl.BlockSpec((1, kb_per, AT_WIDTH, KBLK), lambda i, j: (i, j, 0, 0)),
                   pl.BlockSpec((1, tm, KIDX_W), lambda i, j: (i, j, 0)),
                   pl.BlockSpec((1, SUBLANES, tm), lambda i, j: (i, 0, j))],
        out_shape=[jax.ShapeDtypeStruct((b, AT_WIDTH, s), BF16),
                   jax.ShapeDtypeStruct((b, IDX_HEADS * IDX_DH, s), F32),
                   jax.ShapeDtypeStruct((b, s, K_W), BF16),
                   jax.ShapeDtypeStruct((b, nb, AT_WIDTH, KBLK), BF16),
                   jax.ShapeDtypeStruct((b, s, KIDX_W), BF16),
                   jax.ShapeDtypeStruct((b, SUBLANES, s), F32)],
        scratch_shapes=[pltpu.VMEM((6 * d, SEL_W), BF16)],
        compiler_params=_params(("arbitrary", "arbitrary")),
        name="dsa_prep",
    )(x, *args)


def _sortable(x):
    bits = lax.bitcast_convert_type(x, I32)
    return jnp.where(bits < 0, bits ^ jnp.int32(0x7FFFFFFF), bits)


def _dsa_attn_kernel(qT_ref, qiT_ref, wT_ref, kidx_ref, k_ref, vT_ref, o_ref,
                     keys_ref, q2_ref, qbd_ref, acc_ref, m_ref, l_ref, *, k_top):
    qb = pl.program_id(1)
    t0 = qb * QBLK
    e_max = t0 + QBLK
    nkb = (e_max + KBLK - 1) >> 8

    lane = lax.broadcasted_iota(I32, (1, QBLK), 1)
    t_idx = t0 + lane
    end_t = ((t_idx >> 6) + 1) * CHUNK
    srow = lax.broadcasted_iota(I32, (KBLK, QBLK), 0)

    for h in range(IDX_HEADS):
        q2_ref[:, h * QBLK:(h + 1) * QBLK] = _cat_wgt6(qiT_ref[0, h * IDX_DH:(h + 1) * IDX_DH, :], 0)
    wrow = jnp.concatenate([wT_ref[0, h:h + 1, :] for h in range(IDX_HEADS)], axis=1)

    def score_body(kb, carry):
        rows = pl.ds(pl.multiple_of(kb * KBLK, KBLK), KBLK)
        logits = _dot(kidx_ref[0, rows, :], q2_ref[...])
        r = jnp.maximum(logits, 0.0) * wrow
        sc = r[:, :QBLK]
        for h in range(1, IDX_HEADS):
            sc = sc + r[:, h * QBLK:(h + 1) * QBLK]
        valid = (srow + kb * KBLK) < end_t
        keys_ref[rows, :] = _sortable(jnp.where(valid, sc, -jnp.inf))
        return carry

    lax.fori_loop(0, nkb, score_body, 0)

    def count_ge(cand):
        def body(kb, acc):
            rows = pl.ds(pl.multiple_of(kb * KBLK, KBLK), KBLK)
            hit = jnp.where(keys_ref[rows, :] >= cand, 1, 0).astype(I32)
            return acc + jnp.sum(hit.reshape(KBLK // SUBLANES, SUBLANES, QBLK), axis=0)
        acc = lax.fori_loop(0, nkb, body, jnp.zeros((SUBLANES, QBLK), I32))
        return jnp.sum(acc, axis=0, keepdims=True)

    zero = jnp.zeros((1, QBLK), I32)
    thr = jnp.where(count_ge(zero) >= k_top, zero, jnp.full((1, QBLK), INT_MIN, I32))

    def bit_body(i, thr):
        cand = thr | (jnp.int32(1) << (30 - i))
        return jnp.where(count_ge(cand) >= k_top, cand, thr)

    thr = lax.fori_loop(0, 31, bit_body, thr)
    need = (k_top - (count_ge(thr + 1))).astype(F32)

    log2e = [p.astype(F32) for p in _split3(jnp.full((1, QBLK), LOG2E, F32))]
    ci16 = lax.broadcasted_iota(I32, (2 * SUBLANES, QBLK), 0)

    def pos_coef(h):
        slope = 2.0 ** (-8.0 * (h + 1) / AT_HEADS)
        blk = jnp.zeros((2 * SUBLANES, QBLK), F32)
        for i in range(3):
            blk = jnp.where(ci16 == i, log2e[i] * (POS_RADIX * slope), blk)
            blk = jnp.where(ci16 == 3 + i, log2e[i] * slope, blk)
        return blk.astype(BF16)

    for j in range(AT_HEADS // 2):
        pair = qT_ref[0, j * LANES:(j + 1) * LANES, :]
        rr = lax.broadcasted_iota(I32, pair.shape, 0)
        zeros = jnp.zeros_like(pair)
        qbd_ref[j, :LANES, :QBLK] = jnp.where(rr < AT_DH, pair, zeros)
        qbd_ref[j, :LANES, QBLK:] = jnp.where(rr >= AT_DH, pair, zeros)
        qbd_ref[j, LANES:LANES + 2 * SUBLANES, :QBLK] = pos_coef(2 * j)
        qbd_ref[j, LANES:LANES + 2 * SUBLANES, QBLK:] = pos_coef(2 * j + 1)
        qbd_ref[j, LANES + 2 * SUBLANES:, :] = jnp.zeros((KP_W - LANES - 2 * SUBLANES, 2 * QBLK), BF16)
    acc_ref[...] = jnp.zeros_like(acc_ref)
    m_ref[...] = jnp.full_like(m_ref, -jnp.inf)
    l_ref[...] = jnp.zeros_like(l_ref)

    ri = lax.broadcasted_iota(I32, (KBLK, KBLK), 0)
    ci = lax.broadcasted_iota(I32, (KBLK, KBLK), 1)
    lstrict = jnp.where(ci < ri, 1.0, 0.0).astype(BF16)

    def attn_body(kb, carry, last=False):
        rows = pl.ds(pl.multiple_of(kb * KBLK, KBLK), KBLK)
        kblk = keys_ref[rows, :]
        s_idx = srow + kb * KBLK
        eq = kblk == thr
        eqf = jnp.where(eq, 1.0, 0.0)
        before = _dot(lstrict, eqf.astype(BF16)) + carry
        sel = ((kblk > thr) | (eq & (before < need))) & (s_idx < end_t)
        bias = jnp.where(sel, 0.0, -jnp.inf)
        if last:
            ahead = jnp.maximum(s_idx - t_idx, 0).astype(F32)
        s2s = [_dot(k_ref[0, rows, j * KP_W:(j + 1) * KP_W], qbd_ref[j])
               for j in range(AT_HEADS // 2)]
        for j in range(AT_HEADS // 2):
            ps, alphas = [], []
            for half in range(2):
                h = 2 * j + half
                st = s2s[j][:, half * QBLK:(half + 1) * QBLK] + bias
                if last:
                    st = st - (2.0 * LOG2E * 2.0 ** (-8.0 * (h + 1) / AT_HEADS)) * ahead
                m_old = m_ref[h:h + 1, :]
                m_new = jnp.maximum(m_old, jnp.max(st, axis=0, keepdims=True))
                m_safe = jnp.where(m_new == -jnp.inf, 0.0, m_new)
                alpha = jnp.exp2(m_old - m_safe)
                p = jnp.exp2(st - m_safe)
                l_ref[h:h + 1, :] = alpha * l_ref[h:h + 1, :] + jnp.sum(p, axis=0, keepdims=True)
                m_ref[h:h + 1, :] = m_new
                ps.append(p.astype(BF16))
                alphas.append(alpha)
            o2 = _dot(vT_ref[0, kb, j * LANES:(j + 1) * LANES, :], jnp.concatenate(ps, axis=1))
            for half in range(2):
                hs = slice((2 * j + half) * AT_DH, (2 * j + half + 1) * AT_DH)
                acc_ref[hs, :] = (alphas[half] * acc_ref[hs, :]
                                  + o2[half * AT_DH:(half + 1) * AT_DH, half * QBLK:(half + 1) * QBLK])
        return carry + jnp.sum(eqf, axis=0, keepdims=True)

    ties_seen = lax.fori_loop(0, nkb - 1, attn_body, jnp.zeros((1, QBLK), F32))
    attn_body(nkb - 1, ties_seen, last=True)

    for h in range(AT_HEADS):
        hs = slice(h * AT_DH, (h + 1) * AT_DH)
        acc_ref[hs, :] = acc_ref[hs, :] / l_ref[h:h + 1, :]
    o_ref[0] = acc_ref[...].T.astype(o_ref.dtype)


def _dsa_attn(qT, qiT, wT, kidx, k, vT):
    b, _, s = qT.shape
    nb = s // KBLK
    k_top = min(TOPK_MAX, s // 4)
    return pl.pallas_call(
        functools.partial(_dsa_attn_kernel, k_top=k_top),
        grid=(b, s // QBLK),
        in_specs=[pl.BlockSpec((1, AT_WIDTH, QBLK), lambda i, j: (i, 0, j)),
                  pl.BlockSpec((1, IDX_HEADS * IDX_DH, QBLK), lambda i, j: (i, 0, j)),
                  pl.BlockSpec((1, SUBLANES, QBLK), lambda i, j: (i, 0, j)),
                  pl.BlockSpec((1, s, KIDX_W), lambda i, j: (i, 0, 0)),
                  pl.BlockSpec((1, s, K_W), lambda i, j: (i, 0, 0)),
                  pl.BlockSpec((1, nb, AT_WIDTH, KBLK), lambda i, j: (i, 0, 0, 0))],
        out_specs=pl.BlockSpec((1, QBLK, AT_WIDTH), lambda i, j: (i, j, 0)),
        out_shape=jax.ShapeDtypeStruct((b, s, AT_WIDTH), BF16),
        scratch_shapes=[pltpu.VMEM((s, QBLK), I32),
                        pltpu.VMEM((KIDX_W, IDX_HEADS * QBLK), BF16),
                        pltpu.VMEM((AT_HEADS // 2, KP_W, 2 * QBLK), BF16),
                        pltpu.VMEM((AT_WIDTH, QBLK), F32),
                        pltpu.VMEM((AT_HEADS, QBLK), F32),
                        pltpu.VMEM((AT_HEADS, QBLK), F32)],
        compiler_params=_params(("parallel", "arbitrary")),
        name="dsa_attn",
    )(qT, qiT, wT, kidx, k, vT)


def _layer_norm(v, g, b):
    mu = jnp.mean(v, axis=-1, keepdims=True)
    var = jnp.mean(jnp.square(v - mu), axis=-1, keepdims=True)
    return (v - mu) * lax.rsqrt(var + LN_EPS) * g + b


def _out_ln_kernel(oa_ref, ob_ref, x_ref, wa_ref, wb_ref, g_ref, b_ref, wrT_ref, bias_ref,
                   h_ref, hb_ref, eidx_ref, gT_ref, cnt_ref):
    mix = _dot(oa_ref[...], wa_ref[...]) + _dot(ob_ref[...], wb_ref[...])
    h = _layer_norm(ALPHA * x_ref[...] + mix, g_ref[...], b_ref[...])
    h_ref[...] = h
    hb_ref[...] = h.astype(BF16)
    eidx_ref[...], gT_ref[...], cnt_ref[0] = _route(h, wrT_ref[...], bias_ref[...])


def _out_ln(oa, ob, x2, wa, wb, g, b, wrT, bias_col, tm):
    t, d = x2.shape
    nt = t // tm
    row = lambda w: pl.BlockSpec((tm, w), lambda i: (i, 0))
    full = lambda a: pl.BlockSpec(a.shape, lambda i: (0, 0))
    return pl.pallas_call(
        _out_ln_kernel,
        grid=(nt,),
        in_specs=[row(oa.shape[1]), row(ob.shape[1]), row(d), full(wa), full(wb), full(g), full(b),
                  full(wrT), full(bias_col)],
        out_specs=[row(d), row(d),
                   pl.BlockSpec((TOP_K, tm), lambda i: (0, i)),
                   row(LANES),
                   pl.BlockSpec((1, N_EXPERTS, LANES), lambda i: (i, 0, 0))],
        out_shape=[jax.ShapeDtypeStruct((t, d), F32), jax.ShapeDtypeStruct((t, d), BF16),
                   jax.ShapeDtypeStruct((TOP_K, t), I32),
                   jax.ShapeDtypeStruct((t, LANES), F32),
                   jax.ShapeDtypeStruct((nt, N_EXPERTS, LANES), F32)],
        compiler_params=_params(("parallel",)),
        name="out_ln_route",
    )(oa, ob, x2, wa, wb, g, b, wrT, bias_col)


def _first_max(v, idx, big):
    m = jnp.max(v, axis=0, keepdims=True)
    first = jnp.min(jnp.where(v == m, idx, big), axis=0, keepdims=True)
    return m, first


def _route(h, wrT, bias_col):
    tm = h.shape[0]
    per_group = N_EXPERTS // N_GROUPS
    logits = _dot_nt(_cat_hlh(wrT, 1), _cat_hhl(h, 1))
    scores = _sigmoid(logits)
    sel = scores + bias_col
    iota_g = lax.broadcasted_iota(I32, (per_group, tm), 0)
    neg = -jnp.inf

    grp = jnp.zeros((N_GROUPS, tm), F32)
    iota_grp = lax.broadcasted_iota(I32, (N_GROUPS, tm), 0)
    for g in range(N_GROUPS):
        blk = sel[g * per_group:(g + 1) * per_group, :]
        m1, i1 = _first_max(blk, iota_g, per_group)
        m2 = jnp.max(jnp.where(iota_g == i1, neg, blk), axis=0, keepdims=True)
        grp = jnp.where(iota_grp == g, m1 + m2, grp)

    gmask = jnp.zeros((N_GROUPS, tm), F32)
    work = grp
    for _ in range(TOPK_GROUPS):
        _, gi = _first_max(work, iota_grp, N_GROUPS)
        hit = iota_grp == gi
        gmask = jnp.where(hit, 1.0, gmask)
        work = jnp.where(hit, neg, work)

    e_mask = jnp.concatenate(
        [jnp.broadcast_to(gmask[g:g + 1, :], (per_group, tm)) for g in range(N_GROUPS)], axis=0)
    iota_e = lax.broadcasted_iota(I32, (N_EXPERTS, tm), 0)
    work = jnp.where(e_mask > 0.5, sel, neg)
    iota_k = lax.broadcasted_iota(I32, (TOP_K, tm), 0)
    eidx = jnp.zeros((TOP_K, tm), I32)
    w = jnp.zeros((N_EXPERTS, tm), F32)
    chosen = jnp.zeros((N_EXPERTS, tm), F32)
    for k in range(TOP_K):
        _, ei = _first_max(work, iota_e, N_EXPERTS)
        hit = iota_e == ei
        eidx = jnp.where(iota_k == k, ei, eidx)
        w = jnp.where(hit, scores, w)
        chosen = jnp.where(hit, 1.0, chosen)
        work = jnp.where(hit, neg, work)

    gates = w / jnp.sum(w, axis=0, keepdims=True) * ROUTED_SCALE
    gates = jnp.concatenate([gates, jnp.zeros((LANES - N_EXPERTS, tm), F32)], axis=0)
    cnt = jnp.broadcast_to(jnp.sum(chosen, axis=1, keepdims=True), (N_EXPERTS, LANES))
    return eidx, gates.T, cnt


MOE_TM = 512
SEG = 16
EXP_TM = 768
ROW_BUF = TOP_K * MOE_TM + N_EXPERTS * SEG
CH_MAX = ROW_BUF // SEG
P_BLK = 512
G_BLK = 512


def _moe_plan(cnt, n_exp_tiles):
    nt = cnt.shape[0]
    n16 = (cnt + (SEG - 1)) // SEG
    so16 = jnp.cumsum(n16, axis=1) - n16
    r16 = jnp.sum(n16, axis=0)
    per = EXP_TM // SEG
    rp16 = (r16 + (per - 1)) // per * per
    ends = jnp.cumsum(rp16)
    off16 = ends - rp16
    go16 = off16[None, :] + jnp.cumsum(n16, axis=0) - n16
    n_et = (ends[-1] // per).astype(I32).reshape(1)
    tiles = jnp.arange(n_exp_tiles, dtype=I32)
    tile_expert = jnp.sum((ends // per)[None, :] <= tiles[:, None], axis=1)
    tile_expert = jnp.minimum(tile_expert, N_EXPERTS - 1).astype(I32)
    c = jnp.arange(CH_MAX, dtype=I32)[None, :, None]
    lo, hi = so16[:, None, :], (so16 + n16)[:, None, :]
    inside = (lo <= c) & (c < hi)
    dst16 = jnp.sum(jnp.where(inside, go16[:, None, :] + c - lo, 0), axis=2)
    dst16 = dst16.reshape(nt * CH_MAX).astype(I32)
    twice = lambda a: jnp.concatenate([a, a], axis=1).astype(F32)
    seg_lo, seg_hi = twice(so16 * SEG), twice((so16 + n16) * SEG)
    as_rows = lambda a: jnp.broadcast_to(a[:, None, :], (nt, SUBLANES, 2 * N_EXPERTS))
    as_cols = lambda a: jnp.broadcast_to(a[:, :, None], (nt, 2 * N_EXPERTS, LANES))
    return dict(dst16=dst16,
                rtot16=jnp.sum(n16, axis=1).astype(I32), r16=r16.astype(I32), rp16=rp16.astype(I32),
                off16=off16.astype(I32), n_et=n_et, tile_expert=tile_expert,
                lo_rows=as_rows(seg_lo), hi_rows=as_rows(seg_hi),
                lo_cols=as_cols(seg_lo), hi_cols=as_cols(seg_hi))


def _chunk(ref, c):
    return ref.at[pl.ds(pl.multiple_of(c * SEG, SEG), SEG)]


def _segment_copies(i, dst16_ref, rtot_ref, make_copy):
    n = rtot_ref[i]

    def chunk_body(c, carry):
        make_copy(c, dst16_ref[i * CH_MAX + c]).start()
        return carry

    lax.fori_loop(0, n, chunk_body, 0)
    return n


def _wait_copies(count, make_copy):
    def body(c, carry):
        make_copy(0, 0).wait()
        return carry

    lax.fori_loop(0, count, body, 0)


POS_SPLIT = 64


def _dispatch_kernel(dst16_ref, rtot_ref, r16_ref, rp16_ref, off16_ref,
                     x_ref, eidx_ref, locol_ref, lorow_ref, hirow_ref,
                     xs_hbm, posT_ref, buf_ref, zero_ref, pend_ref, sem):
    i = pl.program_id(0)
    slot = lax.rem(i, 2)
    buf = buf_ref.at[slot]
    tm = x_ref.shape[0]
    eidx = eidx_ref[...]
    iota_e = lax.broadcasted_iota(I32, (N_EXPERTS, tm), 0)
    onehot = jnp.zeros((N_EXPERTS, tm), F32)
    for k in range(TOP_K):
        onehot = jnp.where(iota_e == eidx[k:k + 1, :], 1.0, onehot)
    earlier = lax.broadcasted_iota(I32, (tm, tm), 0) < lax.broadcasted_iota(I32, (tm, tm), 1)
    rank = _dot(onehot.astype(BF16), jnp.where(earlier, 1.0, 0.0).astype(BF16))
    pos1 = jnp.where(onehot > 0.5, locol_ref[0][:N_EXPERTS, 0:1] + rank + 1.0, 0.0)
    pos_hi = jnp.floor(pos1 * (1.0 / POS_SPLIT)) * POS_SPLIT
    pos2 = jnp.concatenate([pos_hi, pos1 - pos_hi], axis=0)
    posT_ref[...] = pos2.T
    pos2b = pos2.astype(BF16)

    x = x_ref[...]
    seg_lo = lorow_ref[0][0:1, :]
    seg_hi = hirow_ref[0][0:1, :]
    n_blk = (rtot_ref[i] * SEG + (P_BLK - 1)) // P_BLK

    def blk_body(rb, carry):
        r0 = pl.multiple_of(rb * P_BLK, P_BLK)
        r_e = (lax.broadcasted_iota(I32, (P_BLK, 2 * N_EXPERTS), 0) + r0).astype(F32)
        owner = jnp.where((r_e >= seg_lo) & (r_e < seg_hi), 1.0, 0.0).astype(BF16)
        want = _dot(owner, pos2b)
        r_t = (lax.broadcasted_iota(I32, (P_BLK, tm), 0) + (r0 + 1)).astype(F32)
        p = jnp.where(want == r_t, 1.0, 0.0).astype(BF16)
        buf[pl.ds(r0, P_BLK), :] = _dot(p, x).astype(BF16)
        return carry

    lax.fori_loop(0, n_blk, blk_body, 0)

    @pl.when(i > 0)
    def _():
        prev_copy = lambda src, dst: pltpu.make_async_copy(
            _chunk(buf_ref.at[1 - slot], src), _chunk(xs_hbm, dst), sem.at[1 - slot])
        _wait_copies(pend_ref[0], prev_copy)

    out_copy = lambda src, dst: pltpu.make_async_copy(_chunk(buf, src), _chunk(xs_hbm, dst),
                                                      sem.at[slot])
    pend_ref[0] = _segment_copies(i, dst16_ref, rtot_ref, out_copy)

    @pl.when(i == pl.num_programs(0) - 1)
    def _():
        _wait_copies(pend_ref[0], out_copy)
        zero_ref[...] = jnp.zeros_like(zero_ref)
        zero_copy = lambda src, dst: pltpu.make_async_copy(_chunk(zero_ref, 0), _chunk(xs_hbm, dst),
                                                           sem.at[slot])

        def zero_chunk(c, carry):
            zero_copy(0, c).start()
            return carry

        def tail_body(e, total):
            lax.fori_loop(off16_ref[e] + r16_ref[e], off16_ref[e] + rp16_ref[e], zero_chunk, 0)
            return total + rp16_ref[e] - r16_ref[e]

        _wait_copies(lax.fori_loop(0, N_EXPERTS, tail_body, 0), zero_copy)

        per = EXP_TM // SEG
        first = (off16_ref[N_EXPERTS - 1] + rp16_ref[N_EXPERTS - 1]) // per
        n_tiles = xs_hbm.shape[0] // EXP_TM
        tile_copy = lambda tile: pltpu.make_async_copy(
            zero_ref, xs_hbm.at[pl.ds(pl.multiple_of(tile * EXP_TM, EXP_TM), EXP_TM)], sem.at[slot])

        def zero_tile(tile, carry):
            tile_copy(tile).start()
            return carry

        def wait_tile(tile, carry):
            tile_copy(0).wait()
            return carry

        lax.fori_loop(first, n_tiles, zero_tile, 0)
        lax.fori_loop(first, n_tiles, wait_tile, 0)


def _dispatch(hb, eidx, plan, n_rows):
    t, d = hb.shape
    nt = t // MOE_TM
    grid_spec = pltpu.PrefetchScalarGridSpec(
        num_scalar_prefetch=5,
        grid=(nt,),
        in_specs=[pl.BlockSpec((MOE_TM, d), lambda i, *_: (i, 0)),
                  pl.BlockSpec((TOP_K, MOE_TM), lambda i, *_: (0, i)),
                  pl.BlockSpec((1, 2 * N_EXPERTS, LANES), lambda i, *_: (i, 0, 0)),
                  pl.BlockSpec((1, SUBLANES, 2 * N_EXPERTS), lambda i, *_: (i, 0, 0)),
                  pl.BlockSpec((1, SUBLANES, 2 * N_EXPERTS), lambda i, *_: (i, 0, 0))],
        out_specs=[pl.BlockSpec(memory_space=pl.ANY),
                   pl.BlockSpec((MOE_TM, LANES), lambda i, *_: (i, 0))],
        scratch_shapes=[pltpu.VMEM((2, ROW_BUF, d), BF16), pltpu.VMEM((EXP_TM, d), BF16),
                        pltpu.SMEM((1,), I32), pltpu.SemaphoreType.DMA((2,))])
    return pl.pallas_call(
        _dispatch_kernel,
        grid_spec=grid_spec,
        out_shape=[jax.ShapeDtypeStruct((n_rows, d), BF16),
                   jax.ShapeDtypeStruct((t, LANES), F32)],
        compiler_params=_params(("arbitrary",)),
        name="moe_dispatch",
    )(plan["dst16"], plan["rtot16"], plan["r16"], plan["rp16"],
      plan["off16"], hb, eidx, plan["lo_cols"], plan["lo_rows"], plan["hi_rows"])


def _experts_kernel(te_ref, x_ref, wg_ref, wu_ref, wd_ref, y_ref, wgb_ref, wub_ref, wdb_ref):
    j = pl.program_id(0)

    @pl.when((j == 0) | (te_ref[j] != te_ref[jnp.maximum(j - 1, 0)]))
    def _():
        wgb_ref[...] = wg_ref[0].astype(BF16)
        wub_ref[...] = wu_ref[0].astype(BF16)
        wdb_ref[...] = wd_ref[0].astype(BF16)

    x = x_ref[...]
    hmid = _silu(_dot(x, wgb_ref[...])) * _dot(x, wub_ref[...])
    y_ref[...] = _dot(hmid.astype(BF16), wdb_ref[...]).astype(y_ref.dtype)


def _experts(xs, wg, wu, wd, plan):
    n_rows, d = xs.shape
    row_map = lambda j, te: (j, 0)
    w_map = lambda j, te: (te[j], 0, 0)
    grid_spec = pltpu.PrefetchScalarGridSpec(
        num_scalar_prefetch=1,
        grid=(plan["n_et"][0],),
        in_specs=[pl.BlockSpec((EXP_TM, d), row_map),
                  pl.BlockSpec((1, d, D_EXPERT), w_map),
                  pl.BlockSpec((1, d, D_EXPERT), w_map),
                  pl.BlockSpec((1, D_EXPERT, d), w_map)],
        out_specs=pl.BlockSpec((EXP_TM, d), row_map),
        scratch_shapes=[pltpu.VMEM((d, D_EXPERT), BF16), pltpu.VMEM((d, D_EXPERT), BF16),
                        pltpu.VMEM((D_EXPERT, d), BF16)])
    return pl.pallas_call(
        _experts_kernel,
        grid_spec=grid_spec,
        out_shape=jax.ShapeDtypeStruct((n_rows, d), BF16),
        input_output_aliases={1: 0},
        compiler_params=_params(("arbitrary",)),
        name="moe_experts",
    )(plan["tile_expert"], xs, wg, wu, wd)


def _combine_kernel(dst16_ref, rtot_ref,
                    ys_hbm, posT_ref, gT_ref, locol_ref, hicol_ref, h_ref,
                    sg_ref, su_ref, sd_ref, lg_ref, lb_ref, o_ref, ybuf_ref, acc_ref, pend_ref, sem):
    i = pl.program_id(0)
    slot = lax.rem(i, 2)
    ybuf = ybuf_ref.at[slot]
    tm = h_ref.shape[0]

    def fetch(tile, to_slot):
        copy = lambda dst, src: pltpu.make_async_copy(
            _chunk(ys_hbm, src), _chunk(ybuf_ref.at[to_slot], dst), sem.at[to_slot])
        pend_ref[to_slot] = _segment_copies(tile, dst16_ref, rtot_ref, copy)

    @pl.when(i == 0)
    def _():
        ybuf_ref[...] = jnp.zeros_like(ybuf_ref)
        fetch(0, 0)

    @pl.when(i + 1 < pl.num_programs(0))
    def _():
        fetch(i + 1, 1 - slot)

    h = h_ref[...]
    xb = h.astype(BF16)
    hs = _silu(_dot(xb, sg_ref[...])) * _dot(xb, su_ref[...])
    acc_ref[...] = ALPHA * h + _dot(hs.astype(BF16), sd_ref[...])
    _wait_copies(pend_ref[slot], lambda dst, src: pltpu.make_async_copy(
        _chunk(ys_hbm, src), _chunk(ybuf, dst), sem.at[slot]))

    pos2b = posT_ref[...].astype(BF16)
    g_hi, g_lo = _split(gT_ref[...])
    gate2b = (g_hi.astype(F32) + pltpu.roll(g_lo.astype(F32), N_EXPERTS, 1)).astype(BF16)
    seg_lo = jnp.broadcast_to(locol_ref[0][:, 0:1], (2 * N_EXPERTS, G_BLK))
    seg_hi = jnp.broadcast_to(hicol_ref[0][:, 0:1], (2 * N_EXPERTS, G_BLK))
    n_blk = (rtot_ref[i] * SEG + (G_BLK - 1)) // G_BLK

    def blk_body(cb, carry):
        c0 = pl.multiple_of(cb * G_BLK, G_BLK)
        c_e = (lax.broadcasted_iota(I32, (2 * N_EXPERTS, G_BLK), 1) + c0).astype(F32)
        owner = jnp.where((c_e >= seg_lo) & (c_e < seg_hi), 1.0, 0.0).astype(BF16)
        want = _dot(pos2b, owner)
        gate = _dot(gate2b, owner)
        c_t = (lax.broadcasted_iota(I32, (tm, G_BLK), 1) + (c0 + 1)).astype(F32)
        g = jnp.where(want == c_t, gate, 0.0).astype(BF16)
        acc_ref[...] += _dot(g, ybuf[pl.ds(c0, G_BLK), :])
        return carry

    lax.fori_loop(0, n_blk, blk_body, 0)
    o_ref[...] = _layer_norm(acc_ref[...], lg_ref[...], lb_ref[...])


def _combine(ys, posT, gT, h1, sg, su, sd, lg, lb, plan):
    t, d = h1.shape
    nt = t // MOE_TM
    row = lambda w: pl.BlockSpec((MOE_TM, w), lambda i, *_: (i, 0))
    full = lambda a: pl.BlockSpec(a.shape, lambda i, *_: (0, 0))
    seg = pl.BlockSpec((1, 2 * N_EXPERTS, LANES), lambda i, *_: (i, 0, 0))
    grid_spec = pltpu.PrefetchScalarGridSpec(
        num_scalar_prefetch=2,
        grid=(nt,),
        in_specs=[pl.BlockSpec(memory_space=pl.ANY), row(LANES), row(LANES), seg, seg, row(d),
                  full(sg), full(su), full(sd), full(lg), full(lb)],
        out_specs=row(d),
        scratch_shapes=[pltpu.VMEM((2, ROW_BUF, d), BF16), pltpu.VMEM((MOE_TM, d), F32),
                        pltpu.SMEM((2,), I32), pltpu.SemaphoreType.DMA((2,))])
    return pl.pallas_call(
        _combine_kernel,
        grid_spec=grid_spec,
        out_shape=jax.ShapeDtypeStruct((t, d), F32),
        compiler_params=_params(("arbitrary",)),
        name="moe_combine",
    )(plan["dst16"], plan["rtot16"], ys, posT, gT,
      plan["lo_cols"], plan["hi_cols"], h1, sg, su, sd, lg, lb)


def _moe(h1, hb, eidx, gT, cnt, wg, wu, wd, sg, su, sd, lg, lb):
    t = h1.shape[0]
    nt = t // MOE_TM
    max_rows = nt * ROW_BUF + N_EXPERTS * EXP_TM
    n_exp_tiles = pl.cdiv(max_rows, EXP_TM)
    plan = _moe_plan(cnt[:, :, 0].astype(I32), n_exp_tiles)
    xs, posT = _dispatch(hb, eidx, plan, n_exp_tiles * EXP_TM)
    ys = _experts(xs, wg, wu, wd, plan)
    return _combine(ys, posT, gT, h1, sg, su, sd, lg, lb, plan)


def kernel(x, w_in, hg_lb_logits, hg_norm_g, q_norm_g, w_q_up, w_qidx_up, kv_norm_g, w_kv_up,
           idx_ln_g, idx_ln_b, w_out, ln1_g, ln1_b, w_router, router_bias, w_e_gate, w_e_up,
           w_e_down, w_s_gate, w_s_up, w_s_down, ln2_g, ln2_b):
    b, s, d = x.shape
    h = x
    for l in range(DEPTH):
        h2 = h.reshape(b * s, d)
        n_main = 4 * HG_WIDTH
        w_main = w_in[l, :, :n_main].astype(BF16)
        c_kv, c_idx = n_main + Q_RANK, n_main + Q_RANK + KV_RANK
        w_sel = jnp.concatenate([w_in[l, :, n_main:c_kv], w_in[l, :, c_idx:]], axis=1)
        w_sel = jnp.pad(w_sel, ((0, 0), (0, SEL_W - w_sel.shape[1])))
        w_ckv = w_in[l, :, c_kv:c_idx].astype(BF16)

        o_a = _hgrn(h, w_main, hg_lb_logits, hg_norm_g[l], l)

        wqT = w_q_up[l].T.astype(BF16)
        wqiT = w_qidx_up[l].T
        wk = w_kv_up[l][:, :AT_WIDTH].astype(BF16)
        wvT = w_kv_up[l][:, AT_WIDTH:].T.astype(BF16)
        qT, qiT, k, vT, kidx, wT = _dsa_prep(h, w_sel, w_ckv, q_norm_g[l], kv_norm_g[l],
                                             idx_ln_g[l], idx_ln_b[l], wqT, wqiT, wk, wvT)
        o_b = _dsa_attn(qT, qiT, wT, kidx, k, vT)

        h1, hb, eidx, gT, cnt = _out_ln(
            o_a.reshape(b * s, HG_WIDTH), o_b.reshape(b * s, AT_WIDTH), h2,
            w_out[l, :HG_WIDTH].astype(BF16), w_out[l, HG_WIDTH:].astype(BF16),
            ln1_g[l].reshape(1, d), ln1_b[l].reshape(1, d),
            w_router[l].T, router_bias[l].reshape(N_EXPERTS, 1), MOE_TM)
        out = _moe(h1, hb, eidx, gT, cnt, w_e_gate[l], w_e_up[l], w_e_down[l],
                   w_s_gate[l].astype(BF16), w_s_up[l].astype(BF16), w_s_down[l].astype(BF16),
                   ln2_g[l].reshape(1, d), ln2_b[l].reshape(1, d))
        h = out.reshape(b, s, d)
    return h
```

```python
import functools

import jax
import jax.numpy as jnp
from jax import lax
from jax.experimental import pallas as pl
from jax.experimental.pallas import tpu as pltpu

F32 = jnp.float32
BF16 = jnp.bfloat16
I32 = jnp.int32
HIGHEST = lax.Precision.HIGHEST

CHUNK = 64
HG_HEADS = 4
HG_DK = 128
HG_DV = 128
HG_WIDTH = HG_HEADS * HG_DV
AT_HEADS = 8
AT_DH = 64
AT_WIDTH = AT_HEADS * AT_DH
Q_RANK = 256
KV_RANK = 128
IDX_HEADS = 4
IDX_DH = 64
TOPK_MAX = 256
N_EXPERTS = 64
TOP_K = 8
N_GROUPS = 8
TOPK_GROUPS = 4
D_EXPERT = 256
ROUTED_SCALE = 2.5
DEPTH = 1
ALPHA = (2.0 * DEPTH) ** 0.25
LN_EPS = 1e-5
RMS_EPS = 1e-6
LOG2E = 1.4426950408889634

LANES = 128
SUBLANES = 8
QBLK = 4 * CHUNK
KBLK = 256
VMEM_LIMIT = 56 * 1024 * 1024
INT_MIN = -2 ** 31


def _params(sem, vmem=VMEM_LIMIT):
    return pltpu.CompilerParams(dimension_semantics=sem, vmem_limit_bytes=vmem)


def _dot(a, b, precision=None):
    return jnp.dot(a, b, preferred_element_type=F32, precision=precision)


def _dot_nt(a, b, precision=None):
    return lax.dot_general(a, b, (((1,), (1,)), ((), ())), preferred_element_type=F32,
                           precision=precision)


def _dot_tn(a, b, precision=None):
    return lax.dot_general(a, b, (((0,), (0,)), ((), ())), preferred_element_type=F32,
                           precision=precision)


def _split(x):
    hi = x.astype(BF16)
    return hi, (x - hi.astype(F32)).astype(BF16)


def _cat_hhl(x, axis):
    hi, lo = _split(x)
    return jnp.concatenate([hi, hi, lo], axis=axis)


def _cat_hlh(x, axis):
    hi, lo = _split(x)
    return jnp.concatenate([hi, lo, hi], axis=axis)


def _split3(x):
    hi = x.astype(BF16)
    r = x - hi.astype(F32)
    mid = r.astype(BF16)
    return hi, mid, (r - mid.astype(F32)).astype(BF16)


def _cat_act6(x, axis):
    h, m, l = _split3(x)
    return jnp.concatenate([h, h, m, h, l, m], axis=axis)


def _cat_wgt6(x, axis):
    h, m, l = _split3(x)
    return jnp.concatenate([h, m, h, l, h, m], axis=axis)


def _sigmoid(x):
    return 1.0 / (1.0 + jnp.exp(-x))


def _silu(x):
    return x * _sigmoid(x)


def _hgrn_kernel(x_ref, w_ref, lbl_ref, ng_ref, o_ref, p_ref, st_ref, *, n_chunks, layer):
    @pl.when(pl.program_id(1) == 0)
    def _():
        st_ref[...] = jnp.zeros_like(st_ref)

    for sq in range(x_ref.shape[0]):
        p_ref[sq] = _dot(x_ref[sq].astype(BF16), w_ref[...])

    lg = lbl_ref[...]
    ex = jnp.exp(lg - jnp.max(lg, axis=0, keepdims=True))
    lb_all = jnp.sum(ex[: layer + 1], axis=0, keepdims=True) / jnp.sum(ex, axis=0, keepdims=True)
    ng = ng_ref[...]

    r = lax.broadcasted_iota(I32, (CHUNK, CHUNK), 0)
    c = lax.broadcasted_iota(I32, (CHUNK, CHUNK), 1)
    causal = c <= r
    tri = jnp.where(causal, 1.0, 0.0).astype(BF16)
    tri2 = jnp.concatenate([tri, tri], axis=1)

    units = [(sq, h) for sq in range(p_ref.shape[0]) for h in range(HG_HEADS)]

    def chunk_body(j, carry):
        rows = pl.ds(pl.multiple_of(j * CHUNK, CHUNK), CHUNK)
        ks, bs = [], []
        for sq, h in units:
            lo = h * HG_DK
            lb = lb_all[:, lo:lo + HG_DK]
            f = lb + (1.0 - lb) * _sigmoid(p_ref[sq, rows, HG_WIDTH + lo:HG_WIDTH + lo + HG_DK])
            lf_hi, lf_lo = _split(jnp.log(f))
            ks.append(1.0 - f)
            bs.append(_dot(tri2, jnp.concatenate([lf_hi, lf_lo], axis=0)))
        scs, ois, vbs = [], [], []
        for (sq, h), k, b in zip(units, ks, bs):
            lo = h * HG_DK
            b_last = b[CHUNK - 1:CHUNK, :]
            q_dec = (_silu(p_ref[sq, rows, lo:lo + HG_DK]) * jnp.exp(b)).astype(BF16)
            k_inv = (k * jnp.exp(-b)).astype(BF16)
            k_dec = (k * jnp.exp(b_last - b)).astype(BF16)
            vb = p_ref[sq, rows, 2 * HG_WIDTH + lo:2 * HG_WIDTH + lo + HG_DV].astype(BF16)
            st = st_ref[sq, h]
            scs.append(jnp.where(causal, _dot_nt(q_dec, k_inv), 0.0).astype(BF16))
            ois.append(_dot_nt(q_dec, st.astype(BF16)))
            st_ref[sq, h] = st * jnp.exp(b_last) + _dot_tn(vb, k_dec)
            vbs.append(vb)
        for (sq, h), sc, oi, vb in zip(units, scs, ois, vbs):
            lo = h * HG_DK
            o = _dot(sc, vb) + oi
            o = o * lax.rsqrt(jnp.mean(o * o, axis=-1, keepdims=True) + RMS_EPS) * ng
            gate = p_ref[sq, rows, 3 * HG_WIDTH + lo:3 * HG_WIDTH + lo + HG_DV]
            o_ref[sq, rows, lo:lo + HG_DV] = (o * _silu(gate)).astype(o_ref.dtype)
        return carry

    lax.fori_loop(0, n_chunks, chunk_body, 0)


def _hgrn(x, w_main, lb_logits, norm_g, layer, ct=512, n_seq=2):
    b, s, d = x.shape
    w = w_main.shape[1]
    n_seq = min(n_seq, b)
    ct = min(ct, s)
    n_chunks = ct // CHUNK
    return pl.pallas_call(
        functools.partial(_hgrn_kernel, n_chunks=n_chunks, layer=layer),
        grid=(b // n_seq, s // ct),
        in_specs=[pl.BlockSpec((n_seq, ct, d), lambda i, j: (i, j, 0)),
                  pl.BlockSpec(w_main.shape, lambda i, j: (0, 0)),
                  pl.BlockSpec(lb_logits.shape, lambda i, j: (0, 0)),
                  pl.BlockSpec((1, HG_DV), lambda i, j: (0, 0))],
        out_specs=pl.BlockSpec((n_seq, ct, HG_WIDTH), lambda i, j: (i, j, 0)),
        out_shape=jax.ShapeDtypeStruct((b, s, HG_WIDTH), BF16),
        scratch_shapes=[pltpu.VMEM((n_seq, ct, w), F32),
                        pltpu.VMEM((n_seq, HG_HEADS, HG_DV, HG_DK), F32)],
        compiler_params=_params(("parallel", "arbitrary")),
        name="hgrn",
    )(x, w_main, lb_logits, norm_g.reshape(1, HG_DV))


POS_SHIFT = 4
POS_RADIX = 1 << POS_SHIFT
KP_W = 2 * LANES
K_W = (AT_HEADS // 2) * KP_W
SEL_W = Q_RANK + LANES
KIDX_W = 6 * IDX_DH


def _dsa_prep_kernel(x_ref, wsel_ref, wckv_ref, qg_ref, kvg_ref, lng_ref, lnb_ref, wqT_ref, wqiT_ref,
                     wk_ref, wvT_ref, qT_ref, qiT_ref, k_ref, vT_ref, kidx_ref, wT_ref, w6_ref):
    @pl.when((pl.program_id(0) == 0) & (pl.program_id(1) == 0))
    def _():
        w6_ref[...] = _cat_wgt6(wsel_ref[...], 0)

    x = x_ref[...]
    sel = _dot(_cat_act6(x, 1), w6_ref[...])
    cq = sel[:, :Q_RANK]
    cqn = cq * lax.rsqrt(jnp.mean(cq * cq, axis=-1, keepdims=True) + RMS_EPS) * qg_ref[...]
    qT_ref[0] = (_dot_nt(wqT_ref[...], cqn.astype(BF16)) * (AT_DH ** -0.5 * LOG2E)).astype(qT_ref.dtype)
    qiT_ref[0] = _dot_nt(_cat_wgt6(wqiT_ref[...], 1), _cat_act6(cqn, 1))
    ckv = _dot(x.astype(BF16), wckv_ref[...])
    ckvn = (ckv * lax.rsqrt(jnp.mean(ckv * ckv, axis=-1, keepdims=True) + RMS_EPS)
            * kvg_ref[...]).astype(BF16)
    kmat = _dot(ckvn, wk_ref[...])
    tm = kmat.shape[0]
    s_abs = pl.program_id(1) * tm + lax.broadcasted_iota(I32, (tm, LANES), 0)
    lane = lax.broadcasted_iota(I32, (tm, LANES), 1)
    pos = jnp.where(lane < 3, s_abs >> POS_SHIFT,
                    jnp.where(lane < 6, s_abs & (POS_RADIX - 1), 0)).astype(F32)
    k_ref[0] = jnp.concatenate(
        [blk for j in range(AT_HEADS // 2) for blk in (kmat[:, j * LANES:(j + 1) * LANES], pos)],
        axis=1).astype(k_ref.dtype)
    for kb in range(vT_ref.shape[1]):
        vT_ref[0, kb] = _dot_nt(wvT_ref[...], ckvn[kb * KBLK:(kb + 1) * KBLK]).astype(vT_ref.dtype)
    k128 = sel[:, Q_RANK:]
    in_k = lax.broadcasted_iota(I32, (1, LANES), 1) < IDX_DH
    mu = jnp.sum(jnp.where(in_k, k128, 0.0), axis=-1, keepdims=True) * (1.0 / IDX_DH)
    dev = jnp.where(in_k, k128 - mu, 0.0)
    var = jnp.sum(dev * dev, axis=-1, keepdims=True) * (1.0 / IDX_DH)
    y = dev * lax.rsqrt(var + LN_EPS) * lng_ref[...] + lnb_ref[...]
    h, m, l = (p.astype(F32) for p in _split3(y))
    up = lambda p: pltpu.roll(p, IDX_DH, 1)
    kidx_ref[0] = jnp.concatenate([h + up(h), m + up(h), l + up(m)], axis=1).astype(BF16)
    tT = k128.T
    wT_ref[0] = tT[IDX_DH:IDX_DH + SUBLANES, :] * (IDX_HEADS ** -0.5 * IDX_DH ** -0.5)


def _dsa_prep(x, w_sel, w_ckv, q_norm_g, kv_norm_g, idx_ln_g, idx_ln_b, wqT, wqiT, wk, wvT):
    b, s, d = x.shape
    tm = min(2 * KBLK, s)
    kb_per = tm // KBLK
    nb = s // KBLK
    full = lambda a: pl.BlockSpec(a.shape, lambda i, j: (0,) * a.ndim)
    pad_lanes = lambda v: jnp.pad(v.reshape(1, -1), ((0, 0), (0, LANES - v.shape[0])))
    args = (w_sel, w_ckv, q_norm_g.reshape(1, -1), kv_norm_g.reshape(1, -1),
            pad_lanes(idx_ln_g), pad_lanes(idx_ln_b), wqT, wqiT, wk, wvT)
    return pl.pallas_call(
        _dsa_prep_kernel,
        grid=(b, s // tm),
        in_specs=[pl.BlockSpec((None, tm, d), lambda i, j: (i, j, 0))] + [full(a) for a in args],
        out_specs=[pl.BlockSpec((1, AT_WIDTH, tm), lambda i, j: (i, 0, j)),
                   pl.BlockSpec((1, IDX_HEADS * IDX_DH, tm), lambda i, j: (i, 0, j)),
                   pl.BlockSpec((1, tm, K_W), lambda i, j: (i, j, 0)),
                   pl.BlockSpec((1, kb_per, AT_WIDTH, KBLK), lambda i, j: (i, j, 0, 0)),
                   pl.BlockSpec((1, tm, KIDX_W), lambda i, j: (i, j, 0)),
                   pl.BlockSpec((1, SUBLANES, tm), lambda i, j: (i, 0, j))],
        out_shape=[jax.ShapeDtypeStruct((b, AT_WIDTH, s), BF16),
                   jax.ShapeDtypeStruct((b, IDX_HEADS * IDX_DH, s), F32),
                   jax.ShapeDtypeStruct((b, s, K_W), BF16),
                   jax.ShapeDtypeStruct((b, nb, AT_WIDTH, KBLK), BF16),
                   jax.ShapeDtypeStruct((b, s, KIDX_W), BF16),
                   jax.ShapeDtypeStruct((b, SUBLANES, s), F32)],
        scratch_shapes=[pltpu.VMEM((6 * d, SEL_W), BF16)],
        compiler_params=_params(("arbitrary", "arbitrary")),
        name="dsa_prep",
    )(x, *args)


def _sortable(x):
    bits = lax.bitcast_convert_type(x, I32)
    return jnp.where(bits < 0, bits ^ jnp.int32(0x7FFFFFFF), bits)


def _dsa_attn_kernel(qT_ref, qiT_ref, wT_ref, kidx_ref, k_ref, vT_ref, o_ref,
                     keys_ref, q2_ref, qbd_ref, acc_ref, m_ref, l_ref, *, k_top):
    qb = pl.program_id(1)
    t0 = qb * QBLK
    e_max = t0 + QBLK
    nkb = (e_max + KBLK - 1) >> 8

    lane = lax.broadcasted_iota(I32, (1, QBLK), 1)
    t_idx = t0 + lane
    end_t = ((t_idx >> 6) + 1) * CHUNK
    srow = lax.broadcasted_iota(I32, (KBLK, QBLK), 0)

    for h in range(IDX_HEADS):
        q2_ref[:, h * QBLK:(h + 1) * QBLK] = _cat_wgt6(qiT_ref[0, h * IDX_DH:(h + 1) * IDX_DH, :], 0)
    wrow = jnp.concatenate([wT_ref[0, h:h + 1, :] for h in range(IDX_HEADS)], axis=1)

    def score_body(kb, carry):
        rows = pl.ds(pl.multiple_of(kb * KBLK, KBLK), KBLK)
        logits = _dot(kidx_ref[0, rows, :], q2_ref[...])
        r = jnp.maximum(logits, 0.0) * wrow
        sc = r[:, :QBLK]
        for h in range(1, IDX_HEADS):
            sc = sc + r[:, h * QBLK:(h + 1) * QBLK]
        valid = (srow + kb * KBLK) < end_t
        keys_ref[rows, :] = _sortable(jnp.where(valid, sc, -jnp.inf))
        return carry

    lax.fori_loop(0, nkb, score_body, 0)

    def count_ge(cand):
        def body(kb, acc):
            rows = pl.ds(pl.multiple_of(kb * KBLK, KBLK), KBLK)
            hit = jnp.where(keys_ref[rows, :] >= cand, 1, 0).astype(I32)
            return acc + jnp.sum(hit.reshape(KBLK // SUBLANES, SUBLANES, QBLK), axis=0)
        acc = lax.fori_loop(0, nkb, body, jnp.zeros((SUBLANES, QBLK), I32))
        return jnp.sum(acc, axis=0, keepdims=True)

    zero = jnp.zeros((1, QBLK), I32)
    thr = jnp.where(count_ge(zero) >= k_top, zero, jnp.full((1, QBLK), INT_MIN, I32))

    def bit_body(i, thr):
        cand = thr | (jnp.int32(1) << (30 - i))
        return jnp.where(count_ge(cand) >= k_top, cand, thr)

    thr = lax.fori_loop(0, 31, bit_body, thr)
    need = (k_top - (count_ge(thr + 1))).astype(F32)

    log2e = [p.astype(F32) for p in _split3(jnp.full((1, QBLK), LOG2E, F32))]
    ci16 = lax.broadcasted_iota(I32, (2 * SUBLANES, QBLK), 0)

    def pos_coef(h):
        slope = 2.0 ** (-8.0 * (h + 1) / AT_HEADS)
        blk = jnp.zeros((2 * SUBLANES, QBLK), F32)
        for i in range(3):
            blk = jnp.where(ci16 == i, log2e[i] * (POS_RADIX * slope), blk)
            blk = jnp.where(ci16 == 3 + i, log2e[i] * slope, blk)
        return blk.astype(BF16)

    for j in range(AT_HEADS // 2):
        pair = qT_ref[0, j * LANES:(j + 1) * LANES, :]
        rr = lax.broadcasted_iota(I32, pair.shape, 0)
        zeros = jnp.zeros_like(pair)
        qbd_ref[j, :LANES, :QBLK] = jnp.where(rr < AT_DH, pair, zeros)
        qbd_ref[j, :LANES, QBLK:] = jnp.where(rr >= AT_DH, pair, zeros)
        qbd_ref[j, LANES:LANES + 2 * SUBLANES, :QBLK] = pos_coef(2 * j)
        qbd_ref[j, LANES:LANES + 2 * SUBLANES, QBLK:] = pos_coef(2 * j + 1)
        qbd_ref[j, LANES + 2 * SUBLANES:, :] = jnp.zeros((KP_W - LANES - 2 * SUBLANES, 2 * QBLK), BF16)
    acc_ref[...] = jnp.zeros_like(acc_ref)
    m_ref[...] = jnp.full_like(m_ref, -jnp.inf)
    l_ref[...] = jnp.zeros_like(l_ref)

    ri = lax.broadcasted_iota(I32, (KBLK, KBLK), 0)
    ci = lax.broadcasted_iota(I32, (KBLK, KBLK), 1)
    lstrict = jnp.where(ci < ri, 1.0, 0.0).astype(BF16)

    def attn_body(kb, carry, last=False):
        rows = pl.ds(pl.multiple_of(kb * KBLK, KBLK), KBLK)
        kblk = keys_ref[rows, :]
        s_idx = srow + kb * KBLK
        eq = kblk == thr
        eqf = jnp.where(eq, 1.0, 0.0)
        before = _dot(lstrict, eqf.astype(BF16)) + carry
        sel = ((kblk > thr) | (eq & (before < need))) & (s_idx < end_t)
        bias = jnp.where(sel, 0.0, -jnp.inf)
        if last:
            ahead = jnp.maximum(s_idx - t_idx, 0).astype(F32)
        s2s = [_dot(k_ref[0, rows, j * KP_W:(j + 1) * KP_W], qbd_ref[j])
               for j in range(AT_HEADS // 2)]
        for j in range(AT_HEADS // 2):
            ps, alphas = [], []
            for half in range(2):
                h = 2 * j + half
                st = s2s[j][:, half * QBLK:(half + 1) * QBLK] + bias
                if last:
                    st = st - (2.0 * LOG2E * 2.0 ** (-8.0 * (h + 1) / AT_HEADS)) * ahead
                m_old = m_ref[h:h + 1, :]
                m_new = jnp.maximum(m_old, jnp.max(st, axis=0, keepdims=True))
                m_safe = jnp.where(m_new == -jnp.inf, 0.0, m_new)
                alpha = jnp.exp2(m_old - m_safe)
                p = jnp.exp2(st - m_safe)
                l_ref[h:h + 1, :] = alpha * l_ref[h:h + 1, :] + jnp.sum(p, axis=0, keepdims=True)
                m_ref[h:h + 1, :] = m_new
                ps.append(p.astype(BF16))
                alphas.append(alpha)
            o2 = _dot(vT_ref[0, kb, j * LANES:(j + 1) * LANES, :], jnp.concatenate(ps, axis=1))
            for half in range(2):
                hs = slice((2 * j + half) * AT_DH, (2 * j + half + 1) * AT_DH)
                acc_ref[hs, :] = (alphas[half] * acc_ref[hs, :]
                                  + o2[half * AT_DH:(half + 1) * AT_DH, half * QBLK:(half + 1) * QBLK])
        return carry + jnp.sum(eqf, axis=0, keepdims=True)

    ties_seen = lax.fori_loop(0, nkb - 1, attn_body, jnp.zeros((1, QBLK), F32))
    attn_body(nkb - 1, ties_seen, last=True)

    for h in range(AT_HEADS):
        hs = slice(h * AT_DH, (h + 1) * AT_DH)
        acc_ref[hs, :] = acc_ref[hs, :] / l_ref[h:h + 1, :]
    o_ref[0] = acc_ref[...].T.astype(o_ref.dtype)


def _dsa_attn(qT, qiT, wT, kidx, k, vT):
    b, _, s = qT.shape
    nb = s // KBLK
    k_top = min(TOPK_MAX, s // 4)
    return pl.pallas_call(
        functools.partial(_dsa_attn_kernel, k_top=k_top),
        grid=(b, s // QBLK),
        in_specs=[pl.BlockSpec((1, AT_WIDTH, QBLK), lambda i, j: (i, 0, j)),
                  pl.BlockSpec((1, IDX_HEADS * IDX_DH, QBLK), lambda i, j: (i, 0, j)),
                  pl.BlockSpec((1, SUBLANES, QBLK), lambda i, j: (i, 0, j)),
                  pl.BlockSpec((1, s, KIDX_W), lambda i, j: (i, 0, 0)),
                  pl.BlockSpec((1, s, K_W), lambda i, j: (i, 0, 0)),
                  pl.BlockSpec((1, nb, AT_WIDTH, KBLK), lambda i, j: (i, 0, 0, 0))],
        out_specs=pl.BlockSpec((1, QBLK, AT_WIDTH), lambda i, j: (i, j, 0)),
        out_shape=jax.ShapeDtypeStruct((b, s, AT_WIDTH), BF16),
        scratch_shapes=[pltpu.VMEM((s, QBLK), I32),
                        pltpu.VMEM((KIDX_W, IDX_HEADS * QBLK), BF16),
                        pltpu.VMEM((AT_HEADS // 2, KP_W, 2 * QBLK), BF16),
                        pltpu.VMEM((AT_WIDTH, QBLK), F32),
                        pltpu.VMEM((AT_HEADS, QBLK), F32),
                        pltpu.VMEM((AT_HEADS, QBLK), F32)],
        compiler_params=_params(("parallel", "arbitrary")),
        name="dsa_attn",
    )(qT, qiT, wT, kidx, k, vT)


def _layer_norm(v, g, b):
    mu = jnp.mean(v, axis=-1, keepdims=True)
    var = jnp.mean(jnp.square(v - mu), axis=-1, keepdims=True)
    return (v - mu) * lax.rsqrt(var + LN_EPS) * g + b


def _out_ln_kernel(oa_ref, ob_ref, x_ref, wa_ref, wb_ref, g_ref, b_ref, wrT_ref, bias_ref,
                   h_ref, hb_ref, eidx_ref, gT_ref, cnt_ref):
    mix = _dot(oa_ref[...], wa_ref[...]) + _dot(ob_ref[...], wb_ref[...])
    h = _layer_norm(ALPHA * x_ref[...] + mix, g_ref[...], b_ref[...])
    h_ref[...] = h
    hb_ref[...] = h.astype(BF16)
    eidx_ref[...], gT_ref[...], cnt_ref[0] = _route(h, wrT_ref[...], bias_ref[...])


def _out_ln(oa, ob, x2, wa, wb, g, b, wrT, bias_col, tm):
    t, d = x2.shape
    nt = t // tm
    row = lambda w: pl.BlockSpec((tm, w), lambda i: (i, 0))
    full = lambda a: pl.BlockSpec(a.shape, lambda i: (0, 0))
    return pl.pallas_call(
        _out_ln_kernel,
        grid=(nt,),
        in_specs=[row(oa.shape[1]), row(ob.shape[1]), row(d), full(wa), full(wb), full(g), full(b),
                  full(wrT), full(bias_col)],
        out_specs=[row(d), row(d),
                   pl.BlockSpec((TOP_K, tm), lambda i: (0, i)),
                   row(LANES),
                   pl.BlockSpec((1, N_EXPERTS, LANES), lambda i: (i, 0, 0))],
        out_shape=[jax.ShapeDtypeStruct((t, d), F32), jax.ShapeDtypeStruct((t, d), BF16),
                   jax.ShapeDtypeStruct((TOP_K, t), I32),
                   jax.ShapeDtypeStruct((t, LANES), F32),
                   jax.ShapeDtypeStruct((nt, N_EXPERTS, LANES), F32)],
        compiler_params=_params(("parallel",)),
        name="out_ln_route",
    )(oa, ob, x2, wa, wb, g, b, wrT, bias_col)


def _first_max(v, idx, big):
    m = jnp.max(v, axis=0, keepdims=True)
    first = jnp.min(jnp.where(v == m, idx, big), axis=0, keepdims=True)
    return m, first


def _route(h, wrT, bias_col):
    tm = h.shape[0]
    per_group = N_EXPERTS // N_GROUPS
    logits = _dot_nt(_cat_hlh(wrT, 1), _cat_hhl(h, 1))
    scores = _sigmoid(logits)
    sel = scores + bias_col
    iota_g = lax.broadcasted_iota(I32, (per_group, tm), 0)
    neg = -jnp.inf

    grp = jnp.zeros((N_GROUPS, tm), F32)
    iota_grp = lax.broadcasted_iota(I32, (N_GROUPS, tm), 0)
    for g in range(N_GROUPS):
        blk = sel[g * per_group:(g + 1) * per_group, :]
        m1, i1 = _first_max(blk, iota_g, per_group)
        m2 = jnp.max(jnp.where(iota_g == i1, neg, blk), axis=0, keepdims=True)
        grp = jnp.where(iota_grp == g, m1 + m2, grp)

    gmask = jnp.zeros((N_GROUPS, tm), F32)
    work = grp
    for _ in range(TOPK_GROUPS):
        _, gi = _first_max(work, iota_grp, N_GROUPS)
        hit = iota_grp == gi
        gmask = jnp.where(hit, 1.0, gmask)
        work = jnp.where(hit, neg, work)

    e_mask = jnp.concatenate(
        [jnp.broadcast_to(gmask[g:g + 1, :], (per_group, tm)) for g in range(N_GROUPS)], axis=0)
    iota_e = lax.broadcasted_iota(I32, (N_EXPERTS, tm), 0)
    work = jnp.where(e_mask > 0.5, sel, neg)
    iota_k = lax.broadcasted_iota(I32, (TOP_K, tm), 0)
    eidx = jnp.zeros((TOP_K, tm), I32)
    w = jnp.zeros((N_EXPERTS, tm), F32)
    chosen = jnp.zeros((N_EXPERTS, tm), F32)
    for k in range(TOP_K):
        _, ei = _first_max(work, iota_e, N_EXPERTS)
        hit = iota_e == ei
        eidx = jnp.where(iota_k == k, ei, eidx)
        w = jnp.where(hit, scores, w)
        chosen = jnp.where(hit, 1.0, chosen)
        work = jnp.where(hit, neg, work)

    gates = w / jnp.sum(w, axis=0, keepdims=True) * ROUTED_SCALE
    gates = jnp.concatenate([gates, jnp.zeros((LANES - N_EXPERTS, tm), F32)], axis=0)
    cnt = jnp.broadcast_to(jnp.sum(chosen, axis=1, keepdims=True), (N_EXPERTS, LANES))
    return eidx, gates.T, cnt


MOE_TM = 512
SEG = 16
EXP_TM = 768
ROW_BUF = TOP_K * MOE_TM + N_EXPERTS * SEG
CH_MAX = ROW_BUF // SEG
P_BLK = 512
G_BLK = 512


def _moe_plan(cnt, n_exp_tiles):
    nt = cnt.shape[0]
    n16 = (cnt + (SEG - 1)) // SEG
    so16 = jnp.cumsum(n16, axis=1) - n16
    r16 = jnp.sum(n16, axis=0)
    per = EXP_TM // SEG
    rp16 = (r16 + (per - 1)) // per * per
    ends = jnp.cumsum(rp16)
    off16 = ends - rp16
    go16 = off16[None, :] + jnp.cumsum(n16, axis=0) - n16
    n_et = (ends[-1] // per).astype(I32).reshape(1)
    tiles = jnp.arange(n_exp_tiles, dtype=I32)
    tile_expert = jnp.sum((ends // per)[None, :] <= tiles[:, None], axis=1)
    tile_expert = jnp.minimum(tile_expert, N_EXPERTS - 1).astype(I32)
    c = jnp.arange(CH_MAX, dtype=I32)[None, :, None]
    lo, hi = so16[:, None, :], (so16 + n16)[:, None, :]
    inside = (lo <= c) & (c < hi)
    dst16 = jnp.sum(jnp.where(inside, go16[:, None, :] + c - lo, 0), axis=2)
    dst16 = dst16.reshape(nt * CH_MAX).astype(I32)
    twice = lambda a: jnp.concatenate([a, a], axis=1).astype(F32)
    seg_lo, seg_hi = twice(so16 * SEG), twice((so16 + n16) * SEG)
    as_rows = lambda a: jnp.broadcast_to(a[:, None, :], (nt, SUBLANES, 2 * N_EXPERTS))
    as_cols = lambda a: jnp.broadcast_to(a[:, :, None], (nt, 2 * N_EXPERTS, LANES))
    return dict(dst16=dst16,
                rtot16=jnp.sum(n16, axis=1).astype(I32), r16=r16.astype(I32), rp16=rp16.astype(I32),
                off16=off16.astype(I32), n_et=n_et, tile_expert=tile_expert,
                lo_rows=as_rows(seg_lo), hi_rows=as_rows(seg_hi),
                lo_cols=as_cols(seg_lo), hi_cols=as_cols(seg_hi))


WAIT_BATCH = 16


def _chunk(ref, c, n=1):
    return ref.at[pl.ds(pl.multiple_of(c * SEG, SEG), n * SEG)]


def _segment_copies(i, dst16_ref, rtot_ref, make_copy):
    n = rtot_ref[i]

    def chunk_body(c, carry):
        make_copy(c, dst16_ref[i * CH_MAX + c]).start()
        return carry

    lax.fori_loop(0, n, chunk_body, 0)
    return n


def _wait_copies(count, make_copy):
    for n, trips in ((WAIT_BATCH, count // WAIT_BATCH), (1, count % WAIT_BATCH)):
        def body(c, carry, n=n):
            make_copy(0, 0, n).wait()
            return carry

        lax.fori_loop(0, trips, body, 0)


POS_SPLIT = 64


def _dispatch_kernel(dst16_ref, rtot_ref, r16_ref, rp16_ref, off16_ref,
                     x_ref, eidx_ref, locol_ref, lorow_ref, hirow_ref,
                     xs_hbm, posT_ref, buf_ref, zero_ref, pend_ref, sem):
    i = pl.program_id(0)
    slot = lax.rem(i, 2)
    buf = buf_ref.at[slot]
    tm = x_ref.shape[0]
    eidx = eidx_ref[...]
    iota_e = lax.broadcasted_iota(I32, (N_EXPERTS, tm), 0)
    onehot = jnp.zeros((N_EXPERTS, tm), F32)
    for k in range(TOP_K):
        onehot = jnp.where(iota_e == eidx[k:k + 1, :], 1.0, onehot)
    earlier = lax.broadcasted_iota(I32, (tm, tm), 0) < lax.broadcasted_iota(I32, (tm, tm), 1)
    rank = _dot(onehot.astype(BF16), jnp.where(earlier, 1.0, 0.0).astype(BF16))
    pos1 = jnp.where(onehot > 0.5, locol_ref[0][:N_EXPERTS, 0:1] + rank + 1.0, 0.0)
    pos_hi = jnp.floor(pos1 * (1.0 / POS_SPLIT)) * POS_SPLIT
    pos2 = jnp.concatenate([pos_hi, pos1 - pos_hi], axis=0)
    posT_ref[...] = pos2.T
    pos2b = pos2.astype(BF16)

    x = x_ref[...]
    seg_lo = lorow_ref[0][0:1, :]
    seg_hi = hirow_ref[0][0:1, :]
    n_blk = (rtot_ref[i] * SEG + (P_BLK - 1)) // P_BLK

    def blk_body(rb, carry):
        r0 = pl.multiple_of(rb * P_BLK, P_BLK)
        r_e = (lax.broadcasted_iota(I32, (P_BLK, 2 * N_EXPERTS), 0) + r0).astype(F32)
        owner = jnp.where((r_e >= seg_lo) & (r_e < seg_hi), 1.0, 0.0).astype(BF16)
        want = _dot(owner, pos2b)
        r_t = (lax.broadcasted_iota(I32, (P_BLK, tm), 0) + (r0 + 1)).astype(F32)
        p = jnp.where(want == r_t, 1.0, 0.0).astype(BF16)
        buf[pl.ds(r0, P_BLK), :] = _dot(p, x).astype(BF16)
        return carry

    lax.fori_loop(0, n_blk, blk_body, 0)

    @pl.when(i > 0)
    def _():
        prev_copy = lambda src, dst, n=1: pltpu.make_async_copy(
            _chunk(buf_ref.at[1 - slot], src, n), _chunk(xs_hbm, dst, n), sem.at[1 - slot])
        _wait_copies(pend_ref[0], prev_copy)

    out_copy = lambda src, dst, n=1: pltpu.make_async_copy(_chunk(buf, src, n), _chunk(xs_hbm, dst, n),
                                                           sem.at[slot])
    pend_ref[0] = _segment_copies(i, dst16_ref, rtot_ref, out_copy)

    @pl.when(i == pl.num_programs(0) - 1)
    def _():
        _wait_copies(pend_ref[0], out_copy)
        zero_ref[...] = jnp.zeros_like(zero_ref)
        zero_copy = lambda src, dst, n=1: pltpu.make_async_copy(
            _chunk(zero_ref, 0, n), _chunk(xs_hbm, dst, n), sem.at[slot])

        def zero_chunk(c, carry):
            zero_copy(0, c).start()
            return carry

        def tail_body(e, total):
            lax.fori_loop(off16_ref[e] + r16_ref[e], off16_ref[e] + rp16_ref[e], zero_chunk, 0)
            return total + rp16_ref[e] - r16_ref[e]

        _wait_copies(lax.fori_loop(0, N_EXPERTS, tail_body, 0), zero_copy)

        per = EXP_TM // SEG
        first = (off16_ref[N_EXPERTS - 1] + rp16_ref[N_EXPERTS - 1]) // per
        n_tiles = xs_hbm.shape[0] // EXP_TM
        tile_copy = lambda tile: pltpu.make_async_copy(
            zero_ref, xs_hbm.at[pl.ds(pl.multiple_of(tile * EXP_TM, EXP_TM), EXP_TM)], sem.at[slot])

        def zero_tile(tile, carry):
            tile_copy(tile).start()
            return carry

        def wait_tile(tile, carry):
            tile_copy(0).wait()
            return carry

        lax.fori_loop(first, n_tiles, zero_tile, 0)
        lax.fori_loop(first, n_tiles, wait_tile, 0)


def _dispatch(hb, eidx, plan, n_rows):
    t, d = hb.shape
    nt = t // MOE_TM
    grid_spec = pltpu.PrefetchScalarGridSpec(
        num_scalar_prefetch=5,
        grid=(nt,),
        in_specs=[pl.BlockSpec((MOE_TM, d), lambda i, *_: (i, 0)),
                  pl.BlockSpec((TOP_K, MOE_TM), lambda i, *_: (0, i)),
                  pl.BlockSpec((1, 2 * N_EXPERTS, LANES), lambda i, *_: (i, 0, 0)),
                  pl.BlockSpec((1, SUBLANES, 2 * N_EXPERTS), lambda i, *_: (i, 0, 0)),
                  pl.BlockSpec((1, SUBLANES, 2 * N_EXPERTS), lambda i, *_: (i, 0, 0))],
        out_specs=[pl.BlockSpec(memory_space=pl.ANY),
                   pl.BlockSpec((MOE_TM, LANES), lambda i, *_: (i, 0))],
        scratch_shapes=[pltpu.VMEM((2, ROW_BUF, d), BF16), pltpu.VMEM((EXP_TM, d), BF16),
                        pltpu.SMEM((1,), I32), pltpu.SemaphoreType.DMA((2,))])
    return pl.pallas_call(
        _dispatch_kernel,
        grid_spec=grid_spec,
        out_shape=[jax.ShapeDtypeStruct((n_rows, d), BF16),
                   jax.ShapeDtypeStruct((t, LANES), F32)],
        compiler_params=_params(("arbitrary",)),
        name="moe_dispatch",
    )(plan["dst16"], plan["rtot16"], plan["r16"], plan["rp16"],
      plan["off16"], hb, eidx, plan["lo_cols"], plan["lo_rows"], plan["hi_rows"])


def _experts_kernel(te_ref, net_ref, x_ref, wg_ref, wu_ref, wd_ref, y_ref, wgb_ref, wub_ref, wdb_ref):
    j = pl.program_id(0)
    used = j < net_ref[0]

    @pl.when(used & ((j == 0) | (te_ref[j] != te_ref[jnp.maximum(j - 1, 0)])))
    def _():
        wgb_ref[...] = wg_ref[0].astype(BF16)
        wub_ref[...] = wu_ref[0].astype(BF16)
        wdb_ref[...] = wd_ref[0].astype(BF16)

    @pl.when(used)
    def _():
        x = x_ref[...]
        hmid = _silu(_dot(x, wgb_ref[...])) * _dot(x, wub_ref[...])
        y_ref[...] = _dot(hmid.astype(BF16), wdb_ref[...]).astype(y_ref.dtype)


def _experts(xs, wg, wu, wd, plan):
    n_rows, d = xs.shape
    row_map = lambda j, te, net: (jnp.minimum(j, net[0] - 1), 0)
    w_map = lambda j, te, net: (te[jnp.minimum(j, net[0] - 1)], 0, 0)
    grid_spec = pltpu.PrefetchScalarGridSpec(
        num_scalar_prefetch=2,
        grid=(n_rows // EXP_TM,),
        in_specs=[pl.BlockSpec((EXP_TM, d), row_map),
                  pl.BlockSpec((1, d, D_EXPERT), w_map),
                  pl.BlockSpec((1, d, D_EXPERT), w_map),
                  pl.BlockSpec((1, D_EXPERT, d), w_map)],
        out_specs=pl.BlockSpec((EXP_TM, d), row_map),
        scratch_shapes=[pltpu.VMEM((d, D_EXPERT), BF16), pltpu.VMEM((d, D_EXPERT), BF16),
                        pltpu.VMEM((D_EXPERT, d), BF16)])
    return pl.pallas_call(
        _experts_kernel,
        grid_spec=grid_spec,
        out_shape=jax.ShapeDtypeStruct((n_rows, d), BF16),
        input_output_aliases={2: 0},
        compiler_params=_params(("arbitrary",)),
        name="moe_experts",
    )(plan["tile_expert"], plan["n_et"], xs, wg, wu, wd)


def _combine_kernel(dst16_ref, rtot_ref,
                    ys_hbm, posT_ref, gT_ref, locol_ref, hicol_ref, h_ref,
                    sg_ref, su_ref, sd_ref, lg_ref, lb_ref, o_ref, ybuf_ref, acc_ref, pend_ref, sem):
    i = pl.program_id(0)
    slot = lax.rem(i, 2)
    ybuf = ybuf_ref.at[slot]
    tm = h_ref.shape[0]

    def fetch(tile, to_slot):
        copy = lambda dst, src: pltpu.make_async_copy(
            _chunk(ys_hbm, src), _chunk(ybuf_ref.at[to_slot], dst), sem.at[to_slot])
        pend_ref[to_slot] = _segment_copies(tile, dst16_ref, rtot_ref, copy)

    @pl.when(i == 0)
    def _():
        ybuf_ref[...] = jnp.zeros_like(ybuf_ref)
        fetch(0, 0)

    @pl.when(i + 1 < pl.num_programs(0))
    def _():
        fetch(i + 1, 1 - slot)

    h = h_ref[...]
    xb = h.astype(BF16)
    hs = _silu(_dot(xb, sg_ref[...])) * _dot(xb, su_ref[...])
    acc_ref[...] = ALPHA * h + _dot(hs.astype(BF16), sd_ref[...])
    _wait_copies(pend_ref[slot], lambda dst, src, n=1: pltpu.make_async_copy(
        _chunk(ys_hbm, src, n), _chunk(ybuf, dst, n), sem.at[slot]))

    pos2b = posT_ref[...].astype(BF16)
    g_hi, g_lo = _split(gT_ref[...])
    gate2b = (g_hi.astype(F32) + pltpu.roll(g_lo.astype(F32), N_EXPERTS, 1)).astype(BF16)
    seg_lo = jnp.broadcast_to(locol_ref[0][:, 0:1], (2 * N_EXPERTS, G_BLK))
    seg_hi = jnp.broadcast_to(hicol_ref[0][:, 0:1], (2 * N_EXPERTS, G_BLK))
    n_blk = (rtot_ref[i] * SEG + (G_BLK - 1)) // G_BLK

    def blk_body(cb, carry):
        c0 = pl.multiple_of(cb * G_BLK, G_BLK)
        c_e = (lax.broadcasted_iota(I32, (2 * N_EXPERTS, G_BLK), 1) + c0).astype(F32)
        owner = jnp.where((c_e >= seg_lo) & (c_e < seg_hi), 1.0, 0.0).astype(BF16)
        want = _dot(pos2b, owner)
        gate = _dot(gate2b, owner)
        c_t = (lax.broadcasted_iota(I32, (tm, G_BLK), 1) + (c0 + 1)).astype(F32)
        g = jnp.where(want == c_t, gate, 0.0).astype(BF16)
        acc_ref[...] += _dot(g, ybuf[pl.ds(c0, G_BLK), :])
        return carry

    lax.fori_loop(0, n_blk, blk_body, 0)
    o_ref[...] = _layer_norm(acc_ref[...], lg_ref[...], lb_ref[...])


def _combine(ys, posT, gT, h1, sg, su, sd, lg, lb, plan):
    t, d = h1.shape
    nt = t // MOE_TM
    row = lambda w: pl.BlockSpec((MOE_TM, w), lambda i, *_: (i, 0))
    full = lambda a: pl.BlockSpec(a.shape, lambda i, *_: (0, 0))
    seg = pl.BlockSpec((1, 2 * N_EXPERTS, LANES), lambda i, *_: (i, 0, 0))
    grid_spec = pltpu.PrefetchScalarGridSpec(
        num_scalar_prefetch=2,
        grid=(nt,),
        in_specs=[pl.BlockSpec(memory_space=pl.ANY), row(LANES), row(LANES), seg, seg, row(d),
                  full(sg), full(su), full(sd), full(lg), full(lb)],
        out_specs=row(d),
        scratch_shapes=[pltpu.VMEM((2, ROW_BUF, d), BF16), pltpu.VMEM((MOE_TM, d), F32),
                        pltpu.SMEM((2,), I32), pltpu.SemaphoreType.DMA((2,))])
    return pl.pallas_call(
        _combine_kernel,
        grid_spec=grid_spec,
        out_shape=jax.ShapeDtypeStruct((t, d), F32),
        compiler_params=_params(("arbitrary",)),
        name="moe_combine",
    )(plan["dst16"], plan["rtot16"], ys, posT, gT,
      plan["lo_cols"], plan["hi_cols"], h1, sg, su, sd, lg, lb)


def _moe(h1, hb, eidx, gT, cnt, wg, wu, wd, sg, su, sd, lg, lb):
    t = h1.shape[0]
    nt = t // MOE_TM
    max_rows = nt * ROW_BUF + N_EXPERTS * EXP_TM
    n_exp_tiles = pl.cdiv(max_rows, EXP_TM)
    plan = _moe_plan(cnt[:, :, 0].astype(I32), n_exp_tiles)
    xs, posT = _dispatch(hb, eidx, plan, n_exp_tiles * EXP_TM)
    ys = _experts(xs, wg, wu, wd, plan)
    return _combine(ys, posT, gT, h1, sg, su, sd, lg, lb, plan)


def kernel(x, w_in, hg_lb_logits, hg_norm_g, q_norm_g, w_q_up, w_qidx_up, kv_norm_g, w_kv_up,
           idx_ln_g, idx_ln_b, w_out, ln1_g, ln1_b, w_router, router_bias, w_e_gate, w_e_up,
           w_e_down, w_s_gate, w_s_up, w_s_down, ln2_g, ln2_b):
    b, s, d = x.shape
    h = x
    for l in range(DEPTH):
        h2 = h.reshape(b * s, d)
        n_main = 4 * HG_WIDTH
        w_main = w_in[l, :, :n_main].astype(BF16)
        c_kv, c_idx = n_main + Q_RANK, n_main + Q_RANK + KV_RANK
        w_sel = jnp.concatenate([w_in[l, :, n_main:c_kv], w_in[l, :, c_idx:]], axis=1)
        w_sel = jnp.pad(w_sel, ((0, 0), (0, SEL_W - w_sel.shape[1])))
        w_ckv = w_in[l, :, c_kv:c_idx].astype(BF16)

        o_a = _hgrn(h, w_main, hg_lb_logits, hg_norm_g[l], l)

        wqT = w_q_up[l].T.astype(BF16)
        wqiT = w_qidx_up[l].T
        wk = w_kv_up[l][:, :AT_WIDTH].astype(BF16)
        wvT = w_kv_up[l][:, AT_WIDTH:].T.astype(BF16)
        qT, qiT, k, vT, kidx, wT = _dsa_prep(h, w_sel, w_ckv, q_norm_g[l], kv_norm_g[l],
                                             idx_ln_g[l], idx_ln_b[l], wqT, wqiT, wk, wvT)
        o_b = _dsa_attn(qT, qiT, wT, kidx, k, vT)

        h1, hb, eidx, gT, cnt = _out_ln(
            o_a.reshape(b * s, HG_WIDTH), o_b.reshape(b * s, AT_WIDTH), h2,
            w_out[l, :HG_WIDTH].astype(BF16), w_out[l, HG_WIDTH:].astype(BF16),
            ln1_g[l].reshape(1, d), ln1_b[l].reshape(1, d),
            w_router[l].T, router_bias[l].reshape(N_EXPERTS, 1), MOE_TM)
        out = _moe(h1, hb, eidx, gT, cnt, w_e_gate[l], w_e_up[l], w_e_down[l],
                   w_s_gate[l].astype(BF16), w_s_up[l].astype(BF16), w_s_down[l].astype(BF16),
                   ln2_g[l].reshape(1, d), ln2_b[l].reshape(1, d))
        h = out.reshape(b, s, d)
    return h
```

```python
import functools

import jax
import jax.numpy as jnp
from jax import lax
from jax.experimental import pallas as pl
from jax.experimental.pallas import tpu as pltpu

F32 = jnp.float32
BF16 = jnp.bfloat16
I32 = jnp.int32
HIGHEST = lax.Precision.HIGHEST

CHUNK = 64
HG_HEADS = 4
HG_DK = 128
HG_DV = 128
HG_WIDTH = HG_HEADS * HG_DV
AT_HEADS = 8
AT_DH = 64
AT_WIDTH = AT_HEADS * AT_DH
Q_RANK = 256
KV_RANK = 128
IDX_HEADS = 4
IDX_DH = 64
TOPK_MAX = 256
N_EXPERTS = 64
TOP_K = 8
N_GROUPS = 8
TOPK_GROUPS = 4
D_EXPERT = 256
ROUTED_SCALE = 2.5
DEPTH = 1
ALPHA = (2.0 * DEPTH) ** 0.25
LN_EPS = 1e-5
RMS_EPS = 1e-6
LOG2E = 1.4426950408889634

LANES = 128
SUBLANES = 8
QBLK = 4 * CHUNK
KBLK = 256
VMEM_LIMIT = 56 * 1024 * 1024
INT_MIN = -2 ** 31


def _params(sem, vmem=VMEM_LIMIT):
    return pltpu.CompilerParams(dimension_semantics=sem, vmem_limit_bytes=vmem)


def _dot(a, b, precision=None):
    return jnp.dot(a, b, preferred_element_type=F32, precision=precision)


def _dot_nt(a, b, precision=None):
    return lax.dot_general(a, b, (((1,), (1,)), ((), ())), preferred_element_type=F32,
                           precision=precision)


def _dot_tn(a, b, precision=None):
    return lax.dot_general(a, b, (((0,), (0,)), ((), ())), preferred_element_type=F32,
                           precision=precision)


def _split(x):
    hi = x.astype(BF16)
    return hi, (x - hi.astype(F32)).astype(BF16)


def _cat_hhl(x, axis):
    hi, lo = _split(x)
    return jnp.concatenate([hi, hi, lo], axis=axis)


def _cat_hlh(x, axis):
    hi, lo = _split(x)
    return jnp.concatenate([hi, lo, hi], axis=axis)


def _split3(x):
    hi = x.astype(BF16)
    r = x - hi.astype(F32)
    mid = r.astype(BF16)
    return hi, mid, (r - mid.astype(F32)).astype(BF16)


def _cat_act6(x, axis):
    h, m, l = _split3(x)
    return jnp.concatenate([h, h, m, h, l, m], axis=axis)


def _cat_wgt6(x, axis):
    h, m, l = _split3(x)
    return jnp.concatenate([h, m, h, l, h, m], axis=axis)


def _sigmoid(x):
    return 1.0 / (1.0 + jnp.exp(-x))


def _silu(x):
    return x * _sigmoid(x)


def _hgrn_kernel(x_ref, w_ref, lbl_ref, ng_ref, o_ref, p_ref, st_ref, *, n_chunks, layer):
    @pl.when(pl.program_id(1) == 0)
    def _():
        st_ref[...] = jnp.zeros_like(st_ref)

    for sq in range(x_ref.shape[0]):
        p_ref[sq] = _dot(x_ref[sq].astype(BF16), w_ref[...])

    lg = lbl_ref[...]
    ex = jnp.exp(lg - jnp.max(lg, axis=0, keepdims=True))
    lb_all = jnp.sum(ex[: layer + 1], axis=0, keepdims=True) / jnp.sum(ex, axis=0, keepdims=True)
    ng = ng_ref[...]

    r = lax.broadcasted_iota(I32, (CHUNK, CHUNK), 0)
    c = lax.broadcasted_iota(I32, (CHUNK, CHUNK), 1)
    causal = c <= r
    tri = jnp.where(causal, 1.0, 0.0).astype(BF16)
    tri2 = jnp.concatenate([tri, tri], axis=1)

    units = [(sq, h) for sq in range(p_ref.shape[0]) for h in range(HG_HEADS)]

    def chunk_body(j, carry):
        rows = pl.ds(pl.multiple_of(j * CHUNK, CHUNK), CHUNK)
        ks, bs = [], []
        for sq, h in units:
            lo = h * HG_DK
            lb = lb_all[:, lo:lo + HG_DK]
            f = lb + (1.0 - lb) * _sigmoid(p_ref[sq, rows, HG_WIDTH + lo:HG_WIDTH + lo + HG_DK])
            lf_hi, lf_lo = _split(jnp.log(f))
            ks.append(1.0 - f)
            bs.append(_dot(tri2, jnp.concatenate([lf_hi, lf_lo], axis=0)))
        scs, ois, vbs = [], [], []
        for (sq, h), k, b in zip(units, ks, bs):
            lo = h * HG_DK
            b_last = b[CHUNK - 1:CHUNK, :]
            q_dec = (_silu(p_ref[sq, rows, lo:lo + HG_DK]) * jnp.exp(b)).astype(BF16)
            k_inv = (k * jnp.exp(-b)).astype(BF16)
            k_dec = (k * jnp.exp(b_last - b)).astype(BF16)
            vb = p_ref[sq, rows, 2 * HG_WIDTH + lo:2 * HG_WIDTH + lo + HG_DV].astype(BF16)
            st = st_ref[sq, h]
            scs.append(jnp.where(causal, _dot_nt(q_dec, k_inv), 0.0).astype(BF16))
            ois.append(_dot_nt(q_dec, st.astype(BF16)))
            st_ref[sq, h] = st * jnp.exp(b_last) + _dot_tn(vb, k_dec)
            vbs.append(vb)
        for (sq, h), sc, oi, vb in zip(units, scs, ois, vbs):
            lo = h * HG_DK
            o = _dot(sc, vb) + oi
            o = o * lax.rsqrt(jnp.mean(o * o, axis=-1, keepdims=True) + RMS_EPS) * ng
            gate = p_ref[sq, rows, 3 * HG_WIDTH + lo:3 * HG_WIDTH + lo + HG_DV]
            o_ref[sq, rows, lo:lo + HG_DV] = (o * _silu(gate)).astype(o_ref.dtype)
        return carry

    lax.fori_loop(0, n_chunks, chunk_body, 0)


def _hgrn(x, w_main, lb_logits, norm_g, layer, ct=512, n_seq=2):
    b, s, d = x.shape
    w = w_main.shape[1]
    n_seq = min(n_seq, b)
    ct = min(ct, s)
    n_chunks = ct // CHUNK
    return pl.pallas_call(
        functools.partial(_hgrn_kernel, n_chunks=n_chunks, layer=layer),
        grid=(b // n_seq, s // ct),
        in_specs=[pl.BlockSpec((n_seq, ct, d), lambda i, j: (i, j, 0)),
                  pl.BlockSpec(w_main.shape, lambda i, j: (0, 0)),
                  pl.BlockSpec(lb_logits.shape, lambda i, j: (0, 0)),
                  pl.BlockSpec((1, HG_DV), lambda i, j: (0, 0))],
        out_specs=pl.BlockSpec((n_seq, ct, HG_WIDTH), lambda i, j: (i, j, 0)),
        out_shape=jax.ShapeDtypeStruct((b, s, HG_WIDTH), BF16),
        scratch_shapes=[pltpu.VMEM((n_seq, ct, w), F32),
                        pltpu.VMEM((n_seq, HG_HEADS, HG_DV, HG_DK), F32)],
        compiler_params=_params(("parallel", "arbitrary")),
        name="hgrn",
    )(x, w_main, lb_logits, norm_g.reshape(1, HG_DV))


POS_SHIFT = 4
POS_RADIX = 1 << POS_SHIFT
KP_W = 2 * LANES
K_W = (AT_HEADS // 2) * KP_W
SEL_W = Q_RANK + LANES
KIDX_W = 6 * IDX_DH


def _dsa_prep_kernel(x_ref, wsel_ref, wckv_ref, qg_ref, kvg_ref, lng_ref, lnb_ref, wqT_ref, wqiT_ref,
                     wk_ref, wvT_ref, qT_ref, qiT_ref, k_ref, vT_ref, kidx_ref, wT_ref, w6_ref):
    @pl.when((pl.program_id(0) == 0) & (pl.program_id(1) == 0))
    def _():
        w6_ref[...] = _cat_wgt6(wsel_ref[...], 0)

    x = x_ref[...]
    sel = _dot(_cat_act6(x, 1), w6_ref[...])
    cq = sel[:, :Q_RANK]
    cqn = cq * lax.rsqrt(jnp.mean(cq * cq, axis=-1, keepdims=True) + RMS_EPS) * qg_ref[...]
    qT_ref[0] = (_dot_nt(wqT_ref[...], cqn.astype(BF16)) * (AT_DH ** -0.5 * LOG2E)).astype(qT_ref.dtype)
    qiT_ref[0] = _dot_nt(_cat_wgt6(wqiT_ref[...], 1), _cat_act6(cqn, 1))
    ckv = _dot(x.astype(BF16), wckv_ref[...])
    ckvn = (ckv * lax.rsqrt(jnp.mean(ckv * ckv, axis=-1, keepdims=True) + RMS_EPS)
            * kvg_ref[...]).astype(BF16)
    kmat = _dot(ckvn, wk_ref[...])
    tm = kmat.shape[0]
    s_abs = pl.program_id(1) * tm + lax.broadcasted_iota(I32, (tm, LANES), 0)
    lane = lax.broadcasted_iota(I32, (tm, LANES), 1)
    pos = jnp.where(lane < 3, s_abs >> POS_SHIFT,
                    jnp.where(lane < 6, s_abs & (POS_RADIX - 1), 0)).astype(F32)
    k_ref[0] = jnp.concatenate(
        [blk for j in range(AT_HEADS // 2) for blk in (kmat[:, j * LANES:(j + 1) * LANES], pos)],
        axis=1).astype(k_ref.dtype)
    for kb in range(vT_ref.shape[1]):
        vT_ref[0, kb] = _dot_nt(wvT_ref[...], ckvn[kb * KBLK:(kb + 1) * KBLK]).astype(vT_ref.dtype)
    k128 = sel[:, Q_RANK:]
    in_k = lax.broadcasted_iota(I32, (1, LANES), 1) < IDX_DH
    mu = jnp.sum(jnp.where(in_k, k128, 0.0), axis=-1, keepdims=True) * (1.0 / IDX_DH)
    dev = jnp.where(in_k, k128 - mu, 0.0)
    var = jnp.sum(dev * dev, axis=-1, keepdims=True) * (1.0 / IDX_DH)
    y = dev * lax.rsqrt(var + LN_EPS) * lng_ref[...] + lnb_ref[...]
    h, m, l = (p.astype(F32) for p in _split3(y))
    up = lambda p: pltpu.roll(p, IDX_DH, 1)
    kidx_ref[0] = jnp.concatenate([h + up(h), m + up(h), l + up(m)], axis=1).astype(BF16)
    tT = k128.T
    wT_ref[0] = tT[IDX_DH:IDX_DH + SUBLANES, :] * (IDX_HEADS ** -0.5 * IDX_DH ** -0.5)


def _dsa_prep(x, w_sel, w_ckv, q_norm_g, kv_norm_g, idx_ln_g, idx_ln_b, wqT, wqiT, wk, wvT):
    b, s, d = x.shape
    tm = min(2 * KBLK, s)
    kb_per = tm // KBLK
    nb = s // KBLK
    full = lambda a: pl.BlockSpec(a.shape, lambda i, j: (0,) * a.ndim)
    pad_lanes = lambda v: jnp.pad(v.reshape(1, -1), ((0, 0), (0, LANES - v.shape[0])))
    args = (w_sel, w_ckv, q_norm_g.reshape(1, -1), kv_norm_g.reshape(1, -1),
            pad_lanes(idx_ln_g), pad_lanes(idx_ln_b), wqT, wqiT, wk, wvT)
    return pl.pallas_call(
        _dsa_prep_kernel,
        grid=(b, s // tm),
        in_specs=[pl.BlockSpec((None, tm, d), lambda i, j: (i, j, 0))] + [full(a) for a in args],
        out_specs=[pl.BlockSpec((1, AT_WIDTH, tm), lambda i, j: (i, 0, j)),
                   pl.BlockSpec((1, IDX_HEADS * IDX_DH, tm), lambda i, j: (i, 0, j)),
                   pl.BlockSpec((1, tm, K_W), lambda i, j: (i, j, 0)),
                   pl.BlockSpec((1, kb_per, AT_WIDTH, KBLK), lambda i, j: (i, j, 0, 0)),
                   pl.BlockSpec((1, tm, KIDX_W), lambda i, j: (i, j, 0)),
                   pl.BlockSpec((1, SUBLANES, tm), lambda i, j: (i, 0, j))],
        out_shape=[jax.ShapeDtypeStruct((b, AT_WIDTH, s), BF16),
                   jax.ShapeDtypeStruct((b, IDX_HEADS * IDX_DH, s), F32),
                   jax.ShapeDtypeStruct((b, s, K_W), BF16),
                   jax.ShapeDtypeStruct((b, nb, AT_WIDTH, KBLK), BF16),
                   jax.ShapeDtypeStruct((b, s, KIDX_W), BF16),
                   jax.ShapeDtypeStruct((b, SUBLANES, s), F32)],
        scratch_shapes=[pltpu.VMEM((6 * d, SEL_W), BF16)],
        compiler_params=_params(("arbitrary", "arbitrary")),
        name="dsa_prep",
    )(x, *args)


def _sortable(x):
    bits = lax.bitcast_convert_type(x, I32)
    return jnp.where(bits < 0, bits ^ jnp.int32(0x7FFFFFFF), bits)


def _dsa_attn_kernel(qT_ref, qiT_ref, wT_ref, kidx_ref, k_ref, vT_ref, o_ref,
                     keys_ref, q2_ref, qbd_ref, acc_ref, m_ref, l_ref, *, k_top):
    qb = pl.program_id(1)
    t0 = qb * QBLK
    e_max = t0 + QBLK
    nkb = (e_max + KBLK - 1) >> 8

    lane = lax.broadcasted_iota(I32, (1, QBLK), 1)
    t_idx = t0 + lane
    end_t = ((t_idx >> 6) + 1) * CHUNK
    srow = lax.broadcasted_iota(I32, (KBLK, QBLK), 0)

    for h in range(IDX_HEADS):
        q2_ref[:, h * QBLK:(h + 1) * QBLK] = _cat_wgt6(qiT_ref[0, h * IDX_DH:(h + 1) * IDX_DH, :], 0)
    wrow = jnp.concatenate([wT_ref[0, h:h + 1, :] for h in range(IDX_HEADS)], axis=1)

    def score_body(kb, carry):
        rows = pl.ds(pl.multiple_of(kb * KBLK, KBLK), KBLK)
        logits = _dot(kidx_ref[0, rows, :], q2_ref[...])
        r = jnp.maximum(logits, 0.0) * wrow
        sc = r[:, :QBLK]
        for h in range(1, IDX_HEADS):
            sc = sc + r[:, h * QBLK:(h + 1) * QBLK]
        valid = (srow + kb * KBLK) < end_t
        keys_ref[rows, :] = _sortable(jnp.where(valid, sc, -jnp.inf))
        return carry

    lax.fori_loop(0, nkb, score_body, 0)

    def count_ge(cand):
        def body(kb, acc):
            rows = pl.ds(pl.multiple_of(kb * KBLK, KBLK), KBLK)
            hit = jnp.where(keys_ref[rows, :] >= cand, 1, 0).astype(I32)
            return acc + jnp.sum(hit.reshape(KBLK // SUBLANES, SUBLANES, QBLK), axis=0)
        acc = lax.fori_loop(0, nkb, body, jnp.zeros((SUBLANES, QBLK), I32))
        return jnp.sum(acc, axis=0, keepdims=True)

    zero = jnp.zeros((1, QBLK), I32)
    thr = jnp.where(count_ge(zero) >= k_top, zero, jnp.full((1, QBLK), INT_MIN, I32))

    def bit_body(i, thr):
        cand = thr | (jnp.int32(1) << (30 - i))
        return jnp.where(count_ge(cand) >= k_top, cand, thr)

    thr = lax.fori_loop(0, 31, bit_body, thr)
    need = (k_top - (count_ge(thr + 1))).astype(F32)

    log2e = [p.astype(F32) for p in _split3(jnp.full((1, QBLK), LOG2E, F32))]
    ci16 = lax.broadcasted_iota(I32, (2 * SUBLANES, QBLK), 0)

    def pos_coef(h):
        slope = 2.0 ** (-8.0 * (h + 1) / AT_HEADS)
        blk = jnp.zeros((2 * SUBLANES, QBLK), F32)
        for i in range(3):
            blk = jnp.where(ci16 == i, log2e[i] * (POS_RADIX * slope), blk)
            blk = jnp.where(ci16 == 3 + i, log2e[i] * slope, blk)
        return blk.astype(BF16)

    for j in range(AT_HEADS // 2):
        pair = qT_ref[0, j * LANES:(j + 1) * LANES, :]
        rr = lax.broadcasted_iota(I32, pair.shape, 0)
        zeros = jnp.zeros_like(pair)
        qbd_ref[j, :LANES, :QBLK] = jnp.where(rr < AT_DH, pair, zeros)
        qbd_ref[j, :LANES, QBLK:] = jnp.where(rr >= AT_DH, pair, zeros)
        qbd_ref[j, LANES:LANES + 2 * SUBLANES, :QBLK] = pos_coef(2 * j)
        qbd_ref[j, LANES:LANES + 2 * SUBLANES, QBLK:] = pos_coef(2 * j + 1)
        qbd_ref[j, LANES + 2 * SUBLANES:, :] = jnp.zeros((KP_W - LANES - 2 * SUBLANES, 2 * QBLK), BF16)
    acc_ref[...] = jnp.zeros_like(acc_ref)
    m_ref[...] = jnp.full_like(m_ref, -jnp.inf)
    l_ref[...] = jnp.zeros_like(l_ref)

    ri = lax.broadcasted_iota(I32, (KBLK, KBLK), 0)
    ci = lax.broadcasted_iota(I32, (KBLK, KBLK), 1)
    lstrict = jnp.where(ci < ri, 1.0, 0.0).astype(BF16)

    def attn_body(kb, carry, last=False):
        rows = pl.ds(pl.multiple_of(kb * KBLK, KBLK), KBLK)
        kblk = keys_ref[rows, :]
        s_idx = srow + kb * KBLK
        eq = kblk == thr
        eqf = jnp.where(eq, 1.0, 0.0)
        before = _dot(lstrict, eqf.astype(BF16)) + carry
        sel = ((kblk > thr) | (eq & (before < need))) & (s_idx < end_t)
        bias = jnp.where(sel, 0.0, -jnp.inf)
        if last:
            ahead = jnp.maximum(s_idx - t_idx, 0).astype(F32)
        s2s = [_dot(k_ref[0, rows, j * KP_W:(j + 1) * KP_W], qbd_ref[j])
               for j in range(AT_HEADS // 2)]
        for j in range(AT_HEADS // 2):
            ps, alphas = [], []
            for half in range(2):
                h = 2 * j + half
                st = s2s[j][:, half * QBLK:(half + 1) * QBLK] + bias
                if last:
                    st = st - (2.0 * LOG2E * 2.0 ** (-8.0 * (h + 1) / AT_HEADS)) * ahead
                m_old = m_ref[h:h + 1, :]
                m_new = jnp.maximum(m_old, jnp.max(st, axis=0, keepdims=True))
                m_safe = jnp.where(m_new == -jnp.inf, 0.0, m_new)
                alpha = jnp.exp2(m_old - m_safe)
                p = jnp.exp2(st - m_safe)
                l_ref[h:h + 1, :] = alpha * l_ref[h:h + 1, :] + jnp.sum(p, axis=0, keepdims=True)
                m_ref[h:h + 1, :] = m_new
                ps.append(p.astype(BF16))
                alphas.append(alpha)
            o2 = _dot(vT_ref[0, kb, j * LANES:(j + 1) * LANES, :], jnp.concatenate(ps, axis=1))
            for half in range(2):
                hs = slice((2 * j + half) * AT_DH, (2 * j + half + 1) * AT_DH)
                acc_ref[hs, :] = (alphas[half] * acc_ref[hs, :]
                                  + o2[half * AT_DH:(half + 1) * AT_DH, half * QBLK:(half + 1) * QBLK])
        return carry + jnp.sum(eqf, axis=0, keepdims=True)

    ties_seen = lax.fori_loop(0, nkb - 1, attn_body, jnp.zeros((1, QBLK), F32))
    attn_body(nkb - 1, ties_seen, last=True)

    for h in range(AT_HEADS):
        hs = slice(h * AT_DH, (h + 1) * AT_DH)
        acc_ref[hs, :] = acc_ref[hs, :] / l_ref[h:h + 1, :]
    o_ref[0] = acc_ref[...].T.astype(o_ref.dtype)


def _dsa_attn(qT, qiT, wT, kidx, k, vT):
    b, _, s = qT.shape
    nb = s // KBLK
    k_top = min(TOPK_MAX, s // 4)
    return pl.pallas_call(
        functools.partial(_dsa_attn_kernel, k_top=k_top),
        grid=(b, s // QBLK),
        in_specs=[pl.BlockSpec((1, AT_WIDTH, QBLK), lambda i, j: (i, 0, j)),
                  pl.BlockSpec((1, IDX_HEADS * IDX_DH, QBLK), lambda i, j: (i, 0, j)),
                  pl.BlockSpec((1, SUBLANES, QBLK), lambda i, j: (i, 0, j)),
                  pl.BlockSpec((1, s, KIDX_W), lambda i, j: (i, 0, 0)),
                  pl.BlockSpec((1, s, K_W), lambda i, j: (i, 0, 0)),
                  pl.BlockSpec((1, nb, AT_WIDTH, KBLK), lambda i, j: (i, 0, 0, 0))],
        out_specs=pl.BlockSpec((1, QBLK, AT_WIDTH), lambda i, j: (i, j, 0)),
        out_shape=jax.ShapeDtypeStruct((b, s, AT_WIDTH), BF16),
        scratch_shapes=[pltpu.VMEM((s, QBLK), I32),
                        pltpu.VMEM((KIDX_W, IDX_HEADS * QBLK), BF16),
                        pltpu.VMEM((AT_HEADS // 2, KP_W, 2 * QBLK), BF16),
                        pltpu.VMEM((AT_WIDTH, QBLK), F32),
                        pltpu.VMEM((AT_HEADS, QBLK), F32),
                        pltpu.VMEM((AT_HEADS, QBLK), F32)],
        compiler_params=_params(("parallel", "arbitrary")),
        name="dsa_attn",
    )(qT, qiT, wT, kidx, k, vT)


def _layer_norm(v, g, b):
    mu = jnp.mean(v, axis=-1, keepdims=True)
    var = jnp.mean(jnp.square(v - mu), axis=-1, keepdims=True)
    return (v - mu) * lax.rsqrt(var + LN_EPS) * g + b


def _out_ln_kernel(oa_ref, ob_ref, x_ref, wa_ref, wb_ref, g_ref, b_ref, wrT_ref, bias_ref,
                   h_ref, hb_ref, eidx_ref, gT_ref, cnt_ref):
    mix = _dot(oa_ref[...], wa_ref[...]) + _dot(ob_ref[...], wb_ref[...])
    h = _layer_norm(ALPHA * x_ref[...] + mix, g_ref[...], b_ref[...])
    h_ref[...] = h
    hb_ref[...] = h.astype(BF16)
    eidx_ref[...], gT_ref[...], cnt_ref[0] = _route(h, wrT_ref[...], bias_ref[...])


def _out_ln(oa, ob, x2, wa, wb, g, b, wrT, bias_col, tm):
    t, d = x2.shape
    nt = t // tm
    row = lambda w: pl.BlockSpec((tm, w), lambda i: (i, 0))
    full = lambda a: pl.BlockSpec(a.shape, lambda i: (0, 0))
    return pl.pallas_call(
        _out_ln_kernel,
        grid=(nt,),
        in_specs=[row(oa.shape[1]), row(ob.shape[1]), row(d), full(wa), full(wb), full(g), full(b),
                  full(wrT), full(bias_col)],
        out_specs=[row(d), row(d),
                   pl.BlockSpec((TOP_K, tm), lambda i: (0, i)),
                   row(LANES),
                   pl.BlockSpec((1, N_EXPERTS, LANES), lambda i: (i, 0, 0))],
        out_shape=[jax.ShapeDtypeStruct((t, d), F32), jax.ShapeDtypeStruct((t, d), BF16),
                   jax.ShapeDtypeStruct((TOP_K, t), I32),
                   jax.ShapeDtypeStruct((t, LANES), F32),
                   jax.ShapeDtypeStruct((nt, N_EXPERTS, LANES), F32)],
        compiler_params=_params(("parallel",)),
        name="out_ln_route",
    )(oa, ob, x2, wa, wb, g, b, wrT, bias_col)


def _first_max(v, idx, big):
    m = jnp.max(v, axis=0, keepdims=True)
    first = jnp.min(jnp.where(v == m, idx, big), axis=0, keepdims=True)
    return m, first


def _route(h, wrT, bias_col):
    tm = h.shape[0]
    per_group = N_EXPERTS // N_GROUPS
    logits = _dot_nt(_cat_hlh(wrT, 1), _cat_hhl(h, 1))
    scores = _sigmoid(logits)
    sel = scores + bias_col
    iota_g = lax.broadcasted_iota(I32, (per_group, tm), 0)
    neg = -jnp.inf

    grp = jnp.zeros((N_GROUPS, tm), F32)
    iota_grp = lax.broadcasted_iota(I32, (N_GROUPS, tm), 0)
    for g in range(N_GROUPS):
        blk = sel[g * per_group:(g + 1) * per_group, :]
        m1, i1 = _first_max(blk, iota_g, per_group)
        m2 = jnp.max(jnp.where(iota_g == i1, neg, blk), axis=0, keepdims=True)
        grp = jnp.where(iota_grp == g, m1 + m2, grp)

    gmask = jnp.zeros((N_GROUPS, tm), F32)
    work = grp
    for _ in range(TOPK_GROUPS):
        _, gi = _first_max(work, iota_grp, N_GROUPS)
        hit = iota_grp == gi
        gmask = jnp.where(hit, 1.0, gmask)
        work = jnp.where(hit, neg, work)

    e_mask = jnp.concatenate(
        [jnp.broadcast_to(gmask[g:g + 1, :], (per_group, tm)) for g in range(N_GROUPS)], axis=0)
    iota_e = lax.broadcasted_iota(I32, (N_EXPERTS, tm), 0)
    work = jnp.where(e_mask > 0.5, sel, neg)
    iota_k = lax.broadcasted_iota(I32, (TOP_K, tm), 0)
    eidx = jnp.zeros((TOP_K, tm), I32)
    w = jnp.zeros((N_EXPERTS, tm), F32)
    chosen = jnp.zeros((N_EXPERTS, tm), F32)
    for k in range(TOP_K):
        _, ei = _first_max(work, iota_e, N_EXPERTS)
        hit = iota_e == ei
        eidx = jnp.where(iota_k == k, ei, eidx)
        w = jnp.where(hit, scores, w)
        chosen = jnp.where(hit, 1.0, chosen)
        work = jnp.where(hit, neg, work)

    gates = w / jnp.sum(w, axis=0, keepdims=True) * ROUTED_SCALE
    gates = jnp.concatenate([gates, jnp.zeros((LANES - N_EXPERTS, tm), F32)], axis=0)
    cnt = jnp.broadcast_to(jnp.sum(chosen, axis=1, keepdims=True), (N_EXPERTS, LANES))
    return eidx, gates.T, cnt


MOE_TM = 512
SEG = 16
EXP_TM = 768
ROW_BUF = TOP_K * MOE_TM + N_EXPERTS * SEG
CH_MAX = ROW_BUF // SEG
P_BLK = 512
G_BLK = 512


def _moe_plan(cnt, n_exp_tiles):
    nt = cnt.shape[0]
    n16 = (cnt + (SEG - 1)) // SEG
    so16 = jnp.cumsum(n16, axis=1) - n16
    r16 = jnp.sum(n16, axis=0)
    per = EXP_TM // SEG
    rp16 = (r16 + (per - 1)) // per * per
    ends = jnp.cumsum(rp16)
    off16 = ends - rp16
    go16 = off16[None, :] + jnp.cumsum(n16, axis=0) - n16
    n_et = (ends[-1] // per).astype(I32).reshape(1)
    tiles = jnp.arange(n_exp_tiles, dtype=I32)
    tile_expert = jnp.sum((ends // per)[None, :] <= tiles[:, None], axis=1)
    tile_expert = jnp.minimum(tile_expert, N_EXPERTS - 1).astype(I32)
    def copy_list(per_seg, first_chunk, step, size):
        start = jnp.cumsum(per_seg, axis=1) - per_seg
        k = jnp.arange(size, dtype=I32)[None, :, None]
        mine = (start[:, None, :] <= k) & (k < (start + per_seg)[:, None, :])
        rel = first_chunk[:, None, :] + (k - start[:, None, :]) * step
        pick = lambda base: jnp.sum(jnp.where(mine, base[:, None, :] + rel, 0), axis=2)
        flat = lambda a: a.reshape(nt * size).astype(I32)
        return flat(pick(so16)), flat(pick(go16)), jnp.sum(per_seg, axis=1).astype(I32)

    n_big = n16 // BIG
    big = copy_list(n_big, jnp.zeros_like(n16), BIG, BIG_MAX)
    small = copy_list(n16 - n_big * BIG, n_big * BIG, 1, SMALL_MAX)
    copies = (big[0], big[1], small[0], small[1], big[2], small[2])
    twice = lambda a: jnp.concatenate([a, a], axis=1).astype(F32)
    seg_lo, seg_hi = twice(so16 * SEG), twice((so16 + n16) * SEG)
    as_rows = lambda a: jnp.broadcast_to(a[:, None, :], (nt, SUBLANES, 2 * N_EXPERTS))
    as_cols = lambda a: jnp.broadcast_to(a[:, :, None], (nt, 2 * N_EXPERTS, LANES))
    return dict(copies=copies,
                rtot16=jnp.sum(n16, axis=1).astype(I32), r16=r16.astype(I32), rp16=rp16.astype(I32),
                off16=off16.astype(I32), n_et=n_et, tile_expert=tile_expert,
                lo_rows=as_rows(seg_lo), hi_rows=as_rows(seg_hi),
                lo_cols=as_cols(seg_lo), hi_cols=as_cols(seg_hi))


WAIT_BATCH = 16
BIG = 4
BIG_MAX = CH_MAX // BIG
SMALL_MAX = N_EXPERTS * (BIG - 1)


def _chunk(ref, c, n=1):
    return ref.at[pl.ds(pl.multiple_of(c * SEG, SEG), n * SEG)]


def _segment_copies(i, copies, rtot_ref, make_copy):
    bsrc_ref, bdst_ref, ssrc_ref, sdst_ref, nbig_ref, nsmall_ref = copies

    def big_body(k, carry):
        make_copy(bsrc_ref[i * BIG_MAX + k], bdst_ref[i * BIG_MAX + k], BIG).start()
        return carry

    def small_body(k, carry):
        make_copy(ssrc_ref[i * SMALL_MAX + k], sdst_ref[i * SMALL_MAX + k], 1).start()
        return carry

    lax.fori_loop(0, nbig_ref[i], big_body, 0)
    lax.fori_loop(0, nsmall_ref[i], small_body, 0)
    return rtot_ref[i]


def _wait_copies(count, make_copy):
    for n, trips in ((WAIT_BATCH, count // WAIT_BATCH), (1, count % WAIT_BATCH)):
        def body(c, carry, n=n):
            make_copy(0, 0, n).wait()
            return carry

        lax.fori_loop(0, trips, body, 0)


POS_SPLIT = 64


def _dispatch_kernel(bsrc_ref, bdst_ref, ssrc_ref, sdst_ref, nbig_ref, nsmall_ref,
                     rtot_ref, r16_ref, rp16_ref, off16_ref,
                     x_ref, eidx_ref, locol_ref, lorow_ref, hirow_ref,
                     xs_hbm, posT_ref, buf_ref, zero_ref, pend_ref, sem):
    i = pl.program_id(0)
    slot = lax.rem(i, 2)
    buf = buf_ref.at[slot]
    tm = x_ref.shape[0]
    eidx = eidx_ref[...]
    iota_e = lax.broadcasted_iota(I32, (N_EXPERTS, tm), 0)
    onehot = jnp.zeros((N_EXPERTS, tm), F32)
    for k in range(TOP_K):
        onehot = jnp.where(iota_e == eidx[k:k + 1, :], 1.0, onehot)
    earlier = lax.broadcasted_iota(I32, (tm, tm), 0) < lax.broadcasted_iota(I32, (tm, tm), 1)
    rank = _dot(onehot.astype(BF16), jnp.where(earlier, 1.0, 0.0).astype(BF16))
    pos1 = jnp.where(onehot > 0.5, locol_ref[0][:N_EXPERTS, 0:1] + rank + 1.0, 0.0)
    pos_hi = jnp.floor(pos1 * (1.0 / POS_SPLIT)) * POS_SPLIT
    pos2 = jnp.concatenate([pos_hi, pos1 - pos_hi], axis=0)
    posT_ref[...] = pos2.T
    pos2b = pos2.astype(BF16)

    x = x_ref[...]
    seg_lo = lorow_ref[0][0:1, :]
    seg_hi = hirow_ref[0][0:1, :]
    n_blk = (rtot_ref[i] * SEG + (P_BLK - 1)) // P_BLK

    def blk_body(rb, carry):
        r0 = pl.multiple_of(rb * P_BLK, P_BLK)
        r_e = (lax.broadcasted_iota(I32, (P_BLK, 2 * N_EXPERTS), 0) + r0).astype(F32)
        owner = jnp.where((r_e >= seg_lo) & (r_e < seg_hi), 1.0, 0.0).astype(BF16)
        want = _dot(owner, pos2b)
        r_t = (lax.broadcasted_iota(I32, (P_BLK, tm), 0) + (r0 + 1)).astype(F32)
        p = jnp.where(want == r_t, 1.0, 0.0).astype(BF16)
        buf[pl.ds(r0, P_BLK), :] = _dot(p, x).astype(BF16)
        return carry

    lax.fori_loop(0, n_blk, blk_body, 0)

    @pl.when(i > 0)
    def _():
        prev_copy = lambda src, dst, n=1: pltpu.make_async_copy(
            _chunk(buf_ref.at[1 - slot], src, n), _chunk(xs_hbm, dst, n), sem.at[1 - slot])
        _wait_copies(pend_ref[0], prev_copy)

    out_copy = lambda src, dst, n=1: pltpu.make_async_copy(_chunk(buf, src, n), _chunk(xs_hbm, dst, n),
                                                           sem.at[slot])
    copies = (bsrc_ref, bdst_ref, ssrc_ref, sdst_ref, nbig_ref, nsmall_ref)
    pend_ref[0] = _segment_copies(i, copies, rtot_ref, out_copy)

    @pl.when(i == pl.num_programs(0) - 1)
    def _():
        _wait_copies(pend_ref[0], out_copy)
        zero_ref[...] = jnp.zeros_like(zero_ref)
        zero_copy = lambda src, dst, n=1: pltpu.make_async_copy(
            _chunk(zero_ref, 0, n), _chunk(xs_hbm, dst, n), sem.at[slot])

        def zero_chunk(c, carry):
            zero_copy(0, c).start()
            return carry

        def tail_body(e, total):
            lax.fori_loop(off16_ref[e] + r16_ref[e], off16_ref[e] + rp16_ref[e], zero_chunk, 0)
            return total + rp16_ref[e] - r16_ref[e]

        _wait_copies(lax.fori_loop(0, N_EXPERTS, tail_body, 0), zero_copy)

        per = EXP_TM // SEG
        first = (off16_ref[N_EXPERTS - 1] + rp16_ref[N_EXPERTS - 1]) // per
        n_tiles = xs_hbm.shape[0] // EXP_TM
        tile_copy = lambda tile: pltpu.make_async_copy(
            zero_ref, xs_hbm.at[pl.ds(pl.multiple_of(tile * EXP_TM, EXP_TM), EXP_TM)], sem.at[slot])

        def zero_tile(tile, carry):
            tile_copy(tile).start()
            return carry

        def wait_tile(tile, carry):
            tile_copy(0).wait()
            return carry

        lax.fori_loop(first, n_tiles, zero_tile, 0)
        lax.fori_loop(first, n_tiles, wait_tile, 0)


def _dispatch(hb, eidx, plan, n_rows):
    t, d = hb.shape
    nt = t // MOE_TM
    grid_spec = pltpu.PrefetchScalarGridSpec(
        num_scalar_prefetch=10,
        grid=(nt,),
        in_specs=[pl.BlockSpec((MOE_TM, d), lambda i, *_: (i, 0)),
                  pl.BlockSpec((TOP_K, MOE_TM), lambda i, *_: (0, i)),
                  pl.BlockSpec((1, 2 * N_EXPERTS, LANES), lambda i, *_: (i, 0, 0)),
                  pl.BlockSpec((1, SUBLANES, 2 * N_EXPERTS), lambda i, *_: (i, 0, 0)),
                  pl.BlockSpec((1, SUBLANES, 2 * N_EXPERTS), lambda i, *_: (i, 0, 0))],
        out_specs=[pl.BlockSpec(memory_space=pl.ANY),
                   pl.BlockSpec((MOE_TM, LANES), lambda i, *_: (i, 0))],
        scratch_shapes=[pltpu.VMEM((2, ROW_BUF, d), BF16), pltpu.VMEM((EXP_TM, d), BF16),
                        pltpu.SMEM((1,), I32), pltpu.SemaphoreType.DMA((2,))])
    return pl.pallas_call(
        _dispatch_kernel,
        grid_spec=grid_spec,
        out_shape=[jax.ShapeDtypeStruct((n_rows, d), BF16),
                   jax.ShapeDtypeStruct((t, LANES), F32)],
        compiler_params=_params(("arbitrary",)),
        name="moe_dispatch",
    )(*plan["copies"], plan["rtot16"], plan["r16"], plan["rp16"],
      plan["off16"], hb, eidx, plan["lo_cols"], plan["lo_rows"], plan["hi_rows"])


def _experts_kernel(te_ref, net_ref, x_ref, wg_ref, wu_ref, wd_ref, y_ref, wgb_ref, wub_ref, wdb_ref):
    j = pl.program_id(0)
    used = j < net_ref[0]

    @pl.when(used & ((j == 0) | (te_ref[j] != te_ref[jnp.maximum(j - 1, 0)])))
    def _():
        wgb_ref[...] = wg_ref[0].astype(BF16)
        wub_ref[...] = wu_ref[0].astype(BF16)
        wdb_ref[...] = wd_ref[0].astype(BF16)

    @pl.when(used)
    def _():
        x = x_ref[...]
        hmid = _silu(_dot(x, wgb_ref[...])) * _dot(x, wub_ref[...])
        y_ref[...] = _dot(hmid.astype(BF16), wdb_ref[...]).astype(y_ref.dtype)


def _experts(xs, wg, wu, wd, plan):
    n_rows, d = xs.shape
    row_map = lambda j, te, net: (jnp.minimum(j, net[0] - 1), 0)
    w_map = lambda j, te, net: (te[jnp.minimum(j, net[0] - 1)], 0, 0)
    grid_spec = pltpu.PrefetchScalarGridSpec(
        num_scalar_prefetch=2,
        grid=(n_rows // EXP_TM,),
        in_specs=[pl.BlockSpec((EXP_TM, d), row_map),
                  pl.BlockSpec((1, d, D_EXPERT), w_map),
                  pl.BlockSpec((1, d, D_EXPERT), w_map),
                  pl.BlockSpec((1, D_EXPERT, d), w_map)],
        out_specs=pl.BlockSpec((EXP_TM, d), row_map),
        scratch_shapes=[pltpu.VMEM((d, D_EXPERT), BF16), pltpu.VMEM((d, D_EXPERT), BF16),
                        pltpu.VMEM((D_EXPERT, d), BF16)])
    return pl.pallas_call(
        _experts_kernel,
        grid_spec=grid_spec,
        out_shape=jax.ShapeDtypeStruct((n_rows, d), BF16),
        input_output_aliases={2: 0},
        compiler_params=_params(("arbitrary",)),
        name="moe_experts",
    )(plan["tile_expert"], plan["n_et"], xs, wg, wu, wd)


def _combine_kernel(bsrc_ref, bdst_ref, ssrc_ref, sdst_ref, nbig_ref, nsmall_ref, rtot_ref,
                    ys_hbm, posT_ref, gT_ref, locol_ref, hicol_ref, h_ref,
                    sg_ref, su_ref, sd_ref, lg_ref, lb_ref, o_ref, ybuf_ref, acc_ref, pend_ref, sem):
    i = pl.program_id(0)
    slot = lax.rem(i, 2)
    ybuf = ybuf_ref.at[slot]
    tm = h_ref.shape[0]

    def fetch(tile, to_slot):
        copy = lambda dst, src, n=1: pltpu.make_async_copy(
            _chunk(ys_hbm, src, n), _chunk(ybuf_ref.at[to_slot], dst, n), sem.at[to_slot])
        copies = (bsrc_ref, bdst_ref, ssrc_ref, sdst_ref, nbig_ref, nsmall_ref)
        pend_ref[to_slot] = _segment_copies(tile, copies, rtot_ref, copy)

    @pl.when(i == 0)
    def _():
        ybuf_ref[...] = jnp.zeros_like(ybuf_ref)
        fetch(0, 0)

    @pl.when(i + 1 < pl.num_programs(0))
    def _():
        fetch(i + 1, 1 - slot)

    h = h_ref[...]
    xb = h.astype(BF16)
    hs = _silu(_dot(xb, sg_ref[...])) * _dot(xb, su_ref[...])
    acc_ref[...] = ALPHA * h + _dot(hs.astype(BF16), sd_ref[...])
    _wait_copies(pend_ref[slot], lambda dst, src, n=1: pltpu.make_async_copy(
        _chunk(ys_hbm, src, n), _chunk(ybuf, dst, n), sem.at[slot]))

    pos2b = posT_ref[...].astype(BF16)
    g_hi, g_lo = _split(gT_ref[...])
    gate2b = (g_hi.astype(F32) + pltpu.roll(g_lo.astype(F32), N_EXPERTS, 1)).astype(BF16)
    seg_lo = jnp.broadcast_to(locol_ref[0][:, 0:1], (2 * N_EXPERTS, G_BLK))
    seg_hi = jnp.broadcast_to(hicol_ref[0][:, 0:1], (2 * N_EXPERTS, G_BLK))
    n_blk = (rtot_ref[i] * SEG + (G_BLK - 1)) // G_BLK

    def blk_body(cb, carry):
        c0 = pl.multiple_of(cb * G_BLK, G_BLK)
        c_e = (lax.broadcasted_iota(I32, (2 * N_EXPERTS, G_BLK), 1) + c0).astype(F32)
        owner = jnp.where((c_e >= seg_lo) & (c_e < seg_hi), 1.0, 0.0).astype(BF16)
        want = _dot(pos2b, owner)
        gate = _dot(gate2b, owner)
        c_t = (lax.broadcasted_iota(I32, (tm, G_BLK), 1) + (c0 + 1)).astype(F32)
        g = jnp.where(want == c_t, gate, 0.0).astype(BF16)
        acc_ref[...] += _dot(g, ybuf[pl.ds(c0, G_BLK), :])
        return carry

    lax.fori_loop(0, n_blk, blk_body, 0)
    o_ref[...] = _layer_norm(acc_ref[...], lg_ref[...], lb_ref[...])


def _combine(ys, posT, gT, h1, sg, su, sd, lg, lb, plan):
    t, d = h1.shape
    nt = t // MOE_TM
    row = lambda w: pl.BlockSpec((MOE_TM, w), lambda i, *_: (i, 0))
    full = lambda a: pl.BlockSpec(a.shape, lambda i, *_: (0, 0))
    seg = pl.BlockSpec((1, 2 * N_EXPERTS, LANES), lambda i, *_: (i, 0, 0))
    grid_spec = pltpu.PrefetchScalarGridSpec(
        num_scalar_prefetch=7,
        grid=(nt,),
        in_specs=[pl.BlockSpec(memory_space=pl.ANY), row(LANES), row(LANES), seg, seg, row(d),
                  full(sg), full(su), full(sd), full(lg), full(lb)],
        out_specs=row(d),
        scratch_shapes=[pltpu.VMEM((2, ROW_BUF, d), BF16), pltpu.VMEM((MOE_TM, d), F32),
                        pltpu.SMEM((2,), I32), pltpu.SemaphoreType.DMA((2,))])
    return pl.pallas_call(
        _combine_kernel,
        grid_spec=grid_spec,
        out_shape=jax.ShapeDtypeStruct((t, d), F32),
        compiler_params=_params(("arbitrary",)),
        name="moe_combine",
    )(*plan["copies"], plan["rtot16"], ys, posT, gT,
      plan["lo_cols"], plan["hi_cols"], h1, sg, su, sd, lg, lb)


def _moe(h1, hb, eidx, gT, cnt, wg, wu, wd, sg, su, sd, lg, lb):
    t = h1.shape[0]
    nt = t // MOE_TM
    max_rows = nt * ROW_BUF + N_EXPERTS * EXP_TM
    n_exp_tiles = pl.cdiv(max_rows, EXP_TM)
    plan = _moe_plan(cnt[:, :, 0].astype(I32), n_exp_tiles)
    xs, posT = _dispatch(hb, eidx, plan, n_exp_tiles * EXP_TM)
    ys = _experts(xs, wg, wu, wd, plan)
    return _combine(ys, posT, gT, h1, sg, su, sd, lg, lb, plan)


def kernel(x, w_in, hg_lb_logits, hg_norm_g, q_norm_g, w_q_up, w_qidx_up, kv_norm_g, w_kv_up,
           idx_ln_g, idx_ln_b, w_out, ln1_g, ln1_b, w_router, router_bias, w_e_gate, w_e_up,
           w_e_down, w_s_gate, w_s_up, w_s_down, ln2_g, ln2_b):
    b, s, d = x.shape
    h = x
    for l in range(DEPTH):
        h2 = h.reshape(b * s, d)
        n_main = 4 * HG_WIDTH
        w_main = w_in[l, :, :n_main].astype(BF16)
        c_kv, c_idx = n_main + Q_RANK, n_main + Q_RANK + KV_RANK
        w_sel = jnp.concatenate([w_in[l, :, n_main:c_kv], w_in[l, :, c_idx:]], axis=1)
        w_sel = jnp.pad(w_sel, ((0, 0), (0, SEL_W - w_sel.shape[1])))
        w_ckv = w_in[l, :, c_kv:c_idx].astype(BF16)

        o_a = _hgrn(h, w_main, hg_lb_logits, hg_norm_g[l], l)

        wqT = w_q_up[l].T.astype(BF16)
        wqiT = w_qidx_up[l].T
        wk = w_kv_up[l][:, :AT_WIDTH].astype(BF16)
        wvT = w_kv_up[l][:, AT_WIDTH:].T.astype(BF16)
        qT, qiT, k, vT, kidx, wT = _dsa_prep(h, w_sel, w_ckv, q_norm_g[l], kv_norm_g[l],
                                             idx_ln_g[l], idx_ln_b[l], wqT, wqiT, wk, wvT)
        o_b = _dsa_attn(qT, qiT, wT, kidx, k, vT)

        h1, hb, eidx, gT, cnt = _out_ln(
            o_a.reshape(b * s, HG_WIDTH), o_b.reshape(b * s, AT_WIDTH), h2,
            w_out[l, :HG_WIDTH].astype(BF16), w_out[l, HG_WIDTH:].astype(BF16),
            ln1_g[l].reshape(1, d), ln1_b[l].reshape(1, d),
            w_router[l].T, router_bias[l].reshape(N_EXPERTS, 1), MOE_TM)
        out = _moe(h1, hb, eidx, gT, cnt, w_e_gate[l], w_e_up[l], w_e_down[l],
                   w_s_gate[l].astype(BF16), w_s_up[l].astype(BF16), w_s_down[l].astype(BF16),
                   ln2_g[l].reshape(1, d), ln2_b[l].reshape(1, d))
        h = out.reshape(b, s, d)
    return h
```

```python
import functools

import jax
import jax.numpy as jnp
from jax import lax
from jax.experimental import pallas as pl
from jax.experimental.pallas import tpu as pltpu

F32 = jnp.float32
BF16 = jnp.bfloat16
I32 = jnp.int32
HIGHEST = lax.Precision.HIGHEST

CHUNK = 64
HG_HEADS = 4
HG_DK = 128
HG_DV = 128
HG_WIDTH = HG_HEADS * HG_DV
AT_HEADS = 8
AT_DH = 64
AT_WIDTH = AT_HEADS * AT_DH
Q_RANK = 256
KV_RANK = 128
IDX_HEADS = 4
IDX_DH = 64
TOPK_MAX = 256
N_EXPERTS = 64
TOP_K = 8
N_GROUPS = 8
TOPK_GROUPS = 4
D_EXPERT = 256
ROUTED_SCALE = 2.5
DEPTH = 1
ALPHA = (2.0 * DEPTH) ** 0.25
LN_EPS = 1e-5
RMS_EPS = 1e-6
LOG2E = 1.4426950408889634

LANES = 128
SUBLANES = 8
QBLK = 4 * CHUNK
KBLK = 256
VMEM_LIMIT = 56 * 1024 * 1024
INT_MIN = -2 ** 31


def _params(sem, vmem=VMEM_LIMIT):
    return pltpu.CompilerParams(dimension_semantics=sem, vmem_limit_bytes=vmem)


def _dot(a, b, precision=None):
    return jnp.dot(a, b, preferred_element_type=F32, precision=precision)


def _dot_nt(a, b, precision=None):
    return lax.dot_general(a, b, (((1,), (1,)), ((), ())), preferred_element_type=F32,
                           precision=precision)


def _dot_tn(a, b, precision=None):
    return lax.dot_general(a, b, (((0,), (0,)), ((), ())), preferred_element_type=F32,
                           precision=precision)


def _split(x):
    hi = x.astype(BF16)
    return hi, (x - hi.astype(F32)).astype(BF16)


def _cat_hhl(x, axis):
    hi, lo = _split(x)
    return jnp.concatenate([hi, hi, lo], axis=axis)


def _cat_hlh(x, axis):
    hi, lo = _split(x)
    return jnp.concatenate([hi, lo, hi], axis=axis)


def _split3(x):
    hi = x.astype(BF16)
    r = x - hi.astype(F32)
    mid = r.astype(BF16)
    return hi, mid, (r - mid.astype(F32)).astype(BF16)


def _cat_act6(x, axis):
    h, m, l = _split3(x)
    return jnp.concatenate([h, h, m, h, l, m], axis=axis)


def _cat_wgt6(x, axis):
    h, m, l = _split3(x)
    return jnp.concatenate([h, m, h, l, h, m], axis=axis)


def _sigmoid(x):
    return 1.0 / (1.0 + jnp.exp(-x))


def _silu(x):
    return x * _sigmoid(x)


def _hgrn_kernel(x_ref, w_ref, lbl_ref, ng_ref, o_ref, p_ref, st_ref, *, n_chunks, layer):
    @pl.when(pl.program_id(1) == 0)
    def _():
        st_ref[...] = jnp.zeros_like(st_ref)

    for sq in range(x_ref.shape[0]):
        p_ref[sq] = _dot(x_ref[sq].astype(BF16), w_ref[...])

    lg = lbl_ref[...]
    ex = jnp.exp(lg - jnp.max(lg, axis=0, keepdims=True))
    lb_all = jnp.sum(ex[: layer + 1], axis=0, keepdims=True) / jnp.sum(ex, axis=0, keepdims=True)
    ng = ng_ref[...]

    r = lax.broadcasted_iota(I32, (CHUNK, CHUNK), 0)
    c = lax.broadcasted_iota(I32, (CHUNK, CHUNK), 1)
    causal = c <= r
    tri = jnp.where(causal, 1.0, 0.0).astype(BF16)
    tri2 = jnp.concatenate([tri, tri], axis=1)

    units = [(sq, h) for sq in range(p_ref.shape[0]) for h in range(HG_HEADS)]

    def chunk_body(j, carry):
        rows = pl.ds(pl.multiple_of(j * CHUNK, CHUNK), CHUNK)
        ks, bs = [], []
        for sq, h in units:
            lo = h * HG_DK
            lb = lb_all[:, lo:lo + HG_DK]
            f = lb + (1.0 - lb) * _sigmoid(p_ref[sq, rows, HG_WIDTH + lo:HG_WIDTH + lo + HG_DK])
            lf_hi, lf_lo = _split(jnp.log(f))
            ks.append(1.0 - f)
            bs.append(_dot(tri2, jnp.concatenate([lf_hi, lf_lo], axis=0)))
        scs, ois, vbs = [], [], []
        for (sq, h), k, b in zip(units, ks, bs):
            lo = h * HG_DK
            b_last = b[CHUNK - 1:CHUNK, :]
            q_dec = (_silu(p_ref[sq, rows, lo:lo + HG_DK]) * jnp.exp(b)).astype(BF16)
            k_inv = (k * jnp.exp(-b)).astype(BF16)
            k_dec = (k * jnp.exp(b_last - b)).astype(BF16)
            vb = p_ref[sq, rows, 2 * HG_WIDTH + lo:2 * HG_WIDTH + lo + HG_DV].astype(BF16)
            st = st_ref[sq, h]
            scs.append(jnp.where(causal, _dot_nt(q_dec, k_inv), 0.0).astype(BF16))
            ois.append(_dot_nt(q_dec, st.astype(BF16)))
            st_ref[sq, h] = st * jnp.exp(b_last) + _dot_tn(vb, k_dec)
            vbs.append(vb)
        for (sq, h), sc, oi, vb in zip(units, scs, ois, vbs):
            lo = h * HG_DK
            o = _dot(sc, vb) + oi
            o = o * lax.rsqrt(jnp.mean(o * o, axis=-1, keepdims=True) + RMS_EPS) * ng
            gate = p_ref[sq, rows, 3 * HG_WIDTH + lo:3 * HG_WIDTH + lo + HG_DV]
            o_ref[sq, rows, lo:lo + HG_DV] = (o * _silu(gate)).astype(o_ref.dtype)
        return carry

    lax.fori_loop(0, n_chunks, chunk_body, 0)


def _hgrn(x, w_main, lb_logits, norm_g, layer, ct=512, n_seq=2):
    b, s, d = x.shape
    w = w_main.shape[1]
    n_seq = min(n_seq, b)
    ct = min(ct, s)
    n_chunks = ct // CHUNK
    return pl.pallas_call(
        functools.partial(_hgrn_kernel, n_chunks=n_chunks, layer=layer),
        grid=(b // n_seq, s // ct),
        in_specs=[pl.BlockSpec((n_seq, ct, d), lambda i, j: (i, j, 0)),
                  pl.BlockSpec(w_main.shape, lambda i, j: (0, 0)),
                  pl.BlockSpec(lb_logits.shape, lambda i, j: (0, 0)),
                  pl.BlockSpec((1, HG_DV), lambda i, j: (0, 0))],
        out_specs=pl.BlockSpec((n_seq, ct, HG_WIDTH), lambda i, j: (i, j, 0)),
        out_shape=jax.ShapeDtypeStruct((b, s, HG_WIDTH), BF16),
        scratch_shapes=[pltpu.VMEM((n_seq, ct, w), F32),
                        pltpu.VMEM((n_seq, HG_HEADS, HG_DV, HG_DK), F32)],
        compiler_params=_params(("parallel", "arbitrary")),
        name="hgrn",
    )(x, w_main, lb_logits, norm_g.reshape(1, HG_DV))


POS_SHIFT = 4
POS_RADIX = 1 << POS_SHIFT
KP_W = 2 * LANES
K_W = (AT_HEADS // 2) * KP_W
SEL_W = Q_RANK + LANES
KIDX_W = 6 * IDX_DH


def _dsa_prep_kernel(x_ref, wsel_ref, wckv_ref, qg_ref, kvg_ref, lng_ref, lnb_ref, wqT_ref, wqiT_ref,
                     wk_ref, wvT_ref, qT_ref, qiT_ref, k_ref, vT_ref, kidx_ref, wT_ref, w6_ref):
    @pl.when((pl.program_id(0) == 0) & (pl.program_id(1) == 0))
    def _():
        w6_ref[...] = _cat_wgt6(wsel_ref[...], 0)

    x = x_ref[...]
    sel = _dot(_cat_act6(x, 1), w6_ref[...])
    cq = sel[:, :Q_RANK]
    cqn = cq * lax.rsqrt(jnp.mean(cq * cq, axis=-1, keepdims=True) + RMS_EPS) * qg_ref[...]
    qT_ref[0] = (_dot_nt(wqT_ref[...], cqn.astype(BF16)) * (AT_DH ** -0.5 * LOG2E)).astype(qT_ref.dtype)
    qiT_ref[0] = _dot_nt(_cat_wgt6(wqiT_ref[...], 1), _cat_act6(cqn, 1))
    ckv = _dot(x.astype(BF16), wckv_ref[...])
    ckvn = (ckv * lax.rsqrt(jnp.mean(ckv * ckv, axis=-1, keepdims=True) + RMS_EPS)
            * kvg_ref[...]).astype(BF16)
    kmat = _dot(ckvn, wk_ref[...])
    tm = kmat.shape[0]
    s_abs = pl.program_id(1) * tm + lax.broadcasted_iota(I32, (tm, LANES), 0)
    lane = lax.broadcasted_iota(I32, (tm, LANES), 1)
    pos = jnp.where(lane < 3, s_abs >> POS_SHIFT,
                    jnp.where(lane < 6, s_abs & (POS_RADIX - 1), 0)).astype(F32)
    k_ref[0] = jnp.concatenate(
        [blk for j in range(AT_HEADS // 2) for blk in (kmat[:, j * LANES:(j + 1) * LANES], pos)],
        axis=1).astype(k_ref.dtype)
    for kb in range(vT_ref.shape[1]):
        vT_ref[0, kb] = _dot_nt(wvT_ref[...], ckvn[kb * KBLK:(kb + 1) * KBLK]).astype(vT_ref.dtype)
    k128 = sel[:, Q_RANK:]
    in_k = lax.broadcasted_iota(I32, (1, LANES), 1) < IDX_DH
    mu = jnp.sum(jnp.where(in_k, k128, 0.0), axis=-1, keepdims=True) * (1.0 / IDX_DH)
    dev = jnp.where(in_k, k128 - mu, 0.0)
    var = jnp.sum(dev * dev, axis=-1, keepdims=True) * (1.0 / IDX_DH)
    y = dev * lax.rsqrt(var + LN_EPS) * lng_ref[...] + lnb_ref[...]
    h, m, l = (p.astype(F32) for p in _split3(y))
    up = lambda p: pltpu.roll(p, IDX_DH, 1)
    kidx_ref[0] = jnp.concatenate([h + up(h), m + up(h), l + up(m)], axis=1).astype(BF16)
    tT = k128.T
    wT_ref[0] = tT[IDX_DH:IDX_DH + SUBLANES, :] * (IDX_HEADS ** -0.5 * IDX_DH ** -0.5)


def _dsa_prep(x, w_sel, w_ckv, q_norm_g, kv_norm_g, idx_ln_g, idx_ln_b, wqT, wqiT, wk, wvT):
    b, s, d = x.shape
    tm = min(2 * KBLK, s)
    kb_per = tm // KBLK
    nb = s // KBLK
    full = lambda a: pl.BlockSpec(a.shape, lambda i, j: (0,) * a.ndim)
    pad_lanes = lambda v: jnp.pad(v.reshape(1, -1), ((0, 0), (0, LANES - v.shape[0])))
    args = (w_sel, w_ckv, q_norm_g.reshape(1, -1), kv_norm_g.reshape(1, -1),
            pad_lanes(idx_ln_g), pad_lanes(idx_ln_b), wqT, wqiT, wk, wvT)
    return pl.pallas_call(
        _dsa_prep_kernel,
        grid=(b, s // tm),
        in_specs=[pl.BlockSpec((None, tm, d), lambda i, j: (i, j, 0))] + [full(a) for a in args],
        out_specs=[pl.BlockSpec((1, AT_WIDTH, tm), lambda i, j: (i, 0, j)),
                   pl.BlockSpec((1, IDX_HEADS * IDX_DH, tm), lambda i, j: (i, 0, j)),
                   pl.BlockSpec((1, tm, K_W), lambda i, j: (i, j, 0)),
                   pl.BlockSpec((1, kb_per, AT_WIDTH, KBLK), lambda i, j: (i, j, 0, 0)),
                   pl.BlockSpec((1, tm, KIDX_W), lambda i, j: (i, j, 0)),
                   pl.BlockSpec((1, SUBLANES, tm), lambda i, j: (i, 0, j))],
        out_shape=[jax.ShapeDtypeStruct((b, AT_WIDTH, s), BF16),
                   jax.ShapeDtypeStruct((b, IDX_HEADS * IDX_DH, s), F32),
                   jax.ShapeDtypeStruct((b, s, K_W), BF16),
                   jax.ShapeDtypeStruct((b, nb, AT_WIDTH, KBLK), BF16),
                   jax.ShapeDtypeStruct((b, s, KIDX_W), BF16),
                   jax.ShapeDtypeStruct((b, SUBLANES, s), F32)],
        scratch_shapes=[pltpu.VMEM((6 * d, SEL_W), BF16)],
        compiler_params=_params(("arbitrary", "arbitrary")),
        name="dsa_prep",
    )(x, *args)


def _sortable(x):
    bits = lax.bitcast_convert_type(x, I32)
    return jnp.where(bits < 0, bits ^ jnp.int32(0x7FFFFFFF), bits)


def _dsa_attn_kernel(qT_ref, qiT_ref, wT_ref, kidx_ref, k_ref, vT_ref, o_ref,
                     keys_ref, q2_ref, qbd_ref, acc_ref, m_ref, l_ref, *, k_top):
    qb = pl.program_id(1)
    t0 = qb * QBLK
    e_max = t0 + QBLK
    nkb = (e_max + KBLK - 1) >> 8

    lane = lax.broadcasted_iota(I32, (1, QBLK), 1)
    t_idx = t0 + lane
    end_t = ((t_idx >> 6) + 1) * CHUNK
    srow = lax.broadcasted_iota(I32, (KBLK, QBLK), 0)

    for h in range(IDX_HEADS):
        q2_ref[:, h * QBLK:(h + 1) * QBLK] = _cat_wgt6(qiT_ref[0, h * IDX_DH:(h + 1) * IDX_DH, :], 0)
    wrow = jnp.concatenate([wT_ref[0, h:h + 1, :] for h in range(IDX_HEADS)], axis=1)

    def score_body(kb, carry):
        rows = pl.ds(pl.multiple_of(kb * KBLK, KBLK), KBLK)
        logits = _dot(kidx_ref[0, rows, :], q2_ref[...])
        r = jnp.maximum(logits, 0.0) * wrow
        sc = r[:, :QBLK]
        for h in range(1, IDX_HEADS):
            sc = sc + r[:, h * QBLK:(h + 1) * QBLK]
        valid = (srow + kb * KBLK) < end_t
        keys_ref[rows, :] = _sortable(jnp.where(valid, sc, -jnp.inf))
        return carry

    lax.fori_loop(0, nkb, score_body, 0)

    def count_ge(cand):
        def body(kb, acc):
            rows = pl.ds(pl.multiple_of(kb * KBLK, KBLK), KBLK)
            hit = jnp.where(keys_ref[rows, :] >= cand, 1, 0).astype(I32)
            return acc + jnp.sum(hit.reshape(KBLK // SUBLANES, SUBLANES, QBLK), axis=0)
        acc = lax.fori_loop(0, nkb, body, jnp.zeros((SUBLANES, QBLK), I32))
        return jnp.sum(acc, axis=0, keepdims=True)

    zero = jnp.zeros((1, QBLK), I32)
    thr = jnp.where(count_ge(zero) >= k_top, zero, jnp.full((1, QBLK), INT_MIN, I32))

    def bit_body(i, thr):
        cand = thr | (jnp.int32(1) << (30 - i))
        return jnp.where(count_ge(cand) >= k_top, cand, thr)

    thr = lax.fori_loop(0, 31, bit_body, thr)
    need = (k_top - (count_ge(thr + 1))).astype(F32)

    log2e = [p.astype(F32) for p in _split3(jnp.full((1, QBLK), LOG2E, F32))]
    ci16 = lax.broadcasted_iota(I32, (2 * SUBLANES, QBLK), 0)

    def pos_coef(h):
        slope = 2.0 ** (-8.0 * (h + 1) / AT_HEADS)
        blk = jnp.zeros((2 * SUBLANES, QBLK), F32)
        for i in range(3):
            blk = jnp.where(ci16 == i, log2e[i] * (POS_RADIX * slope), blk)
            blk = jnp.where(ci16 == 3 + i, log2e[i] * slope, blk)
        return blk.astype(BF16)

    for j in range(AT_HEADS // 2):
        pair = qT_ref[0, j * LANES:(j + 1) * LANES, :]
        rr = lax.broadcasted_iota(I32, pair.shape, 0)
        zeros = jnp.zeros_like(pair)
        qbd_ref[j, :LANES, :QBLK] = jnp.where(rr < AT_DH, pair, zeros)
        qbd_ref[j, :LANES, QBLK:] = jnp.where(rr >= AT_DH, pair, zeros)
        qbd_ref[j, LANES:LANES + 2 * SUBLANES, :QBLK] = pos_coef(2 * j)
        qbd_ref[j, LANES:LANES + 2 * SUBLANES, QBLK:] = pos_coef(2 * j + 1)
        qbd_ref[j, LANES + 2 * SUBLANES:, :] = jnp.zeros((KP_W - LANES - 2 * SUBLANES, 2 * QBLK), BF16)
    acc_ref[...] = jnp.zeros_like(acc_ref)
    m_ref[...] = jnp.full_like(m_ref, -jnp.inf)
    l_ref[...] = jnp.zeros_like(l_ref)

    ri = lax.broadcasted_iota(I32, (KBLK, KBLK), 0)
    ci = lax.broadcasted_iota(I32, (KBLK, KBLK), 1)
    lstrict = jnp.where(ci < ri, 1.0, 0.0).astype(BF16)

    def attn_body(kb, carry, last=False):
        rows = pl.ds(pl.multiple_of(kb * KBLK, KBLK), KBLK)
        kblk = keys_ref[rows, :]
        s_idx = srow + kb * KBLK
        eq = kblk == thr
        eqf = jnp.where(eq, 1.0, 0.0)
        before = _dot(lstrict, eqf.astype(BF16)) + carry
        sel = ((kblk > thr) | (eq & (before < need))) & (s_idx < end_t)
        bias = jnp.where(sel, 0.0, -jnp.inf)
        if last:
            ahead = jnp.maximum(s_idx - t_idx, 0).astype(F32)
        s2s = [_dot(k_ref[0, rows, j * KP_W:(j + 1) * KP_W], qbd_ref[j])
               for j in range(AT_HEADS // 2)]
        for j in range(AT_HEADS // 2):
            ps, alphas = [], []
            for half in range(2):
                h = 2 * j + half
                st = s2s[j][:, half * QBLK:(half + 1) * QBLK] + bias
                if last:
                    st = st - (2.0 * LOG2E * 2.0 ** (-8.0 * (h + 1) / AT_HEADS)) * ahead
                m_old = m_ref[h:h + 1, :]
                m_new = jnp.maximum(m_old, jnp.max(st, axis=0, keepdims=True))
                m_safe = jnp.where(m_new == -jnp.inf, 0.0, m_new)
                alpha = jnp.exp2(m_old - m_safe)
                p = jnp.exp2(st - m_safe)
                l_ref[h:h + 1, :] = alpha * l_ref[h:h + 1, :] + jnp.sum(p, axis=0, keepdims=True)
                m_ref[h:h + 1, :] = m_new
                ps.append(p.astype(BF16))
                alphas.append(alpha)
            o2 = _dot(vT_ref[0, kb, j * LANES:(j + 1) * LANES, :], jnp.concatenate(ps, axis=1))
            for half in range(2):
                hs = slice((2 * j + half) * AT_DH, (2 * j + half + 1) * AT_DH)
                acc_ref[hs, :] = (alphas[half] * acc_ref[hs, :]
                                  + o2[half * AT_DH:(half + 1) * AT_DH, half * QBLK:(half + 1) * QBLK])
        return carry + jnp.sum(eqf, axis=0, keepdims=True)

    ties_seen = lax.fori_loop(0, nkb - 1, attn_body, jnp.zeros((1, QBLK), F32))
    attn_body(nkb - 1, ties_seen, last=True)

    for h in range(AT_HEADS):
        hs = slice(h * AT_DH, (h + 1) * AT_DH)
        acc_ref[hs, :] = acc_ref[hs, :] / l_ref[h:h + 1, :]
    o_ref[0] = acc_ref[...].T.astype(o_ref.dtype)


def _dsa_attn(qT, qiT, wT, kidx, k, vT):
    b, _, s = qT.shape
    nb = s // KBLK
    k_top = min(TOPK_MAX, s // 4)
    return pl.pallas_call(
        functools.partial(_dsa_attn_kernel, k_top=k_top),
        grid=(b, s // QBLK),
        in_specs=[pl.BlockSpec((1, AT_WIDTH, QBLK), lambda i, j: (i, 0, j)),
                  pl.BlockSpec((1, IDX_HEADS * IDX_DH, QBLK), lambda i, j: (i, 0, j)),
                  pl.BlockSpec((1, SUBLANES, QBLK), lambda i, j: (i, 0, j)),
                  pl.BlockSpec((1, s, KIDX_W), lambda i, j: (i, 0, 0)),
                  pl.BlockSpec((1, s, K_W), lambda i, j: (i, 0, 0)),
                  pl.BlockSpec((1, nb, AT_WIDTH, KBLK), lambda i, j: (i, 0, 0, 0))],
        out_specs=pl.BlockSpec((1, QBLK, AT_WIDTH), lambda i, j: (i, j, 0)),
        out_shape=jax.ShapeDtypeStruct((b, s, AT_WIDTH), BF16),
        scratch_shapes=[pltpu.VMEM((s, QBLK), I32),
                        pltpu.VMEM((KIDX_W, IDX_HEADS * QBLK), BF16),
                        pltpu.VMEM((AT_HEADS // 2, KP_W, 2 * QBLK), BF16),
                        pltpu.VMEM((AT_WIDTH, QBLK), F32),
                        pltpu.VMEM((AT_HEADS, QBLK), F32),
                        pltpu.VMEM((AT_HEADS, QBLK), F32)],
        compiler_params=_params(("parallel", "arbitrary")),
        name="dsa_attn",
    )(qT, qiT, wT, kidx, k, vT)


def _layer_norm(v, g, b):
    mu = jnp.mean(v, axis=-1, keepdims=True)
    var = jnp.mean(jnp.square(v - mu), axis=-1, keepdims=True)
    return (v - mu) * lax.rsqrt(var + LN_EPS) * g + b


def _out_ln_kernel(oa_ref, ob_ref, x_ref, wa_ref, wb_ref, g_ref, b_ref, wrT_ref, bias_ref,
                   h_ref, hb_ref, eidx_ref, gT_ref, cnt_ref):
    mix = _dot(oa_ref[...], wa_ref[...]) + _dot(ob_ref[...], wb_ref[...])
    h = _layer_norm(ALPHA * x_ref[...] + mix, g_ref[...], b_ref[...])
    h_ref[...] = h
    hb_ref[...] = h.astype(BF16)
    eidx_ref[...], gT_ref[...], cnt_ref[0] = _route(h, wrT_ref[...], bias_ref[...])


def _out_ln(oa, ob, x2, wa, wb, g, b, wrT, bias_col, tm):
    t, d = x2.shape
    nt = t // tm
    row = lambda w: pl.BlockSpec((tm, w), lambda i: (i, 0))
    full = lambda a: pl.BlockSpec(a.shape, lambda i: (0, 0))
    return pl.pallas_call(
        _out_ln_kernel,
        grid=(nt,),
        in_specs=[row(oa.shape[1]), row(ob.shape[1]), row(d), full(wa), full(wb), full(g), full(b),
                  full(wrT), full(bias_col)],
        out_specs=[row(d), row(d),
                   pl.BlockSpec((TOP_K, tm), lambda i: (0, i)),
                   row(LANES),
                   pl.BlockSpec((1, N_EXPERTS, LANES), lambda i: (i, 0, 0))],
        out_shape=[jax.ShapeDtypeStruct((t, d), F32), jax.ShapeDtypeStruct((t, d), BF16),
                   jax.ShapeDtypeStruct((TOP_K, t), I32),
                   jax.ShapeDtypeStruct((t, LANES), F32),
                   jax.ShapeDtypeStruct((nt, N_EXPERTS, LANES), F32)],
        compiler_params=_params(("parallel",)),
        name="out_ln_route",
    )(oa, ob, x2, wa, wb, g, b, wrT, bias_col)


def _first_max(v, idx, big):
    m = jnp.max(v, axis=0, keepdims=True)
    first = jnp.min(jnp.where(v == m, idx, big), axis=0, keepdims=True)
    return m, first


def _route(h, wrT, bias_col):
    tm = h.shape[0]
    per_group = N_EXPERTS // N_GROUPS
    logits = _dot_nt(_cat_hlh(wrT, 1), _cat_hhl(h, 1))
    scores = _sigmoid(logits)
    sel = scores + bias_col
    iota_g = lax.broadcasted_iota(I32, (per_group, tm), 0)
    neg = -jnp.inf

    grp = jnp.zeros((N_GROUPS, tm), F32)
    iota_grp = lax.broadcasted_iota(I32, (N_GROUPS, tm), 0)
    for g in range(N_GROUPS):
        blk = sel[g * per_group:(g + 1) * per_group, :]
        m1, i1 = _first_max(blk, iota_g, per_group)
        m2 = jnp.max(jnp.where(iota_g == i1, neg, blk), axis=0, keepdims=True)
        grp = jnp.where(iota_grp == g, m1 + m2, grp)

    gmask = jnp.zeros((N_GROUPS, tm), F32)
    work = grp
    for _ in range(TOPK_GROUPS):
        _, gi = _first_max(work, iota_grp, N_GROUPS)
        hit = iota_grp == gi
        gmask = jnp.where(hit, 1.0, gmask)
        work = jnp.where(hit, neg, work)

    e_mask = jnp.concatenate(
        [jnp.broadcast_to(gmask[g:g + 1, :], (per_group, tm)) for g in range(N_GROUPS)], axis=0)
    iota_e = lax.broadcasted_iota(I32, (N_EXPERTS, tm), 0)
    work = jnp.where(e_mask > 0.5, sel, neg)
    iota_k = lax.broadcasted_iota(I32, (TOP_K, tm), 0)
    eidx = jnp.zeros((TOP_K, tm), I32)
    w = jnp.zeros((N_EXPERTS, tm), F32)
    chosen = jnp.zeros((N_EXPERTS, tm), F32)
    for k in range(TOP_K):
        _, ei = _first_max(work, iota_e, N_EXPERTS)
        hit = iota_e == ei
        eidx = jnp.where(iota_k == k, ei, eidx)
        w = jnp.where(hit, scores, w)
        chosen = jnp.where(hit, 1.0, chosen)
        work = jnp.where(hit, neg, work)

    gates = w / jnp.sum(w, axis=0, keepdims=True) * ROUTED_SCALE
    gates = jnp.concatenate([gates, jnp.zeros((LANES - N_EXPERTS, tm), F32)], axis=0)
    cnt = jnp.broadcast_to(jnp.sum(chosen, axis=1, keepdims=True), (N_EXPERTS, LANES))
    return eidx, gates.T, cnt


MOE_TM = 512
SEG = 16
EXP_TM = 1200
ROW_BUF = TOP_K * MOE_TM + N_EXPERTS * SEG
CH_MAX = ROW_BUF // SEG
P_BLK = 512
G_BLK = 512


def _moe_plan(cnt, n_exp_tiles):
    nt = cnt.shape[0]
    n16 = (cnt + (SEG - 1)) // SEG
    so16 = jnp.cumsum(n16, axis=1) - n16
    r16 = jnp.sum(n16, axis=0)
    per = EXP_TM // SEG
    rp16 = (r16 + (per - 1)) // per * per
    ends = jnp.cumsum(rp16)
    off16 = ends - rp16
    go16 = off16[None, :] + jnp.cumsum(n16, axis=0) - n16
    n_et = (ends[-1] // per).astype(I32).reshape(1)
    tiles = jnp.arange(n_exp_tiles, dtype=I32)
    tile_expert = jnp.sum((ends // per)[None, :] <= tiles[:, None], axis=1)
    tile_expert = jnp.minimum(tile_expert, N_EXPERTS - 1).astype(I32)
    def copy_list(per_seg, first_chunk, step, size):
        start = jnp.cumsum(per_seg, axis=1) - per_seg
        k = jnp.arange(size, dtype=I32)[None, :, None]
        mine = (start[:, None, :] <= k) & (k < (start + per_seg)[:, None, :])
        rel = first_chunk[:, None, :] + (k - start[:, None, :]) * step
        pick = lambda base: jnp.sum(jnp.where(mine, base[:, None, :] + rel, 0), axis=2)
        flat = lambda a: a.reshape(nt * size).astype(I32)
        return flat(pick(so16)), flat(pick(go16)), jnp.sum(per_seg, axis=1).astype(I32)

    n_big = n16 // BIG
    big = copy_list(n_big, jnp.zeros_like(n16), BIG, BIG_MAX)
    small = copy_list(n16 - n_big * BIG, n_big * BIG, 1, SMALL_MAX)
    copies = (big[0], big[1], small[0], small[1], big[2], small[2])
    twice = lambda a: jnp.concatenate([a, a], axis=1).astype(F32)
    seg_lo, seg_hi = twice(so16 * SEG), twice((so16 + n16) * SEG)
    as_rows = lambda a: jnp.broadcast_to(a[:, None, :], (nt, SUBLANES, 2 * N_EXPERTS))
    as_cols = lambda a: jnp.broadcast_to(a[:, :, None], (nt, 2 * N_EXPERTS, LANES))
    return dict(copies=copies,
                rtot16=jnp.sum(n16, axis=1).astype(I32), r16=r16.astype(I32), rp16=rp16.astype(I32),
                off16=off16.astype(I32), n_et=n_et, tile_expert=tile_expert,
                lo_rows=as_rows(seg_lo), hi_rows=as_rows(seg_hi),
                lo_cols=as_cols(seg_lo), hi_cols=as_cols(seg_hi))


WAIT_BATCH = 16
BIG = 4
BIG_MAX = CH_MAX // BIG
SMALL_MAX = N_EXPERTS * (BIG - 1)


def _chunk(ref, c, n=1):
    return ref.at[pl.ds(pl.multiple_of(c * SEG, SEG), n * SEG)]


def _segment_copies(i, copies, rtot_ref, make_copy):
    bsrc_ref, bdst_ref, ssrc_ref, sdst_ref, nbig_ref, nsmall_ref = copies

    def big_body(k, carry):
        make_copy(bsrc_ref[i * BIG_MAX + k], bdst_ref[i * BIG_MAX + k], BIG).start()
        return carry

    def small_body(k, carry):
        make_copy(ssrc_ref[i * SMALL_MAX + k], sdst_ref[i * SMALL_MAX + k], 1).start()
        return carry

    lax.fori_loop(0, nbig_ref[i], big_body, 0)
    lax.fori_loop(0, nsmall_ref[i], small_body, 0)
    return rtot_ref[i]


def _wait_copies(count, make_copy):
    for n, trips in ((WAIT_BATCH, count // WAIT_BATCH), (1, count % WAIT_BATCH)):
        def body(c, carry, n=n):
            make_copy(0, 0, n).wait()
            return carry

        lax.fori_loop(0, trips, body, 0)


POS_SPLIT = 64


def _dispatch_kernel(bsrc_ref, bdst_ref, ssrc_ref, sdst_ref, nbig_ref, nsmall_ref,
                     rtot_ref, r16_ref, rp16_ref, off16_ref,
                     x_ref, eidx_ref, locol_ref, lorow_ref, hirow_ref,
                     xs_hbm, posT_ref, buf_ref, zero_ref, pend_ref, sem):
    i = pl.program_id(0)
    slot = lax.rem(i, 2)
    buf = buf_ref.at[slot]
    tm = x_ref.shape[0]
    eidx = eidx_ref[...]
    iota_e = lax.broadcasted_iota(I32, (N_EXPERTS, tm), 0)
    onehot = jnp.zeros((N_EXPERTS, tm), F32)
    for k in range(TOP_K):
        onehot = jnp.where(iota_e == eidx[k:k + 1, :], 1.0, onehot)
    earlier = lax.broadcasted_iota(I32, (tm, tm), 0) < lax.broadcasted_iota(I32, (tm, tm), 1)
    rank = _dot(onehot.astype(BF16), jnp.where(earlier, 1.0, 0.0).astype(BF16))
    pos1 = jnp.where(onehot > 0.5, locol_ref[0][:N_EXPERTS, 0:1] + rank + 1.0, 0.0)
    pos_hi = jnp.floor(pos1 * (1.0 / POS_SPLIT)) * POS_SPLIT
    pos2 = jnp.concatenate([pos_hi, pos1 - pos_hi], axis=0)
    posT_ref[...] = pos2.T
    pos2b = pos2.astype(BF16)

    x = x_ref[...]
    seg_lo = lorow_ref[0][0:1, :]
    seg_hi = hirow_ref[0][0:1, :]
    n_blk = (rtot_ref[i] * SEG + (P_BLK - 1)) // P_BLK

    def blk_body(rb, carry):
        r0 = pl.multiple_of(rb * P_BLK, P_BLK)
        r_e = (lax.broadcasted_iota(I32, (P_BLK, 2 * N_EXPERTS), 0) + r0).astype(F32)
        owner = jnp.where((r_e >= seg_lo) & (r_e < seg_hi), 1.0, 0.0).astype(BF16)
        want = _dot(owner, pos2b)
        r_t = (lax.broadcasted_iota(I32, (P_BLK, tm), 0) + (r0 + 1)).astype(F32)
        p = jnp.where(want == r_t, 1.0, 0.0).astype(BF16)
        buf[pl.ds(r0, P_BLK), :] = _dot(p, x).astype(BF16)
        return carry

    lax.fori_loop(0, n_blk, blk_body, 0)

    @pl.when(i > 0)
    def _():
        prev_copy = lambda src, dst, n=1: pltpu.make_async_copy(
            _chunk(buf_ref.at[1 - slot], src, n), _chunk(xs_hbm, dst, n), sem.at[1 - slot])
        _wait_copies(pend_ref[0], prev_copy)

    out_copy = lambda src, dst, n=1: pltpu.make_async_copy(_chunk(buf, src, n), _chunk(xs_hbm, dst, n),
                                                           sem.at[slot])
    copies = (bsrc_ref, bdst_ref, ssrc_ref, sdst_ref, nbig_ref, nsmall_ref)
    pend_ref[0] = _segment_copies(i, copies, rtot_ref, out_copy)

    @pl.when(i == pl.num_programs(0) - 1)
    def _():
        _wait_copies(pend_ref[0], out_copy)
        zero_ref[...] = jnp.zeros_like(zero_ref)
        zero_copy = lambda src, dst, n=1: pltpu.make_async_copy(
            _chunk(zero_ref, 0, n), _chunk(xs_hbm, dst, n), sem.at[slot])

        def zero_chunk(c, carry):
            zero_copy(0, c).start()
            return carry

        def tail_body(e, total):
            lax.fori_loop(off16_ref[e] + r16_ref[e], off16_ref[e] + rp16_ref[e], zero_chunk, 0)
            return total + rp16_ref[e] - r16_ref[e]

        _wait_copies(lax.fori_loop(0, N_EXPERTS, tail_body, 0), zero_copy)

        per = EXP_TM // SEG
        first = (off16_ref[N_EXPERTS - 1] + rp16_ref[N_EXPERTS - 1]) // per
        n_tiles = xs_hbm.shape[0] // EXP_TM
        tile_copy = lambda tile: pltpu.make_async_copy(
            zero_ref, xs_hbm.at[pl.ds(pl.multiple_of(tile * EXP_TM, EXP_TM), EXP_TM)], sem.at[slot])

        def zero_tile(tile, carry):
            tile_copy(tile).start()
            return carry

        def wait_tile(tile, carry):
            tile_copy(0).wait()
            return carry

        lax.fori_loop(first, n_tiles, zero_tile, 0)
        lax.fori_loop(first, n_tiles, wait_tile, 0)


def _dispatch(hb, eidx, plan, n_rows):
    t, d = hb.shape
    nt = t // MOE_TM
    grid_spec = pltpu.PrefetchScalarGridSpec(
        num_scalar_prefetch=10,
        grid=(nt,),
        in_specs=[pl.BlockSpec((MOE_TM, d), lambda i, *_: (i, 0)),
                  pl.BlockSpec((TOP_K, MOE_TM), lambda i, *_: (0, i)),
                  pl.BlockSpec((1, 2 * N_EXPERTS, LANES), lambda i, *_: (i, 0, 0)),
                  pl.BlockSpec((1, SUBLANES, 2 * N_EXPERTS), lambda i, *_: (i, 0, 0)),
                  pl.BlockSpec((1, SUBLANES, 2 * N_EXPERTS), lambda i, *_: (i, 0, 0))],
        out_specs=[pl.BlockSpec(memory_space=pl.ANY),
                   pl.BlockSpec((MOE_TM, LANES), lambda i, *_: (i, 0))],
        scratch_shapes=[pltpu.VMEM((2, ROW_BUF, d), BF16), pltpu.VMEM((EXP_TM, d), BF16),
                        pltpu.SMEM((1,), I32), pltpu.SemaphoreType.DMA((2,))])
    return pl.pallas_call(
        _dispatch_kernel,
        grid_spec=grid_spec,
        out_shape=[jax.ShapeDtypeStruct((n_rows, d), BF16),
                   jax.ShapeDtypeStruct((t, LANES), F32)],
        compiler_params=_params(("arbitrary",)),
        name="moe_dispatch",
    )(*plan["copies"], plan["rtot16"], plan["r16"], plan["rp16"],
      plan["off16"], hb, eidx, plan["lo_cols"], plan["lo_rows"], plan["hi_rows"])


def _experts_kernel(te_ref, net_ref, x_ref, wg_ref, wu_ref, wd_ref, y_ref, wgb_ref, wub_ref, wdb_ref):
    j = pl.program_id(0)
    used = j < net_ref[0]

    @pl.when(used & ((j == 0) | (te_ref[j] != te_ref[jnp.maximum(j - 1, 0)])))
    def _():
        wgb_ref[...] = wg_ref[0].astype(BF16)
        wub_ref[...] = wu_ref[0].astype(BF16)
        wdb_ref[...] = wd_ref[0].astype(BF16)

    @pl.when(used)
    def _():
        x = x_ref[...]
        hmid = _silu(_dot(x, wgb_ref[...])) * _dot(x, wub_ref[...])
        y_ref[...] = _dot(hmid.astype(BF16), wdb_ref[...]).astype(y_ref.dtype)


def _experts(xs, wg, wu, wd, plan):
    n_rows, d = xs.shape
    row_map = lambda j, te, net: (jnp.minimum(j, net[0] - 1), 0)
    w_map = lambda j, te, net: (te[jnp.minimum(j, net[0] - 1)], 0, 0)
    grid_spec = pltpu.PrefetchScalarGridSpec(
        num_scalar_prefetch=2,
        grid=(n_rows // EXP_TM,),
        in_specs=[pl.BlockSpec((EXP_TM, d), row_map),
                  pl.BlockSpec((1, d, D_EXPERT), w_map),
                  pl.BlockSpec((1, d, D_EXPERT), w_map),
                  pl.BlockSpec((1, D_EXPERT, d), w_map)],
        out_specs=pl.BlockSpec((EXP_TM, d), row_map),
        scratch_shapes=[pltpu.VMEM((d, D_EXPERT), BF16), pltpu.VMEM((d, D_EXPERT), BF16),
                        pltpu.VMEM((D_EXPERT, d), BF16)])
    return pl.pallas_call(
        _experts_kernel,
        grid_spec=grid_spec,
        out_shape=jax.ShapeDtypeStruct((n_rows, d), BF16),
        input_output_aliases={2: 0},
        compiler_params=_params(("arbitrary",)),
        name="moe_experts",
    )(plan["tile_expert"], plan["n_et"], xs, wg, wu, wd)


def _combine_kernel(bsrc_ref, bdst_ref, ssrc_ref, sdst_ref, nbig_ref, nsmall_ref, rtot_ref,
                    ys_hbm, posT_ref, gT_ref, locol_ref, hicol_ref, h_ref,
                    sg_ref, su_ref, sd_ref, lg_ref, lb_ref, o_ref, ybuf_ref, acc_ref, pend_ref, sem):
    i = pl.program_id(0)
    slot = lax.rem(i, 2)
    ybuf = ybuf_ref.at[slot]
    tm = h_ref.shape[0]

    def fetch(tile, to_slot):
        copy = lambda dst, src, n=1: pltpu.make_async_copy(
            _chunk(ys_hbm, src, n), _chunk(ybuf_ref.at[to_slot], dst, n), sem.at[to_slot])
        copies = (bsrc_ref, bdst_ref, ssrc_ref, sdst_ref, nbig_ref, nsmall_ref)
        pend_ref[to_slot] = _segment_copies(tile, copies, rtot_ref, copy)

    @pl.when(i == 0)
    def _():
        ybuf_ref[...] = jnp.zeros_like(ybuf_ref)
        fetch(0, 0)

    @pl.when(i + 1 < pl.num_programs(0))
    def _():
        fetch(i + 1, 1 - slot)

    h = h_ref[...]
    xb = h.astype(BF16)
    hs = _silu(_dot(xb, sg_ref[...])) * _dot(xb, su_ref[...])
    acc_ref[...] = ALPHA * h + _dot(hs.astype(BF16), sd_ref[...])
    _wait_copies(pend_ref[slot], lambda dst, src, n=1: pltpu.make_async_copy(
        _chunk(ys_hbm, src, n), _chunk(ybuf, dst, n), sem.at[slot]))

    pos2b = posT_ref[...].astype(BF16)
    g_hi, g_lo = _split(gT_ref[...])
    gate2b = (g_hi.astype(F32) + pltpu.roll(g_lo.astype(F32), N_EXPERTS, 1)).astype(BF16)
    seg_lo = jnp.broadcast_to(locol_ref[0][:, 0:1], (2 * N_EXPERTS, G_BLK))
    seg_hi = jnp.broadcast_to(hicol_ref[0][:, 0:1], (2 * N_EXPERTS, G_BLK))
    n_blk = (rtot_ref[i] * SEG + (G_BLK - 1)) // G_BLK

    def blk_body(cb, carry):
        c0 = pl.multiple_of(cb * G_BLK, G_BLK)
        c_e = (lax.broadcasted_iota(I32, (2 * N_EXPERTS, G_BLK), 1) + c0).astype(F32)
        owner = jnp.where((c_e >= seg_lo) & (c_e < seg_hi), 1.0, 0.0).astype(BF16)
        want = _dot(pos2b, owner)
        gate = _dot(gate2b, owner)
        c_t = (lax.broadcasted_iota(I32, (tm, G_BLK), 1) + (c0 + 1)).astype(F32)
        g = jnp.where(want == c_t, gate, 0.0).astype(BF16)
        acc_ref[...] += _dot(g, ybuf[pl.ds(c0, G_BLK), :])
        return carry

    lax.fori_loop(0, n_blk, blk_body, 0)
    o_ref[...] = _layer_norm(acc_ref[...], lg_ref[...], lb_ref[...])


def _combine(ys, posT, gT, h1, sg, su, sd, lg, lb, plan):
    t, d = h1.shape
    nt = t // MOE_TM
    row = lambda w: pl.BlockSpec((MOE_TM, w), lambda i, *_: (i, 0))
    full = lambda a: pl.BlockSpec(a.shape, lambda i, *_: (0, 0))
    seg = pl.BlockSpec((1, 2 * N_EXPERTS, LANES), lambda i, *_: (i, 0, 0))
    grid_spec = pltpu.PrefetchScalarGridSpec(
        num_scalar_prefetch=7,
        grid=(nt,),
        in_specs=[pl.BlockSpec(memory_space=pl.ANY), row(LANES), row(LANES), seg, seg, row(d),
                  full(sg), full(su), full(sd), full(lg), full(lb)],
        out_specs=row(d),
        scratch_shapes=[pltpu.VMEM((2, ROW_BUF, d), BF16), pltpu.VMEM((MOE_TM, d), F32),
                        pltpu.SMEM((2,), I32), pltpu.SemaphoreType.DMA((2,))])
    return pl.pallas_call(
        _combine_kernel,
        grid_spec=grid_spec,
        out_shape=jax.ShapeDtypeStruct((t, d), F32),
        compiler_params=_params(("arbitrary",)),
        name="moe_combine",
    )(*plan["copies"], plan["rtot16"], ys, posT, gT,
      plan["lo_cols"], plan["hi_cols"], h1, sg, su, sd, lg, lb)


def _moe(h1, hb, eidx, gT, cnt, wg, wu, wd, sg, su, sd, lg, lb):
    t = h1.shape[0]
    nt = t // MOE_TM
    max_rows = nt * ROW_BUF + N_EXPERTS * EXP_TM
    n_exp_tiles = pl.cdiv(max_rows, EXP_TM)
    plan = _moe_plan(cnt[:, :, 0].astype(I32), n_exp_tiles)
    xs, posT = _dispatch(hb, eidx, plan, n_exp_tiles * EXP_TM)
    ys = _experts(xs, wg, wu, wd, plan)
    return _combine(ys, posT, gT, h1, sg, su, sd, lg, lb, plan)


def kernel(x, w_in, hg_lb_logits, hg_norm_g, q_norm_g, w_q_up, w_qidx_up, kv_norm_g, w_kv_up,
           idx_ln_g, idx_ln_b, w_out, ln1_g, ln1_b, w_router, router_bias, w_e_gate, w_e_up,
           w_e_down, w_s_gate, w_s_up, w_s_down, ln2_g, ln2_b):
    b, s, d = x.shape
    h = x
    for l in range(DEPTH):
        h2 = h.reshape(b * s, d)
        n_main = 4 * HG_WIDTH
        w_main = w_in[l, :, :n_main].astype(BF16)
        c_kv, c_idx = n_main + Q_RANK, n_main + Q_RANK + KV_RANK
        w_sel = jnp.concatenate([w_in[l, :, n_main:c_kv], w_in[l, :, c_idx:]], axis=1)
        w_sel = jnp.pad(w_sel, ((0, 0), (0, SEL_W - w_sel.shape[1])))
        w_ckv = w_in[l, :, c_kv:c_idx].astype(BF16)

        o_a = _hgrn(h, w_main, hg_lb_logits, hg_norm_g[l], l)

        wqT = w_q_up[l].T.astype(BF16)
        wqiT = w_qidx_up[l].T
        wk = w_kv_up[l][:, :AT_WIDTH].astype(BF16)
        wvT = w_kv_up[l][:, AT_WIDTH:].T.astype(BF16)
        qT, qiT, k, vT, kidx, wT = _dsa_prep(h, w_sel, w_ckv, q_norm_g[l], kv_norm_g[l],
                                             idx_ln_g[l], idx_ln_b[l], wqT, wqiT, wk, wvT)
        o_b = _dsa_attn(qT, qiT, wT, kidx, k, vT)

        h1, hb, eidx, gT, cnt = _out_ln(
            o_a.reshape(b * s, HG_WIDTH), o_b.reshape(b * s, AT_WIDTH), h2,
            w_out[l, :HG_WIDTH].astype(BF16), w_out[l, HG_WIDTH:].astype(BF16),
            ln1_g[l].reshape(1, d), ln1_b[l].reshape(1, d),
            w_router[l].T, router_bias[l].reshape(N_EXPERTS, 1), MOE_TM)
        out = _moe(h1, hb, eidx, gT, cnt, w_e_gate[l], w_e_up[l], w_e_down[l],
                   w_s_gate[l].astype(BF16), w_s_up[l].astype(BF16), w_s_down[l].astype(BF16),
                   ln2_g[l].reshape(1, d), ln2_b[l].reshape(1, d))
        h = out.reshape(b, s, d)
    return h
```

```python
import functools

import jax
import jax.numpy as jnp
from jax import lax
from jax.experimental import pallas as pl
from jax.experimental.pallas import tpu as pltpu

F32 = jnp.float32
BF16 = jnp.bfloat16
I32 = jnp.int32
HIGHEST = lax.Precision.HIGHEST

CHUNK = 64
HG_HEADS = 4
HG_DK = 128
HG_DV = 128
HG_WIDTH = HG_HEADS * HG_DV
AT_HEADS = 8
AT_DH = 64
AT_WIDTH = AT_HEADS * AT_DH
Q_RANK = 256
KV_RANK = 128
IDX_HEADS = 4
IDX_DH = 64
TOPK_MAX = 256
N_EXPERTS = 64
TOP_K = 8
N_GROUPS = 8
TOPK_GROUPS = 4
D_EXPERT = 256
ROUTED_SCALE = 2.5
DEPTH = 1
ALPHA = (2.0 * DEPTH) ** 0.25
LN_EPS = 1e-5
RMS_EPS = 1e-6
LOG2E = 1.4426950408889634

LANES = 128
SUBLANES = 8
QBLK = 4 * CHUNK
KBLK = 256
VMEM_LIMIT = 56 * 1024 * 1024
INT_MIN = -2 ** 31


def _params(sem, vmem=VMEM_LIMIT):
    return pltpu.CompilerParams(dimension_semantics=sem, vmem_limit_bytes=vmem)


def _dot(a, b, precision=None):
    return jnp.dot(a, b, preferred_element_type=F32, precision=precision)


def _dot_nt(a, b, precision=None):
    return lax.dot_general(a, b, (((1,), (1,)), ((), ())), preferred_element_type=F32,
                           precision=precision)


def _dot_tn(a, b, precision=None):
    return lax.dot_general(a, b, (((0,), (0,)), ((), ())), preferred_element_type=F32,
                           precision=precision)


def _split(x):
    hi = x.astype(BF16)
    return hi, (x - hi.astype(F32)).astype(BF16)


def _cat_hhl(x, axis):
    hi, lo = _split(x)
    return jnp.concatenate([hi, hi, lo], axis=axis)


def _cat_hlh(x, axis):
    hi, lo = _split(x)
    return jnp.concatenate([hi, lo, hi], axis=axis)


def _split3(x):
    hi = x.astype(BF16)
    r = x - hi.astype(F32)
    mid = r.astype(BF16)
    return hi, mid, (r - mid.astype(F32)).astype(BF16)


def _cat_act6(x, axis):
    h, m, l = _split3(x)
    return jnp.concatenate([h, h, m, h, l, m], axis=axis)


def _cat_wgt6(x, axis):
    h, m, l = _split3(x)
    return jnp.concatenate([h, m, h, l, h, m], axis=axis)


def _sigmoid(x):
    return 1.0 / (1.0 + jnp.exp(-x))


def _silu(x):
    return x * _sigmoid(x)


def _hgrn_kernel(x_ref, w_ref, lbl_ref, ng_ref, o_ref, p_ref, st_ref, *, n_chunks, layer):
    @pl.when(pl.program_id(1) == 0)
    def _():
        st_ref[...] = jnp.zeros_like(st_ref)

    for sq in range(x_ref.shape[0]):
        p_ref[sq] = _dot(x_ref[sq].astype(BF16), w_ref[...])

    lg = lbl_ref[...]
    ex = jnp.exp(lg - jnp.max(lg, axis=0, keepdims=True))
    lb_all = jnp.sum(ex[: layer + 1], axis=0, keepdims=True) / jnp.sum(ex, axis=0, keepdims=True)
    ng = ng_ref[...]

    r = lax.broadcasted_iota(I32, (CHUNK, CHUNK), 0)
    c = lax.broadcasted_iota(I32, (CHUNK, CHUNK), 1)
    causal = c <= r
    tri = jnp.where(causal, 1.0, 0.0).astype(BF16)
    tri2 = jnp.concatenate([tri, tri], axis=1)

    units = [(sq, h) for sq in range(p_ref.shape[0]) for h in range(HG_HEADS)]

    def chunk_body(j, carry):
        rows = pl.ds(pl.multiple_of(j * CHUNK, CHUNK), CHUNK)
        ks, bs = [], []
        for sq, h in units:
            lo = h * HG_DK
            lb = lb_all[:, lo:lo + HG_DK]
            f = lb + (1.0 - lb) * _sigmoid(p_ref[sq, rows, HG_WIDTH + lo:HG_WIDTH + lo + HG_DK])
            lf_hi, lf_lo = _split(jnp.log(f))
            ks.append(1.0 - f)
            bs.append(_dot(tri2, jnp.concatenate([lf_hi, lf_lo], axis=0)))
        scs, ois, vbs = [], [], []
        for (sq, h), k, b in zip(units, ks, bs):
            lo = h * HG_DK
            b_last = b[CHUNK - 1:CHUNK, :]
            q_dec = (_silu(p_ref[sq, rows, lo:lo + HG_DK]) * jnp.exp(b)).astype(BF16)
            k_inv = (k * jnp.exp(-b)).astype(BF16)
            k_dec = (k * jnp.exp(b_last - b)).astype(BF16)
            vb = p_ref[sq, rows, 2 * HG_WIDTH + lo:2 * HG_WIDTH + lo + HG_DV].astype(BF16)
            st = st_ref[sq, h]
            scs.append(jnp.where(causal, _dot_nt(q_dec, k_inv), 0.0).astype(BF16))
            ois.append(_dot_nt(q_dec, st.astype(BF16)))
            st_ref[sq, h] = st * jnp.exp(b_last) + _dot_tn(vb, k_dec)
            vbs.append(vb)
        for (sq, h), sc, oi, vb in zip(units, scs, ois, vbs):
            lo = h * HG_DK
            o = _dot(sc, vb) + oi
            o = o * lax.rsqrt(jnp.mean(o * o, axis=-1, keepdims=True) + RMS_EPS) * ng
            gate = p_ref[sq, rows, 3 * HG_WIDTH + lo:3 * HG_WIDTH + lo + HG_DV]
            o_ref[sq, rows, lo:lo + HG_DV] = (o * _silu(gate)).astype(o_ref.dtype)
        return carry

    lax.fori_loop(0, n_chunks, chunk_body, 0)


def _hgrn(x, w_main, lb_logits, norm_g, layer, ct=512, n_seq=2):
    b, s, d = x.shape
    w = w_main.shape[1]
    n_seq = min(n_seq, b)
    ct = min(ct, s)
    n_chunks = ct // CHUNK
    return pl.pallas_call(
        functools.partial(_hgrn_kernel, n_chunks=n_chunks, layer=layer),
        grid=(b // n_seq, s // ct),
        in_specs=[pl.BlockSpec((n_seq, ct, d), lambda i, j: (i, j, 0)),
                  pl.BlockSpec(w_main.shape, lambda i, j: (0, 0)),
                  pl.BlockSpec(lb_logits.shape, lambda i, j: (0, 0)),
                  pl.BlockSpec((1, HG_DV), lambda i, j: (0, 0))],
        out_specs=pl.BlockSpec((n_seq, ct, HG_WIDTH), lambda i, j: (i, j, 0)),
        out_shape=jax.ShapeDtypeStruct((b, s, HG_WIDTH), BF16),
        scratch_shapes=[pltpu.VMEM((n_seq, ct, w), F32),
                        pltpu.VMEM((n_seq, HG_HEADS, HG_DV, HG_DK), F32)],
        compiler_params=_params(("parallel", "arbitrary")),
        name="hgrn",
    )(x, w_main, lb_logits, norm_g.reshape(1, HG_DV))


POS_SHIFT = 4
POS_RADIX = 1 << POS_SHIFT
KP_W = 2 * LANES
K_W = (AT_HEADS // 2) * KP_W
SEL_W = Q_RANK + LANES
KIDX_W = 6 * IDX_DH


def _dsa_prep_kernel(x_ref, wsel_ref, wckv_ref, qg_ref, kvg_ref, lng_ref, lnb_ref, wqT_ref, wqiT_ref,
                     wk_ref, wvT_ref, qT_ref, qiT_ref, k_ref, vT_ref, kidx_ref, wT_ref, w6_ref):
    @pl.when((pl.program_id(0) == 0) & (pl.program_id(1) == 0))
    def _():
        w6_ref[...] = _cat_wgt6(wsel_ref[...], 0)

    x = x_ref[...]
    sel = _dot(_cat_act6(x, 1), w6_ref[...])
    cq = sel[:, :Q_RANK]
    cqn = cq * lax.rsqrt(jnp.mean(cq * cq, axis=-1, keepdims=True) + RMS_EPS) * qg_ref[...]
    qT_ref[0] = (_dot_nt(wqT_ref[...], cqn.astype(BF16)) * (AT_DH ** -0.5 * LOG2E)).astype(qT_ref.dtype)
    qiT_ref[0] = _dot_nt(_cat_wgt6(wqiT_ref[...], 1), _cat_act6(cqn, 1))
    ckv = _dot(x.astype(BF16), wckv_ref[...])
    ckvn = (ckv * lax.rsqrt(jnp.mean(ckv * ckv, axis=-1, keepdims=True) + RMS_EPS)
            * kvg_ref[...]).astype(BF16)
    kmat = _dot(ckvn, wk_ref[...])
    tm = kmat.shape[0]
    s_abs = pl.program_id(1) * tm + lax.broadcasted_iota(I32, (tm, LANES), 0)
    lane = lax.broadcasted_iota(I32, (tm, LANES), 1)
    pos = jnp.where(lane < 3, s_abs >> POS_SHIFT,
                    jnp.where(lane < 6, s_abs & (POS_RADIX - 1), 0)).astype(F32)
    k_ref[0] = jnp.concatenate(
        [blk for j in range(AT_HEADS // 2) for blk in (kmat[:, j * LANES:(j + 1) * LANES], pos)],
        axis=1).astype(k_ref.dtype)
    for kb in range(vT_ref.shape[1]):
        vT_ref[0, kb] = _dot_nt(wvT_ref[...], ckvn[kb * KBLK:(kb + 1) * KBLK]).astype(vT_ref.dtype)
    k128 = sel[:, Q_RANK:]
    in_k = lax.broadcasted_iota(I32, (1, LANES), 1) < IDX_DH
    mu = jnp.sum(jnp.where(in_k, k128, 0.0), axis=-1, keepdims=True) * (1.0 / IDX_DH)
    dev = jnp.where(in_k, k128 - mu, 0.0)
    var = jnp.sum(dev * dev, axis=-1, keepdims=True) * (1.0 / IDX_DH)
    y = dev * lax.rsqrt(var + LN_EPS) * lng_ref[...] + lnb_ref[...]
    h, m, l = (p.astype(F32) for p in _split3(y))
    up = lambda p: pltpu.roll(p, IDX_DH, 1)
    kidx_ref[0] = jnp.concatenate([h + up(h), m + up(h), l + up(m)], axis=1).astype(BF16)
    tT = k128.T
    wT_ref[0] = tT[IDX_DH:IDX_DH + SUBLANES, :] * (IDX_HEADS ** -0.5 * IDX_DH ** -0.5)


def _dsa_prep(x, w_sel, w_ckv, q_norm_g, kv_norm_g, idx_ln_g, idx_ln_b, wqT, wqiT, wk, wvT):
    b, s, d = x.shape
    tm = min(2 * KBLK, s)
    kb_per = tm // KBLK
    nb = s // KBLK
    full = lambda a: pl.BlockSpec(a.shape, lambda i, j: (0,) * a.ndim)
    pad_lanes = lambda v: jnp.pad(v.reshape(1, -1), ((0, 0), (0, LANES - v.shape[0])))
    args = (w_sel, w_ckv, q_norm_g.reshape(1, -1), kv_norm_g.reshape(1, -1),
            pad_lanes(idx_ln_g), pad_lanes(idx_ln_b), wqT, wqiT, wk, wvT)
    return pl.pallas_call(
        _dsa_prep_kernel,
        grid=(b, s // tm),
        in_specs=[pl.BlockSpec((None, tm, d), lambda i, j: (i, j, 0))] + [full(a) for a in args],
        out_specs=[pl.BlockSpec((1, AT_WIDTH, tm), lambda i, j: (i, 0, j)),
                   pl.BlockSpec((1, IDX_HEADS * IDX_DH, tm), lambda i, j: (i, 0, j)),
                   pl.BlockSpec((1, tm, K_W), lambda i, j: (i, j, 0)),
                   pl.BlockSpec((1, kb_per, AT_WIDTH, KBLK), lambda i, j: (i, j, 0, 0)),
                   pl.BlockSpec((1, tm, KIDX_W), lambda i, j: (i, j, 0)),
                   pl.BlockSpec((1, SUBLANES, tm), lambda i, j: (i, 0, j))],
        out_shape=[jax.ShapeDtypeStruct((b, AT_WIDTH, s), BF16),
                   jax.ShapeDtypeStruct((b, IDX_HEADS * IDX_DH, s), F32),
                   jax.ShapeDtypeStruct((b, s, K_W), BF16),
                   jax.ShapeDtypeStruct((b, nb, AT_WIDTH, KBLK), BF16),
                   jax.ShapeDtypeStruct((b, s, KIDX_W), BF16),
                   jax.ShapeDtypeStruct((b, SUBLANES, s), F32)],
        scratch_shapes=[pltpu.VMEM((6 * d, SEL_W), BF16)],
        compiler_params=_params(("arbitrary", "arbitrary")),
        name="dsa_prep",
    )(x, *args)


def _sortable(x):
    bits = lax.bitcast_convert_type(x, I32)
    return jnp.where(bits < 0, bits ^ jnp.int32(0x7FFFFFFF), bits)


def _dsa_attn_kernel(qT_ref, qiT_ref, wT_ref, kidx_ref, k_ref, vT_ref, o_ref,
                     keys_ref, q2_ref, qbd_ref, acc_ref, m_ref, l_ref, *, k_top):
    qb = pl.program_id(1)
    t0 = qb * QBLK
    e_max = t0 + QBLK
    nkb = (e_max + KBLK - 1) >> 8

    lane = lax.broadcasted_iota(I32, (1, QBLK), 1)
    t_idx = t0 + lane
    end_t = ((t_idx >> 6) + 1) * CHUNK
    srow = lax.broadcasted_iota(I32, (KBLK, QBLK), 0)

    for h in range(IDX_HEADS):
        q2_ref[:, h * QBLK:(h + 1) * QBLK] = _cat_wgt6(qiT_ref[0, h * IDX_DH:(h + 1) * IDX_DH, :], 0)
    wrow = jnp.concatenate([wT_ref[0, h:h + 1, :] for h in range(IDX_HEADS)], axis=1)

    def score_body(kb, carry):
        rows = pl.ds(pl.multiple_of(kb * KBLK, KBLK), KBLK)
        logits = _dot(kidx_ref[0, rows, :], q2_ref[...])
        r = jnp.maximum(logits, 0.0) * wrow
        sc = r[:, :QBLK]
        for h in range(1, IDX_HEADS):
            sc = sc + r[:, h * QBLK:(h + 1) * QBLK]
        valid = (srow + kb * KBLK) < end_t
        keys_ref[rows, :] = _sortable(jnp.where(valid, sc, -jnp.inf))
        return carry

    lax.fori_loop(0, nkb, score_body, 0)

    def count_ge(cand):
        def body(kb, acc):
            rows = pl.ds(pl.multiple_of(kb * KBLK, KBLK), KBLK)
            hit = jnp.where(keys_ref[rows, :] >= cand, 1, 0).astype(I32)
            return acc + jnp.sum(hit.reshape(KBLK // SUBLANES, SUBLANES, QBLK), axis=0)
        acc = lax.fori_loop(0, nkb, body, jnp.zeros((SUBLANES, QBLK), I32))
        return jnp.sum(acc, axis=0, keepdims=True)

    zero = jnp.zeros((1, QBLK), I32)
    thr = jnp.where(count_ge(zero) >= k_top, zero, jnp.full((1, QBLK), INT_MIN, I32))

    def bit_body(i, thr):
        cand = thr | (jnp.int32(1) << (30 - i))
        return jnp.where(count_ge(cand) >= k_top, cand, thr)

    thr = lax.fori_loop(0, 31, bit_body, thr)
    need = (k_top - (count_ge(thr + 1))).astype(F32)

    log2e = [p.astype(F32) for p in _split3(jnp.full((1, QBLK), LOG2E, F32))]
    ci16 = lax.broadcasted_iota(I32, (2 * SUBLANES, QBLK), 0)

    def pos_coef(h):
        slope = 2.0 ** (-8.0 * (h + 1) / AT_HEADS)
        blk = jnp.zeros((2 * SUBLANES, QBLK), F32)
        for i in range(3):
            blk = jnp.where(ci16 == i, log2e[i] * (POS_RADIX * slope), blk)
            blk = jnp.where(ci16 == 3 + i, log2e[i] * slope, blk)
        return blk.astype(BF16)

    for j in range(AT_HEADS // 2):
        pair = qT_ref[0, j * LANES:(j + 1) * LANES, :]
        rr = lax.broadcasted_iota(I32, pair.shape, 0)
        zeros = jnp.zeros_like(pair)
        qbd_ref[j, :LANES, :QBLK] = jnp.where(rr < AT_DH, pair, zeros)
        qbd_ref[j, :LANES, QBLK:] = jnp.where(rr >= AT_DH, pair, zeros)
        qbd_ref[j, LANES:LANES + 2 * SUBLANES, :QBLK] = pos_coef(2 * j)
        qbd_ref[j, LANES:LANES + 2 * SUBLANES, QBLK:] = pos_coef(2 * j + 1)
        qbd_ref[j, LANES + 2 * SUBLANES:, :] = jnp.zeros((KP_W - LANES - 2 * SUBLANES, 2 * QBLK), BF16)
    acc_ref[...] = jnp.zeros_like(acc_ref)
    m_ref[...] = jnp.full_like(m_ref, -jnp.inf)
    l_ref[...] = jnp.zeros_like(l_ref)

    ri = lax.broadcasted_iota(I32, (KBLK, KBLK), 0)
    ci = lax.broadcasted_iota(I32, (KBLK, KBLK), 1)
    lstrict = jnp.where(ci < ri, 1.0, 0.0).astype(BF16)

    def attn_body(kb, carry, last=False):
        rows = pl.ds(pl.multiple_of(kb * KBLK, KBLK), KBLK)
        kblk = keys_ref[rows, :]
        s_idx = srow + kb * KBLK
        eq = kblk == thr
        eqf = jnp.where(eq, 1.0, 0.0)
        before = _dot(lstrict, eqf.astype(BF16)) + carry
        sel = ((kblk > thr) | (eq & (before < need))) & (s_idx < end_t)
        bias = jnp.where(sel, 0.0, -jnp.inf)
        if last:
            ahead = jnp.maximum(s_idx - t_idx, 0).astype(F32)
        s2s = [_dot(k_ref[0, rows, j * KP_W:(j + 1) * KP_W], qbd_ref[j])
               for j in range(AT_HEADS // 2)]
        for j in range(AT_HEADS // 2):
            ps, alphas = [], []
            for half in range(2):
                h = 2 * j + half
                st = s2s[j][:, half * QBLK:(half + 1) * QBLK] + bias
                if last:
                    st = st - (2.0 * LOG2E * 2.0 ** (-8.0 * (h + 1) / AT_HEADS)) * ahead
                m_old = m_ref[h:h + 1, :]
                m_new = jnp.maximum(m_old, jnp.max(st, axis=0, keepdims=True))
                m_safe = jnp.where(m_new == -jnp.inf, 0.0, m_new)
                alpha = jnp.exp2(m_old - m_safe)
                p = jnp.exp2(st - m_safe)
                l_ref[h:h + 1, :] = alpha * l_ref[h:h + 1, :] + jnp.sum(p, axis=0, keepdims=True)
                m_ref[h:h + 1, :] = m_new
                ps.append(p.astype(BF16))
                alphas.append(alpha)
            o2 = _dot(vT_ref[0, kb, j * LANES:(j + 1) * LANES, :], jnp.concatenate(ps, axis=1))
            for half in range(2):
                hs = slice((2 * j + half) * AT_DH, (2 * j + half + 1) * AT_DH)
                acc_ref[hs, :] = (alphas[half] * acc_ref[hs, :]
                                  + o2[half * AT_DH:(half + 1) * AT_DH, half * QBLK:(half + 1) * QBLK])
        return carry + jnp.sum(eqf, axis=0, keepdims=True)

    ties_seen = lax.fori_loop(0, nkb - 1, attn_body, jnp.zeros((1, QBLK), F32))
    attn_body(nkb - 1, ties_seen, last=True)

    for h in range(AT_HEADS):
        hs = slice(h * AT_DH, (h + 1) * AT_DH)
        acc_ref[hs, :] = acc_ref[hs, :] / l_ref[h:h + 1, :]
    o_ref[0] = acc_ref[...].T.astype(o_ref.dtype)


def _dsa_attn(qT, qiT, wT, kidx, k, vT):
    b, _, s = qT.shape
    nb = s // KBLK
    k_top = min(TOPK_MAX, s // 4)
    return pl.pallas_call(
        functools.partial(_dsa_attn_kernel, k_top=k_top),
        grid=(b, s // QBLK),
        in_specs=[pl.BlockSpec((1, AT_WIDTH, QBLK), lambda i, j: (i, 0, j)),
                  pl.BlockSpec((1, IDX_HEADS * IDX_DH, QBLK), lambda i, j: (i, 0, j)),
                  pl.BlockSpec((1, SUBLANES, QBLK), lambda i, j: (i, 0, j)),
                  pl.BlockSpec((1, s, KIDX_W), lambda i, j: (i, 0, 0)),
                  pl.BlockSpec((1, s, K_W), lambda i, j: (i, 0, 0)),
                  pl.BlockSpec((1, nb, AT_WIDTH, KBLK), lambda i, j: (i, 0, 0, 0))],
        out_specs=pl.BlockSpec((1, QBLK, AT_WIDTH), lambda i, j: (i, j, 0)),
        out_shape=jax.ShapeDtypeStruct((b, s, AT_WIDTH), BF16),
        scratch_shapes=[pltpu.VMEM((s, QBLK), I32),
                        pltpu.VMEM((KIDX_W, IDX_HEADS * QBLK), BF16),
                        pltpu.VMEM((AT_HEADS // 2, KP_W, 2 * QBLK), BF16),
                        pltpu.VMEM((AT_WIDTH, QBLK), F32),
                        pltpu.VMEM((AT_HEADS, QBLK), F32),
                        pltpu.VMEM((AT_HEADS, QBLK), F32)],
        compiler_params=_params(("parallel", "arbitrary")),
        name="dsa_attn",
    )(qT, qiT, wT, kidx, k, vT)


def _layer_norm(v, g, b):
    mu = jnp.mean(v, axis=-1, keepdims=True)
    var = jnp.mean(jnp.square(v - mu), axis=-1, keepdims=True)
    return (v - mu) * lax.rsqrt(var + LN_EPS) * g + b


def _out_ln_kernel(oa_ref, ob_ref, x_ref, wa_ref, wb_ref, g_ref, b_ref, wrT_ref, bias_ref,
                   h_ref, hb_ref, eidx_ref, gT_ref, cnt_ref):
    mix = _dot(oa_ref[...], wa_ref[...]) + _dot(ob_ref[...], wb_ref[...])
    h = _layer_norm(ALPHA * x_ref[...] + mix, g_ref[...], b_ref[...])
    h_ref[...] = h
    hb_ref[...] = h.astype(BF16)
    eidx_ref[...], gT_ref[...], cnt_ref[0] = _route(h, wrT_ref[...], bias_ref[...])


def _out_ln(oa, ob, x2, wa, wb, g, b, wrT, bias_col, tm):
    t, d = x2.shape
    nt = t // tm
    row = lambda w: pl.BlockSpec((tm, w), lambda i: (i, 0))
    full = lambda a: pl.BlockSpec(a.shape, lambda i: (0, 0))
    return pl.pallas_call(
        _out_ln_kernel,
        grid=(nt,),
        in_specs=[row(oa.shape[1]), row(ob.shape[1]), row(d), full(wa), full(wb), full(g), full(b),
                  full(wrT), full(bias_col)],
        out_specs=[row(d), row(d),
                   pl.BlockSpec((TOP_K, tm), lambda i: (0, i)),
                   row(LANES),
                   pl.BlockSpec((1, N_EXPERTS, LANES), lambda i: (i, 0, 0))],
        out_shape=[jax.ShapeDtypeStruct((t, d), F32), jax.ShapeDtypeStruct((t, d), BF16),
                   jax.ShapeDtypeStruct((TOP_K, t), I32),
                   jax.ShapeDtypeStruct((t, LANES), F32),
                   jax.ShapeDtypeStruct((nt, N_EXPERTS, LANES), F32)],
        compiler_params=_params(("parallel",)),
        name="out_ln_route",
    )(oa, ob, x2, wa, wb, g, b, wrT, bias_col)


def _first_max(v, idx, big):
    m = jnp.max(v, axis=0, keepdims=True)
    first = jnp.min(jnp.where(v == m, idx, big), axis=0, keepdims=True)
    return m, first


def _route(h, wrT, bias_col):
    tm = h.shape[0]
    per_group = N_EXPERTS // N_GROUPS
    logits = _dot_nt(_cat_hlh(wrT, 1), _cat_hhl(h, 1))
    scores = _sigmoid(logits)
    sel = scores + bias_col
    iota_g = lax.broadcasted_iota(I32, (per_group, tm), 0)
    neg = -jnp.inf

    grp = jnp.zeros((N_GROUPS, tm), F32)
    iota_grp = lax.broadcasted_iota(I32, (N_GROUPS, tm), 0)
    for g in range(N_GROUPS):
        blk = sel[g * per_group:(g + 1) * per_group, :]
        m1, i1 = _first_max(blk, iota_g, per_group)
        m2 = jnp.max(jnp.where(iota_g == i1, neg, blk), axis=0, keepdims=True)
        grp = jnp.where(iota_grp == g, m1 + m2, grp)

    gmask = jnp.zeros((N_GROUPS, tm), F32)
    work = grp
    for _ in range(TOPK_GROUPS):
        _, gi = _first_max(work, iota_grp, N_GROUPS)
        hit = iota_grp == gi
        gmask = jnp.where(hit, 1.0, gmask)
        work = jnp.where(hit, neg, work)

    e_mask = jnp.concatenate(
        [jnp.broadcast_to(gmask[g:g + 1, :], (per_group, tm)) for g in range(N_GROUPS)], axis=0)
    iota_e = lax.broadcasted_iota(I32, (N_EXPERTS, tm), 0)
    work = jnp.where(e_mask > 0.5, sel, neg)
    iota_k = lax.broadcasted_iota(I32, (TOP_K, tm), 0)
    eidx = jnp.zeros((TOP_K, tm), I32)
    w = jnp.zeros((N_EXPERTS, tm), F32)
    chosen = jnp.zeros((N_EXPERTS, tm), F32)
    for k in range(TOP_K):
        _, ei = _first_max(work, iota_e, N_EXPERTS)
        hit = iota_e == ei
        eidx = jnp.where(iota_k == k, ei, eidx)
        w = jnp.where(hit, scores, w)
        chosen = jnp.where(hit, 1.0, chosen)
        work = jnp.where(hit, neg, work)

    gates = w / jnp.sum(w, axis=0, keepdims=True) * ROUTED_SCALE
    gates = jnp.concatenate([gates, jnp.zeros((LANES - N_EXPERTS, tm), F32)], axis=0)
    cnt = jnp.broadcast_to(jnp.sum(chosen, axis=1, keepdims=True), (N_EXPERTS, LANES))
    return eidx, gates.T, cnt


MOE_TM = 512
SEG = 16
EXP_TM = 1200
ROW_BUF = TOP_K * MOE_TM + N_EXPERTS * SEG
CH_MAX = ROW_BUF // SEG
P_BLK = 512
G_BLK = 512


def _moe_plan(cnt, n_exp_tiles):
    nt = cnt.shape[0]
    n16 = (cnt + (SEG - 1)) // SEG
    so16 = jnp.cumsum(n16, axis=1) - n16
    r16 = jnp.sum(n16, axis=0)
    per = EXP_TM // SEG
    rp16 = (r16 + (per - 1)) // per * per
    ends = jnp.cumsum(rp16)
    off16 = ends - rp16
    go16 = off16[None, :] + jnp.cumsum(n16, axis=0) - n16
    n_et = (ends[-1] // per).astype(I32).reshape(1)
    tiles = jnp.arange(n_exp_tiles, dtype=I32)
    tile_expert = jnp.sum((ends // per)[None, :] <= tiles[:, None], axis=1)
    tile_expert = jnp.minimum(tile_expert, N_EXPERTS - 1).astype(I32)
    def copy_list(per_seg, first_chunk, step, size):
        start = jnp.cumsum(per_seg, axis=1) - per_seg
        k = jnp.arange(size, dtype=I32)[None, :, None]
        mine = (start[:, None, :] <= k) & (k < (start + per_seg)[:, None, :])
        rel = first_chunk[:, None, :] + (k - start[:, None, :]) * step
        pick = lambda base: jnp.sum(jnp.where(mine, base[:, None, :] + rel, 0), axis=2)
        flat = lambda a: a.reshape(nt * size).astype(I32)
        return flat(pick(so16)), flat(pick(go16)), jnp.sum(per_seg, axis=1).astype(I32)

    n_big = n16 // BIG
    big = copy_list(n_big, jnp.zeros_like(n16), BIG, BIG_MAX)
    small = copy_list(n16 - n_big * BIG, n_big * BIG, 1, SMALL_MAX)
    copies = (big[0], big[1], small[0], small[1], big[2], small[2])
    twice = lambda a: jnp.concatenate([a, a], axis=1).astype(F32)
    seg_lo, seg_hi = twice(so16 * SEG), twice((so16 + n16) * SEG)
    as_rows = lambda a: jnp.broadcast_to(a[:, None, :], (nt, SUBLANES, 2 * N_EXPERTS))
    as_cols = lambda a: jnp.broadcast_to(a[:, :, None], (nt, 2 * N_EXPERTS, LANES))
    return dict(copies=copies,
                rtot16=jnp.sum(n16, axis=1).astype(I32), r16=r16.astype(I32), rp16=rp16.astype(I32),
                off16=off16.astype(I32), n_et=n_et, tile_expert=tile_expert,
                lo_rows=as_rows(seg_lo), hi_rows=as_rows(seg_hi),
                lo_cols=as_cols(seg_lo), hi_cols=as_cols(seg_hi))


WAIT_BATCH = 16
BIG = 4
BIG_MAX = CH_MAX // BIG
SMALL_MAX = N_EXPERTS * (BIG - 1)


def _chunk(ref, c, n=1):
    return ref.at[pl.ds(pl.multiple_of(c * SEG, SEG), n * SEG)]


def _segment_copies(i, copies, rtot_ref, make_copy):
    bsrc_ref, bdst_ref, ssrc_ref, sdst_ref, nbig_ref, nsmall_ref = copies

    def big_body(k, carry):
        make_copy(bsrc_ref[i * BIG_MAX + k], bdst_ref[i * BIG_MAX + k], BIG).start()
        return carry

    def small_body(k, carry):
        make_copy(ssrc_ref[i * SMALL_MAX + k], sdst_ref[i * SMALL_MAX + k], 1).start()
        return carry

    lax.fori_loop(0, nbig_ref[i], big_body, 0)
    lax.fori_loop(0, nsmall_ref[i], small_body, 0)
    return rtot_ref[i]


def _wait_copies(count, make_copy):
    for n, trips in ((WAIT_BATCH, count // WAIT_BATCH), (1, count % WAIT_BATCH)):
        def body(c, carry, n=n):
            make_copy(0, 0, n).wait()
            return carry

        lax.fori_loop(0, trips, body, 0)


POS_SPLIT = 64


def _dispatch_kernel(bsrc_ref, bdst_ref, ssrc_ref, sdst_ref, nbig_ref, nsmall_ref,
                     rtot_ref, r16_ref, rp16_ref, off16_ref,
                     x_ref, eidx_ref, locol_ref, lorow_ref, hirow_ref,
                     xs_hbm, posT_ref, buf_ref, zero_ref, pend_ref, sem):
    i = pl.program_id(0)
    slot = lax.rem(i, 2)
    buf = buf_ref.at[slot]
    tm = x_ref.shape[0]

    zero_copy = lambda src, dst, n=1: pltpu.make_async_copy(
        _chunk(zero_ref, 0, n), _chunk(xs_hbm, dst, n), sem.at[2])
    tile_copy = lambda tile: pltpu.make_async_copy(
        zero_ref, xs_hbm.at[pl.ds(pl.multiple_of(tile * EXP_TM, EXP_TM), EXP_TM)], sem.at[2])
    first_free = (off16_ref[N_EXPERTS - 1] + rp16_ref[N_EXPERTS - 1]) // (EXP_TM // SEG)
    n_tiles = xs_hbm.shape[0] // EXP_TM

    @pl.when(i == 0)
    def _():
        zero_ref[...] = jnp.zeros_like(zero_ref)

        def zero_chunk(c, carry):
            zero_copy(0, c).start()
            return carry

        def tail_body(e, carry):
            lax.fori_loop(off16_ref[e] + r16_ref[e], off16_ref[e] + rp16_ref[e], zero_chunk, 0)
            return carry

        def zero_tile(tile, carry):
            tile_copy(tile).start()
            return carry

        lax.fori_loop(0, N_EXPERTS, tail_body, 0)
        lax.fori_loop(first_free, n_tiles, zero_tile, 0)

    eidx = eidx_ref[...]
    iota_e = lax.broadcasted_iota(I32, (N_EXPERTS, tm), 0)
    onehot = jnp.zeros((N_EXPERTS, tm), F32)
    for k in range(TOP_K):
        onehot = jnp.where(iota_e == eidx[k:k + 1, :], 1.0, onehot)
    earlier = lax.broadcasted_iota(I32, (tm, tm), 0) < lax.broadcasted_iota(I32, (tm, tm), 1)
    rank = _dot(onehot.astype(BF16), jnp.where(earlier, 1.0, 0.0).astype(BF16))
    pos1 = jnp.where(onehot > 0.5, locol_ref[0][:N_EXPERTS, 0:1] + rank + 1.0, 0.0)
    pos_hi = jnp.floor(pos1 * (1.0 / POS_SPLIT)) * POS_SPLIT
    pos2 = jnp.concatenate([pos_hi, pos1 - pos_hi], axis=0)
    posT_ref[...] = pos2.T
    pos2b = pos2.astype(BF16)

    x = x_ref[...]
    seg_lo = lorow_ref[0][0:1, :]
    seg_hi = hirow_ref[0][0:1, :]
    n_blk = (rtot_ref[i] * SEG + (P_BLK - 1)) // P_BLK

    def blk_body(rb, carry):
        r0 = pl.multiple_of(rb * P_BLK, P_BLK)
        r_e = (lax.broadcasted_iota(I32, (P_BLK, 2 * N_EXPERTS), 0) + r0).astype(F32)
        owner = jnp.where((r_e >= seg_lo) & (r_e < seg_hi), 1.0, 0.0).astype(BF16)
        want = _dot(owner, pos2b)
        r_t = (lax.broadcasted_iota(I32, (P_BLK, tm), 0) + (r0 + 1)).astype(F32)
        p = jnp.where(want == r_t, 1.0, 0.0).astype(BF16)
        buf[pl.ds(r0, P_BLK), :] = _dot(p, x).astype(BF16)
        return carry

    lax.fori_loop(0, n_blk, blk_body, 0)

    @pl.when(i > 0)
    def _():
        prev_copy = lambda src, dst, n=1: pltpu.make_async_copy(
            _chunk(buf_ref.at[1 - slot], src, n), _chunk(xs_hbm, dst, n), sem.at[1 - slot])
        _wait_copies(pend_ref[0], prev_copy)

    out_copy = lambda src, dst, n=1: pltpu.make_async_copy(_chunk(buf, src, n), _chunk(xs_hbm, dst, n),
                                                           sem.at[slot])
    copies = (bsrc_ref, bdst_ref, ssrc_ref, sdst_ref, nbig_ref, nsmall_ref)
    pend_ref[0] = _segment_copies(i, copies, rtot_ref, out_copy)

    @pl.when(i == pl.num_programs(0) - 1)
    def _():
        _wait_copies(pend_ref[0], out_copy)
        n_tail = lax.fori_loop(0, N_EXPERTS, lambda e, tot: tot + rp16_ref[e] - r16_ref[e], 0)
        _wait_copies(n_tail, zero_copy)

        def wait_tile(tile, carry):
            tile_copy(0).wait()
            return carry

        lax.fori_loop(first_free, n_tiles, wait_tile, 0)


def _dispatch(hb, eidx, plan, n_rows):
    t, d = hb.shape
    nt = t // MOE_TM
    grid_spec = pltpu.PrefetchScalarGridSpec(
        num_scalar_prefetch=10,
        grid=(nt,),
        in_specs=[pl.BlockSpec((MOE_TM, d), lambda i, *_: (i, 0)),
                  pl.BlockSpec((TOP_K, MOE_TM), lambda i, *_: (0, i)),
                  pl.BlockSpec((1, 2 * N_EXPERTS, LANES), lambda i, *_: (i, 0, 0)),
                  pl.BlockSpec((1, SUBLANES, 2 * N_EXPERTS), lambda i, *_: (i, 0, 0)),
                  pl.BlockSpec((1, SUBLANES, 2 * N_EXPERTS), lambda i, *_: (i, 0, 0))],
        out_specs=[pl.BlockSpec(memory_space=pl.ANY),
                   pl.BlockSpec((MOE_TM, LANES), lambda i, *_: (i, 0))],
        scratch_shapes=[pltpu.VMEM((2, ROW_BUF, d), BF16), pltpu.VMEM((EXP_TM, d), BF16),
                        pltpu.SMEM((1,), I32), pltpu.SemaphoreType.DMA((3,))])
    return pl.pallas_call(
        _dispatch_kernel,
        grid_spec=grid_spec,
        out_shape=[jax.ShapeDtypeStruct((n_rows, d), BF16),
                   jax.ShapeDtypeStruct((t, LANES), F32)],
        compiler_params=_params(("arbitrary",)),
        name="moe_dispatch",
    )(*plan["copies"], plan["rtot16"], plan["r16"], plan["rp16"],
      plan["off16"], hb, eidx, plan["lo_cols"], plan["lo_rows"], plan["hi_rows"])


def _experts_kernel(te_ref, net_ref, x_ref, wg_ref, wu_ref, wd_ref, y_ref, wgb_ref, wub_ref, wdb_ref):
    j = pl.program_id(0)
    used = j < net_ref[0]

    @pl.when(used & ((j == 0) | (te_ref[j] != te_ref[jnp.maximum(j - 1, 0)])))
    def _():
        wgb_ref[...] = wg_ref[0].astype(BF16)
        wub_ref[...] = wu_ref[0].astype(BF16)
        wdb_ref[...] = wd_ref[0].astype(BF16)

    @pl.when(used)
    def _():
        x = x_ref[...]
        hmid = _silu(_dot(x, wgb_ref[...])) * _dot(x, wub_ref[...])
        y_ref[...] = _dot(hmid.astype(BF16), wdb_ref[...]).astype(y_ref.dtype)


def _experts(xs, wg, wu, wd, plan):
    n_rows, d = xs.shape
    row_map = lambda j, te, net: (jnp.minimum(j, net[0] - 1), 0)
    w_map = lambda j, te, net: (te[jnp.minimum(j, net[0] - 1)], 0, 0)
    grid_spec = pltpu.PrefetchScalarGridSpec(
        num_scalar_prefetch=2,
        grid=(n_rows // EXP_TM,),
        in_specs=[pl.BlockSpec((EXP_TM, d), row_map),
                  pl.BlockSpec((1, d, D_EXPERT), w_map),
                  pl.BlockSpec((1, d, D_EXPERT), w_map),
                  pl.BlockSpec((1, D_EXPERT, d), w_map)],
        out_specs=pl.BlockSpec((EXP_TM, d), row_map),
        scratch_shapes=[pltpu.VMEM((d, D_EXPERT), BF16), pltpu.VMEM((d, D_EXPERT), BF16),
                        pltpu.VMEM((D_EXPERT, d), BF16)])
    return pl.pallas_call(
        _experts_kernel,
        grid_spec=grid_spec,
        out_shape=jax.ShapeDtypeStruct((n_rows, d), BF16),
        input_output_aliases={2: 0},
        compiler_params=_params(("arbitrary",)),
        name="moe_experts",
    )(plan["tile_expert"], plan["n_et"], xs, wg, wu, wd)


def _combine_kernel(bsrc_ref, bdst_ref, ssrc_ref, sdst_ref, nbig_ref, nsmall_ref, rtot_ref,
                    ys_hbm, posT_ref, gT_ref, locol_ref, hicol_ref, h_ref,
                    sg_ref, su_ref, sd_ref, lg_ref, lb_ref, o_ref, ybuf_ref, acc_ref, pend_ref, sem):
    i = pl.program_id(0)
    slot = lax.rem(i, 2)
    ybuf = ybuf_ref.at[slot]
    tm = h_ref.shape[0]

    def fetch(tile, to_slot):
        copy = lambda dst, src, n=1: pltpu.make_async_copy(
            _chunk(ys_hbm, src, n), _chunk(ybuf_ref.at[to_slot], dst, n), sem.at[to_slot])
        copies = (bsrc_ref, bdst_ref, ssrc_ref, sdst_ref, nbig_ref, nsmall_ref)
        pend_ref[to_slot] = _segment_copies(tile, copies, rtot_ref, copy)

    @pl.when(i == 0)
    def _():
        ybuf_ref[...] = jnp.zeros_like(ybuf_ref)
        fetch(0, 0)

    @pl.when(i + 1 < pl.num_programs(0))
    def _():
        fetch(i + 1, 1 - slot)

    h = h_ref[...]
    xb = h.astype(BF16)
    hs = _silu(_dot(xb, sg_ref[...])) * _dot(xb, su_ref[...])
    acc_ref[...] = ALPHA * h + _dot(hs.astype(BF16), sd_ref[...])
    _wait_copies(pend_ref[slot], lambda dst, src, n=1: pltpu.make_async_copy(
        _chunk(ys_hbm, src, n), _chunk(ybuf, dst, n), sem.at[slot]))

    pos2b = posT_ref[...].astype(BF16)
    g_hi, g_lo = _split(gT_ref[...])
    gate2b = (g_hi.astype(F32) + pltpu.roll(g_lo.astype(F32), N_EXPERTS, 1)).astype(BF16)
    seg_lo = jnp.broadcast_to(locol_ref[0][:, 0:1], (2 * N_EXPERTS, G_BLK))
    seg_hi = jnp.broadcast_to(hicol_ref[0][:, 0:1], (2 * N_EXPERTS, G_BLK))
    n_blk = (rtot_ref[i] * SEG + (G_BLK - 1)) // G_BLK

    def blk_body(cb, carry):
        c0 = pl.multiple_of(cb * G_BLK, G_BLK)
        c_e = (lax.broadcasted_iota(I32, (2 * N_EXPERTS, G_BLK), 1) + c0).astype(F32)
        owner = jnp.where((c_e >= seg_lo) & (c_e < seg_hi), 1.0, 0.0).astype(BF16)
        want = _dot(pos2b, owner)
        gate = _dot(gate2b, owner)
        c_t = (lax.broadcasted_iota(I32, (tm, G_BLK), 1) + (c0 + 1)).astype(F32)
        g = jnp.where(want == c_t, gate, 0.0).astype(BF16)
        acc_ref[...] += _dot(g, ybuf[pl.ds(c0, G_BLK), :])
        return carry

    lax.fori_loop(0, n_blk, blk_body, 0)
    o_ref[...] = _layer_norm(acc_ref[...], lg_ref[...], lb_ref[...])


def _combine(ys, posT, gT, h1, sg, su, sd, lg, lb, plan):
    t, d = h1.shape
    nt = t // MOE_TM
    row = lambda w: pl.BlockSpec((MOE_TM, w), lambda i, *_: (i, 0))
    full = lambda a: pl.BlockSpec(a.shape, lambda i, *_: (0, 0))
    seg = pl.BlockSpec((1, 2 * N_EXPERTS, LANES), lambda i, *_: (i, 0, 0))
    grid_spec = pltpu.PrefetchScalarGridSpec(
        num_scalar_prefetch=7,
        grid=(nt,),
        in_specs=[pl.BlockSpec(memory_space=pl.ANY), row(LANES), row(LANES), seg, seg, row(d),
                  full(sg), full(su), full(sd), full(lg), full(lb)],
        out_specs=row(d),
        scratch_shapes=[pltpu.VMEM((2, ROW_BUF, d), BF16), pltpu.VMEM((MOE_TM, d), F32),
                        pltpu.SMEM((2,), I32), pltpu.SemaphoreType.DMA((2,))])
    return pl.pallas_call(
        _combine_kernel,
        grid_spec=grid_spec,
        out_shape=jax.ShapeDtypeStruct((t, d), F32),
        compiler_params=_params(("arbitrary",)),
        name="moe_combine",
    )(*plan["copies"], plan["rtot16"], ys, posT, gT,
      plan["lo_cols"], plan["hi_cols"], h1, sg, su, sd, lg, lb)


def _moe(h1, hb, eidx, gT, cnt, wg, wu, wd, sg, su, sd, lg, lb):
    t = h1.shape[0]
    nt = t // MOE_TM
    max_rows = nt * ROW_BUF + N_EXPERTS * EXP_TM
    n_exp_tiles = pl.cdiv(max_rows, EXP_TM)
    plan = _moe_plan(cnt[:, :, 0].astype(I32), n_exp_tiles)
    xs, posT = _dispatch(hb, eidx, plan, n_exp_tiles * EXP_TM)
    ys = _experts(xs, wg, wu, wd, plan)
    return _combine(ys, posT, gT, h1, sg, su, sd, lg, lb, plan)


def kernel(x, w_in, hg_lb_logits, hg_norm_g, q_norm_g, w_q_up, w_qidx_up, kv_norm_g, w_kv_up,
           idx_ln_g, idx_ln_b, w_out, ln1_g, ln1_b, w_router, router_bias, w_e_gate, w_e_up,
           w_e_down, w_s_gate, w_s_up, w_s_down, ln2_g, ln2_b):
    b, s, d = x.shape
    h = x
    for l in range(DEPTH):
        h2 = h.reshape(b * s, d)
        n_main = 4 * HG_WIDTH
        w_main = w_in[l, :, :n_main].astype(BF16)
        c_kv, c_idx = n_main + Q_RANK, n_main + Q_RANK + KV_RANK
        w_sel = jnp.concatenate([w_in[l, :, n_main:c_kv], w_in[l, :, c_idx:]], axis=1)
        w_sel = jnp.pad(w_sel, ((0, 0), (0, SEL_W - w_sel.shape[1])))
        w_ckv = w_in[l, :, c_kv:c_idx].astype(BF16)

        o_a = _hgrn(h, w_main, hg_lb_logits, hg_norm_g[l], l)

        wqT = w_q_up[l].T.astype(BF16)
        wqiT = w_qidx_up[l].T
        wk = w_kv_up[l][:, :AT_WIDTH].astype(BF16)
        wvT = w_kv_up[l][:, AT_WIDTH:].T.astype(BF16)
        qT, qiT, k, vT, kidx, wT = _dsa_prep(h, w_sel, w_ckv, q_norm_g[l], kv_norm_g[l],
                                             idx_ln_g[l], idx_ln_b[l], wqT, wqiT, wk, wvT)
        o_b = _dsa_attn(qT, qiT, wT, kidx, k, vT)

        h1, hb, eidx, gT, cnt = _out_ln(
            o_a.reshape(b * s, HG_WIDTH), o_b.reshape(b * s, AT_WIDTH), h2,
            w_out[l, :HG_WIDTH].astype(BF16), w_out[l, HG_WIDTH:].astype(BF16),
            ln1_g[l].reshape(1, d), ln1_b[l].reshape(1, d),
            w_router[l].T, router_bias[l].reshape(N_EXPERTS, 1), MOE_TM)
        out = _moe(h1, hb, eidx, gT, cnt, w_e_gate[l], w_e_up[l], w_e_down[l],
                   w_s_gate[l].astype(BF16), w_s_up[l].astype(BF16), w_s_down[l].astype(BF16),
                   ln2_g[l].reshape(1, d), ln2_b[l].reshape(1, d))
        h = out.reshape(b, s, d)
    return h
```
